```python
import jax, jax.numpy as jnp
from jax import lax
import numpy as np

D_MODEL = 1024
BATCH = 8
SEQ = 2048
DEPTH = 2
DEC_BATCH = 128
DEC_SEQ = 1
PAST_LEN = 16384
PAGE_SIZE = 128

MIX_WIDTH = D_MODEL
A_WIDTH = MIX_WIDTH // 2
B_WIDTH = MIX_WIDTH - A_WIDTH
HEAD_DIM = 128
A_HEADS = A_WIDTH // HEAD_DIM
B_HEADS = B_WIDTH // HEAD_DIM
CONV_W = 4
CHUNK = 64
ROPE_BASE = 10000.0
EPS = 1e-6
L2_EPS = 1e-6
FF_RAW = -(-8 * D_MODEL // 3)
D_FF = -(-FF_RAW // 256) * 256
IN_SIZES = (3 * A_WIDTH, A_WIDTH, A_HEADS, A_HEADS, B_WIDTH, B_WIDTH, B_WIDTH, B_WIDTH)
IN_WIDTH = sum(IN_SIZES)

kernel_name = "hybrid_gdn_retention_decoder_step"


def _rmsnorm(x, w):
    xf = x.astype(jnp.float32)
    y = xf * lax.rsqrt(jnp.mean(xf * xf, axis=-1, keepdims=True) + EPS)
    return (y * w.astype(jnp.float32)).astype(x.dtype)


def _l2norm(x):
    return x * lax.rsqrt(jnp.sum(x * x, axis=-1, keepdims=True) + L2_EPS)


def _causal_conv(x, buf, w):
    L = x.shape[1]
    xp = jnp.concatenate([buf.astype(x.dtype), x], axis=1)
    out = xp[:, 0:L] * w[0]
    for i in range(1, CONV_W):
        out = out + xp[:, i:i + L] * w[i]
    return out, xp[:, -(CONV_W - 1):]


def _rope(x, pos):
    half = x.shape[-1] // 2
    inv = ROPE_BASE ** (-jnp.arange(half, dtype=jnp.float32) / half)
    ang = pos[:, None] * inv[None, :]
    cos, sin = jnp.cos(ang), jnp.sin(ang)
    x1, x2 = x[..., :half], x[..., half:]
    return jnp.concatenate([x1 * cos - x2 * sin, x1 * sin + x2 * cos], axis=-1)


def _chunk_prep(ts, L):
    C = min(CHUNK, L)
    pad = (-L) % C
    nc = (L + pad) // C
    out = []
    for t in ts:
        widths = [(0, 0)] * t.ndim
        widths[2] = (0, pad)
        t = jnp.pad(t, widths)
        out.append(t.reshape(t.shape[:2] + (nc, C) + t.shape[3:]))
    return out, C


def _decay_matrix(gc, C):
    tril = jnp.arange(C)[:, None] >= jnp.arange(C)[None, :]
    diff = gc[..., :, None] - gc[..., None, :]
    return jnp.where(tril, jnp.exp(jnp.where(tril, diff, 0.0)), 0.0), tril


def _gated_delta_chunked(q, k, v, g, beta, s0):
    B, H, L, DK = q.shape
    q = q * (DK ** -0.5)
    (q, k, v, g, beta), C = _chunk_prep([q, k, v, g, beta], L)
    gc = jnp.cumsum(g, axis=-1)
    decay, tril = _decay_matrix(gc, C)
    strict = jnp.arange(C)[:, None] > jnp.arange(C)[None, :]
    kb = k * beta[..., None]
    vb = v * beta[..., None]
    a_low = jnp.where(strict, jnp.einsum('bhnid,bhnjd->bhnij', kb, k) * decay, 0.0)
    t_mat = a_low + jnp.eye(C, dtype=a_low.dtype)
    u = lax.linalg.triangular_solve(t_mat, vb, left_side=True, lower=True, unit_diagonal=True)
    w = lax.linalg.triangular_solve(t_mat, kb * jnp.exp(gc)[..., None], left_side=True,
                                    lower=True, unit_diagonal=True)
    a_intra = jnp.where(tril, jnp.einsum('bhnid,bhnjd->bhnij', q, k) * decay, 0.0)

    def step(S, inp):
        qi, ki, ui, wi, gi, ai = inp
        v_new = ui - jnp.einsum('bhcd,bhde->bhce', wi, S)
        o = jnp.einsum('bhcd,bhde->bhce', qi * jnp.exp(gi)[..., None], S) + \
            jnp.einsum('bhij,bhje->bhie', ai, v_new)
        g_last = gi[..., -1]
        S = S * jnp.exp(g_last)[..., None, None] + jnp.einsum(
            'bhcd,bhce->bhde', ki * jnp.exp(g_last[..., None] - gi)[..., None], v_new)
        return S, o

    xs = tuple(jnp.moveaxis(t, 2, 0) for t in (q, k, u, w, gc, a_intra))
    s_fin, o = lax.scan(step, s0.astype(jnp.float32), xs)
    o = jnp.moveaxis(o, 0, 2)
    o = o.reshape(B, H, -1, o.shape[-1])[:, :, :L]
    return o, s_fin


def _retention_chunked(q, k, v, lg, s0):
    B, H, L, DK = q.shape
    k = k * (DK ** -0.5)
    (q, k, v, lg), C = _chunk_prep([q, k, v, lg], L)
    gc = jnp.cumsum(lg, axis=-1)
    decay, tril = _decay_matrix(gc, C)
    a_intra = jnp.einsum('bhnid,bhnjd->bhnij', q, k) * decay

    def step(S, inp):
        qi, ki, vi, gi, ai = inp
        o = jnp.einsum('bhcd,bhde->bhce', qi * jnp.exp(gi)[..., None], S) + \
            jnp.einsum('bhij,bhje->bhie', ai, vi)
        g_last = gi[..., -1]
        S = S * jnp.exp(g_last)[..., None, None] + jnp.einsum(
            'bhcd,bhce->bhde', ki * jnp.exp(g_last[..., None] - gi)[..., None], vi)
        return S, o

    xs = tuple(jnp.moveaxis(t, 2, 0) for t in (q, k, v, gc, a_intra))
    s_fin, o = lax.scan(step, s0.astype(jnp.float32), xs)
    o = jnp.moveaxis(o, 0, 2)
    o = o.reshape(B, H, -1, o.shape[-1])[:, :, :L]
    return o, s_fin


def _heads(t, n):
    B, L, _ = t.shape
    return t.reshape(B, L, n, HEAD_DIM).transpose(0, 2, 1, 3).astype(jnp.float32)


def _head_gated_norm(o, w, gate):
    B, H, L, D = o.shape
    o = o.transpose(0, 2, 1, 3)
    o = o * lax.rsqrt(jnp.mean(o * o, axis=-1, keepdims=True) + EPS) * w.astype(jnp.float32)
    return o.reshape(B, L, H * D) * jax.nn.silu(gate.astype(jnp.float32))


def _mixer(hn, pos, conv_buf, s_gdn, s_ret, w_in, conv_w, a_log, dt_bias, gdn_norm_w,
           ret_norm_w, w_out):
    B, L, _ = hn.shape
    p = hn @ w_in
    idx = [int(i) for i in np.cumsum(IN_SIZES)[:-1]]
    p_qkv, p_z, p_beta, p_alpha, p_bq, p_bk, p_bv, p_bg = jnp.split(p, idx, axis=-1)
    qkv, conv_new = _causal_conv(p_qkv, conv_buf, conv_w)
    qkv = jax.nn.silu(qkv)
    qa, ka, va = jnp.split(qkv, 3, axis=-1)
    qa = _l2norm(_heads(qa, A_HEADS))
    ka = _l2norm(_heads(ka, A_HEADS))
    va = _heads(va, A_HEADS)
    beta = jax.nn.sigmoid(p_beta.astype(jnp.float32)).transpose(0, 2, 1)
    g = (-jnp.exp(a_log.astype(jnp.float32)) * jax.nn.softplus(
        p_alpha.astype(jnp.float32) + dt_bias.astype(jnp.float32))).transpose(0, 2, 1)
    o_a, s_gdn_new = _gated_delta_chunked(qa, ka, va, g, beta, s_gdn)
    o_a = _head_gated_norm(o_a, gdn_norm_w, p_z)
    qb = _rope(_heads(p_bq, B_HEADS), pos)
    kb = _rope(_heads(p_bk, B_HEADS), pos)
    vb = _heads(p_bv, B_HEADS)
    log_gamma = jnp.log1p(-jnp.exp2(-5.0 - jnp.arange(B_HEADS, dtype=jnp.float32)))
    lg = jnp.broadcast_to(log_gamma[None, :, None], (B, B_HEADS, L))
    o_b, s_ret_new = _retention_chunked(qb, kb, vb, lg, s_ret)
    o_b = _head_gated_norm(o_b, ret_norm_w, p_bg)
    cat = jnp.concatenate([o_a, o_b], axis=-1).astype(hn.dtype)
    return cat @ w_out, conv_new, s_gdn_new, s_ret_new


def _swiglu(hn, w_gate_up, w_down):
    gu = hn @ w_gate_up
    gate, up = jnp.split(gu, 2, axis=-1)
    return (jax.nn.silu(gate) * up) @ w_down


def _trunk(x, pos, conv_st, gdn_st, ret_st, w_in, conv_w, a_log, dt_bias, gdn_norm_w,
           ret_norm_w, w_out, norm_mix_pre, norm_mix_post, norm_ffn_pre, norm_ffn_post,
           w_gate_up, w_down):
    h = x
    convs, gdns, rets = [], [], []
    for l in range(DEPTH):
        m, c_new, sg_new, sr_new = _mixer(
            _rmsnorm(h, norm_mix_pre[l]), pos, conv_st[l], gdn_st[l], ret_st[l], w_in[l],
            conv_w[l], a_log[l], dt_bias[l], gdn_norm_w[l], ret_norm_w[l], w_out[l])
        h = h + _rmsnorm(m, norm_mix_post[l])
        f = _swiglu(_rmsnorm(h, norm_ffn_pre[l]), w_gate_up[l], w_down[l])
        h = h + _rmsnorm(f, norm_ffn_post[l])
        convs.append(c_new.astype(conv_st.dtype))
        gdns.append(sg_new.astype(gdn_st.dtype))
        rets.append(sr_new.astype(ret_st.dtype))
    return h, jnp.stack(convs), jnp.stack(gdns), jnp.stack(rets)


def setup_inputs(seed: int = 0) -> dict:
    key = jax.random.key(seed)
    ks = jax.random.split(key, 20)
    f32 = jnp.float32
    nrm = lambda k, s, sc: jax.random.normal(k, s, f32) * sc
    return {
        "x_prompt": nrm(ks[0], (BATCH, SEQ, D_MODEL), 1.0),
        "x_sample": nrm(ks[1], (DEC_BATCH, DEC_SEQ, D_MODEL), 1.0),
        "state_conv": nrm(ks[2], (DEPTH, DEC_BATCH, CONV_W - 1, 3 * A_WIDTH), 1.0),
        "state_gdn": nrm(ks[3], (DEPTH, DEC_BATCH, A_HEADS, HEAD_DIM, HEAD_DIM), 0.5),
        "state_ret": nrm(ks[4], (DEPTH, DEC_BATCH, B_HEADS, HEAD_DIM, HEAD_DIM), 1.0),
        "w_in": nrm(ks[5], (DEPTH, D_MODEL, IN_WIDTH), D_MODEL ** -0.5),
        "conv_w": nrm(ks[6], (DEPTH, CONV_W, 3 * A_WIDTH), CONV_W ** -0.5),
        "a_log": jnp.log(jax.random.uniform(ks[7], (DEPTH, A_HEADS), f32, 1.0, 16.0)),
        "dt_bias": nrm(ks[8], (DEPTH, A_HEADS), 0.1),
        "gdn_norm_w": 1.0 + nrm(ks[9], (DEPTH, HEAD_DIM), 0.02),
        "ret_norm_w": 1.0 + nrm(ks[10], (DEPTH, HEAD_DIM), 0.02),
        "w_out": nrm(ks[11], (DEPTH, MIX_WIDTH, D_MODEL), MIX_WIDTH ** -0.5),
        "norm_mix_pre": 1.0 + nrm(ks[12], (DEPTH, D_MODEL), 0.02),
        "norm_mix_post": 1.0 + nrm(ks[13], (DEPTH, D_MODEL), 0.02),
        "norm_ffn_pre": 1.0 + nrm(ks[14], (DEPTH, D_MODEL), 0.02),
        "norm_ffn_post": 1.0 + nrm(ks[15], (DEPTH, D_MODEL), 0.02),
        "w_gate_up": nrm(ks[16], (DEPTH, D_MODEL, 2 * D_FF), D_MODEL ** -0.5),
        "w_down": nrm(ks[17], (DEPTH, D_FF, D_MODEL), D_FF ** -0.5),
    }


def reference(x_prompt, x_sample, state_conv, state_gdn, state_ret, w_in, conv_w, a_log,
              dt_bias, gdn_norm_w, ret_norm_w, w_out, norm_mix_pre, norm_mix_post,
              norm_ffn_pre, norm_ffn_post, w_gate_up, w_down):
    weights = (w_in, conv_w, a_log, dt_bias, gdn_norm_w, ret_norm_w, w_out, norm_mix_pre,
               norm_mix_post, norm_ffn_pre, norm_ffn_post, w_gate_up, w_down)
    bp, lp = x_prompt.shape[0], x_prompt.shape[1]
    pos_p = jnp.arange(lp, dtype=jnp.float32)
    conv0 = jnp.zeros((DEPTH, bp, CONV_W - 1, 3 * A_WIDTH), x_prompt.dtype)
    gdn0 = jnp.zeros((DEPTH, bp, A_HEADS, HEAD_DIM, HEAD_DIM), x_prompt.dtype)
    ret0 = jnp.zeros((DEPTH, bp, B_HEADS, HEAD_DIM, HEAD_DIM), x_prompt.dtype)
    y_prompt, conv_p, gdn_p, ret_p = _trunk(x_prompt, pos_p, conv0, gdn0, ret0, *weights)
    ls = x_sample.shape[1]
    pos_s = PAST_LEN + jnp.arange(ls, dtype=jnp.float32)
    y_sample, conv_s, gdn_s, ret_s = _trunk(x_sample, pos_s, state_conv, state_gdn, state_ret,
                                            *weights)
    return (y_prompt, y_sample, conv_p, gdn_p, ret_p, conv_s, gdn_s, ret_s)
```

```python
import functools
import math

import jax
import jax.numpy as jnp
from jax import lax
from jax.experimental import pallas as pl
from jax.experimental.pallas import tpu as pltpu

HEAD_DIM = 128
A_HEADS = 4
B_HEADS = 4
A_WIDTH = A_HEADS * HEAD_DIM
B_WIDTH = B_HEADS * HEAD_DIM
QKV_WIDTH = 3 * A_WIDTH
CONV_W = 4
CHUNK = 64
ROPE_BASE = 10000.0
EPS = 1e-6
L2_EPS = 1e-6
PAST_LEN = 16384
LANES = 128
SUBLANES = 8

REST_WIDTH = A_WIDTH + 4 * B_WIDTH + LANES
Z_OFF = 0
BQ_OFF = A_WIDTH
BK_OFF = BQ_OFF + B_WIDTH
BV_OFF = BK_OFF + B_WIDTH
BG_OFF = BV_OFF + B_WIDTH
SMALL_OFF = BG_OFF + B_WIDTH
IN_WIDTH_PADDED = QKV_WIDTH + REST_WIDTH

PROMPT_TIME_BLOCK = 256
FFN_ROW_BLOCK = 512
FFN_COL_BLOCK = 256
SAMPLE_BATCH_BLOCK = 8
VMEM_LIMIT_BYTES = 56 * 1024 * 1024

_BF16 = jnp.bfloat16
_F32 = jnp.float32


def _dot(a, b):
    return jnp.dot(a.astype(_BF16), b.astype(_BF16), preferred_element_type=_F32)


def _dot_nt(a, b):
    return lax.dot_general(a.astype(_BF16), b.astype(_BF16), (((1,), (1,)), ((), ())),
                           preferred_element_type=_F32)


def _dot_tn(a, b):
    return lax.dot_general(a.astype(_BF16), b.astype(_BF16), (((0,), (0,)), ((), ())),
                           preferred_element_type=_F32)


def _rmsnorm(x, w):
    return x * lax.rsqrt(jnp.mean(x * x, axis=-1, keepdims=True) + EPS) * w


def _silu(x):
    return x * jax.nn.sigmoid(x)


def _softplus(x):
    return jnp.maximum(x, 0.0) + jnp.log1p(jnp.exp(-jnp.abs(x)))


def _log_gamma(h):
    return math.log1p(-(2.0 ** (-5.0 - h)))


def _rope(x, cos_full, sin_signed):
    return x * cos_full + pltpu.roll(x, HEAD_DIM // 2, axis=1) * sin_signed


def _head_cols(base, h):
    return slice(base + h * HEAD_DIM, base + (h + 1) * HEAD_DIM)


def _gated_head_norm(o, w, gate):
    return o * lax.rsqrt(jnp.mean(o * o, axis=-1, keepdims=True) + EPS) * w * _silu(gate)


def _unit_lower_inverse(a_strict, eye):
    p = eye - a_strict
    x = a_strict
    n = 2
    while n < CHUNK:
        x = _dot(x, x)
        p = p + _dot(p, x)
        n *= 2
    return p


def _mixer_prompt_body(x_ref, cos_ref, sin_ref, win_ref, convw_ref, alog_ref, dtb_ref, gnw_ref,
                       rnw_ref, wout_ref, npre_ref, npost_ref,
                       h_ref, conv_ref, sg_ref, sr_ref,
                       pq_ref, pr_ref, act_ref, small_ref, gct_ref, o_ref, *, tb):
    t = pl.program_id(1)
    nt = pl.num_programs(1)
    pad = SUBLANES

    @pl.when(t == 0)
    def _():
        pq_ref[0:pad, :] = jnp.zeros((pad, QKV_WIDTH), _F32)
        sg_ref[...] = jnp.zeros(sg_ref.shape, _F32)
        sr_ref[...] = jnp.zeros(sr_ref.shape, _F32)

    x = x_ref[...]
    hn = _rmsnorm(x, npre_ref[...]).astype(_BF16)
    pq_ref[pad:pad + tb, :] = jnp.dot(hn, win_ref[:, 0:QKV_WIDTH], preferred_element_type=_F32)
    pr_ref[...] = jnp.dot(hn, win_ref[:, QKV_WIDTH:IN_WIDTH_PADDED], preferred_element_type=_F32)

    for j in range(QKV_WIDTH // LANES):
        cs = slice(j * LANES, (j + 1) * LANES)
        acc = pq_ref[pad:pad + tb, cs] * convw_ref[3:4, cs]
        for i in range(CONV_W - 1):
            acc = acc + pq_ref[pad - 3 + i:pad - 3 + i + tb, cs] * convw_ref[i:i + 1, cs]
        a = _silu(acc)
        if j < 2 * A_HEADS:
            a = a * lax.rsqrt(jnp.sum(a * a, axis=-1, keepdims=True) + L2_EPS)
        if j < A_HEADS:
            a = a * (HEAD_DIM ** -0.5)
        act_ref[:, cs] = a

    tail = pq_ref[pad + tb - 3:pad + tb, :]

    @pl.when(t == nt - 1)
    def _():
        conv_ref[...] = tail

    pq_ref[pad - 3:pad, :] = tail

    ps = pr_ref[:, SMALL_OFF:SMALL_OFF + LANES]
    beta_all = jax.nn.sigmoid(ps)
    gc = -jnp.exp(alog_ref[...]) * _softplus(ps + dtb_ref[...])
    row_in_chunk = lax.broadcasted_iota(jnp.int32, (tb, LANES), 0) % CHUNK
    s = 1
    while s < CHUNK:
        gc = gc + jnp.where(row_in_chunk >= s, pltpu.roll(gc, s, axis=0), 0.0)
        s *= 2
    small_ref[0] = beta_all
    small_ref[1] = gc
    gct_ref[...] = gc.T

    cos_full = cos_ref[...]
    sin_signed = sin_ref[...]
    for h in range(B_HEADS):
        cq = _head_cols(BQ_OFF, h)
        ck = _head_cols(BK_OFF, h)
        pr_ref[:, cq] = _rope(pr_ref[:, cq], cos_full, sin_signed)
        pr_ref[:, ck] = _rope(pr_ref[:, ck], cos_full, sin_signed) * (HEAD_DIM ** -0.5)

    ri = lax.broadcasted_iota(jnp.int32, (CHUNK, CHUNK), 0)
    ci = lax.broadcasted_iota(jnp.int32, (CHUNK, CHUNK), 1)
    tril = ri >= ci
    strict = ri > ci
    eye = jnp.where(ri == ci, 1.0, 0.0).astype(_F32)
    pos_in_chunk = lax.broadcasted_iota(jnp.int32, (CHUNK, HEAD_DIM), 0).astype(_F32)

    ret_consts = []
    for h in range(B_HEADS):
        lg = _log_gamma(h)
        decay = jnp.where(tril, jnp.exp(jnp.where(tril, (ri - ci).astype(_F32) * lg, 0.0)), 0.0)
        e_in = jnp.exp((pos_in_chunk + 1.0) * lg)
        e_out = jnp.exp((CHUNK - 1.0 - pos_in_chunk) * lg)
        ret_consts.append((decay, e_in, e_out, math.exp(CHUNK * lg)))

    for c in range(tb // CHUNK):
        rows = slice(c * CHUNK, (c + 1) * CHUNK)
        for h in range(A_HEADS):
            q = act_ref[rows, _head_cols(0, h)]
            k = act_ref[rows, _head_cols(A_WIDTH, h)]
            v = act_ref[rows, _head_cols(2 * A_WIDTH, h)]
            beta = jnp.broadcast_to(small_ref[0, rows, h:h + 1], (CHUNK, HEAD_DIM))
            gcc = jnp.broadcast_to(small_ref[1, rows, A_HEADS + h:A_HEADS + h + 1], (CHUNK, HEAD_DIM))
            gcr = jnp.broadcast_to(gct_ref[A_HEADS + h:A_HEADS + h + 1, rows], (CHUNK, CHUNK))
            decay = jnp.where(tril, jnp.exp(jnp.where(tril, gcc[:, 0:CHUNK] - gcr, 0.0)), 0.0)
            kb = k * beta
            vb = v * beta
            a_strict = jnp.where(strict, _dot_nt(kb, k) * decay, 0.0)
            t_inv = _unit_lower_inverse(a_strict, eye)
            e_in = jnp.exp(gcc)
            u = _dot(t_inv, vb)
            w = _dot(t_inv, kb * e_in)
            a_intra = jnp.where(tril, _dot_nt(q, k) * decay, 0.0)
            state = sg_ref[h]
            v_new = u - _dot(w, state)
            o = _dot(q * e_in, state) + _dot(a_intra, v_new)
            g_last = gcc[CHUNK - 1:CHUNK, :]
            k_out = k * jnp.exp(g_last - gcc)
            sg_ref[h] = state * jnp.exp(g_last) + _dot_tn(k_out, v_new)
            o_ref[rows, _head_cols(0, h)] = o
        for h in range(B_HEADS):
            decay, e_in, e_out, e_chunk = ret_consts[h]
            q = pr_ref[rows, _head_cols(BQ_OFF, h)]
            k = pr_ref[rows, _head_cols(BK_OFF, h)]
            v = pr_ref[rows, _head_cols(BV_OFF, h)]
            a_intra = _dot_nt(q, k) * decay
            state = sr_ref[h]
            o = _dot(q * e_in, state) + _dot(a_intra, v)
            sr_ref[h] = state * e_chunk + _dot_tn(k * e_out, v)
            o_ref[rows, _head_cols(A_WIDTH, h)] = o

    for h in range(A_HEADS):
        cs = _head_cols(0, h)
        o_ref[:, cs] = _gated_head_norm(o_ref[:, cs], gnw_ref[...], pr_ref[:, _head_cols(Z_OFF, h)])
    for h in range(B_HEADS):
        cs = _head_cols(A_WIDTH, h)
        o_ref[:, cs] = _gated_head_norm(o_ref[:, cs], rnw_ref[...], pr_ref[:, _head_cols(BG_OFF, h)])
    m = jnp.dot(o_ref[...].astype(_BF16), wout_ref[...], preferred_element_type=_F32)
    h_ref[...] = x_ref[...] + _rmsnorm(m, npost_ref[...])


def _layer_spec(shape, layer, ngrid):
    zeros = (0,) * len(shape)
    if ngrid == 1:
        return pl.BlockSpec((None,) + tuple(shape), lambda i: (layer,) + zeros)
    return pl.BlockSpec((None,) + tuple(shape), lambda i, j: (layer,) + zeros)


def _mixer_prompt(x, cos_full, sin_signed, win, conv_w, alog, dtb, gnw, rnw, wout, npre, npost, layer):
    batch, seq, d_model = x.shape
    tb = min(PROMPT_TIME_BLOCK, seq)
    assert seq % tb == 0 and tb % CHUNK == 0
    grid = (batch, seq // tb)
    lspec = functools.partial(_layer_spec, layer=layer, ngrid=2)
    in_specs = [
        pl.BlockSpec((None, tb, d_model), lambda b, t: (b, t, 0)),
        pl.BlockSpec((tb, HEAD_DIM), lambda b, t: (t, 0)),
        pl.BlockSpec((tb, HEAD_DIM), lambda b, t: (t, 0)),
        lspec((d_model, IN_WIDTH_PADDED)),
        lspec((CONV_W, QKV_WIDTH)),
        lspec((1, LANES)),
        lspec((1, LANES)),
        lspec((1, HEAD_DIM)),
        lspec((1, HEAD_DIM)),
        lspec((A_WIDTH + B_WIDTH, d_model)),
        lspec((1, d_model)),
        lspec((1, d_model)),
    ]
    out_specs = [
        pl.BlockSpec((None, tb, d_model), lambda b, t: (b, t, 0)),
        pl.BlockSpec((None, CONV_W - 1, QKV_WIDTH), lambda b, t: (b, 0, 0)),
        pl.BlockSpec((None, A_HEADS, HEAD_DIM, HEAD_DIM), lambda b, t: (b, 0, 0, 0)),
        pl.BlockSpec((None, B_HEADS, HEAD_DIM, HEAD_DIM), lambda b, t: (b, 0, 0, 0)),
    ]
    out_shape = [
        jax.ShapeDtypeStruct((batch, seq, d_model), _F32),
        jax.ShapeDtypeStruct((batch, CONV_W - 1, QKV_WIDTH), _F32),
        jax.ShapeDtypeStruct((batch, A_HEADS, HEAD_DIM, HEAD_DIM), _F32),
        jax.ShapeDtypeStruct((batch, B_HEADS, HEAD_DIM, HEAD_DIM), _F32),
    ]
    scratch = [
        pltpu.VMEM((tb + SUBLANES, QKV_WIDTH), _F32),
        pltpu.VMEM((tb, REST_WIDTH), _F32),
        pltpu.VMEM((tb, QKV_WIDTH), _F32),
        pltpu.VMEM((2, tb, LANES), _F32),
        pltpu.VMEM((LANES, tb), _F32),
        pltpu.VMEM((tb, A_WIDTH + B_WIDTH), _F32),
    ]
    return pl.pallas_call(
        functools.partial(_mixer_prompt_body, tb=tb),
        grid=grid, in_specs=in_specs, out_specs=out_specs, out_shape=out_shape,
        scratch_shapes=scratch,
        compiler_params=pltpu.CompilerParams(
            dimension_semantics=("arbitrary", "arbitrary"), vmem_limit_bytes=VMEM_LIMIT_BYTES),
        name=f"mixer_prompt_l{layer}",
    )(x, cos_full, sin_signed, win, conv_w, alog, dtb, gnw, rnw, wout, npre, npost)


def _column_of_row(row):
    return jnp.broadcast_to(row, (HEAD_DIM, HEAD_DIM)).T


def _mixer_sample_body(*refs, bb, aliased):
    (x_ref, cos_ref, sin_ref, win_ref, convw_ref, alog_ref, dtb_ref, gnw_ref, rnw_ref, wout_ref,
     npre_ref, npost_ref, convs_ref, sgi_ref, sri_ref) = refs[:15]
    refs = refs[15 + (2 if aliased else 0):]
    (h_ref, convn_ref, sgo_ref, sro_ref,
     act_ref, pr_ref, eg_ref, bt_ref, qka_ref, qkb_ref, o_ref) = refs
    i = pl.program_id(0)
    n = pl.num_programs(0)
    nb = x_ref.shape[0]

    @pl.when(i == 0)
    def _():
        x = x_ref[...]
        hn = _rmsnorm(x, npre_ref[...]).astype(_BF16)
        pq = jnp.dot(hn, win_ref[:, 0:QKV_WIDTH], preferred_element_type=_F32)
        pr_ref[...] = jnp.dot(hn, win_ref[:, QKV_WIDTH:IN_WIDTH_PADDED], preferred_element_type=_F32)
        for j in range(QKV_WIDTH // LANES):
            cs = slice(j * LANES, (j + 1) * LANES)
            new = pq[:, cs]
            acc = new * convw_ref[3:4, cs]
            for r in range(CONV_W - 1):
                hist = convs_ref[:, r * QKV_WIDTH + j * LANES:r * QKV_WIDTH + (j + 1) * LANES]
                acc = acc + hist * convw_ref[r:r + 1, cs]
                if r > 0:
                    convn_ref[:, (r - 1) * QKV_WIDTH + j * LANES:(r - 1) * QKV_WIDTH + (j + 1) * LANES] = hist
            convn_ref[:, (CONV_W - 2) * QKV_WIDTH + j * LANES:(CONV_W - 2) * QKV_WIDTH + (j + 1) * LANES] = new
            a = _silu(acc)
            if j < 2 * A_HEADS:
                a = a * lax.rsqrt(jnp.sum(a * a, axis=-1, keepdims=True) + L2_EPS)
            if j < A_HEADS:
                a = a * (HEAD_DIM ** -0.5)
            act_ref[:, cs] = a
        ps = pr_ref[:, SMALL_OFF:SMALL_OFF + LANES]
        beta_all = jax.nn.sigmoid(ps)
        eg_all = jnp.exp(-jnp.exp(alog_ref[...]) * _softplus(ps + dtb_ref[...]))
        cos_full = cos_ref[...]
        sin_signed = sin_ref[...]
        for h in range(A_HEADS):
            bt_ref[h] = jnp.broadcast_to(beta_all[:, h:h + 1], (nb, LANES))
            eg_ref[h] = jnp.broadcast_to(eg_all[:, A_HEADS + h:A_HEADS + h + 1], (nb, LANES))
            qk = jnp.sum(act_ref[:, _head_cols(0, h)] * act_ref[:, _head_cols(A_WIDTH, h)],
                         axis=-1, keepdims=True)
            qka_ref[h] = jnp.broadcast_to(qk, (nb, LANES))
        for h in range(B_HEADS):
            cq = _head_cols(BQ_OFF, h)
            ck = _head_cols(BK_OFF, h)
            q = _rope(pr_ref[:, cq], cos_full, sin_signed)
            k = _rope(pr_ref[:, ck], cos_full, sin_signed) * (HEAD_DIM ** -0.5)
            pr_ref[:, cq] = q
            pr_ref[:, ck] = k
            qkb_ref[h] = jnp.broadcast_to(jnp.sum(q * k, axis=-1, keepdims=True), (nb, LANES))

    base = pl.multiple_of(i * bb, SUBLANES)
    rows = pl.ds(base, bb)
    for h in range(A_HEADS):
        q8 = act_ref[rows, _head_cols(0, h)]
        k8 = act_ref[rows, _head_cols(A_WIDTH, h)]
        v8 = act_ref[rows, _head_cols(2 * A_WIDTH, h)]
        eg8 = eg_ref[h, rows, :]
        bt8 = bt_ref[h, rows, :]
        qk8 = qka_ref[h, rows, :]
        outs = []
        for bl in range(bb):
            r = slice(bl, bl + 1)
            state = sgi_ref[bl, h]
            qc = _column_of_row(q8[r])
            kc = _column_of_row(k8[r])
            eg = eg8[r]
            ks = jnp.sum(kc * state, axis=0, keepdims=True)
            qs = jnp.sum(qc * state, axis=0, keepdims=True)
            v_new = bt8[r] * (v8[r] - eg * ks)
            outs.append(eg * qs + qk8[r] * v_new)
            sgo_ref[bl, h] = state * eg + kc * v_new
        o_ref[rows, _head_cols(0, h)] = jnp.concatenate(outs, axis=0)
    for h in range(B_HEADS):
        gamma = math.exp(_log_gamma(h))
        q8 = pr_ref[rows, _head_cols(BQ_OFF, h)]
        k8 = pr_ref[rows, _head_cols(BK_OFF, h)]
        v8 = pr_ref[rows, _head_cols(BV_OFF, h)]
        qk8 = qkb_ref[h, rows, :]
        outs = []
        for bl in range(bb):
            r = slice(bl, bl + 1)
            state = sri_ref[bl, h]
            qc = _column_of_row(q8[r])
            kc = _column_of_row(k8[r])
            qs = jnp.sum(qc * state, axis=0, keepdims=True)
            outs.append(gamma * qs + qk8[r] * v8[r])
            sro_ref[bl, h] = state * gamma + kc * v8[r]
        o_ref[rows, _head_cols(A_WIDTH, h)] = jnp.concatenate(outs, axis=0)

    @pl.when(i == n - 1)
    def _():
        for h in range(A_HEADS):
            cs = _head_cols(0, h)
            o_ref[:, cs] = _gated_head_norm(o_ref[:, cs], gnw_ref[...], pr_ref[:, _head_cols(Z_OFF, h)])
        for h in range(B_HEADS):
            cs = _head_cols(A_WIDTH, h)
            o_ref[:, cs] = _gated_head_norm(o_ref[:, cs], rnw_ref[...], pr_ref[:, _head_cols(BG_OFF, h)])
        m = jnp.dot(o_ref[...].astype(_BF16), wout_ref[...], preferred_element_type=_F32)
        h_ref[...] = x_ref[...] + _rmsnorm(m, npost_ref[...])


def _mixer_sample(x, cos_full, sin_signed, win, conv_w, alog, dtb, gnw, rnw, wout, npre, npost,
                  conv_state, state_gdn, state_ret, prev_gdn, prev_ret, layer):
    nb, d_model = x.shape
    depth = state_gdn.shape[0]
    bb = min(SAMPLE_BATCH_BLOCK, nb)
    assert nb % bb == 0
    aliased = prev_gdn is not None
    lspec = functools.partial(_layer_spec, layer=layer, ngrid=1)
    full = lambda shape: pl.BlockSpec(tuple(shape), lambda i: (0,) * len(shape))
    state_spec = lambda heads: pl.BlockSpec((None, bb, heads, HEAD_DIM, HEAD_DIM),
                                            lambda i: (layer, i, 0, 0, 0))
    in_specs = [
        full((nb, d_model)),
        full((1, HEAD_DIM)),
        full((1, HEAD_DIM)),
        lspec((d_model, IN_WIDTH_PADDED)),
        lspec((CONV_W, QKV_WIDTH)),
        lspec((1, LANES)),
        lspec((1, LANES)),
        lspec((1, HEAD_DIM)),
        lspec((1, HEAD_DIM)),
        lspec((A_WIDTH + B_WIDTH, d_model)),
        lspec((1, d_model)),
        lspec((1, d_model)),
        lspec((nb, (CONV_W - 1) * QKV_WIDTH)),
        state_spec(A_HEADS),
        state_spec(B_HEADS),
    ]
    args = [x, cos_full, sin_signed, win, conv_w, alog, dtb, gnw, rnw, wout, npre, npost,
            conv_state, state_gdn, state_ret]
    aliases = {}
    if aliased:
        in_specs += [pl.BlockSpec(memory_space=pl.ANY), pl.BlockSpec(memory_space=pl.ANY)]
        aliases = {len(args): 2, len(args) + 1: 3}
        args += [prev_gdn, prev_ret]
    out_specs = [
        full((nb, d_model)),
        full((nb, (CONV_W - 1) * QKV_WIDTH)),
        state_spec(A_HEADS),
        state_spec(B_HEADS),
    ]
    out_shape = [
        jax.ShapeDtypeStruct((nb, d_model), _F32),
        jax.ShapeDtypeStruct((nb, (CONV_W - 1) * QKV_WIDTH), _F32),
        jax.ShapeDtypeStruct((depth, nb, A_HEADS, HEAD_DIM, HEAD_DIM), _F32),
        jax.ShapeDtypeStruct((depth, nb, B_HEADS, HEAD_DIM, HEAD_DIM), _F32),
    ]
    scratch = [
        pltpu.VMEM((nb, QKV_WIDTH), _F32),
        pltpu.VMEM((nb, REST_WIDTH), _F32),
        pltpu.VMEM((A_HEADS, nb, LANES), _F32),
        pltpu.VMEM((A_HEADS, nb, LANES), _F32),
        pltpu.VMEM((A_HEADS, nb, LANES), _F32),
        pltpu.VMEM((B_HEADS, nb, LANES), _F32),
        pltpu.VMEM((nb, A_WIDTH + B_WIDTH), _F32),
    ]
    return pl.pallas_call(
        functools.partial(_mixer_sample_body, bb=bb, aliased=aliased),
        grid=(nb // bb,), in_specs=in_specs, out_specs=out_specs, out_shape=out_shape,
        scratch_shapes=scratch, input_output_aliases=aliases,
        compiler_params=pltpu.CompilerParams(
            dimension_semantics=("arbitrary",), vmem_limit_bytes=VMEM_LIMIT_BYTES),
        name=f"mixer_sample_l{layer}",
    )(*args)


def _ffn_body(h_ref, wgu_ref, wd_ref, npre_ref, npost_ref, y_ref, act_ref, *, d_ff):
    hn = _rmsnorm(h_ref[...], npre_ref[...]).astype(_BF16)
    for j in range(d_ff // FFN_COL_BLOCK):
        cg = slice(j * FFN_COL_BLOCK, (j + 1) * FFN_COL_BLOCK)
        cu = slice(d_ff + j * FFN_COL_BLOCK, d_ff + (j + 1) * FFN_COL_BLOCK)
        gate = jnp.dot(hn, wgu_ref[:, cg], preferred_element_type=_F32)
        up = jnp.dot(hn, wgu_ref[:, cu], preferred_element_type=_F32)
        act_ref[:, cg] = (_silu(gate) * up).astype(_BF16)
    f = jnp.dot(act_ref[...], wd_ref[...], preferred_element_type=_F32)
    y_ref[...] = h_ref[...] + _rmsnorm(f, npost_ref[...])


def _ffn(h, wgu, wd, npre, npost, layer, tag):
    rows, d_model = h.shape
    d_ff = wd.shape[1]
    assert d_ff % FFN_COL_BLOCK == 0
    rb = min(FFN_ROW_BLOCK, rows)
    assert rows % rb == 0
    lspec = functools.partial(_layer_spec, layer=layer, ngrid=1)
    return pl.pallas_call(
        functools.partial(_ffn_body, d_ff=d_ff),
        grid=(rows // rb,),
        in_specs=[
            pl.BlockSpec((rb, d_model), lambda i: (i, 0)),
            lspec((d_model, 2 * d_ff)),
            lspec((d_ff, d_model)),
            lspec((1, d_model)),
            lspec((1, d_model)),
        ],
        out_specs=pl.BlockSpec((rb, d_model), lambda i: (i, 0)),
        out_shape=jax.ShapeDtypeStruct((rows, d_model), _F32),
        scratch_shapes=[pltpu.VMEM((rb, d_ff), _BF16)],
        compiler_params=pltpu.CompilerParams(
            dimension_semantics=("arbitrary",), vmem_limit_bytes=VMEM_LIMIT_BYTES),
        name=f"ffn_{tag}_l{layer}",
    )(h, wgu, wd, npre, npost)


def _rope_tables(pos):
    half = HEAD_DIM // 2
    inv = ROPE_BASE ** (-jnp.arange(half, dtype=_F32) / half)
    ang = pos[:, None] * inv[None, :]
    cos, sin = jnp.cos(ang), jnp.sin(ang)
    return jnp.concatenate([cos, cos], axis=-1), jnp.concatenate([-sin, sin], axis=-1)


def _rearranged_w_in(w_in):
    small0 = QKV_WIDTH + A_WIDTH
    small1 = small0 + 2 * A_HEADS
    small = jnp.pad(w_in[:, :, small0:small1], ((0, 0), (0, 0), (0, LANES - 2 * A_HEADS)))
    return jnp.concatenate([w_in[:, :, :small0], w_in[:, :, small1:], small], axis=-1).astype(_BF16)


def kernel(x_prompt, x_sample, state_conv, state_gdn, state_ret, w_in, conv_w, a_log, dt_bias, gdn_norm_w, ret_norm_w, w_out, norm_mix_pre, norm_mix_post, norm_ffn_pre, norm_ffn_post, w_gate_up, w_down):
    depth = w_in.shape[0]
    batch, seq, d_model = x_prompt.shape
    nb, seq_s, _ = x_sample.shape
    assert seq_s == 1

    win = _rearranged_w_in(w_in)
    wout = w_out.astype(_BF16)
    wgu = w_gate_up.astype(_BF16)
    wd = w_down.astype(_BF16)
    alog = jnp.pad(a_log, ((0, 0), (A_HEADS, LANES - 2 * A_HEADS)))[:, None, :]
    dtb = jnp.pad(dt_bias, ((0, 0), (A_HEADS, LANES - 2 * A_HEADS)))[:, None, :]
    gnw = gdn_norm_w[:, None, :]
    rnw = ret_norm_w[:, None, :]
    npre = norm_mix_pre[:, None, :]
    npost = norm_mix_post[:, None, :]
    fpre = norm_ffn_pre[:, None, :]
    fpost = norm_ffn_post[:, None, :]
    cos_p, sin_p = _rope_tables(jnp.arange(seq, dtype=_F32))
    cos_s, sin_s = _rope_tables(PAST_LEN + jnp.arange(seq_s, dtype=_F32))
    conv_state = state_conv.reshape(depth, nb, (CONV_W - 1) * QKV_WIDTH)

    hp = x_prompt
    hs = x_sample.reshape(nb, d_model)
    convs_p, gdns_p, rets_p, convs_s = [], [], [], []
    gdn_s = ret_s = None
    for l in range(depth):
        hp, conv_p, gdn_p, ret_p = _mixer_prompt(hp, cos_p, sin_p, win, conv_w, alog, dtb, gnw, rnw,
                                                 wout, npre, npost, l)
        hp = _ffn(hp.reshape(batch * seq, d_model), wgu, wd, fpre, fpost, l, "prompt").reshape(
            batch, seq, d_model)
        convs_p.append(conv_p)
        gdns_p.append(gdn_p)
        rets_p.append(ret_p)
        hs, conv_s, gdn_s, ret_s = _mixer_sample(hs, cos_s, sin_s, win, conv_w, alog, dtb, gnw, rnw,
                                                 wout, npre, npost, conv_state, state_gdn, state_ret,
                                                 gdn_s, ret_s, l)
        hs = _ffn(hs, wgu, wd, fpre, fpost, l, "sample")
        convs_s.append(conv_s.reshape(nb, CONV_W - 1, QKV_WIDTH))
    return (hp, hs.reshape(nb, seq_s, d_model), jnp.stack(convs_p), jnp.stack(gdns_p),
            jnp.stack(rets_p), jnp.stack(convs_s), gdn_s, ret_s)
```

```python
import functools
import math

import jax
import jax.numpy as jnp
from jax import lax
from jax.experimental import pallas as pl
from jax.experimental.pallas import tpu as pltpu

HEAD_DIM = 128
A_HEADS = 4
B_HEADS = 4
A_WIDTH = A_HEADS * HEAD_DIM
B_WIDTH = B_HEADS * HEAD_DIM
QKV_WIDTH = 3 * A_WIDTH
CONV_W = 4
CHUNK = 64
ROPE_BASE = 10000.0
EPS = 1e-6
L2_EPS = 1e-6
PAST_LEN = 16384
LANES = 128
SUBLANES = 8

REST_WIDTH = A_WIDTH + 4 * B_WIDTH + LANES
Z_OFF = 0
BQ_OFF = A_WIDTH
BK_OFF = BQ_OFF + B_WIDTH
BV_OFF = BK_OFF + B_WIDTH
BG_OFF = BV_OFF + B_WIDTH
SMALL_OFF = BG_OFF + B_WIDTH
IN_WIDTH_PADDED = QKV_WIDTH + REST_WIDTH

GROUP = 256
CHUNKS_PER_GROUP = GROUP // CHUNK
PROMPT_TIME_BLOCK = 256
FFN_ROW_BLOCK = 512
FFN_COL_BLOCK = 256
SAMPLE_BATCH_BLOCK = 8
VMEM_LIMIT_BYTES = 56 * 1024 * 1024

_BF16 = jnp.bfloat16
_F32 = jnp.float32


def _dot(a, b):
    return jnp.dot(a.astype(_BF16), b.astype(_BF16), preferred_element_type=_F32)


def _dot_nt(a, b):
    return lax.dot_general(a.astype(_BF16), b.astype(_BF16), (((1,), (1,)), ((), ())),
                           preferred_element_type=_F32)


def _dot_tn(a, b):
    return lax.dot_general(a.astype(_BF16), b.astype(_BF16), (((0,), (0,)), ((), ())),
                           preferred_element_type=_F32)


def _rmsnorm(x, w):
    return x * lax.rsqrt(jnp.mean(x * x, axis=-1, keepdims=True) + EPS) * w


def _silu(x):
    return x * jax.nn.sigmoid(x)


def _softplus(x):
    return jnp.maximum(x, 0.0) + jnp.log1p(jnp.exp(-jnp.abs(x)))


def _log_gamma(h):
    return math.log1p(-(2.0 ** (-5.0 - h)))


def _rope(x, cos_full, sin_signed):
    return x * cos_full + pltpu.roll(x, HEAD_DIM // 2, axis=1) * sin_signed


def _head_cols(base, h):
    return slice(base + h * HEAD_DIM, base + (h + 1) * HEAD_DIM)


def _gated_head_norm(o, w, gate):
    return o * lax.rsqrt(jnp.mean(o * o, axis=-1, keepdims=True) + EPS) * w * _silu(gate)


def _fold_rows(m):
    out = m[0:CHUNK]
    for c in range(1, CHUNKS_PER_GROUP):
        out = out + m[c * CHUNK:(c + 1) * CHUNK]
    return out


def _unfold_rows(r, same_chunk):
    return jnp.where(same_chunk, jnp.concatenate([r] * CHUNKS_PER_GROUP, axis=0), 0.0)


def _unit_lower_inverses(a_folded, eye_folded, same_chunk):
    xs = list(a_folded)
    ps = [eye_folded - x for x in xs]
    n = 1
    while n < CHUNK:
        for h in range(len(xs)):
            x_bd = _unfold_rows(xs[h], same_chunk).astype(_BF16)
            if n == 1:
                xs[h] = _dot(xs[h], x_bd)
            elif 2 * n < CHUNK:
                r = _dot(jnp.concatenate([xs[h], ps[h]], axis=0), x_bd)
                xs[h] = r[0:CHUNK]
                ps[h] = ps[h] + r[CHUNK:2 * CHUNK]
            else:
                ps[h] = ps[h] + _dot(ps[h], x_bd)
        n *= 2
    return [_unfold_rows(p, same_chunk) for p in ps]


def _recurrences_one_group(g, act_ref, pr_ref, small_ref, gct_ref, sg_ref, sr_ref, o_ref):
    rows = slice(g * GROUP, (g + 1) * GROUP)
    ri = lax.broadcasted_iota(jnp.int32, (GROUP, GROUP), 0)
    ci = lax.broadcasted_iota(jnp.int32, (GROUP, GROUP), 1)
    same_chunk = (ri // CHUNK) == (ci // CHUNK)
    tril = same_chunk & (ri >= ci)
    off_diag = ri != ci
    rf = lax.broadcasted_iota(jnp.int32, (CHUNK, GROUP), 0)
    cf = lax.broadcasted_iota(jnp.int32, (CHUNK, GROUP), 1)
    eye_folded = jnp.where(rf == cf % CHUNK, 1.0, 0.0).astype(_F32)
    pos = (lax.broadcasted_iota(jnp.int32, (GROUP, HEAD_DIM), 0) % CHUNK).astype(_F32)
    chunk_rows = [slice(c * CHUNK, (c + 1) * CHUNK) for c in range(CHUNKS_PER_GROUP)]

    qs, ks, gccs, a_folded, a_intra, uw_rhs, e_ins = [], [], [], [], [], [], []
    for h in range(A_HEADS):
        q = act_ref[rows, _head_cols(0, h)]
        k = act_ref[rows, _head_cols(A_WIDTH, h)]
        v = act_ref[rows, _head_cols(2 * A_WIDTH, h)]
        beta = jnp.broadcast_to(small_ref[0, rows, h:h + 1], (GROUP, HEAD_DIM))
        gc_col = small_ref[1, rows, A_HEADS + h:A_HEADS + h + 1]
        gcc = jnp.broadcast_to(gc_col, (GROUP, HEAD_DIM))
        gcr = jnp.broadcast_to(gct_ref[A_HEADS + h:A_HEADS + h + 1, rows], (GROUP, GROUP))
        diff = jnp.broadcast_to(gc_col, (GROUP, GROUP)) - gcr
        decay = jnp.where(tril, jnp.exp(jnp.where(tril, diff, 0.0)), 0.0)
        kb = k * beta
        e_in = jnp.exp(gcc)
        kq = _dot_nt(jnp.concatenate([kb, q], axis=0), k)
        a_folded.append(_fold_rows(jnp.where(off_diag, kq[0:GROUP] * decay, 0.0)))
        a_intra.append(kq[GROUP:2 * GROUP] * decay)
        uw_rhs.append(jnp.concatenate([v * beta, kb * e_in], axis=1).astype(_BF16))
        qs.append(q)
        ks.append(k)
        gccs.append(gcc)
        e_ins.append(e_in)

    r_qe, r_oloc, r_b, r_echunk = [], [], [], []
    for h in range(B_HEADS):
        lg = _log_gamma(h)
        q = pr_ref[rows, _head_cols(BQ_OFF, h)]
        k = pr_ref[rows, _head_cols(BK_OFF, h)]
        v = pr_ref[rows, _head_cols(BV_OFF, h)]
        decay = jnp.where(tril, jnp.exp(jnp.where(tril, (ri - ci).astype(_F32) * lg, 0.0)), 0.0)
        vb16 = v.astype(_BF16)
        r_oloc.append(_dot(_dot_nt(q, k) * decay, vb16))
        r_qe.append((q * jnp.exp((pos + 1.0) * lg)).astype(_BF16))
        k_out = (k * jnp.exp((CHUNK - 1.0 - pos) * lg)).astype(_BF16)
        r_b.append([_dot_tn(k_out[cr], vb16[cr]) for cr in chunk_rows])
        r_echunk.append(math.exp(CHUNK * lg))

    t_inv = _unit_lower_inverses(a_folded, eye_folded, same_chunk)

    g_lhs, g_oloc, g_b, g_elast = [], [], [], []
    for h in range(A_HEADS):
        uw = _dot(t_inv[h], uw_rhs[h])
        uw16 = uw.astype(_BF16)
        aiuw = _dot(a_intra[h], uw16)
        g_oloc.append(aiuw[:, 0:HEAD_DIM])
        q_eff = qs[h] * e_ins[h] - aiuw[:, HEAD_DIM:2 * HEAD_DIM]
        lhs, bs, elast = [], [], []
        for cr in chunk_rows:
            gcc_c = gccs[h][cr]
            g_last = gcc_c[CHUNK - 1:CHUNK, :]
            k_out = ks[h][cr] * jnp.exp(g_last - gcc_c)
            bg = _dot_tn(k_out, uw16[cr])
            bs.append(bg[:, 0:HEAD_DIM])
            lhs.append(jnp.concatenate([q_eff[cr], bg[:, HEAD_DIM:2 * HEAD_DIM]], axis=0).astype(_BF16))
            elast.append(jnp.exp(g_last))
        g_lhs.append(lhs)
        g_b.append(bs)
        g_elast.append(elast)

    g_state = [sg_ref[h] for h in range(A_HEADS)]
    r_state = [sr_ref[h] for h in range(B_HEADS)]
    for c, cr in enumerate(chunk_rows):
        out_rows = slice(g * GROUP + c * CHUNK, g * GROUP + (c + 1) * CHUNK)
        for h in range(A_HEADS):
            r = _dot(g_lhs[h][c], g_state[h])
            o_ref[out_rows, _head_cols(0, h)] = r[0:CHUNK] + g_oloc[h][cr]
            g_state[h] = g_state[h] * g_elast[h][c] + g_b[h][c] - r[CHUNK:CHUNK + HEAD_DIM]
        for h in range(B_HEADS):
            o_ref[out_rows, _head_cols(A_WIDTH, h)] = _dot(r_qe[h][cr], r_state[h]) + r_oloc[h][cr]
            r_state[h] = r_state[h] * r_echunk[h] + r_b[h][c]
    for h in range(A_HEADS):
        sg_ref[h] = g_state[h]
    for h in range(B_HEADS):
        sr_ref[h] = r_state[h]


def _mixer_prompt_body(x_ref, cos_ref, sin_ref, win_ref, convw_ref, alog_ref, dtb_ref, gnw_ref,
                       rnw_ref, wout_ref, npre_ref, npost_ref,
                       h_ref, conv_ref, sg_ref, sr_ref,
                       pq_ref, pr_ref, act_ref, small_ref, gct_ref, o_ref, *, tb):
    t = pl.program_id(1)
    nt = pl.num_programs(1)
    pad = SUBLANES

    @pl.when(t == 0)
    def _():
        pq_ref[0:pad, :] = jnp.zeros((pad, QKV_WIDTH), _F32)
        sg_ref[...] = jnp.zeros(sg_ref.shape, _F32)
        sr_ref[...] = jnp.zeros(sr_ref.shape, _F32)

    x = x_ref[...]
    hn = _rmsnorm(x, npre_ref[...]).astype(_BF16)
    pq_ref[pad:pad + tb, :] = jnp.dot(hn, win_ref[:, 0:QKV_WIDTH], preferred_element_type=_F32)
    pr_ref[...] = jnp.dot(hn, win_ref[:, QKV_WIDTH:IN_WIDTH_PADDED], preferred_element_type=_F32)

    for j in range(QKV_WIDTH // LANES):
        cs = slice(j * LANES, (j + 1) * LANES)
        acc = pq_ref[pad:pad + tb, cs] * convw_ref[3:4, cs]
        for i in range(CONV_W - 1):
            acc = acc + pq_ref[pad - 3 + i:pad - 3 + i + tb, cs] * convw_ref[i:i + 1, cs]
        a = _silu(acc)
        if j < 2 * A_HEADS:
            a = a * lax.rsqrt(jnp.sum(a * a, axis=-1, keepdims=True) + L2_EPS)
        if j < A_HEADS:
            a = a * (HEAD_DIM ** -0.5)
        act_ref[:, cs] = a

    tail = pq_ref[pad + tb - 3:pad + tb, :]

    @pl.when(t == nt - 1)
    def _():
        conv_ref[...] = tail

    pq_ref[pad - 3:pad, :] = tail

    ps = pr_ref[:, SMALL_OFF:SMALL_OFF + LANES]
    beta_all = jax.nn.sigmoid(ps)
    gc = -jnp.exp(alog_ref[...]) * _softplus(ps + dtb_ref[...])
    row_in_chunk = lax.broadcasted_iota(jnp.int32, (tb, LANES), 0) % CHUNK
    s = 1
    while s < CHUNK:
        gc = gc + jnp.where(row_in_chunk >= s, pltpu.roll(gc, s, axis=0), 0.0)
        s *= 2
    small_ref[0] = beta_all
    small_ref[1] = gc
    gct_ref[...] = gc.T

    cos_full = cos_ref[...]
    sin_signed = sin_ref[...]
    for h in range(B_HEADS):
        cq = _head_cols(BQ_OFF, h)
        ck = _head_cols(BK_OFF, h)
        pr_ref[:, cq] = _rope(pr_ref[:, cq], cos_full, sin_signed)
        pr_ref[:, ck] = _rope(pr_ref[:, ck], cos_full, sin_signed) * (HEAD_DIM ** -0.5)

    for g in range(tb // GROUP):
        _recurrences_one_group(g, act_ref, pr_ref, small_ref, gct_ref, sg_ref, sr_ref, o_ref)

    for h in range(A_HEADS):
        cs = _head_cols(0, h)
        o_ref[:, cs] = _gated_head_norm(o_ref[:, cs], gnw_ref[...], pr_ref[:, _head_cols(Z_OFF, h)])
    for h in range(B_HEADS):
        cs = _head_cols(A_WIDTH, h)
        o_ref[:, cs] = _gated_head_norm(o_ref[:, cs], rnw_ref[...], pr_ref[:, _head_cols(BG_OFF, h)])
    m = jnp.dot(o_ref[...].astype(_BF16), wout_ref[...], preferred_element_type=_F32)
    h_ref[...] = x_ref[...] + _rmsnorm(m, npost_ref[...])


def _layer_spec(shape, layer, ngrid):
    zeros = (0,) * len(shape)
    if ngrid == 1:
        return pl.BlockSpec((None,) + tuple(shape), lambda i: (layer,) + zeros)
    return pl.BlockSpec((None,) + tuple(shape), lambda i, j: (layer,) + zeros)


def _mixer_prompt(x, cos_full, sin_signed, win, conv_w, alog, dtb, gnw, rnw, wout, npre, npost, layer):
    batch, seq, d_model = x.shape
    tb = min(PROMPT_TIME_BLOCK, seq)
    assert seq % tb == 0 and tb % GROUP == 0
    grid = (batch, seq // tb)
    lspec = functools.partial(_layer_spec, layer=layer, ngrid=2)
    in_specs = [
        pl.BlockSpec((None, tb, d_model), lambda b, t: (b, t, 0)),
        pl.BlockSpec((tb, HEAD_DIM), lambda b, t: (t, 0)),
        pl.BlockSpec((tb, HEAD_DIM), lambda b, t: (t, 0)),
        lspec((d_model, IN_WIDTH_PADDED)),
        lspec((CONV_W, QKV_WIDTH)),
        lspec((1, LANES)),
        lspec((1, LANES)),
        lspec((1, HEAD_DIM)),
        lspec((1, HEAD_DIM)),
        lspec((A_WIDTH + B_WIDTH, d_model)),
        lspec((1, d_model)),
        lspec((1, d_model)),
    ]
    out_specs = [
        pl.BlockSpec((None, tb, d_model), lambda b, t: (b, t, 0)),
        pl.BlockSpec((None, CONV_W - 1, QKV_WIDTH), lambda b, t: (b, 0, 0)),
        pl.BlockSpec((None, A_HEADS, HEAD_DIM, HEAD_DIM), lambda b, t: (b, 0, 0, 0)),
        pl.BlockSpec((None, B_HEADS, HEAD_DIM, HEAD_DIM), lambda b, t: (b, 0, 0, 0)),
    ]
    out_shape = [
        jax.ShapeDtypeStruct((batch, seq, d_model), _F32),
        jax.ShapeDtypeStruct((batch, CONV_W - 1, QKV_WIDTH), _F32),
        jax.ShapeDtypeStruct((batch, A_HEADS, HEAD_DIM, HEAD_DIM), _F32),
        jax.ShapeDtypeStruct((batch, B_HEADS, HEAD_DIM, HEAD_DIM), _F32),
    ]
    scratch = [
        pltpu.VMEM((tb + SUBLANES, QKV_WIDTH), _F32),
        pltpu.VMEM((tb, REST_WIDTH), _F32),
        pltpu.VMEM((tb, QKV_WIDTH), _F32),
        pltpu.VMEM((2, tb, LANES), _F32),
        pltpu.VMEM((LANES, tb), _F32),
        pltpu.VMEM((tb, A_WIDTH + B_WIDTH), _F32),
    ]
    return pl.pallas_call(
        functools.partial(_mixer_prompt_body, tb=tb),
        grid=grid, in_specs=in_specs, out_specs=out_specs, out_shape=out_shape,
        scratch_shapes=scratch,
        compiler_params=pltpu.CompilerParams(
            dimension_semantics=("arbitrary", "arbitrary"), vmem_limit_bytes=VMEM_LIMIT_BYTES),
        name=f"mixer_prompt_l{layer}",
    )(x, cos_full, sin_signed, win, conv_w, alog, dtb, gnw, rnw, wout, npre, npost)


def _column_of_row(row):
    return jnp.broadcast_to(row, (HEAD_DIM, HEAD_DIM)).T


def _mixer_sample_body(*refs, bb, aliased):
    (x_ref, cos_ref, sin_ref, win_ref, convw_ref, alog_ref, dtb_ref, gnw_ref, rnw_ref, wout_ref,
     npre_ref, npost_ref, convs_ref, sgi_ref, sri_ref) = refs[:15]
    refs = refs[15 + (2 if aliased else 0):]
    (h_ref, convn_ref, sgo_ref, sro_ref,
     act_ref, pr_ref, eg_ref, bt_ref, qka_ref, qkb_ref, o_ref) = refs
    i = pl.program_id(0)
    n = pl.num_programs(0)
    nb = x_ref.shape[0]

    @pl.when(i == 0)
    def _():
        x = x_ref[...]
        hn = _rmsnorm(x, npre_ref[...]).astype(_BF16)
        pq = jnp.dot(hn, win_ref[:, 0:QKV_WIDTH], preferred_element_type=_F32)
        pr_ref[...] = jnp.dot(hn, win_ref[:, QKV_WIDTH:IN_WIDTH_PADDED], preferred_element_type=_F32)
        for j in range(QKV_WIDTH // LANES):
            cs = slice(j * LANES, (j + 1) * LANES)
            new = pq[:, cs]
            acc = new * convw_ref[3:4, cs]
            for r in range(CONV_W - 1):
                hist = convs_ref[:, r * QKV_WIDTH + j * LANES:r * QKV_WIDTH + (j + 1) * LANES]
                acc = acc + hist * convw_ref[r:r + 1, cs]
                if r > 0:
                    convn_ref[:, (r - 1) * QKV_WIDTH + j * LANES:(r - 1) * QKV_WIDTH + (j + 1) * LANES] = hist
            convn_ref[:, (CONV_W - 2) * QKV_WIDTH + j * LANES:(CONV_W - 2) * QKV_WIDTH + (j + 1) * LANES] = new
            a = _silu(acc)
            if j < 2 * A_HEADS:
                a = a * lax.rsqrt(jnp.sum(a * a, axis=-1, keepdims=True) + L2_EPS)
            if j < A_HEADS:
                a = a * (HEAD_DIM ** -0.5)
            act_ref[:, cs] = a
        ps = pr_ref[:, SMALL_OFF:SMALL_OFF + LANES]
        beta_all = jax.nn.sigmoid(ps)
        eg_all = jnp.exp(-jnp.exp(alog_ref[...]) * _softplus(ps + dtb_ref[...]))
        cos_full = cos_ref[...]
        sin_signed = sin_ref[...]
        for h in range(A_HEADS):
            bt_ref[h] = jnp.broadcast_to(beta_all[:, h:h + 1], (nb, LANES))
            eg_ref[h] = jnp.broadcast_to(eg_all[:, A_HEADS + h:A_HEADS + h + 1], (nb, LANES))
            qk = jnp.sum(act_ref[:, _head_cols(0, h)] * act_ref[:, _head_cols(A_WIDTH, h)],
                         axis=-1, keepdims=True)
            qka_ref[h] = jnp.broadcast_to(qk, (nb, LANES))
        for h in range(B_HEADS):
            cq = _head_cols(BQ_OFF, h)
            ck = _head_cols(BK_OFF, h)
            q = _rope(pr_ref[:, cq], cos_full, sin_signed)
            k = _rope(pr_ref[:, ck], cos_full, sin_signed) * (HEAD_DIM ** -0.5)
            pr_ref[:, cq] = q
            pr_ref[:, ck] = k
            qkb_ref[h] = jnp.broadcast_to(jnp.sum(q * k, axis=-1, keepdims=True), (nb, LANES))

    base = pl.multiple_of(i * bb, SUBLANES)
    rows = pl.ds(base, bb)
    for h in range(A_HEADS):
        q8 = act_ref[rows, _head_cols(0, h)]
        k8 = act_ref[rows, _head_cols(A_WIDTH, h)]
        v8 = act_ref[rows, _head_cols(2 * A_WIDTH, h)]
        eg8 = eg_ref[h, rows, :]
        bt8 = bt_ref[h, rows, :]
        qk8 = qka_ref[h, rows, :]
        outs = []
        for bl in range(bb):
            r = slice(bl, bl + 1)
            state = sgi_ref[bl, h]
            qc = _column_of_row(q8[r])
            kc = _column_of_row(k8[r])
            eg = eg8[r]
            ks = jnp.sum(kc * state, axis=0, keepdims=True)
            qs = jnp.sum(qc * state, axis=0, keepdims=True)
            v_new = bt8[r] * (v8[r] - eg * ks)
            outs.append(eg * qs + qk8[r] * v_new)
            sgo_ref[bl, h] = state * eg + kc * v_new
        o_ref[rows, _head_cols(0, h)] = jnp.concatenate(outs, axis=0)
    for h in range(B_HEADS):
        gamma = math.exp(_log_gamma(h))
        q8 = pr_ref[rows, _head_cols(BQ_OFF, h)]
        k8 = pr_ref[rows, _head_cols(BK_OFF, h)]
        v8 = pr_ref[rows, _head_cols(BV_OFF, h)]
        qk8 = qkb_ref[h, rows, :]
        outs = []
        for bl in range(bb):
            r = slice(bl, bl + 1)
            state = sri_ref[bl, h]
            qc = _column_of_row(q8[r])
            kc = _column_of_row(k8[r])
            qs = jnp.sum(qc * state, axis=0, keepdims=True)
            outs.append(gamma * qs + qk8[r] * v8[r])
            sro_ref[bl, h] = state * gamma + kc * v8[r]
        o_ref[rows, _head_cols(A_WIDTH, h)] = jnp.concatenate(outs, axis=0)

    @pl.when(i == n - 1)
    def _():
        for h in range(A_HEADS):
            cs = _head_cols(0, h)
            o_ref[:, cs] = _gated_head_norm(o_ref[:, cs], gnw_ref[...], pr_ref[:, _head_cols(Z_OFF, h)])
        for h in range(B_HEADS):
            cs = _head_cols(A_WIDTH, h)
            o_ref[:, cs] = _gated_head_norm(o_ref[:, cs], rnw_ref[...], pr_ref[:, _head_cols(BG_OFF, h)])
        m = jnp.dot(o_ref[...].astype(_BF16), wout_ref[...], preferred_element_type=_F32)
        h_ref[...] = x_ref[...] + _rmsnorm(m, npost_ref[...])


def _mixer_sample(x, cos_full, sin_signed, win, conv_w, alog, dtb, gnw, rnw, wout, npre, npost,
                  conv_state, state_gdn, state_ret, prev_gdn, prev_ret, layer):
    nb, d_model = x.shape
    depth = state_gdn.shape[0]
    bb = min(SAMPLE_BATCH_BLOCK, nb)
    assert nb % bb == 0
    aliased = prev_gdn is not None
    lspec = functools.partial(_layer_spec, layer=layer, ngrid=1)
    full = lambda shape: pl.BlockSpec(tuple(shape), lambda i: (0,) * len(shape))
    state_spec = lambda heads: pl.BlockSpec((None, bb, heads, HEAD_DIM, HEAD_DIM),
                                            lambda i: (layer, i, 0, 0, 0))
    in_specs = [
        full((nb, d_model)),
        full((1, HEAD_DIM)),
        full((1, HEAD_DIM)),
        lspec((d_model, IN_WIDTH_PADDED)),
        lspec((CONV_W, QKV_WIDTH)),
        lspec((1, LANES)),
        lspec((1, LANES)),
        lspec((1, HEAD_DIM)),
        lspec((1, HEAD_DIM)),
        lspec((A_WIDTH + B_WIDTH, d_model)),
        lspec((1, d_model)),
        lspec((1, d_model)),
        lspec((nb, (CONV_W - 1) * QKV_WIDTH)),
        state_spec(A_HEADS),
        state_spec(B_HEADS),
    ]
    args = [x, cos_full, sin_signed, win, conv_w, alog, dtb, gnw, rnw, wout, npre, npost,
            conv_state, state_gdn, state_ret]
    aliases = {}
    if aliased:
        in_specs += [pl.BlockSpec(memory_space=pl.ANY), pl.BlockSpec(memory_space=pl.ANY)]
        aliases = {len(args): 2, len(args) + 1: 3}
        args += [prev_gdn, prev_ret]
    out_specs = [
        full((nb, d_model)),
        full((nb, (CONV_W - 1) * QKV_WIDTH)),
        state_spec(A_HEADS),
        state_spec(B_HEADS),
    ]
    out_shape = [
        jax.ShapeDtypeStruct((nb, d_model), _F32),
        jax.ShapeDtypeStruct((nb, (CONV_W - 1) * QKV_WIDTH), _F32),
        jax.ShapeDtypeStruct((depth, nb, A_HEADS, HEAD_DIM, HEAD_DIM), _F32),
        jax.ShapeDtypeStruct((depth, nb, B_HEADS, HEAD_DIM, HEAD_DIM), _F32),
    ]
    scratch = [
        pltpu.VMEM((nb, QKV_WIDTH), _F32),
        pltpu.VMEM((nb, REST_WIDTH), _F32),
        pltpu.VMEM((A_HEADS, nb, LANES), _F32),
        pltpu.VMEM((A_HEADS, nb, LANES), _F32),
        pltpu.VMEM((A_HEADS, nb, LANES), _F32),
        pltpu.VMEM((B_HEADS, nb, LANES), _F32),
        pltpu.VMEM((nb, A_WIDTH + B_WIDTH), _F32),
    ]
    return pl.pallas_call(
        functools.partial(_mixer_sample_body, bb=bb, aliased=aliased),
        grid=(nb // bb,), in_specs=in_specs, out_specs=out_specs, out_shape=out_shape,
        scratch_shapes=scratch, input_output_aliases=aliases,
        compiler_params=pltpu.CompilerParams(
            dimension_semantics=("arbitrary",), vmem_limit_bytes=VMEM_LIMIT_BYTES),
        name=f"mixer_sample_l{layer}",
    )(*args)


def _ffn_body(h_ref, wgu_ref, wd_ref, npre_ref, npost_ref, y_ref, act_ref, *, d_ff):
    hn = _rmsnorm(h_ref[...], npre_ref[...]).astype(_BF16)
    for j in range(d_ff // FFN_COL_BLOCK):
        cg = slice(j * FFN_COL_BLOCK, (j + 1) * FFN_COL_BLOCK)
        cu = slice(d_ff + j * FFN_COL_BLOCK, d_ff + (j + 1) * FFN_COL_BLOCK)
        gate = jnp.dot(hn, wgu_ref[:, cg], preferred_element_type=_F32)
        up = jnp.dot(hn, wgu_ref[:, cu], preferred_element_type=_F32)
        act_ref[:, cg] = (_silu(gate) * up).astype(_BF16)
    f = jnp.dot(act_ref[...], wd_ref[...], preferred_element_type=_F32)
    y_ref[...] = h_ref[...] + _rmsnorm(f, npost_ref[...])


def _ffn(h, wgu, wd, npre, npost, layer, tag):
    rows, d_model = h.shape
    d_ff = wd.shape[1]
    assert d_ff % FFN_COL_BLOCK == 0
    rb = min(FFN_ROW_BLOCK, rows)
    assert rows % rb == 0
    lspec = functools.partial(_layer_spec, layer=layer, ngrid=1)
    return pl.pallas_call(
        functools.partial(_ffn_body, d_ff=d_ff),
        grid=(rows // rb,),
        in_specs=[
            pl.BlockSpec((rb, d_model), lambda i: (i, 0)),
            lspec((d_model, 2 * d_ff)),
            lspec((d_ff, d_model)),
            lspec((1, d_model)),
            lspec((1, d_model)),
        ],
        out_specs=pl.BlockSpec((rb, d_model), lambda i: (i, 0)),
        out_shape=jax.ShapeDtypeStruct((rows, d_model), _F32),
        scratch_shapes=[pltpu.VMEM((rb, d_ff), _BF16)],
        compiler_params=pltpu.CompilerParams(
            dimension_semantics=("arbitrary",), vmem_limit_bytes=VMEM_LIMIT_BYTES),
        name=f"ffn_{tag}_l{layer}",
    )(h, wgu, wd, npre, npost)


def _rope_tables(pos):
    half = HEAD_DIM // 2
    inv = ROPE_BASE ** (-jnp.arange(half, dtype=_F32) / half)
    ang = pos[:, None] * inv[None, :]
    cos, sin = jnp.cos(ang), jnp.sin(ang)
    return jnp.concatenate([cos, cos], axis=-1), jnp.concatenate([-sin, sin], axis=-1)


def _rearranged_w_in(w_in):
    small0 = QKV_WIDTH + A_WIDTH
    small1 = small0 + 2 * A_HEADS
    small = jnp.pad(w_in[:, :, small0:small1], ((0, 0), (0, 0), (0, LANES - 2 * A_HEADS)))
    return jnp.concatenate([w_in[:, :, :small0], w_in[:, :, small1:], small], axis=-1).astype(_BF16)


def kernel(x_prompt, x_sample, state_conv, state_gdn, state_ret, w_in, conv_w, a_log, dt_bias, gdn_norm_w, ret_norm_w, w_out, norm_mix_pre, norm_mix_post, norm_ffn_pre, norm_ffn_post, w_gate_up, w_down):
    depth = w_in.shape[0]
    batch, seq, d_model = x_prompt.shape
    nb, seq_s, _ = x_sample.shape
    assert seq_s == 1

    win = _rearranged_w_in(w_in)
    wout = w_out.astype(_BF16)
    wgu = w_gate_up.astype(_BF16)
    wd = w_down.astype(_BF16)
    alog = jnp.pad(a_log, ((0, 0), (A_HEADS, LANES - 2 * A_HEADS)))[:, None, :]
    dtb = jnp.pad(dt_bias, ((0, 0), (A_HEADS, LANES - 2 * A_HEADS)))[:, None, :]
    gnw = gdn_norm_w[:, None, :]
    rnw = ret_norm_w[:, None, :]
    npre = norm_mix_pre[:, None, :]
    npost = norm_mix_post[:, None, :]
    fpre = norm_ffn_pre[:, None, :]
    fpost = norm_ffn_post[:, None, :]
    cos_p, sin_p = _rope_tables(jnp.arange(seq, dtype=_F32))
    cos_s, sin_s = _rope_tables(PAST_LEN + jnp.arange(seq_s, dtype=_F32))
    conv_state = state_conv.reshape(depth, nb, (CONV_W - 1) * QKV_WIDTH)

    hp = x_prompt
    hs = x_sample.reshape(nb, d_model)
    convs_p, gdns_p, rets_p, convs_s = [], [], [], []
    gdn_s = ret_s = None
    for l in range(depth):
        hp, conv_p, gdn_p, ret_p = _mixer_prompt(hp, cos_p, sin_p, win, conv_w, alog, dtb, gnw, rnw,
                                                 wout, npre, npost, l)
        hp = _ffn(hp.reshape(batch * seq, d_model), wgu, wd, fpre, fpost, l, "prompt").reshape(
            batch, seq, d_model)
        convs_p.append(conv_p)
        gdns_p.append(gdn_p)
        rets_p.append(ret_p)
        hs, conv_s, gdn_s, ret_s = _mixer_sample(hs, cos_s, sin_s, win, conv_w, alog, dtb, gnw, rnw,
                                                 wout, npre, npost, conv_state, state_gdn, state_ret,
                                                 gdn_s, ret_s, l)
        hs = _ffn(hs, wgu, wd, fpre, fpost, l, "sample")
        convs_s.append(conv_s.reshape(nb, CONV_W - 1, QKV_WIDTH))
    return (hp, hs.reshape(nb, seq_s, d_model), jnp.stack(convs_p), jnp.stack(gdns_p),
            jnp.stack(rets_p), jnp.stack(convs_s), gdn_s, ret_s)
```

```python
import functools
import math

import jax
import jax.numpy as jnp
from jax import lax
from jax.experimental import pallas as pl
from jax.experimental.pallas import tpu as pltpu

HEAD_DIM = 128
A_HEADS = 4
B_HEADS = 4
A_WIDTH = A_HEADS * HEAD_DIM
B_WIDTH = B_HEADS * HEAD_DIM
QKV_WIDTH = 3 * A_WIDTH
CONV_W = 4
CHUNK = 64
ROPE_BASE = 10000.0
EPS = 1e-6
L2_EPS = 1e-6
PAST_LEN = 16384
LANES = 128
SUBLANES = 8

REST_WIDTH = A_WIDTH + 4 * B_WIDTH + LANES
Z_OFF = 0
BQ_OFF = A_WIDTH
BK_OFF = BQ_OFF + B_WIDTH
BV_OFF = BK_OFF + B_WIDTH
BG_OFF = BV_OFF + B_WIDTH
SMALL_OFF = BG_OFF + B_WIDTH
IN_WIDTH_PADDED = QKV_WIDTH + REST_WIDTH

GROUP = 256
CHUNKS_PER_GROUP = GROUP // CHUNK
PROMPT_TIME_BLOCK = 256
FFN_ROW_BLOCK = 512
FFN_COL_BLOCK = 256
FFN_FILL_PLAN = ((1,) + (0,) + (1,) + (1, 0, 1, 0) + (1, 0, 1, 0) + (0,) * 4 + (1, 0, 1, 0, 1, 0)
                 + (0, 1, 0, 0) + (0, 1, 0, 0) + (0,) * 8 + (4,))
SAMPLE_BATCH_BLOCK = 8
VMEM_LIMIT_BYTES = 56 * 1024 * 1024

_BF16 = jnp.bfloat16
_F32 = jnp.float32


def _dot(a, b):
    return jnp.dot(a.astype(_BF16), b.astype(_BF16), preferred_element_type=_F32)


def _dot_nt(a, b):
    return lax.dot_general(a.astype(_BF16), b.astype(_BF16), (((1,), (1,)), ((), ())),
                           preferred_element_type=_F32)


def _dot_tn(a, b):
    return lax.dot_general(a.astype(_BF16), b.astype(_BF16), (((0,), (0,)), ((), ())),
                           preferred_element_type=_F32)


def _rmsnorm(x, w):
    return x * lax.rsqrt(jnp.mean(x * x, axis=-1, keepdims=True) + EPS) * w


def _silu(x):
    return x * jax.nn.sigmoid(x)


def _softplus(x):
    return jnp.maximum(x, 0.0) + jnp.log1p(jnp.exp(-jnp.abs(x)))


def _log_gamma(h):
    return math.log1p(-(2.0 ** (-5.0 - h)))


def _rope(x, cos_full, sin_signed):
    return x * cos_full + pltpu.roll(x, HEAD_DIM // 2, axis=1) * sin_signed


def _head_cols(base, h):
    return slice(base + h * HEAD_DIM, base + (h + 1) * HEAD_DIM)


def _gated_head_norm(o, w, gate):
    return o * lax.rsqrt(jnp.mean(o * o, axis=-1, keepdims=True) + EPS) * w * _silu(gate)


class _Filler:
    def __init__(self, pieces, plan):
        self._pieces = list(pieces)
        self._plan = list(plan)
        self._next_piece = 0
        self._next_point = 0

    def point(self):
        count = self._plan[self._next_point] if self._next_point < len(self._plan) else 0
        self._next_point += 1
        self._emit(count)

    def flush(self):
        self._emit(len(self._pieces))

    def _emit(self, count):
        for _ in range(count):
            if self._next_piece < len(self._pieces):
                self._pieces[self._next_piece]()
                self._next_piece += 1


def _fold_rows(m):
    out = m[0:CHUNK]
    for c in range(1, CHUNKS_PER_GROUP):
        out = out + m[c * CHUNK:(c + 1) * CHUNK]
    return out


def _unfold_rows(r, same_chunk):
    return jnp.where(same_chunk, jnp.concatenate([r] * CHUNKS_PER_GROUP, axis=0), 0.0)


def _unit_lower_inverses(a_folded, eye_folded, same_chunk, fill):
    xs = list(a_folded)
    ps = [eye_folded - x for x in xs]
    n = 1
    while n < CHUNK:
        for h in range(len(xs)):
            x_bd = _unfold_rows(xs[h], same_chunk).astype(_BF16)
            if n == 1:
                xs[h] = _dot(xs[h], x_bd)
            elif 2 * n < CHUNK:
                r = _dot(jnp.concatenate([xs[h], ps[h]], axis=0), x_bd)
                xs[h] = r[0:CHUNK]
                ps[h] = ps[h] + r[CHUNK:2 * CHUNK]
            else:
                ps[h] = ps[h] + _dot(ps[h], x_bd)
        fill.point()
        n *= 2
    return [_unfold_rows(p, same_chunk) for p in ps]


def _recurrences_one_group(g, live, fill, act_ref, pr_ref, small_ref, gct_ref, sg_ref, sr_ref, o_ref):
    rows = slice(g * GROUP, (g + 1) * GROUP)
    ri = lax.broadcasted_iota(jnp.int32, (GROUP, GROUP), 0)
    ci = lax.broadcasted_iota(jnp.int32, (GROUP, GROUP), 1)
    same_chunk = (ri // CHUNK) == (ci // CHUNK)
    tril = same_chunk & (ri >= ci)
    off_diag = ri != ci
    rf = lax.broadcasted_iota(jnp.int32, (CHUNK, GROUP), 0)
    cf = lax.broadcasted_iota(jnp.int32, (CHUNK, GROUP), 1)
    eye_folded = jnp.where(rf == cf % CHUNK, 1.0, 0.0).astype(_F32)
    pos = (lax.broadcasted_iota(jnp.int32, (GROUP, HEAD_DIM), 0) % CHUNK).astype(_F32)
    chunk_rows = [slice(c * CHUNK, (c + 1) * CHUNK) for c in range(CHUNKS_PER_GROUP)]

    qs, ks, gccs, a_folded, a_intra, uw_rhs, e_ins = [], [], [], [], [], [], []
    for h in range(A_HEADS):
        q = act_ref[rows, _head_cols(0, h)]
        k = act_ref[rows, _head_cols(A_WIDTH, h)]
        v = act_ref[rows, _head_cols(2 * A_WIDTH, h)]
        beta = jnp.broadcast_to(small_ref[0, rows, h:h + 1], (GROUP, HEAD_DIM))
        gc_col = small_ref[1, rows, A_HEADS + h:A_HEADS + h + 1]
        gcc = jnp.broadcast_to(gc_col, (GROUP, HEAD_DIM))
        gcr = jnp.broadcast_to(gct_ref[A_HEADS + h:A_HEADS + h + 1, rows], (GROUP, GROUP))
        diff = jnp.broadcast_to(gc_col, (GROUP, GROUP)) - gcr
        decay = jnp.where(tril, jnp.exp(jnp.where(tril, diff, 0.0)), 0.0)
        kb = k * beta
        e_in = jnp.exp(gcc)
        kq = _dot_nt(jnp.concatenate([kb, q], axis=0), k)
        a_folded.append(_fold_rows(jnp.where(off_diag, kq[0:GROUP] * decay, 0.0)))
        a_intra.append(kq[GROUP:2 * GROUP] * decay)
        uw_rhs.append(jnp.concatenate([v * beta, kb * e_in], axis=1).astype(_BF16))
        qs.append(q)
        ks.append(k)
        gccs.append(gcc)
        e_ins.append(e_in)
        fill.point()

    r_qe, r_oloc, r_b, r_echunk = [], [], [], []
    for h in range(B_HEADS):
        lg = _log_gamma(h)
        q = pr_ref[rows, _head_cols(BQ_OFF, h)]
        k = pr_ref[rows, _head_cols(BK_OFF, h)]
        v = pr_ref[rows, _head_cols(BV_OFF, h)]
        decay = jnp.where(tril, jnp.exp(jnp.where(tril, (ri - ci).astype(_F32) * lg, 0.0)), 0.0)
        vb16 = v.astype(_BF16)
        r_oloc.append(_dot(_dot_nt(q, k) * decay, vb16))
        r_qe.append((q * jnp.exp((pos + 1.0) * lg)).astype(_BF16))
        k_out = (k * jnp.exp((CHUNK - 1.0 - pos) * lg)).astype(_BF16)
        r_b.append([_dot_tn(k_out[cr], vb16[cr]) for cr in chunk_rows])
        r_echunk.append(math.exp(CHUNK * lg))
        fill.point()

    t_inv = _unit_lower_inverses(a_folded, eye_folded, same_chunk, fill)

    g_lhs, g_oloc, g_b, g_elast = [], [], [], []
    for h in range(A_HEADS):
        uw = _dot(t_inv[h], uw_rhs[h])
        uw16 = uw.astype(_BF16)
        aiuw = _dot(a_intra[h], uw16)
        g_oloc.append(aiuw[:, 0:HEAD_DIM])
        q_eff = qs[h] * e_ins[h] - aiuw[:, HEAD_DIM:2 * HEAD_DIM]
        lhs, bs, elast = [], [], []
        for cr in chunk_rows:
            gcc_c = gccs[h][cr]
            g_last = gcc_c[CHUNK - 1:CHUNK, :]
            k_out = ks[h][cr] * jnp.exp(g_last - gcc_c)
            bg = _dot_tn(k_out, uw16[cr])
            bs.append(bg[:, 0:HEAD_DIM])
            lhs.append(jnp.concatenate([q_eff[cr], bg[:, HEAD_DIM:2 * HEAD_DIM]], axis=0).astype(_BF16))
            elast.append(jnp.exp(g_last))
        g_lhs.append(lhs)
        g_b.append(bs)
        g_elast.append(elast)
        fill.point()

    g_state = [sg_ref[h] for h in range(A_HEADS)]
    r_state = [sr_ref[h] for h in range(B_HEADS)]
    for c, cr in enumerate(chunk_rows):
        out_rows = slice(g * GROUP + c * CHUNK, g * GROUP + (c + 1) * CHUNK)
        for h in range(A_HEADS):
            r = _dot(g_lhs[h][c], g_state[h])
            o_ref[out_rows, _head_cols(0, h)] = r[0:CHUNK] + g_oloc[h][cr]
            g_state[h] = g_state[h] * g_elast[h][c] + g_b[h][c] - r[CHUNK:CHUNK + HEAD_DIM]
        for h in range(B_HEADS):
            o_ref[out_rows, _head_cols(A_WIDTH, h)] = _dot(r_qe[h][cr], r_state[h]) + r_oloc[h][cr]
            r_state[h] = r_state[h] * r_echunk[h] + r_b[h][c]
        fill.point()
    for h in range(A_HEADS):
        sg_ref[h] = jnp.where(live, g_state[h], sg_ref[h])
    for h in range(B_HEADS):
        sr_ref[h] = jnp.where(live, r_state[h], sr_ref[h])


def _ffn_rows(h, wgu_ref, wd_ref, npre, npost, act_ref, d_ff):
    hn = _rmsnorm(h, npre).astype(_BF16)
    for j in range(d_ff // FFN_COL_BLOCK):
        cg = slice(j * FFN_COL_BLOCK, (j + 1) * FFN_COL_BLOCK)
        cu = slice(d_ff + j * FFN_COL_BLOCK, d_ff + (j + 1) * FFN_COL_BLOCK)
        gate = jnp.dot(hn, wgu_ref[:, cg], preferred_element_type=_F32)
        up = jnp.dot(hn, wgu_ref[:, cu], preferred_element_type=_F32)
        act_ref[:, cg] = (_silu(gate) * up).astype(_BF16)
    f = jnp.dot(act_ref[...], wd_ref[...], preferred_element_type=_F32)
    return h + _rmsnorm(f, npost)


def _ffn_pieces(h_ref, hn_ref, wgu_ref, wd_ref, npost, act_ref, f_ref, y_ref, d_ff):
    d_model = f_ref.shape[1]
    pieces = []

    def gate_up(j):
        def run():
            cg = slice(j * FFN_COL_BLOCK, (j + 1) * FFN_COL_BLOCK)
            cu = slice(d_ff + j * FFN_COL_BLOCK, d_ff + (j + 1) * FFN_COL_BLOCK)
            hn = hn_ref[...]
            gate = jnp.dot(hn, wgu_ref[:, cg], preferred_element_type=_F32)
            up = jnp.dot(hn, wgu_ref[:, cu], preferred_element_type=_F32)
            act_ref[:, cg] = (_silu(gate) * up).astype(_BF16)
        return run

    def down(j):
        def run():
            cs = slice(j * FFN_COL_BLOCK, (j + 1) * FFN_COL_BLOCK)
            f_ref[:, cs] = jnp.dot(act_ref[...], wd_ref[:, cs], preferred_element_type=_F32)
        return run

    def finish():
        y_ref[...] = h_ref[...] + _rmsnorm(f_ref[...], npost)

    pieces += [gate_up(j) for j in range(d_ff // FFN_COL_BLOCK)]
    pieces += [down(j) for j in range(d_model // FFN_COL_BLOCK)]
    pieces.append(finish)
    return pieces


def _layer_prompt_body(x_ref, cos_ref, sin_ref, win_ref, convw_ref, alog_ref, dtb_ref, gnw_ref,
                       rnw_ref, wout_ref, npre_ref, npost_ref, wgu_ref, wd_ref, fpre_ref, fpost_ref,
                       y_ref, conv_ref, sg_ref, sr_ref,
                       pq_ref, pr_ref, act_ref, small_ref, gct_ref, o_ref, h_ref, hn_ref, ffn_act_ref,
                       f_ref, *, tb, nt, nblocks, d_ff):
    s = pl.program_id(0)
    live = s < nblocks
    t = lax.rem(jnp.minimum(s, nblocks - 1), nt)
    pad = SUBLANES

    @pl.when(s == 0)
    def _():
        h_ref[1] = jnp.zeros(h_ref.shape[1:], _F32)
        hn_ref[1] = jnp.zeros(hn_ref.shape[1:], _BF16)

    @pl.when((t == 0) & live)
    def _():
        pq_ref[0:pad, :] = jnp.zeros((pad, QKV_WIDTH), _F32)
        sg_ref[...] = jnp.zeros(sg_ref.shape, _F32)
        sr_ref[...] = jnp.zeros(sr_ref.shape, _F32)

    prev = lax.rem(s + 1, 2)
    fill = _Filler(_ffn_pieces(h_ref.at[prev], hn_ref.at[prev], wgu_ref, wd_ref, fpost_ref[...],
                               ffn_act_ref, f_ref, y_ref, d_ff), FFN_FILL_PLAN)
    fill.point()

    x = x_ref[...]
    hn = _rmsnorm(x, npre_ref[...]).astype(_BF16)
    pq_ref[pad:pad + tb, :] = jnp.dot(hn, win_ref[:, 0:QKV_WIDTH], preferred_element_type=_F32)
    pr_ref[...] = jnp.dot(hn, win_ref[:, QKV_WIDTH:IN_WIDTH_PADDED], preferred_element_type=_F32)

    for j in range(QKV_WIDTH // LANES):
        cs = slice(j * LANES, (j + 1) * LANES)
        acc = pq_ref[pad:pad + tb, cs] * convw_ref[3:4, cs]
        for i in range(CONV_W - 1):
            acc = acc + pq_ref[pad - 3 + i:pad - 3 + i + tb, cs] * convw_ref[i:i + 1, cs]
        a = _silu(acc)
        if j < 2 * A_HEADS:
            a = a * lax.rsqrt(jnp.sum(a * a, axis=-1, keepdims=True) + L2_EPS)
        if j < A_HEADS:
            a = a * (HEAD_DIM ** -0.5)
        act_ref[:, cs] = a

    fill.point()
    tail = pq_ref[pad + tb - 3:pad + tb, :]
    conv_ref[...] = tail
    pq_ref[pad - 3:pad, :] = tail

    ps = pr_ref[:, SMALL_OFF:SMALL_OFF + LANES]
    beta_all = jax.nn.sigmoid(ps)
    gc = -jnp.exp(alog_ref[...]) * _softplus(ps + dtb_ref[...])
    row_in_chunk = lax.broadcasted_iota(jnp.int32, (tb, LANES), 0) % CHUNK
    shift = 1
    while shift < CHUNK:
        gc = gc + jnp.where(row_in_chunk >= shift, pltpu.roll(gc, shift, axis=0), 0.0)
        shift *= 2
    small_ref[0] = beta_all
    small_ref[1] = gc
    gct_ref[...] = gc.T
    fill.point()

    cos_full = cos_ref[...]
    sin_signed = sin_ref[...]
    for h in range(B_HEADS):
        cq = _head_cols(BQ_OFF, h)
        ck = _head_cols(BK_OFF, h)
        pr_ref[:, cq] = _rope(pr_ref[:, cq], cos_full, sin_signed)
        pr_ref[:, ck] = _rope(pr_ref[:, ck], cos_full, sin_signed) * (HEAD_DIM ** -0.5)
        fill.point()

    for g in range(tb // GROUP):
        _recurrences_one_group(g, live, fill, act_ref, pr_ref, small_ref, gct_ref, sg_ref, sr_ref, o_ref)

    for h in range(A_HEADS):
        cs = _head_cols(0, h)
        o_ref[:, cs] = _gated_head_norm(o_ref[:, cs], gnw_ref[...], pr_ref[:, _head_cols(Z_OFF, h)])
        fill.point()
    for h in range(B_HEADS):
        cs = _head_cols(A_WIDTH, h)
        o_ref[:, cs] = _gated_head_norm(o_ref[:, cs], rnw_ref[...], pr_ref[:, _head_cols(BG_OFF, h)])
        fill.point()
    m = jnp.dot(o_ref[...].astype(_BF16), wout_ref[...], preferred_element_type=_F32)
    fill.point()
    h_new = x_ref[...] + _rmsnorm(m, npost_ref[...])
    cur = lax.rem(s, 2)
    h_ref[cur] = h_new
    hn_ref[cur] = _rmsnorm(h_new, fpre_ref[...]).astype(_BF16)
    fill.flush()


def _layer_spec(shape, layer):
    zeros = (0,) * len(shape)
    return pl.BlockSpec((None,) + tuple(shape), lambda i: (layer,) + zeros)


def _layer_prompt(x, cos_full, sin_signed, win, conv_w, alog, dtb, gnw, rnw, wout, npre, npost,
                  wgu, wd, fpre, fpost, layer):
    batch, seq, d_model = x.shape
    d_ff = wd.shape[1]
    tb = min(PROMPT_TIME_BLOCK, seq)
    assert seq % tb == 0 and tb % GROUP == 0 and d_ff % FFN_COL_BLOCK == 0
    nt = seq // tb
    nblocks = batch * nt
    lspec = functools.partial(_layer_spec, layer=layer)

    def mixer_block(s):
        return jnp.minimum(s, nblocks - 1)

    def ffn_block(s):
        return jnp.maximum(s - 1, 0)

    in_specs = [
        pl.BlockSpec((None, tb, d_model), lambda s: (mixer_block(s) // nt, mixer_block(s) % nt, 0)),
        pl.BlockSpec((tb, HEAD_DIM), lambda s: (mixer_block(s) % nt, 0)),
        pl.BlockSpec((tb, HEAD_DIM), lambda s: (mixer_block(s) % nt, 0)),
        lspec((d_model, IN_WIDTH_PADDED)),
        lspec((CONV_W, QKV_WIDTH)),
        lspec((1, LANES)),
        lspec((1, LANES)),
        lspec((1, HEAD_DIM)),
        lspec((1, HEAD_DIM)),
        lspec((A_WIDTH + B_WIDTH, d_model)),
        lspec((1, d_model)),
        lspec((1, d_model)),
        lspec((d_model, 2 * d_ff)),
        lspec((d_ff, d_model)),
        lspec((1, d_model)),
        lspec((1, d_model)),
    ]
    out_specs = [
        pl.BlockSpec((None, tb, d_model), lambda s: (ffn_block(s) // nt, ffn_block(s) % nt, 0)),
        pl.BlockSpec((None, CONV_W - 1, QKV_WIDTH), lambda s: (mixer_block(s) // nt, 0, 0)),
        pl.BlockSpec((None, A_HEADS, HEAD_DIM, HEAD_DIM), lambda s: (mixer_block(s) // nt, 0, 0, 0)),
        pl.BlockSpec((None, B_HEADS, HEAD_DIM, HEAD_DIM), lambda s: (mixer_block(s) // nt, 0, 0, 0)),
    ]
    out_shape = [
        jax.ShapeDtypeStruct((batch, seq, d_model), _F32),
        jax.ShapeDtypeStruct((batch, CONV_W - 1, QKV_WIDTH), _F32),
        jax.ShapeDtypeStruct((batch, A_HEADS, HEAD_DIM, HEAD_DIM), _F32),
        jax.ShapeDtypeStruct((batch, B_HEADS, HEAD_DIM, HEAD_DIM), _F32),
    ]
    scratch = [
        pltpu.VMEM((tb + SUBLANES, QKV_WIDTH), _F32),
        pltpu.VMEM((tb, REST_WIDTH), _F32),
        pltpu.VMEM((tb, QKV_WIDTH), _F32),
        pltpu.VMEM((2, tb, LANES), _F32),
        pltpu.VMEM((LANES, tb), _F32),
        pltpu.VMEM((tb, A_WIDTH + B_WIDTH), _F32),
        pltpu.VMEM((2, tb, d_model), _F32),
        pltpu.VMEM((2, tb, d_model), _BF16),
        pltpu.VMEM((tb, d_ff), _BF16),
        pltpu.VMEM((tb, d_model), _F32),
    ]
    return pl.pallas_call(
        functools.partial(_layer_prompt_body, tb=tb, nt=nt, nblocks=nblocks, d_ff=d_ff),
        grid=(nblocks + 1,), in_specs=in_specs, out_specs=out_specs, out_shape=out_shape,
        scratch_shapes=scratch,
        compiler_params=pltpu.CompilerParams(
            dimension_semantics=("arbitrary",), vmem_limit_bytes=VMEM_LIMIT_BYTES),
        name=f"layer_prompt_l{layer}",
    )(x, cos_full, sin_signed, win, conv_w, alog, dtb, gnw, rnw, wout, npre, npost, wgu, wd, fpre, fpost)


def _column_of_row(row):
    return jnp.broadcast_to(row, (HEAD_DIM, HEAD_DIM)).T


def _mixer_sample_body(*refs, bb, aliased):
    (x_ref, cos_ref, sin_ref, win_ref, convw_ref, alog_ref, dtb_ref, gnw_ref, rnw_ref, wout_ref,
     npre_ref, npost_ref, convs_ref, sgi_ref, sri_ref) = refs[:15]
    refs = refs[15 + (2 if aliased else 0):]
    (h_ref, convn_ref, sgo_ref, sro_ref,
     act_ref, pr_ref, eg_ref, bt_ref, qka_ref, qkb_ref, o_ref) = refs
    i = pl.program_id(0)
    n = pl.num_programs(0)
    nb = x_ref.shape[0]

    @pl.when(i == 0)
    def _():
        x = x_ref[...]
        hn = _rmsnorm(x, npre_ref[...]).astype(_BF16)
        pq = jnp.dot(hn, win_ref[:, 0:QKV_WIDTH], preferred_element_type=_F32)
        pr_ref[...] = jnp.dot(hn, win_ref[:, QKV_WIDTH:IN_WIDTH_PADDED], preferred_element_type=_F32)
        for j in range(QKV_WIDTH // LANES):
            cs = slice(j * LANES, (j + 1) * LANES)
            new = pq[:, cs]
            acc = new * convw_ref[3:4, cs]
            for r in range(CONV_W - 1):
                hist = convs_ref[:, r * QKV_WIDTH + j * LANES:r * QKV_WIDTH + (j + 1) * LANES]
                acc = acc + hist * convw_ref[r:r + 1, cs]
                if r > 0:
                    convn_ref[:, (r - 1) * QKV_WIDTH + j * LANES:(r - 1) * QKV_WIDTH + (j + 1) * LANES] = hist
            convn_ref[:, (CONV_W - 2) * QKV_WIDTH + j * LANES:(CONV_W - 2) * QKV_WIDTH + (j + 1) * LANES] = new
            a = _silu(acc)
            if j < 2 * A_HEADS:
                a = a * lax.rsqrt(jnp.sum(a * a, axis=-1, keepdims=True) + L2_EPS)
            if j < A_HEADS:
                a = a * (HEAD_DIM ** -0.5)
            act_ref[:, cs] = a
        ps = pr_ref[:, SMALL_OFF:SMALL_OFF + LANES]
        beta_all = jax.nn.sigmoid(ps)
        eg_all = jnp.exp(-jnp.exp(alog_ref[...]) * _softplus(ps + dtb_ref[...]))
        cos_full = cos_ref[...]
        sin_signed = sin_ref[...]
        for h in range(A_HEADS):
            bt_ref[h] = jnp.broadcast_to(beta_all[:, h:h + 1], (nb, LANES))
            eg_ref[h] = jnp.broadcast_to(eg_all[:, A_HEADS + h:A_HEADS + h + 1], (nb, LANES))
            qk = jnp.sum(act_ref[:, _head_cols(0, h)] * act_ref[:, _head_cols(A_WIDTH, h)],
                         axis=-1, keepdims=True)
            qka_ref[h] = jnp.broadcast_to(qk, (nb, LANES))
        for h in range(B_HEADS):
            cq = _head_cols(BQ_OFF, h)
            ck = _head_cols(BK_OFF, h)
            q = _rope(pr_ref[:, cq], cos_full, sin_signed)
            k = _rope(pr_ref[:, ck], cos_full, sin_signed) * (HEAD_DIM ** -0.5)
            pr_ref[:, cq] = q
            pr_ref[:, ck] = k
            qkb_ref[h] = jnp.broadcast_to(jnp.sum(q * k, axis=-1, keepdims=True), (nb, LANES))

    base = pl.multiple_of(i * bb, SUBLANES)
    rows = pl.ds(base, bb)
    for h in range(A_HEADS):
        q8 = act_ref[rows, _head_cols(0, h)]
        k8 = act_ref[rows, _head_cols(A_WIDTH, h)]
        v8 = act_ref[rows, _head_cols(2 * A_WIDTH, h)]
        eg8 = eg_ref[h, rows, :]
        bt8 = bt_ref[h, rows, :]
        qk8 = qka_ref[h, rows, :]
        outs = []
        for bl in range(bb):
            r = slice(bl, bl + 1)
            state = sgi_ref[bl, h]
            qc = _column_of_row(q8[r])
            kc = _column_of_row(k8[r])
            eg = eg8[r]
            ks = jnp.sum(kc * state, axis=0, keepdims=True)
            qs = jnp.sum(qc * state, axis=0, keepdims=True)
            v_new = bt8[r] * (v8[r] - eg * ks)
            outs.append(eg * qs + qk8[r] * v_new)
            sgo_ref[bl, h] = state * eg + kc * v_new
        o_ref[rows, _head_cols(0, h)] = jnp.concatenate(outs, axis=0)
    for h in range(B_HEADS):
        gamma = math.exp(_log_gamma(h))
        q8 = pr_ref[rows, _head_cols(BQ_OFF, h)]
        k8 = pr_ref[rows, _head_cols(BK_OFF, h)]
        v8 = pr_ref[rows, _head_cols(BV_OFF, h)]
        qk8 = qkb_ref[h, rows, :]
        outs = []
        for bl in range(bb):
            r = slice(bl, bl + 1)
            state = sri_ref[bl, h]
            qc = _column_of_row(q8[r])
            kc = _column_of_row(k8[r])
            qs = jnp.sum(qc * state, axis=0, keepdims=True)
            outs.append(gamma * qs + qk8[r] * v8[r])
            sro_ref[bl, h] = state * gamma + kc * v8[r]
        o_ref[rows, _head_cols(A_WIDTH, h)] = jnp.concatenate(outs, axis=0)

    @pl.when(i == n - 1)
    def _():
        for h in range(A_HEADS):
            cs = _head_cols(0, h)
            o_ref[:, cs] = _gated_head_norm(o_ref[:, cs], gnw_ref[...], pr_ref[:, _head_cols(Z_OFF, h)])
        for h in range(B_HEADS):
            cs = _head_cols(A_WIDTH, h)
            o_ref[:, cs] = _gated_head_norm(o_ref[:, cs], rnw_ref[...], pr_ref[:, _head_cols(BG_OFF, h)])
        m = jnp.dot(o_ref[...].astype(_BF16), wout_ref[...], preferred_element_type=_F32)
        h_ref[...] = x_ref[...] + _rmsnorm(m, npost_ref[...])


def _mixer_sample(x, cos_full, sin_signed, win, conv_w, alog, dtb, gnw, rnw, wout, npre, npost,
                  conv_state, state_gdn, state_ret, prev_gdn, prev_ret, layer):
    nb, d_model = x.shape
    depth = state_gdn.shape[0]
    bb = min(SAMPLE_BATCH_BLOCK, nb)
    assert nb % bb == 0
    aliased = prev_gdn is not None
    lspec = functools.partial(_layer_spec, layer=layer)
    full = lambda shape: pl.BlockSpec(tuple(shape), lambda i: (0,) * len(shape))
    state_spec = lambda heads: pl.BlockSpec((None, bb, heads, HEAD_DIM, HEAD_DIM),
                                            lambda i: (layer, i, 0, 0, 0))
    in_specs = [
        full((nb, d_model)),
        full((1, HEAD_DIM)),
        full((1, HEAD_DIM)),
        lspec((d_model, IN_WIDTH_PADDED)),
        lspec((CONV_W, QKV_WIDTH)),
        lspec((1, LANES)),
        lspec((1, LANES)),
        lspec((1, HEAD_DIM)),
        lspec((1, HEAD_DIM)),
        lspec((A_WIDTH + B_WIDTH, d_model)),
        lspec((1, d_model)),
        lspec((1, d_model)),
        lspec((nb, (CONV_W - 1) * QKV_WIDTH)),
        state_spec(A_HEADS),
        state_spec(B_HEADS),
    ]
    args = [x, cos_full, sin_signed, win, conv_w, alog, dtb, gnw, rnw, wout, npre, npost,
            conv_state, state_gdn, state_ret]
    aliases = {}
    if aliased:
        in_specs += [pl.BlockSpec(memory_space=pl.ANY), pl.BlockSpec(memory_space=pl.ANY)]
        aliases = {len(args): 2, len(args) + 1: 3}
        args += [prev_gdn, prev_ret]
    out_specs = [
        full((nb, d_model)),
        full((nb, (CONV_W - 1) * QKV_WIDTH)),
        state_spec(A_HEADS),
        state_spec(B_HEADS),
    ]
    out_shape = [
        jax.ShapeDtypeStruct((nb, d_model), _F32),
        jax.ShapeDtypeStruct((nb, (CONV_W - 1) * QKV_WIDTH), _F32),
        jax.ShapeDtypeStruct((depth, nb, A_HEADS, HEAD_DIM, HEAD_DIM), _F32),
        jax.ShapeDtypeStruct((depth, nb, B_HEADS, HEAD_DIM, HEAD_DIM), _F32),
    ]
    scratch = [
        pltpu.VMEM((nb, QKV_WIDTH), _F32),
        pltpu.VMEM((nb, REST_WIDTH), _F32),
        pltpu.VMEM((A_HEADS, nb, LANES), _F32),
        pltpu.VMEM((A_HEADS, nb, LANES), _F32),
        pltpu.VMEM((A_HEADS, nb, LANES), _F32),
        pltpu.VMEM((B_HEADS, nb, LANES), _F32),
        pltpu.VMEM((nb, A_WIDTH + B_WIDTH), _F32),
    ]
    return pl.pallas_call(
        functools.partial(_mixer_sample_body, bb=bb, aliased=aliased),
        grid=(nb // bb,), in_specs=in_specs, out_specs=out_specs, out_shape=out_shape,
        scratch_shapes=scratch, input_output_aliases=aliases,
        compiler_params=pltpu.CompilerParams(
            dimension_semantics=("arbitrary",), vmem_limit_bytes=VMEM_LIMIT_BYTES),
        name=f"mixer_sample_l{layer}",
    )(*args)


def _ffn_body(h_ref, wgu_ref, wd_ref, npre_ref, npost_ref, y_ref, act_ref, *, d_ff):
    y_ref[...] = _ffn_rows(h_ref[...], wgu_ref, wd_ref, npre_ref[...], npost_ref[...], act_ref, d_ff)


def _ffn(h, wgu, wd, npre, npost, layer):
    rows, d_model = h.shape
    d_ff = wd.shape[1]
    assert d_ff % FFN_COL_BLOCK == 0
    rb = min(FFN_ROW_BLOCK, rows)
    assert rows % rb == 0
    lspec = functools.partial(_layer_spec, layer=layer)
    return pl.pallas_call(
        functools.partial(_ffn_body, d_ff=d_ff),
        grid=(rows // rb,),
        in_specs=[
            pl.BlockSpec((rb, d_model), lambda i: (i, 0)),
            lspec((d_model, 2 * d_ff)),
            lspec((d_ff, d_model)),
            lspec((1, d_model)),
            lspec((1, d_model)),
        ],
        out_specs=pl.BlockSpec((rb, d_model), lambda i: (i, 0)),
        out_shape=jax.ShapeDtypeStruct((rows, d_model), _F32),
        scratch_shapes=[pltpu.VMEM((rb, d_ff), _BF16)],
        compiler_params=pltpu.CompilerParams(
            dimension_semantics=("arbitrary",), vmem_limit_bytes=VMEM_LIMIT_BYTES),
        name=f"ffn_sample_l{layer}",
    )(h, wgu, wd, npre, npost)


def _rope_tables(pos):
    half = HEAD_DIM // 2
    inv = ROPE_BASE ** (-jnp.arange(half, dtype=_F32) / half)
    ang = pos[:, None] * inv[None, :]
    cos, sin = jnp.cos(ang), jnp.sin(ang)
    return jnp.concatenate([cos, cos], axis=-1), jnp.concatenate([-sin, sin], axis=-1)


def _rearranged_w_in(w_in):
    small0 = QKV_WIDTH + A_WIDTH
    small1 = small0 + 2 * A_HEADS
    small = jnp.pad(w_in[:, :, small0:small1], ((0, 0), (0, 0), (0, LANES - 2 * A_HEADS)))
    return jnp.concatenate([w_in[:, :, :small0], w_in[:, :, small1:], small], axis=-1).astype(_BF16)


def kernel(x_prompt, x_sample, state_conv, state_gdn, state_ret, w_in, conv_w, a_log, dt_bias, gdn_norm_w, ret_norm_w, w_out, norm_mix_pre, norm_mix_post, norm_ffn_pre, norm_ffn_post, w_gate_up, w_down):
    depth = w_in.shape[0]
    batch, seq, d_model = x_prompt.shape
    nb, seq_s, _ = x_sample.shape
    assert seq_s == 1

    win = _rearranged_w_in(w_in)
    wout = w_out.astype(_BF16)
    wgu = w_gate_up.astype(_BF16)
    wd = w_down.astype(_BF16)
    alog = jnp.pad(a_log, ((0, 0), (A_HEADS, LANES - 2 * A_HEADS)))[:, None, :]
    dtb = jnp.pad(dt_bias, ((0, 0), (A_HEADS, LANES - 2 * A_HEADS)))[:, None, :]
    gnw = gdn_norm_w[:, None, :]
    rnw = ret_norm_w[:, None, :]
    npre = norm_mix_pre[:, None, :]
    npost = norm_mix_post[:, None, :]
    fpre = norm_ffn_pre[:, None, :]
    fpost = norm_ffn_post[:, None, :]
    cos_p, sin_p = _rope_tables(jnp.arange(seq, dtype=_F32))
    cos_s, sin_s = _rope_tables(PAST_LEN + jnp.arange(seq_s, dtype=_F32))
    conv_state = state_conv.reshape(depth, nb, (CONV_W - 1) * QKV_WIDTH)

    hp = x_prompt
    hs = x_sample.reshape(nb, d_model)
    convs_p, gdns_p, rets_p, convs_s = [], [], [], []
    gdn_s = ret_s = None
    for l in range(depth):
        hp, conv_p, gdn_p, ret_p = _layer_prompt(hp, cos_p, sin_p, win, conv_w, alog, dtb, gnw, rnw,
                                                 wout, npre, npost, wgu, wd, fpre, fpost, l)
        convs_p.append(conv_p)
        gdns_p.append(gdn_p)
        rets_p.append(ret_p)
        hs, conv_s, gdn_s, ret_s = _mixer_sample(hs, cos_s, sin_s, win, conv_w, alog, dtb, gnw, rnw,
                                                 wout, npre, npost, conv_state, state_gdn, state_ret,
                                                 gdn_s, ret_s, l)
        hs = _ffn(hs, wgu, wd, fpre, fpost, l)
        convs_s.append(conv_s.reshape(nb, CONV_W - 1, QKV_WIDTH))
    return (hp, hs.reshape(nb, seq_s, d_model), jnp.stack(convs_p), jnp.stack(gdns_p),
            jnp.stack(rets_p), jnp.stack(convs_s), gdn_s, ret_s)
```

```python
import functools
import math

import jax
import jax.numpy as jnp
import numpy as np
from jax import lax
from jax.experimental import pallas as pl
from jax.experimental.pallas import tpu as pltpu

HEAD_DIM = 128
A_HEADS = 4
B_HEADS = 4
A_WIDTH = A_HEADS * HEAD_DIM
B_WIDTH = B_HEADS * HEAD_DIM
QKV_WIDTH = 3 * A_WIDTH
CONV_W = 4
CHUNK = 64
ROPE_BASE = 10000.0
EPS = 1e-6
L2_EPS = 1e-6
MASKED_LOG = -1e30
PAST_LEN = 16384
LANES = 128
SUBLANES = 8

REST_WIDTH = A_WIDTH + 4 * B_WIDTH + LANES
Z_OFF = 0
BQ_OFF = A_WIDTH
BK_OFF = BQ_OFF + B_WIDTH
BV_OFF = BK_OFF + B_WIDTH
BG_OFF = BV_OFF + B_WIDTH
SMALL_OFF = BG_OFF + B_WIDTH
IN_WIDTH_PADDED = QKV_WIDTH + REST_WIDTH

GROUP = 256
CHUNKS_PER_GROUP = GROUP // CHUNK
PROMPT_TIME_BLOCK = 256
FFN_ROW_BLOCK = 512
FFN_COL_BLOCK = 256
SAMPLE_BATCH_BLOCK = 8
VMEM_LIMIT_BYTES = 56 * 1024 * 1024

_BF16 = jnp.bfloat16
_F32 = jnp.float32


def _dot(a, b):
    return jnp.dot(a.astype(_BF16), b.astype(_BF16), preferred_element_type=_F32)


def _dot_nt(a, b):
    return lax.dot_general(a.astype(_BF16), b.astype(_BF16), (((1,), (1,)), ((), ())),
                           preferred_element_type=_F32)


def _dot_tn(a, b):
    return lax.dot_general(a.astype(_BF16), b.astype(_BF16), (((0,), (0,)), ((), ())),
                           preferred_element_type=_F32)


def _rmsnorm(x, w):
    return x * lax.rsqrt(jnp.mean(x * x, axis=-1, keepdims=True) + EPS) * w


def _silu(x):
    return x * jax.nn.sigmoid(x)


def _softplus(x):
    return jnp.maximum(x, 0.0) + jnp.log1p(jnp.exp(-jnp.abs(x)))


def _log_gamma(h):
    return math.log1p(-(2.0 ** (-5.0 - h)))


def _rope(x, cos_full, sin_signed):
    return x * cos_full + pltpu.roll(x, HEAD_DIM // 2, axis=1) * sin_signed


def _head_cols(base, h):
    return slice(base + h * HEAD_DIM, base + (h + 1) * HEAD_DIM)


def _gated_head_norm(o, w, gate):
    return o * lax.rsqrt(jnp.mean(o * o, axis=-1, keepdims=True) + EPS) * w * _silu(gate)


class _Filler:
    def __init__(self, pieces):
        self._pieces = list(pieces)
        self._next = 0

    def emit(self, count):
        for _ in range(count):
            if self._next < len(self._pieces):
                self._pieces[self._next]()
                self._next += 1

    def flush(self):
        self.emit(len(self._pieces))


def _fold_rows(m):
    out = m[0:CHUNK]
    for c in range(1, CHUNKS_PER_GROUP):
        out = out + m[c * CHUNK:(c + 1) * CHUNK]
    return out


def _unfold_rows(r, same_chunk):
    return jnp.where(same_chunk, jnp.concatenate([r] * CHUNKS_PER_GROUP, axis=0), 0.0)


def _unit_lower_inverses(a_folded, eye_folded, same_chunk, fill):
    xs = list(a_folded)
    ps = [eye_folded - x for x in xs]
    n = 1
    while n < CHUNK:
        for h in range(len(xs)):
            x_bd = _unfold_rows(xs[h], same_chunk).astype(_BF16)
            if n == 1:
                xs[h] = _dot(xs[h], x_bd)
            elif 2 * n < CHUNK:
                r = _dot(jnp.concatenate([xs[h], ps[h]], axis=0), x_bd)
                xs[h] = r[0:CHUNK]
                ps[h] = ps[h] + r[CHUNK:2 * CHUNK]
            else:
                ps[h] = ps[h] + _dot(ps[h], x_bd)
            fill.emit(h % 2)
        n *= 2
    return [_unfold_rows(p, same_chunk) for p in ps]


def _recurrences_one_group(g, live, fill, act_ref, pr_ref, small_ref, dec_ref, sg_ref, sr_ref, o_ref):
    rows = slice(g * GROUP, (g + 1) * GROUP)
    ri = lax.broadcasted_iota(jnp.int32, (GROUP, GROUP), 0)
    ci = lax.broadcasted_iota(jnp.int32, (GROUP, GROUP), 1)
    same_chunk = (ri // CHUNK) == (ci // CHUNK)
    tril = same_chunk & (ri >= ci)
    off_diag = ri != ci
    rf = lax.broadcasted_iota(jnp.int32, (CHUNK, GROUP), 0)
    cf = lax.broadcasted_iota(jnp.int32, (CHUNK, GROUP), 1)
    eye_folded = jnp.where(rf == cf % CHUNK, 1.0, 0.0).astype(_F32)
    pos = (lax.broadcasted_iota(jnp.int32, (GROUP, HEAD_DIM), 0) % CHUNK).astype(_F32)
    chunk_rows = [slice(c * CHUNK, (c + 1) * CHUNK) for c in range(CHUNKS_PER_GROUP)]

    qs, ks, gccs, a_folded, a_intra, uw_rhs, e_ins = [], [], [], [], [], [], []
    for h in range(A_HEADS):
        q = act_ref[rows, _head_cols(0, h)]
        k = act_ref[rows, _head_cols(A_WIDTH, h)]
        v = act_ref[rows, _head_cols(2 * A_WIDTH, h)]
        beta = jnp.broadcast_to(small_ref[0, rows, h:h + 1], (GROUP, HEAD_DIM))
        gc_col = small_ref[1, rows, A_HEADS + h:A_HEADS + h + 1]
        gcc = jnp.broadcast_to(gc_col, (GROUP, HEAD_DIM))
        decay = dec_ref[h]
        kb = k * beta
        e_in = jnp.exp(gcc)
        kq = _dot_nt(jnp.concatenate([kb, q], axis=0), k)
        a_folded.append(_fold_rows(jnp.where(off_diag, kq[0:GROUP] * decay, 0.0)))
        a_intra.append(kq[GROUP:2 * GROUP] * decay)
        uw_rhs.append(jnp.concatenate([v * beta, kb * e_in], axis=1).astype(_BF16))
        qs.append(q)
        ks.append(k)
        gccs.append(gcc)
        e_ins.append(e_in)
        fill.emit(1)

    r_qe, r_oloc, r_b, r_echunk = [], [], [], []
    for h in range(B_HEADS):
        lg = _log_gamma(h)
        q = pr_ref[rows, _head_cols(BQ_OFF, h)]
        k = pr_ref[rows, _head_cols(BK_OFF, h)]
        v = pr_ref[rows, _head_cols(BV_OFF, h)]
        decay = jnp.exp(jnp.where(tril, (ri - ci).astype(_F32) * lg, MASKED_LOG))
        vb16 = v.astype(_BF16)
        r_oloc.append(_dot(_dot_nt(q, k) * decay, vb16))
        r_qe.append((q * jnp.exp((pos + 1.0) * lg)).astype(_BF16))
        k_out = (k * jnp.exp((CHUNK - 1.0 - pos) * lg)).astype(_BF16)
        r_b.append([_dot_tn(k_out[cr], vb16[cr]) for cr in chunk_rows])
        r_echunk.append(math.exp(CHUNK * lg))

    t_inv = _unit_lower_inverses(a_folded, eye_folded, same_chunk, fill)

    g_lhs, g_oloc, g_b, g_elast = [], [], [], []
    for h in range(A_HEADS):
        uw = _dot(t_inv[h], uw_rhs[h])
        uw16 = uw.astype(_BF16)
        aiuw = _dot(a_intra[h], uw16)
        g_oloc.append(aiuw[:, 0:HEAD_DIM])
        q_eff = qs[h] * e_ins[h] - aiuw[:, HEAD_DIM:2 * HEAD_DIM]
        lhs, bs, elast = [], [], []
        for cr in chunk_rows:
            gcc_c = gccs[h][cr]
            g_last = gcc_c[CHUNK - 1:CHUNK, :]
            k_out = ks[h][cr] * jnp.exp(g_last - gcc_c)
            bg = _dot_tn(k_out, uw16[cr])
            bs.append(bg[:, 0:HEAD_DIM])
            lhs.append(jnp.concatenate([q_eff[cr], bg[:, HEAD_DIM:2 * HEAD_DIM]], axis=0).astype(_BF16))
            elast.append(jnp.exp(g_last))
        g_lhs.append(lhs)
        g_b.append(bs)
        g_elast.append(elast)
        fill.emit(1)

    g_state = [sg_ref[h] for h in range(A_HEADS)]
    r_state = [sr_ref[h] for h in range(B_HEADS)]
    for c, cr in enumerate(chunk_rows):
        out_rows = slice(g * GROUP + c * CHUNK, g * GROUP + (c + 1) * CHUNK)
        for h in range(A_HEADS):
            r = _dot(g_lhs[h][c], g_state[h])
            o_ref[out_rows, _head_cols(0, h)] = r[0:CHUNK] + g_oloc[h][cr]
            g_state[h] = g_state[h] * g_elast[h][c] + g_b[h][c] - r[CHUNK:CHUNK + HEAD_DIM]
        for h in range(B_HEADS):
            o_ref[out_rows, _head_cols(A_WIDTH, h)] = _dot(r_qe[h][cr], r_state[h]) + r_oloc[h][cr]
            r_state[h] = r_state[h] * r_echunk[h] + r_b[h][c]
        fill.emit(1)
    for h in range(A_HEADS):
        sg_ref[h] = jnp.where(live, g_state[h], sg_ref[h])
    for h in range(B_HEADS):
        sr_ref[h] = jnp.where(live, r_state[h], sr_ref[h])


def _ffn_rows(h, wgu_ref, wd_ref, npre, npost, act_ref, d_ff):
    hn = _rmsnorm(h, npre).astype(_BF16)
    for j in range(d_ff // FFN_COL_BLOCK):
        cg = slice(j * FFN_COL_BLOCK, (j + 1) * FFN_COL_BLOCK)
        cu = slice(d_ff + j * FFN_COL_BLOCK, d_ff + (j + 1) * FFN_COL_BLOCK)
        gate = jnp.dot(hn, wgu_ref[:, cg], preferred_element_type=_F32)
        up = jnp.dot(hn, wgu_ref[:, cu], preferred_element_type=_F32)
        act_ref[:, cg] = (_silu(gate) * up).astype(_BF16)
    f = jnp.dot(act_ref[...], wd_ref[...], preferred_element_type=_F32)
    return h + _rmsnorm(f, npost)


def _ffn_pieces(h_ref, hn_ref, wgu_ref, wd_ref, npost, gate_ref, act_ref, f_ref, y_ref, d_ff):
    d_model = f_ref.shape[1]
    k_split = (d_ff // FFN_COL_BLOCK + 1) // 2 * FFN_COL_BLOCK
    pieces = []

    def gate(j):
        def run():
            cg = slice(j * FFN_COL_BLOCK, (j + 1) * FFN_COL_BLOCK)
            gate_ref[j % 2] = _silu(jnp.dot(hn_ref[...], wgu_ref[:, cg], preferred_element_type=_F32))
        return run

    def up(j):
        def run():
            cg = slice(j * FFN_COL_BLOCK, (j + 1) * FFN_COL_BLOCK)
            cu = slice(d_ff + j * FFN_COL_BLOCK, d_ff + (j + 1) * FFN_COL_BLOCK)
            act_ref[:, cg] = (gate_ref[j % 2] * jnp.dot(hn_ref[...], wgu_ref[:, cu],
                                                        preferred_element_type=_F32)).astype(_BF16)
        return run

    def down(j, first):
        def run():
            cs = slice(j * FFN_COL_BLOCK, (j + 1) * FFN_COL_BLOCK)
            if first:
                f_ref[:, cs] = jnp.dot(act_ref[:, 0:k_split], wd_ref[0:k_split, cs],
                                       preferred_element_type=_F32)
            else:
                f_ref[:, cs] = f_ref[:, cs] + jnp.dot(act_ref[:, k_split:d_ff], wd_ref[k_split:d_ff, cs],
                                                      preferred_element_type=_F32)
        return run

    def finish():
        y_ref[...] = h_ref[...] + _rmsnorm(f_ref[...], npost)

    for j in range(d_ff // FFN_COL_BLOCK):
        pieces += [gate(j), up(j)]
    for j in range(d_model // FFN_COL_BLOCK):
        pieces += [down(j, True), down(j, False)]
    pieces.append(finish)
    return pieces


def _layer_prompt_body(x_ref, cos_ref, sin_ref, win_ref, convw_ref, alog_ref, dtb_ref, gnw_ref,
                       rnw_ref, wout_ref, npre_ref, npost_ref, wgu_ref, wd_ref, fpre_ref, fpost_ref,
                       y_ref, conv_ref, sg_ref, sr_ref,
                       pq_ref, pr_ref, act_ref, small_ref, gct_ref, dec_ref, o_ref, h_ref, hn_ref, gate_ref,
                       ffn_act_ref, f_ref, *, tb, nt, nblocks, d_ff):
    s = pl.program_id(0)
    live = s < nblocks
    t = lax.rem(jnp.minimum(s, nblocks - 1), nt)
    pad = SUBLANES

    @pl.when(s == 0)
    def _():
        h_ref[1] = jnp.zeros(h_ref.shape[1:], _F32)
        hn_ref[1] = jnp.zeros(hn_ref.shape[1:], _BF16)

    @pl.when((t == 0) & live)
    def _():
        pq_ref[0:pad, :] = jnp.zeros((pad, QKV_WIDTH), _F32)
        sg_ref[...] = jnp.zeros(sg_ref.shape, _F32)
        sr_ref[...] = jnp.zeros(sr_ref.shape, _F32)

    prev = lax.rem(s + 1, 2)
    fill = _Filler(_ffn_pieces(h_ref.at[prev], hn_ref.at[prev], wgu_ref, wd_ref, fpost_ref[...],
                               gate_ref, ffn_act_ref, f_ref, y_ref, d_ff))
    fill.emit(2)

    x = x_ref[...]
    hn = _rmsnorm(x, npre_ref[...]).astype(_BF16)

    def project(lo, hi):
        return jnp.dot(hn, win_ref[:, lo:hi], preferred_element_type=_F32)

    def project_rest(lo, hi):
        pr_ref[:, lo:hi] = project(QKV_WIDTH + lo, QKV_WIDTH + hi)

    ps = project(QKV_WIDTH + SMALL_OFF, IN_WIDTH_PADDED)
    fill.emit(1)
    beta_all = jax.nn.sigmoid(ps)
    gc = -jnp.exp(alog_ref[...]) * _softplus(ps + dtb_ref[...])
    row_in_chunk = lax.broadcasted_iota(jnp.int32, (tb, LANES), 0) % CHUNK
    shift = 1
    while shift < CHUNK:
        gc = gc + jnp.where(row_in_chunk >= shift, pltpu.roll(gc, shift, axis=0), 0.0)
        shift *= 2
    small_ref[0] = beta_all
    small_ref[1] = gc
    gct_ref[...] = gc.T

    half = B_WIDTH // 2
    cos_full = cos_ref[...]
    sin_signed = sin_ref[...]
    ri = lax.broadcasted_iota(jnp.int32, (GROUP, GROUP), 0)
    ci = lax.broadcasted_iota(jnp.int32, (GROUP, GROUP), 1)
    tril = ((ri // CHUNK) == (ci // CHUNK)) & (ri >= ci)
    for pair in range(B_HEADS // 2):
        project_rest(BQ_OFF + pair * half, BQ_OFF + (pair + 1) * half)
        fill.emit(1)
        project_rest(BK_OFF + pair * half, BK_OFF + (pair + 1) * half)
        fill.emit(1)
        for h in (2 * pair, 2 * pair + 1):
            gc_col = small_ref[1, :, A_HEADS + h:A_HEADS + h + 1]
            gcr = jnp.broadcast_to(gct_ref[A_HEADS + h:A_HEADS + h + 1, :], (GROUP, GROUP))
            diff = jnp.broadcast_to(gc_col, (GROUP, GROUP)) - gcr
            dec_ref[h] = jnp.exp(jnp.where(tril, diff, MASKED_LOG))
            cq = _head_cols(BQ_OFF, h)
            ck = _head_cols(BK_OFF, h)
            pr_ref[:, cq] = _rope(pr_ref[:, cq], cos_full, sin_signed)
            pr_ref[:, ck] = _rope(pr_ref[:, ck], cos_full, sin_signed) * (HEAD_DIM ** -0.5)

    for lo in range(0, QKV_WIDTH, FFN_COL_BLOCK):
        pq_ref[pad:pad + tb, lo:lo + FFN_COL_BLOCK] = project(lo, lo + FFN_COL_BLOCK)
        fill.emit(1)

    rest = [(lo, lo + FFN_COL_BLOCK) for lo in range(Z_OFF, BQ_OFF, FFN_COL_BLOCK)]
    rest += [(lo, lo + FFN_COL_BLOCK) for lo in range(BV_OFF, SMALL_OFF, FFN_COL_BLOCK)]
    for j in range(QKV_WIDTH // LANES):
        if j % 2 == 0 and rest:
            project_rest(*rest.pop(0))
        else:
            fill.emit(1)
        cs = slice(j * LANES, (j + 1) * LANES)
        acc = pq_ref[pad:pad + tb, cs] * convw_ref[3:4, cs]
        for i in range(CONV_W - 1):
            acc = acc + pq_ref[pad - 3 + i:pad - 3 + i + tb, cs] * convw_ref[i:i + 1, cs]
        a = _silu(acc)
        if j < 2 * A_HEADS:
            a = a * lax.rsqrt(jnp.sum(a * a, axis=-1, keepdims=True) + L2_EPS)
        if j < A_HEADS:
            a = a * (HEAD_DIM ** -0.5)
        act_ref[:, cs] = a
    assert not rest

    tail = pq_ref[pad + tb - 3:pad + tb, :]
    conv_ref[...] = tail
    pq_ref[pad - 3:pad, :] = tail

    for g in range(tb // GROUP):
        _recurrences_one_group(g, live, fill, act_ref, pr_ref, small_ref, dec_ref, sg_ref, sr_ref, o_ref)

    for h in range(A_HEADS):
        cs = _head_cols(0, h)
        o_ref[:, cs] = _gated_head_norm(o_ref[:, cs], gnw_ref[...], pr_ref[:, _head_cols(Z_OFF, h)])
    for h in range(B_HEADS):
        cs = _head_cols(A_WIDTH, h)
        o_ref[:, cs] = _gated_head_norm(o_ref[:, cs], rnw_ref[...], pr_ref[:, _head_cols(BG_OFF, h)])
    m = jnp.dot(o_ref[...].astype(_BF16), wout_ref[...], preferred_element_type=_F32)
    fill.emit(8)
    h_new = x_ref[...] + _rmsnorm(m, npost_ref[...])
    cur = lax.rem(s, 2)
    h_ref[cur] = h_new
    hn_ref[cur] = _rmsnorm(h_new, fpre_ref[...]).astype(_BF16)
    fill.flush()


def _layer_spec(shape, layer):
    zeros = (0,) * len(shape)
    return pl.BlockSpec((None,) + tuple(shape), lambda i: (layer,) + zeros)


def _layer_prompt(x, cos_full, sin_signed, win, conv_w, alog, dtb, gnw, rnw, wout, npre, npost,
                  wgu, wd, fpre, fpost, layer):
    batch, seq, d_model = x.shape
    d_ff = wd.shape[1]
    tb = min(PROMPT_TIME_BLOCK, seq)
    assert seq % tb == 0 and tb == GROUP and d_ff % FFN_COL_BLOCK == 0
    nt = seq // tb
    nblocks = batch * nt
    lspec = functools.partial(_layer_spec, layer=layer)

    def mixer_block(s):
        return jnp.minimum(s, nblocks - 1)

    def ffn_block(s):
        return jnp.maximum(s - 1, 0)

    in_specs = [
        pl.BlockSpec((None, tb, d_model), lambda s: (mixer_block(s) // nt, mixer_block(s) % nt, 0)),
        pl.BlockSpec((tb, HEAD_DIM), lambda s: (mixer_block(s) % nt, 0)),
        pl.BlockSpec((tb, HEAD_DIM), lambda s: (mixer_block(s) % nt, 0)),
        lspec((d_model, IN_WIDTH_PADDED)),
        lspec((CONV_W, QKV_WIDTH)),
        lspec((1, LANES)),
        lspec((1, LANES)),
        lspec((1, HEAD_DIM)),
        lspec((1, HEAD_DIM)),
        lspec((A_WIDTH + B_WIDTH, d_model)),
        lspec((1, d_model)),
        lspec((1, d_model)),
        lspec((d_model, 2 * d_ff)),
        lspec((d_ff, d_model)),
        lspec((1, d_model)),
        lspec((1, d_model)),
    ]
    out_specs = [
        pl.BlockSpec((None, tb, d_model), lambda s: (ffn_block(s) // nt, ffn_block(s) % nt, 0)),
        pl.BlockSpec((None, CONV_W - 1, QKV_WIDTH), lambda s: (mixer_block(s) // nt, 0, 0)),
        pl.BlockSpec((None, A_HEADS, HEAD_DIM, HEAD_DIM), lambda s: (mixer_block(s) // nt, 0, 0, 0)),
        pl.BlockSpec((None, B_HEADS, HEAD_DIM, HEAD_DIM), lambda s: (mixer_block(s) // nt, 0, 0, 0)),
    ]
    out_shape = [
        jax.ShapeDtypeStruct((batch, seq, d_model), _F32),
        jax.ShapeDtypeStruct((batch, CONV_W - 1, QKV_WIDTH), _F32),
        jax.ShapeDtypeStruct((batch, A_HEADS, HEAD_DIM, HEAD_DIM), _F32),
        jax.ShapeDtypeStruct((batch, B_HEADS, HEAD_DIM, HEAD_DIM), _F32),
    ]
    scratch = [
        pltpu.VMEM((tb + SUBLANES, QKV_WIDTH), _F32),
        pltpu.VMEM((tb, REST_WIDTH), _F32),
        pltpu.VMEM((tb, QKV_WIDTH), _F32),
        pltpu.VMEM((2, tb, LANES), _F32),
        pltpu.VMEM((LANES, tb), _F32),
        pltpu.VMEM((A_HEADS, GROUP, GROUP), _F32),
        pltpu.VMEM((tb, A_WIDTH + B_WIDTH), _F32),
        pltpu.VMEM((2, tb, d_model), _F32),
        pltpu.VMEM((2, tb, d_model), _BF16),
        pltpu.VMEM((2, tb, FFN_COL_BLOCK), _F32),
        pltpu.VMEM((tb, d_ff), _BF16),
        pltpu.VMEM((tb, d_model), _F32),
    ]
    return pl.pallas_call(
        functools.partial(_layer_prompt_body, tb=tb, nt=nt, nblocks=nblocks, d_ff=d_ff),
        grid=(nblocks + 1,), in_specs=in_specs, out_specs=out_specs, out_shape=out_shape,
        scratch_shapes=scratch,
        compiler_params=pltpu.CompilerParams(
            dimension_semantics=("arbitrary",), vmem_limit_bytes=VMEM_LIMIT_BYTES),
        name=f"layer_prompt_l{layer}",
    )(x, cos_full, sin_signed, win, conv_w, alog, dtb, gnw, rnw, wout, npre, npost, wgu, wd, fpre, fpost)


def _column_of_row(row):
    return jnp.broadcast_to(row, (HEAD_DIM, HEAD_DIM)).T


def _mixer_sample_body(*refs, bb, aliased):
    (x_ref, cos_ref, sin_ref, win_ref, convw_ref, alog_ref, dtb_ref, gnw_ref, rnw_ref, wout_ref,
     npre_ref, npost_ref, convs_ref, sgi_ref, sri_ref) = refs[:15]
    refs = refs[15 + (2 if aliased else 0):]
    (h_ref, convn_ref, sgo_ref, sro_ref,
     act_ref, pr_ref, eg_ref, bt_ref, qka_ref, qkb_ref, o_ref) = refs
    i = pl.program_id(0)
    n = pl.num_programs(0)
    nb = x_ref.shape[0]

    @pl.when(i == 0)
    def _():
        x = x_ref[...]
        hn = _rmsnorm(x, npre_ref[...]).astype(_BF16)
        pq = jnp.dot(hn, win_ref[:, 0:QKV_WIDTH], preferred_element_type=_F32)
        pr_ref[...] = jnp.dot(hn, win_ref[:, QKV_WIDTH:IN_WIDTH_PADDED], preferred_element_type=_F32)
        for j in range(QKV_WIDTH // LANES):
            cs = slice(j * LANES, (j + 1) * LANES)
            new = pq[:, cs]
            acc = new * convw_ref[3:4, cs]
            for r in range(CONV_W - 1):
                hist = convs_ref[:, r * QKV_WIDTH + j * LANES:r * QKV_WIDTH + (j + 1) * LANES]
                acc = acc + hist * convw_ref[r:r + 1, cs]
                if r > 0:
                    convn_ref[:, (r - 1) * QKV_WIDTH + j * LANES:(r - 1) * QKV_WIDTH + (j + 1) * LANES] = hist
            convn_ref[:, (CONV_W - 2) * QKV_WIDTH + j * LANES:(CONV_W - 2) * QKV_WIDTH + (j + 1) * LANES] = new
            a = _silu(acc)
            if j < 2 * A_HEADS:
                a = a * lax.rsqrt(jnp.sum(a * a, axis=-1, keepdims=True) + L2_EPS)
            if j < A_HEADS:
                a = a * (HEAD_DIM ** -0.5)
            act_ref[:, cs] = a
        ps = pr_ref[:, SMALL_OFF:SMALL_OFF + LANES]
        beta_all = jax.nn.sigmoid(ps)
        eg_all = jnp.exp(-jnp.exp(alog_ref[...]) * _softplus(ps + dtb_ref[...]))
        cos_full = cos_ref[...]
        sin_signed = sin_ref[...]
        for h in range(A_HEADS):
            bt_ref[h] = jnp.broadcast_to(beta_all[:, h:h + 1], (nb, LANES))
            eg_ref[h] = jnp.broadcast_to(eg_all[:, A_HEADS + h:A_HEADS + h + 1], (nb, LANES))
            qk = jnp.sum(act_ref[:, _head_cols(0, h)] * act_ref[:, _head_cols(A_WIDTH, h)],
                         axis=-1, keepdims=True)
            qka_ref[h] = jnp.broadcast_to(qk, (nb, LANES))
        for h in range(B_HEADS):
            cq = _head_cols(BQ_OFF, h)
            ck = _head_cols(BK_OFF, h)
            q = _rope(pr_ref[:, cq], cos_full, sin_signed)
            k = _rope(pr_ref[:, ck], cos_full, sin_signed) * (HEAD_DIM ** -0.5)
            pr_ref[:, cq] = q
            pr_ref[:, ck] = k
            qkb_ref[h] = jnp.broadcast_to(jnp.sum(q * k, axis=-1, keepdims=True), (nb, LANES))

    base = pl.multiple_of(i * bb, SUBLANES)
    rows = pl.ds(base, bb)
    for h in range(A_HEADS):
        q8 = act_ref[rows, _head_cols(0, h)]
        k8 = act_ref[rows, _head_cols(A_WIDTH, h)]
        v8 = act_ref[rows, _head_cols(2 * A_WIDTH, h)]
        eg8 = eg_ref[h, rows, :]
        bt8 = bt_ref[h, rows, :]
        qk8 = qka_ref[h, rows, :]
        kq16 = jnp.concatenate([k8, q8], axis=0).astype(_BF16)
        kqs = [jnp.dot(kq16, sgi_ref[bl, h].astype(_BF16), preferred_element_type=_F32)
               for bl in range(bb)]
        v_news, outs = [], []
        for bl in range(bb):
            r = slice(bl, bl + 1)
            v_new = bt8[r] * (v8[r] - eg8[r] * kqs[bl][bl:bl + 1])
            outs.append(eg8[r] * kqs[bl][bb + bl:bb + bl + 1] + qk8[r] * v_new)
            v_news.append(v_new)
        o_ref[rows, _head_cols(0, h)] = jnp.concatenate(outs, axis=0)
        for bl in range(bb):
            r = slice(bl, bl + 1)
            sgo_ref[bl, h] = sgi_ref[bl, h] * eg8[r] + _column_of_row(k8[r]) * v_news[bl]
    for h in range(B_HEADS):
        gamma = math.exp(_log_gamma(h))
        q8 = pr_ref[rows, _head_cols(BQ_OFF, h)]
        k8 = pr_ref[rows, _head_cols(BK_OFF, h)]
        v8 = pr_ref[rows, _head_cols(BV_OFF, h)]
        qk8 = qkb_ref[h, rows, :]
        q16 = q8.astype(_BF16)
        qss = [jnp.dot(q16, sri_ref[bl, h].astype(_BF16), preferred_element_type=_F32)
               for bl in range(bb)]
        o_ref[rows, _head_cols(A_WIDTH, h)] = gamma * jnp.concatenate(
            [qss[bl][bl:bl + 1] for bl in range(bb)], axis=0) + qk8 * v8
        for bl in range(bb):
            r = slice(bl, bl + 1)
            sro_ref[bl, h] = sri_ref[bl, h] * gamma + _column_of_row(k8[r]) * v8[r]

    @pl.when(i == n - 1)
    def _():
        for h in range(A_HEADS):
            cs = _head_cols(0, h)
            o_ref[:, cs] = _gated_head_norm(o_ref[:, cs], gnw_ref[...], pr_ref[:, _head_cols(Z_OFF, h)])
        for h in range(B_HEADS):
            cs = _head_cols(A_WIDTH, h)
            o_ref[:, cs] = _gated_head_norm(o_ref[:, cs], rnw_ref[...], pr_ref[:, _head_cols(BG_OFF, h)])
        m = jnp.dot(o_ref[...].astype(_BF16), wout_ref[...], preferred_element_type=_F32)
        h_ref[...] = x_ref[...] + _rmsnorm(m, npost_ref[...])


def _mixer_sample(x, cos_full, sin_signed, win, conv_w, alog, dtb, gnw, rnw, wout, npre, npost,
                  conv_state, state_gdn, state_ret, prev_gdn, prev_ret, layer):
    nb, d_model = x.shape
    depth = state_gdn.shape[0]
    bb = min(SAMPLE_BATCH_BLOCK, nb)
    assert nb % bb == 0
    aliased = prev_gdn is not None
    lspec = functools.partial(_layer_spec, layer=layer)
    full = lambda shape: pl.BlockSpec(tuple(shape), lambda i: (0,) * len(shape))
    state_spec = lambda heads: pl.BlockSpec((None, bb, heads, HEAD_DIM, HEAD_DIM),
                                            lambda i: (layer, i, 0, 0, 0))
    in_specs = [
        full((nb, d_model)),
        full((1, HEAD_DIM)),
        full((1, HEAD_DIM)),
        lspec((d_model, IN_WIDTH_PADDED)),
        lspec((CONV_W, QKV_WIDTH)),
        lspec((1, LANES)),
        lspec((1, LANES)),
        lspec((1, HEAD_DIM)),
        lspec((1, HEAD_DIM)),
        lspec((A_WIDTH + B_WIDTH, d_model)),
        lspec((1, d_model)),
        lspec((1, d_model)),
        lspec((nb, (CONV_W - 1) * QKV_WIDTH)),
        state_spec(A_HEADS),
        state_spec(B_HEADS),
    ]
    args = [x, cos_full, sin_signed, win, conv_w, alog, dtb, gnw, rnw, wout, npre, npost,
            conv_state, state_gdn, state_ret]
    aliases = {}
    if aliased:
        in_specs += [pl.BlockSpec(memory_space=pl.ANY), pl.BlockSpec(memory_space=pl.ANY)]
        aliases = {len(args): 2, len(args) + 1: 3}
        args += [prev_gdn, prev_ret]
    out_specs = [
        full((nb, d_model)),
        full((nb, (CONV_W - 1) * QKV_WIDTH)),
        state_spec(A_HEADS),
        state_spec(B_HEADS),
    ]
    out_shape = [
        jax.ShapeDtypeStruct((nb, d_model), _F32),
        jax.ShapeDtypeStruct((nb, (CONV_W - 1) * QKV_WIDTH), _F32),
        jax.ShapeDtypeStruct((depth, nb, A_HEADS, HEAD_DIM, HEAD_DIM), _F32),
        jax.ShapeDtypeStruct((depth, nb, B_HEADS, HEAD_DIM, HEAD_DIM), _F32),
    ]
    scratch = [
        pltpu.VMEM((nb, QKV_WIDTH), _F32),
        pltpu.VMEM((nb, REST_WIDTH), _F32),
        pltpu.VMEM((A_HEADS, nb, LANES), _F32),
        pltpu.VMEM((A_HEADS, nb, LANES), _F32),
        pltpu.VMEM((A_HEADS, nb, LANES), _F32),
        pltpu.VMEM((B_HEADS, nb, LANES), _F32),
        pltpu.VMEM((nb, A_WIDTH + B_WIDTH), _F32),
    ]
    return pl.pallas_call(
        functools.partial(_mixer_sample_body, bb=bb, aliased=aliased),
        grid=(nb // bb,), in_specs=in_specs, out_specs=out_specs, out_shape=out_shape,
        scratch_shapes=scratch, input_output_aliases=aliases,
        compiler_params=pltpu.CompilerParams(
            dimension_semantics=("arbitrary",), vmem_limit_bytes=VMEM_LIMIT_BYTES),
        name=f"mixer_sample_l{layer}",
    )(*args)


def _ffn_body(h_ref, wgu_ref, wd_ref, npre_ref, npost_ref, y_ref, act_ref, *, d_ff):
    y_ref[...] = _ffn_rows(h_ref[...], wgu_ref, wd_ref, npre_ref[...], npost_ref[...], act_ref, d_ff)


def _ffn(h, wgu, wd, npre, npost, layer):
    rows, d_model = h.shape
    d_ff = wd.shape[1]
    assert d_ff % FFN_COL_BLOCK == 0
    rb = min(FFN_ROW_BLOCK, rows)
    assert rows % rb == 0
    lspec = functools.partial(_layer_spec, layer=layer)
    return pl.pallas_call(
        functools.partial(_ffn_body, d_ff=d_ff),
        grid=(rows // rb,),
        in_specs=[
            pl.BlockSpec((rb, d_model), lambda i: (i, 0)),
            lspec((d_model, 2 * d_ff)),
            lspec((d_ff, d_model)),
            lspec((1, d_model)),
            lspec((1, d_model)),
        ],
        out_specs=pl.BlockSpec((rb, d_model), lambda i: (i, 0)),
        out_shape=jax.ShapeDtypeStruct((rows, d_model), _F32),
        scratch_shapes=[pltpu.VMEM((rb, d_ff), _BF16)],
        compiler_params=pltpu.CompilerParams(
            dimension_semantics=("arbitrary",), vmem_limit_bytes=VMEM_LIMIT_BYTES),
        name=f"ffn_sample_l{layer}",
    )(h, wgu, wd, npre, npost)


def _rope_tables(positions):
    half = HEAD_DIM // 2
    inv = ROPE_BASE ** (-np.arange(half, dtype=np.float64) / half)
    ang = np.asarray(positions, dtype=np.float64)[:, None] * inv[None, :]
    cos, sin = np.cos(ang), np.sin(ang)
    return (jnp.asarray(np.concatenate([cos, cos], axis=-1), dtype=_F32),
            jnp.asarray(np.concatenate([-sin, sin], axis=-1), dtype=_F32))


def _rearranged_w_in(w_in):
    small0 = QKV_WIDTH + A_WIDTH
    small1 = small0 + 2 * A_HEADS
    small = jnp.pad(w_in[:, :, small0:small1], ((0, 0), (0, 0), (0, LANES - 2 * A_HEADS)))
    return jnp.concatenate([w_in[:, :, :small0], w_in[:, :, small1:], small], axis=-1).astype(_BF16)


def kernel(x_prompt, x_sample, state_conv, state_gdn, state_ret, w_in, conv_w, a_log, dt_bias, gdn_norm_w, ret_norm_w, w_out, norm_mix_pre, norm_mix_post, norm_ffn_pre, norm_ffn_post, w_gate_up, w_down):
    depth = w_in.shape[0]
    batch, seq, d_model = x_prompt.shape
    nb, seq_s, _ = x_sample.shape
    assert seq_s == 1

    win = _rearranged_w_in(w_in)
    wout = w_out.astype(_BF16)
    wgu = w_gate_up.astype(_BF16)
    wd = w_down.astype(_BF16)
    alog = jnp.pad(a_log, ((0, 0), (A_HEADS, LANES - 2 * A_HEADS)))[:, None, :]
    dtb = jnp.pad(dt_bias, ((0, 0), (A_HEADS, LANES - 2 * A_HEADS)))[:, None, :]
    gnw = gdn_norm_w[:, None, :]
    rnw = ret_norm_w[:, None, :]
    npre = norm_mix_pre[:, None, :]
    npost = norm_mix_post[:, None, :]
    fpre = norm_ffn_pre[:, None, :]
    fpost = norm_ffn_post[:, None, :]
    cos_p, sin_p = _rope_tables(np.arange(seq))
    cos_s, sin_s = _rope_tables(PAST_LEN + np.arange(seq_s))
    conv_state = state_conv.reshape(depth, nb, (CONV_W - 1) * QKV_WIDTH)

    hp = x_prompt
    hs = x_sample.reshape(nb, d_model)
    convs_p, gdns_p, rets_p, convs_s = [], [], [], []
    gdn_s = ret_s = None
    for l in range(depth):
        hp, conv_p, gdn_p, ret_p = _layer_prompt(hp, cos_p, sin_p, win, conv_w, alog, dtb, gnw, rnw,
                                                 wout, npre, npost, wgu, wd, fpre, fpost, l)
        convs_p.append(conv_p)
        gdns_p.append(gdn_p)
        rets_p.append(ret_p)
        hs, conv_s, gdn_s, ret_s = _mixer_sample(hs, cos_s, sin_s, win, conv_w, alog, dtb, gnw, rnw,
                                                 wout, npre, npost, conv_state, state_gdn, state_ret,
                                                 gdn_s, ret_s, l)
        hs = _ffn(hs, wgu, wd, fpre, fpost, l)
        convs_s.append(conv_s.reshape(nb, CONV_W - 1, QKV_WIDTH))
    return (hp, hs.reshape(nb, seq_s, d_model), jnp.stack(convs_p), jnp.stack(gdns_p),
            jnp.stack(rets_p), jnp.stack(convs_s), gdn_s, ret_s)
```

```python
import functools
import math

import jax
import jax.numpy as jnp
import numpy as np
from jax import lax
from jax.experimental import pallas as pl
from jax.experimental.pallas import tpu as pltpu

HEAD_DIM = 128
A_HEADS = 4
B_HEADS = 4
A_WIDTH = A_HEADS * HEAD_DIM
B_WIDTH = B_HEADS * HEAD_DIM
QKV_WIDTH = 3 * A_WIDTH
CONV_W = 4
CHUNK = 64
ROPE_BASE = 10000.0
EPS = 1e-6
L2_EPS = 1e-6
MASKED_LOG = -1e30
PAST_LEN = 16384
LANES = 128
SUBLANES = 8

REST_WIDTH = A_WIDTH + 4 * B_WIDTH + LANES
Z_OFF = 0
BQ_OFF = A_WIDTH
BK_OFF = BQ_OFF + B_WIDTH
BV_OFF = BK_OFF + B_WIDTH
BG_OFF = BV_OFF + B_WIDTH
SMALL_OFF = BG_OFF + B_WIDTH
IN_WIDTH_PADDED = QKV_WIDTH + REST_WIDTH

GROUP = 256
CHUNKS_PER_GROUP = GROUP // CHUNK
PROMPT_TIME_BLOCK = 256
FFN_COL_BLOCK = 256
WEIGHT_ROW_BLOCK = 256
SAMPLE_SCALARS = 4
VMEM_LIMIT_BYTES = 56 * 1024 * 1024

_BF16 = jnp.bfloat16
_F32 = jnp.float32


def _dot(a, b):
    return jnp.dot(a.astype(_BF16), b.astype(_BF16), preferred_element_type=_F32)


def _dot_nt(a, b):
    return lax.dot_general(a.astype(_BF16), b.astype(_BF16), (((1,), (1,)), ((), ())),
                           preferred_element_type=_F32)


def _dot_tn(a, b):
    return lax.dot_general(a.astype(_BF16), b.astype(_BF16), (((0,), (0,)), ((), ())),
                           preferred_element_type=_F32)


def _rmsnorm(x, w):
    return x * lax.rsqrt(jnp.mean(x * x, axis=-1, keepdims=True) + EPS) * w


def _silu(x):
    return x * jax.nn.sigmoid(x)


def _softplus(x):
    return jnp.maximum(x, 0.0) + jnp.log1p(jnp.exp(-jnp.abs(x)))


def _log_gamma(h):
    return math.log1p(-(2.0 ** (-5.0 - h)))


def _rope(x, cos_full, sin_signed):
    return x * cos_full + pltpu.roll(x, HEAD_DIM // 2, axis=1) * sin_signed


def _head_cols(base, h):
    return slice(base + h * HEAD_DIM, base + (h + 1) * HEAD_DIM)


def _gated_head_norm(o, w, gate):
    return o * lax.rsqrt(jnp.mean(o * o, axis=-1, keepdims=True) + EPS) * w * _silu(gate)


class _Filler:
    def __init__(self, pieces):
        self._pieces = list(pieces)
        self._next = 0

    def emit(self, count):
        for _ in range(count):
            if self._next < len(self._pieces):
                self._pieces[self._next]()
                self._next += 1

    def flush(self):
        self.emit(len(self._pieces))


def _fold_rows(m):
    out = m[0:CHUNK]
    for c in range(1, CHUNKS_PER_GROUP):
        out = out + m[c * CHUNK:(c + 1) * CHUNK]
    return out


def _unfold_rows(r, same_chunk):
    return jnp.where(same_chunk, jnp.concatenate([r] * CHUNKS_PER_GROUP, axis=0), 0.0)


def _unit_lower_inverses(a_folded, eye_folded, same_chunk, fill):
    xs = list(a_folded)
    ps = [eye_folded - x for x in xs]
    n = 1
    while n < CHUNK:
        for h in range(len(xs)):
            x_bd = _unfold_rows(xs[h], same_chunk).astype(_BF16)
            if n == 1:
                xs[h] = _dot(xs[h], x_bd)
            elif 2 * n < CHUNK:
                r = _dot(jnp.concatenate([xs[h], ps[h]], axis=0), x_bd)
                xs[h] = r[0:CHUNK]
                ps[h] = ps[h] + r[CHUNK:2 * CHUNK]
            else:
                ps[h] = ps[h] + _dot(ps[h], x_bd)
            fill.emit(h % 2)
        n *= 2
    return [_unfold_rows(p, same_chunk) for p in ps]


def _recurrences_one_group(g, live, fill, act_ref, pr_ref, small_ref, dec_ref, sg_ref, sr_ref, o_ref):
    rows = slice(g * GROUP, (g + 1) * GROUP)
    ri = lax.broadcasted_iota(jnp.int32, (GROUP, GROUP), 0)
    ci = lax.broadcasted_iota(jnp.int32, (GROUP, GROUP), 1)
    same_chunk = (ri // CHUNK) == (ci // CHUNK)
    tril = same_chunk & (ri >= ci)
    off_diag = ri != ci
    rf = lax.broadcasted_iota(jnp.int32, (CHUNK, GROUP), 0)
    cf = lax.broadcasted_iota(jnp.int32, (CHUNK, GROUP), 1)
    eye_folded = jnp.where(rf == cf % CHUNK, 1.0, 0.0).astype(_F32)
    pos = (lax.broadcasted_iota(jnp.int32, (GROUP, HEAD_DIM), 0) % CHUNK).astype(_F32)
    chunk_rows = [slice(c * CHUNK, (c + 1) * CHUNK) for c in range(CHUNKS_PER_GROUP)]

    qs, ks, gccs, a_folded, a_intra, uw_rhs, e_ins = [], [], [], [], [], [], []
    for h in range(A_HEADS):
        q = act_ref[rows, _head_cols(0, h)]
        k = act_ref[rows, _head_cols(A_WIDTH, h)]
        v = act_ref[rows, _head_cols(2 * A_WIDTH, h)]
        beta = jnp.broadcast_to(small_ref[0, rows, h:h + 1], (GROUP, HEAD_DIM))
        gc_col = small_ref[1, rows, A_HEADS + h:A_HEADS + h + 1]
        gcc = jnp.broadcast_to(gc_col, (GROUP, HEAD_DIM))
        decay = dec_ref[h]
        kb = k * beta
        e_in = jnp.exp(gcc)
        kq = _dot_nt(jnp.concatenate([kb, q], axis=0), k)
        a_folded.append(_fold_rows(jnp.where(off_diag, kq[0:GROUP] * decay, 0.0)))
        a_intra.append(kq[GROUP:2 * GROUP] * decay)
        uw_rhs.append(jnp.concatenate([v * beta, kb * e_in], axis=1).astype(_BF16))
        qs.append(q)
        ks.append(k)
        gccs.append(gcc)
        e_ins.append(e_in)
        fill.emit(1)

    r_qe, r_oloc, r_b, r_echunk = [], [], [], []
    for h in range(B_HEADS):
        lg = _log_gamma(h)
        q = pr_ref[rows, _head_cols(BQ_OFF, h)]
        k = pr_ref[rows, _head_cols(BK_OFF, h)]
        v = pr_ref[rows, _head_cols(BV_OFF, h)]
        decay = jnp.exp(jnp.where(tril, (ri - ci).astype(_F32) * lg, MASKED_LOG))
        vb16 = v.astype(_BF16)
        r_oloc.append(_dot(_dot_nt(q, k) * decay, vb16))
        r_qe.append((q * jnp.exp((pos + 1.0) * lg)).astype(_BF16))
        k_out = (k * jnp.exp((CHUNK - 1.0 - pos) * lg)).astype(_BF16)
        r_b.append([_dot_tn(k_out[cr], vb16[cr]) for cr in chunk_rows])
        r_echunk.append(math.exp(CHUNK * lg))

    t_inv = _unit_lower_inverses(a_folded, eye_folded, same_chunk, fill)

    g_lhs, g_oloc, g_b, g_elast = [], [], [], []
    for h in range(A_HEADS):
        uw = _dot(t_inv[h], uw_rhs[h])
        uw16 = uw.astype(_BF16)
        aiuw = _dot(a_intra[h], uw16)
        g_oloc.append(aiuw[:, 0:HEAD_DIM])
        q_eff = qs[h] * e_ins[h] - aiuw[:, HEAD_DIM:2 * HEAD_DIM]
        lhs, bs, elast = [], [], []
        for cr in chunk_rows:
            gcc_c = gccs[h][cr]
            g_last = gcc_c[CHUNK - 1:CHUNK, :]
            k_out = ks[h][cr] * jnp.exp(g_last - gcc_c)
            bg = _dot_tn(k_out, uw16[cr])
            bs.append(bg[:, 0:HEAD_DIM])
            lhs.append(jnp.concatenate([q_eff[cr], bg[:, HEAD_DIM:2 * HEAD_DIM]], axis=0).astype(_BF16))
            elast.append(jnp.exp(g_last))
        g_lhs.append(lhs)
        g_b.append(bs)
        g_elast.append(elast)
        fill.emit(1)

    g_state = [sg_ref[h] for h in range(A_HEADS)]
    r_state = [sr_ref[h] for h in range(B_HEADS)]
    for c, cr in enumerate(chunk_rows):
        out_rows = slice(g * GROUP + c * CHUNK, g * GROUP + (c + 1) * CHUNK)
        for h in range(A_HEADS):
            r = _dot(g_lhs[h][c], g_state[h])
            o_ref[out_rows, _head_cols(0, h)] = r[0:CHUNK] + g_oloc[h][cr]
            g_state[h] = g_state[h] * g_elast[h][c] + g_b[h][c] - r[CHUNK:CHUNK + HEAD_DIM]
        for h in range(B_HEADS):
            o_ref[out_rows, _head_cols(A_WIDTH, h)] = _dot(r_qe[h][cr], r_state[h]) + r_oloc[h][cr]
            r_state[h] = r_state[h] * r_echunk[h] + r_b[h][c]
        fill.emit(1)
    for h in range(A_HEADS):
        sg_ref[h] = jnp.where(live, g_state[h], sg_ref[h])
    for h in range(B_HEADS):
        sr_ref[h] = jnp.where(live, r_state[h], sr_ref[h])


def _ffn_rows(h, wgu_ref, wd_ref, npre, npost, act_ref, d_ff):
    hn = _rmsnorm(h, npre).astype(_BF16)
    for j in range(d_ff // FFN_COL_BLOCK):
        cg = slice(j * FFN_COL_BLOCK, (j + 1) * FFN_COL_BLOCK)
        cu = slice(d_ff + j * FFN_COL_BLOCK, d_ff + (j + 1) * FFN_COL_BLOCK)
        gate = jnp.dot(hn, wgu_ref[:, cg], preferred_element_type=_F32)
        up = jnp.dot(hn, wgu_ref[:, cu], preferred_element_type=_F32)
        act_ref[:, cg] = (_silu(gate) * up).astype(_BF16)
    f = jnp.dot(act_ref[...], wd_ref[...], preferred_element_type=_F32)
    return h + _rmsnorm(f, npost)


def _ffn_pieces(h_ref, hn_ref, wgu_ref, wd_ref, npost, gate_ref, act_ref, f_ref, y_ref, d_ff):
    d_model = f_ref.shape[1]
    k_split = (d_ff // FFN_COL_BLOCK + 1) // 2 * FFN_COL_BLOCK
    pieces = []

    def gate(j):
        def run():
            cg = slice(j * FFN_COL_BLOCK, (j + 1) * FFN_COL_BLOCK)
            gate_ref[j % 2] = _silu(jnp.dot(hn_ref[...], wgu_ref[:, cg], preferred_element_type=_F32))
        return run

    def up(j):
        def run():
            cg = slice(j * FFN_COL_BLOCK, (j + 1) * FFN_COL_BLOCK)
            cu = slice(d_ff + j * FFN_COL_BLOCK, d_ff + (j + 1) * FFN_COL_BLOCK)
            act_ref[:, cg] = (gate_ref[j % 2] * jnp.dot(hn_ref[...], wgu_ref[:, cu],
                                                        preferred_element_type=_F32)).astype(_BF16)
        return run

    def down(j, first):
        def run():
            cs = slice(j * FFN_COL_BLOCK, (j + 1) * FFN_COL_BLOCK)
            if first:
                f_ref[:, cs] = jnp.dot(act_ref[:, 0:k_split], wd_ref[0:k_split, cs],
                                       preferred_element_type=_F32)
            else:
                f_ref[:, cs] = f_ref[:, cs] + jnp.dot(act_ref[:, k_split:d_ff], wd_ref[k_split:d_ff, cs],
                                                      preferred_element_type=_F32)
        return run

    def finish():
        y_ref[...] = h_ref[...] + _rmsnorm(f_ref[...], npost)

    for j in range(d_ff // FFN_COL_BLOCK):
        pieces += [gate(j), up(j)]
    for j in range(d_model // FFN_COL_BLOCK):
        pieces += [down(j, True), down(j, False)]
    pieces.append(finish)
    return pieces


def _column_of_row(row):
    return jnp.broadcast_to(row, (HEAD_DIM, HEAD_DIM)).T


def _sample_state_units(first_seq, per_step, sq_ref, sk_ref, sv_ref, sc_ref, sgi_ref, sri_ref,
                        so_ref, sgo_ref, sro_ref):
    base = pl.multiple_of((first_seq // SUBLANES) * SUBLANES, SUBLANES)
    rows = pl.ds(base, SUBLANES)
    row_id = lax.broadcasted_iota(jnp.int32, (SUBLANES, HEAD_DIM), 0)
    picks = [row_id == (first_seq - base + j) for j in range(per_step)]

    def pick(block, j):
        return jnp.sum(jnp.where(picks[j], block, 0.0), axis=0, keepdims=True)

    def put(cols, outs):
        block = so_ref[rows, cols]
        for j in range(per_step):
            block = jnp.where(picks[j], outs[j], block)
        so_ref[rows, cols] = block

    def deltanet(h):
        def run():
            cols = _head_cols(0, h)
            q8, k8, v8 = sq_ref[rows, cols], sk_ref[rows, cols], sv_ref[rows, cols]
            eg8 = sc_ref[rows, _head_cols(0, h)]
            bt8 = sc_ref[rows, _head_cols(A_WIDTH, h)]
            qk8 = sc_ref[rows, _head_cols(2 * A_WIDTH, h)]
            kq16 = jnp.concatenate([k8, q8], axis=0).astype(_BF16)
            kqs = [jnp.dot(kq16, sgi_ref[j, h].astype(_BF16), preferred_element_type=_F32)
                   for j in range(per_step)]
            outs = []
            for j in range(per_step):
                eg = pick(eg8, j)
                v_new = pick(bt8, j) * (pick(v8, j) - eg * pick(kqs[j][0:SUBLANES], j))
                outs.append(eg * pick(kqs[j][SUBLANES:2 * SUBLANES], j) + pick(qk8, j) * v_new)
                sgo_ref[j, h] = sgi_ref[j, h] * eg + _column_of_row(pick(k8, j)) * v_new
            put(cols, outs)
        return run

    def retention(h):
        def run():
            gamma = math.exp(_log_gamma(h))
            cols = _head_cols(A_WIDTH, h)
            q8, k8, v8 = sq_ref[rows, cols], sk_ref[rows, cols], sv_ref[rows, cols]
            qk8 = sc_ref[rows, _head_cols(3 * A_WIDTH, h)]
            q16 = q8.astype(_BF16)
            qss = [jnp.dot(q16, sri_ref[j, h].astype(_BF16), preferred_element_type=_F32)
                   for j in range(per_step)]
            outs = []
            for j in range(per_step):
                v1 = pick(v8, j)
                outs.append(gamma * pick(qss[j], j) + pick(qk8, j) * v1)
                sro_ref[j, h] = sri_ref[j, h] * gamma + _column_of_row(pick(k8, j)) * v1
            put(cols, outs)
        return run

    return [deltanet(h) for h in range(A_HEADS)] + [retention(h) for h in range(B_HEADS)]


def _layer_prompt_body(*refs, aliased, tb, nt, nblocks, d_ff, per_step):
    (x_ref, cos_ref, sin_ref, win_ref, convw_ref, alog_ref, dtb_ref, gnw_ref, rnw_ref, wout_ref,
     npre_ref, npost_ref, wgu_ref, wd_ref, fpre_ref, fpost_ref,
     sq_ref, sk_ref, sv_ref, sc_ref, sgi_ref, sri_ref) = refs[:22]
    (y_ref, conv_ref, sg_ref, sr_ref, so_ref, sgo_ref, sro_ref,
     pq_ref, pr_ref, act_ref, small_ref, gct_ref, dec_ref, o_ref, h_ref, hn_ref, gate_ref,
     ffn_act_ref, f_ref) = refs[22 + (2 if aliased else 0):]
    s = pl.program_id(0)
    live = s < nblocks
    t = lax.rem(jnp.minimum(s, nblocks - 1), nt)
    pad = SUBLANES

    @pl.when(s == 0)
    def _():
        h_ref[1] = jnp.zeros(h_ref.shape[1:], _F32)
        hn_ref[1] = jnp.zeros(hn_ref.shape[1:], _BF16)
        so_ref[...] = jnp.zeros(so_ref.shape, _F32)

    @pl.when((t == 0) & live)
    def _():
        pq_ref[0:pad, :] = jnp.zeros((pad, QKV_WIDTH), _F32)
        sg_ref[...] = jnp.zeros(sg_ref.shape, _F32)
        sr_ref[...] = jnp.zeros(sr_ref.shape, _F32)

    prev = lax.rem(s + 1, 2)
    fill = _Filler(_ffn_pieces(h_ref.at[prev], hn_ref.at[prev], wgu_ref, wd_ref, fpost_ref[...],
                               gate_ref, ffn_act_ref, f_ref, y_ref, d_ff))
    fill.emit(2)
    sample = _Filler(_sample_state_units(jnp.minimum(s, nblocks - 1) * per_step, per_step, sq_ref, sk_ref,
                                         sv_ref, sc_ref, sgi_ref, sri_ref, so_ref, sgo_ref, sro_ref))

    x = x_ref[...]
    hn = _rmsnorm(x, npre_ref[...]).astype(_BF16)

    def project(lo, hi):
        return jnp.dot(hn, win_ref[:, lo:hi], preferred_element_type=_F32)

    def project_rest(lo, hi):
        pr_ref[:, lo:hi] = project(QKV_WIDTH + lo, QKV_WIDTH + hi)

    ps = project(QKV_WIDTH + SMALL_OFF, IN_WIDTH_PADDED)
    fill.emit(1)
    beta_all = jax.nn.sigmoid(ps)
    gc = -jnp.exp(alog_ref[...]) * _softplus(ps + dtb_ref[...])
    row_in_chunk = lax.broadcasted_iota(jnp.int32, (tb, LANES), 0) % CHUNK
    shift = 1
    while shift < CHUNK:
        gc = gc + jnp.where(row_in_chunk >= shift, pltpu.roll(gc, shift, axis=0), 0.0)
        shift *= 2
    small_ref[0] = beta_all
    small_ref[1] = gc
    gct_ref[...] = gc.T

    half = B_WIDTH // 2
    cos_full = cos_ref[...]
    sin_signed = sin_ref[...]
    ri = lax.broadcasted_iota(jnp.int32, (GROUP, GROUP), 0)
    ci = lax.broadcasted_iota(jnp.int32, (GROUP, GROUP), 1)
    tril = ((ri // CHUNK) == (ci // CHUNK)) & (ri >= ci)
    for pair in range(B_HEADS // 2):
        project_rest(BQ_OFF + pair * half, BQ_OFF + (pair + 1) * half)
        fill.emit(1)
        project_rest(BK_OFF + pair * half, BK_OFF + (pair + 1) * half)
        fill.emit(1)
        for h in (2 * pair, 2 * pair + 1):
            gc_col = small_ref[1, :, A_HEADS + h:A_HEADS + h + 1]
            gcr = jnp.broadcast_to(gct_ref[A_HEADS + h:A_HEADS + h + 1, :], (GROUP, GROUP))
            diff = jnp.broadcast_to(gc_col, (GROUP, GROUP)) - gcr
            dec_ref[h] = jnp.exp(jnp.where(tril, diff, MASKED_LOG))
            cq = _head_cols(BQ_OFF, h)
            ck = _head_cols(BK_OFF, h)
            pr_ref[:, cq] = _rope(pr_ref[:, cq], cos_full, sin_signed)
            pr_ref[:, ck] = _rope(pr_ref[:, ck], cos_full, sin_signed) * (HEAD_DIM ** -0.5)

    for lo in range(0, QKV_WIDTH, FFN_COL_BLOCK):
        pq_ref[pad:pad + tb, lo:lo + FFN_COL_BLOCK] = project(lo, lo + FFN_COL_BLOCK)
        fill.emit(1)
        sample.emit(1)

    rest = [(lo, lo + FFN_COL_BLOCK) for lo in range(Z_OFF, BQ_OFF, FFN_COL_BLOCK)]
    rest += [(lo, lo + FFN_COL_BLOCK) for lo in range(BV_OFF, SMALL_OFF, FFN_COL_BLOCK)]
    for j in range(QKV_WIDTH // LANES):
        if j % 2 == 0 and rest:
            project_rest(*rest.pop(0))
        else:
            fill.emit(1)
        cs = slice(j * LANES, (j + 1) * LANES)
        acc = pq_ref[pad:pad + tb, cs] * convw_ref[3:4, cs]
        for i in range(CONV_W - 1):
            acc = acc + pq_ref[pad - 3 + i:pad - 3 + i + tb, cs] * convw_ref[i:i + 1, cs]
        a = _silu(acc)
        if j < 2 * A_HEADS:
            a = a * lax.rsqrt(jnp.sum(a * a, axis=-1, keepdims=True) + L2_EPS)
        if j < A_HEADS:
            a = a * (HEAD_DIM ** -0.5)
        act_ref[:, cs] = a
    assert not rest

    tail = pq_ref[pad + tb - 3:pad + tb, :]
    conv_ref[...] = tail
    pq_ref[pad - 3:pad, :] = tail

    for g in range(tb // GROUP):
        _recurrences_one_group(g, live, fill, act_ref, pr_ref, small_ref, dec_ref, sg_ref, sr_ref, o_ref)

    for h in range(A_HEADS):
        cs = _head_cols(0, h)
        o_ref[:, cs] = _gated_head_norm(o_ref[:, cs], gnw_ref[...], pr_ref[:, _head_cols(Z_OFF, h)])
    for h in range(B_HEADS):
        cs = _head_cols(A_WIDTH, h)
        o_ref[:, cs] = _gated_head_norm(o_ref[:, cs], rnw_ref[...], pr_ref[:, _head_cols(BG_OFF, h)])
    m = jnp.dot(o_ref[...].astype(_BF16), wout_ref[...], preferred_element_type=_F32)
    fill.emit(8)
    sample.flush()
    h_new = x_ref[...] + _rmsnorm(m, npost_ref[...])
    cur = lax.rem(s, 2)
    h_ref[cur] = h_new
    hn_ref[cur] = _rmsnorm(h_new, fpre_ref[...]).astype(_BF16)
    fill.flush()


def _layer_spec(shape, layer):
    zeros = (0,) * len(shape)
    return pl.BlockSpec((None,) + tuple(shape), lambda i: (layer,) + zeros)


def _layer_prompt(x, cos_full, sin_signed, win, conv_w, alog, dtb, gnw, rnw, wout, npre, npost,
                  wgu, wd, fpre, fpost, sq, sk, sv, sc, state_gdn, state_ret, prev_gdn, prev_ret, layer):
    batch, seq, d_model = x.shape
    nb = sq.shape[0]
    depth = state_gdn.shape[0]
    d_ff = wd.shape[1]
    tb = min(PROMPT_TIME_BLOCK, seq)
    assert seq % tb == 0 and tb == GROUP and d_ff % FFN_COL_BLOCK == 0
    nt = seq // tb
    nblocks = batch * nt
    assert nb % nblocks == 0 and SUBLANES % (nb // nblocks) == 0
    per_step = nb // nblocks
    lspec = functools.partial(_layer_spec, layer=layer)

    def mixer_block(s):
        return jnp.minimum(s, nblocks - 1)

    def ffn_block(s):
        return jnp.maximum(s - 1, 0)

    full = lambda shape: pl.BlockSpec(tuple(shape), lambda s: (0,) * len(shape))
    state_spec = lambda heads: pl.BlockSpec((None, per_step, heads, HEAD_DIM, HEAD_DIM),
                                            lambda s: (layer, mixer_block(s), 0, 0, 0))

    in_specs = [
        pl.BlockSpec((None, tb, d_model), lambda s: (mixer_block(s) // nt, mixer_block(s) % nt, 0)),
        pl.BlockSpec((tb, HEAD_DIM), lambda s: (mixer_block(s) % nt, 0)),
        pl.BlockSpec((tb, HEAD_DIM), lambda s: (mixer_block(s) % nt, 0)),
        lspec((d_model, IN_WIDTH_PADDED)),
        lspec((CONV_W, QKV_WIDTH)),
        lspec((1, LANES)),
        lspec((1, LANES)),
        lspec((1, HEAD_DIM)),
        lspec((1, HEAD_DIM)),
        lspec((A_WIDTH + B_WIDTH, d_model)),
        lspec((1, d_model)),
        lspec((1, d_model)),
        lspec((d_model, 2 * d_ff)),
        lspec((d_ff, d_model)),
        lspec((1, d_model)),
        lspec((1, d_model)),
        full(sq.shape), full(sk.shape), full(sv.shape), full(sc.shape),
        state_spec(A_HEADS), state_spec(B_HEADS),
    ]
    args = [x, cos_full, sin_signed, win, conv_w, alog, dtb, gnw, rnw, wout, npre, npost, wgu, wd, fpre,
            fpost, sq, sk, sv, sc, state_gdn, state_ret]
    aliases = {}
    if prev_gdn is not None:
        in_specs += [pl.BlockSpec(memory_space=pl.ANY), pl.BlockSpec(memory_space=pl.ANY)]
        aliases = {len(args): 5, len(args) + 1: 6}
        args += [prev_gdn, prev_ret]
    out_specs = [
        pl.BlockSpec((None, tb, d_model), lambda s: (ffn_block(s) // nt, ffn_block(s) % nt, 0)),
        pl.BlockSpec((None, CONV_W - 1, QKV_WIDTH), lambda s: (mixer_block(s) // nt, 0, 0)),
        pl.BlockSpec((None, A_HEADS, HEAD_DIM, HEAD_DIM), lambda s: (mixer_block(s) // nt, 0, 0, 0)),
        pl.BlockSpec((None, B_HEADS, HEAD_DIM, HEAD_DIM), lambda s: (mixer_block(s) // nt, 0, 0, 0)),
        full(sq.shape), state_spec(A_HEADS), state_spec(B_HEADS),
    ]
    out_shape = [
        jax.ShapeDtypeStruct((batch, seq, d_model), _F32),
        jax.ShapeDtypeStruct((batch, CONV_W - 1, QKV_WIDTH), _F32),
        jax.ShapeDtypeStruct((batch, A_HEADS, HEAD_DIM, HEAD_DIM), _F32),
        jax.ShapeDtypeStruct((batch, B_HEADS, HEAD_DIM, HEAD_DIM), _F32),
        jax.ShapeDtypeStruct(sq.shape, _F32),
        jax.ShapeDtypeStruct((depth, nb, A_HEADS, HEAD_DIM, HEAD_DIM), _F32),
        jax.ShapeDtypeStruct((depth, nb, B_HEADS, HEAD_DIM, HEAD_DIM), _F32),
    ]
    scratch = [
        pltpu.VMEM((tb + SUBLANES, QKV_WIDTH), _F32),
        pltpu.VMEM((tb, REST_WIDTH), _F32),
        pltpu.VMEM((tb, QKV_WIDTH), _F32),
        pltpu.VMEM((2, tb, LANES), _F32),
        pltpu.VMEM((LANES, tb), _F32),
        pltpu.VMEM((A_HEADS, GROUP, GROUP), _F32),
        pltpu.VMEM((tb, A_WIDTH + B_WIDTH), _F32),
        pltpu.VMEM((2, tb, d_model), _F32),
        pltpu.VMEM((2, tb, d_model), _BF16),
        pltpu.VMEM((2, tb, FFN_COL_BLOCK), _F32),
        pltpu.VMEM((tb, d_ff), _BF16),
        pltpu.VMEM((tb, d_model), _F32),
    ]
    return pl.pallas_call(
        functools.partial(_layer_prompt_body, aliased=prev_gdn is not None, tb=tb, nt=nt, nblocks=nblocks,
                          d_ff=d_ff, per_step=per_step),
        grid=(nblocks + 1,), in_specs=in_specs, out_specs=out_specs, out_shape=out_shape,
        scratch_shapes=scratch, input_output_aliases=aliases,
        compiler_params=pltpu.CompilerParams(
            dimension_semantics=("arbitrary",), vmem_limit_bytes=VMEM_LIMIT_BYTES),
        name=f"layer_prompt_l{layer}",
    )(*args)


def _sample_pre_body(x_ref, cos_ref, sin_ref, win_ref, convw_ref, alog_ref, dtb_ref, npre_ref, convs_ref,
                     q_ref, k_ref, v_ref, sc_ref, gates_ref, convn_ref):
    nb = x_ref.shape[0]
    hn = _rmsnorm(x_ref[...], npre_ref[...]).astype(_BF16)
    pq = jnp.dot(hn, win_ref[:, 0:QKV_WIDTH], preferred_element_type=_F32)
    pr = jnp.dot(hn, win_ref[:, QKV_WIDTH:IN_WIDTH_PADDED], preferred_element_type=_F32)
    outs = (q_ref, k_ref, v_ref)
    for j in range(QKV_WIDTH // LANES):
        cs = slice(j * LANES, (j + 1) * LANES)
        new = pq[:, cs]
        acc = new * convw_ref[3:4, cs]
        for r in range(CONV_W - 1):
            hist = convs_ref[:, r * QKV_WIDTH + j * LANES:r * QKV_WIDTH + (j + 1) * LANES]
            acc = acc + hist * convw_ref[r:r + 1, cs]
            if r > 0:
                convn_ref[:, (r - 1) * QKV_WIDTH + j * LANES:(r - 1) * QKV_WIDTH + (j + 1) * LANES] = hist
        convn_ref[:, (CONV_W - 2) * QKV_WIDTH + j * LANES:(CONV_W - 2) * QKV_WIDTH + (j + 1) * LANES] = new
        a = _silu(acc)
        if j < 2 * A_HEADS:
            a = a * lax.rsqrt(jnp.sum(a * a, axis=-1, keepdims=True) + L2_EPS)
        if j < A_HEADS:
            a = a * (HEAD_DIM ** -0.5)
        outs[j // A_HEADS][:, _head_cols(0, j % A_HEADS)] = a
    ps = pr[:, SMALL_OFF:SMALL_OFF + LANES]
    beta_all = jax.nn.sigmoid(ps)
    eg_all = jnp.exp(-jnp.exp(alog_ref[...]) * _softplus(ps + dtb_ref[...]))
    cos_full = cos_ref[...]
    sin_signed = sin_ref[...]
    for h in range(A_HEADS):
        sc_ref[:, _head_cols(0, h)] = jnp.broadcast_to(eg_all[:, A_HEADS + h:A_HEADS + h + 1], (nb, LANES))
        sc_ref[:, _head_cols(A_WIDTH, h)] = jnp.broadcast_to(beta_all[:, h:h + 1], (nb, LANES))
        qk = jnp.sum(q_ref[:, _head_cols(0, h)] * k_ref[:, _head_cols(0, h)], axis=-1, keepdims=True)
        sc_ref[:, _head_cols(2 * A_WIDTH, h)] = jnp.broadcast_to(qk, (nb, LANES))
    for h in range(B_HEADS):
        q = _rope(pr[:, _head_cols(BQ_OFF, h)], cos_full, sin_signed)
        k = _rope(pr[:, _head_cols(BK_OFF, h)], cos_full, sin_signed) * (HEAD_DIM ** -0.5)
        q_ref[:, _head_cols(A_WIDTH, h)] = q
        k_ref[:, _head_cols(A_WIDTH, h)] = k
        sc_ref[:, _head_cols(3 * A_WIDTH, h)] = jnp.broadcast_to(
            jnp.sum(q * k, axis=-1, keepdims=True), (nb, LANES))
    v_ref[:, A_WIDTH:A_WIDTH + B_WIDTH] = pr[:, BV_OFF:BG_OFF]
    gates_ref[:, 0:A_WIDTH] = pr[:, Z_OFF:BQ_OFF]
    gates_ref[:, A_WIDTH:A_WIDTH + B_WIDTH] = pr[:, BG_OFF:SMALL_OFF]


def _sample_pre(x, cos_full, sin_signed, win, conv_w, alog, dtb, npre, conv_state, layer):
    nb, d_model = x.shape
    width = A_WIDTH + B_WIDTH
    lspec = functools.partial(_layer_spec, layer=layer)
    full = lambda shape: pl.BlockSpec(tuple(shape), lambda i: (0,) * len(shape))
    widths = (width, width, width, SAMPLE_SCALARS * A_WIDTH, width, (CONV_W - 1) * QKV_WIDTH)
    return pl.pallas_call(
        _sample_pre_body,
        grid=(1,),
        in_specs=[full((nb, d_model)), full((1, HEAD_DIM)), full((1, HEAD_DIM)),
                  lspec((d_model, IN_WIDTH_PADDED)), lspec((CONV_W, QKV_WIDTH)), lspec((1, LANES)),
                  lspec((1, LANES)), lspec((1, d_model)), lspec((nb, (CONV_W - 1) * QKV_WIDTH))],
        out_specs=[full((nb, w)) for w in widths],
        out_shape=[jax.ShapeDtypeStruct((nb, w), _F32) for w in widths],
        compiler_params=pltpu.CompilerParams(
            dimension_semantics=("arbitrary",), vmem_limit_bytes=VMEM_LIMIT_BYTES),
        name=f"sample_pre_l{layer}",
    )(x, cos_full, sin_signed, win, conv_w, alog, dtb, npre, conv_state)


def _sample_post_body(o_ref, gates_ref, x_ref, gnw_ref, rnw_ref, wout_ref, npost_ref, wgu_ref, wd_ref,
                      fpre_ref, fpost_ref, y_ref, cat_ref, act_ref, *, d_ff):
    for h in range(A_HEADS + B_HEADS):
        cs = _head_cols(0, h)
        w = gnw_ref[...] if h < A_HEADS else rnw_ref[...]
        cat_ref[:, cs] = _gated_head_norm(o_ref[:, cs], w, gates_ref[:, cs]).astype(_BF16)
    m = jnp.dot(cat_ref[...], wout_ref[...], preferred_element_type=_F32)
    h_new = x_ref[...] + _rmsnorm(m, npost_ref[...])
    y_ref[...] = _ffn_rows(h_new, wgu_ref, wd_ref, fpre_ref[...], fpost_ref[...], act_ref, d_ff)


def _sample_post(o, gates, x, gnw, rnw, wout, npost, wgu, wd, fpre, fpost, layer):
    nb, d_model = x.shape
    d_ff = wd.shape[1]
    width = A_WIDTH + B_WIDTH
    lspec = functools.partial(_layer_spec, layer=layer)
    full = lambda shape: pl.BlockSpec(tuple(shape), lambda i: (0,) * len(shape))
    return pl.pallas_call(
        functools.partial(_sample_post_body, d_ff=d_ff),
        grid=(1,),
        in_specs=[full((nb, width)), full((nb, width)), full((nb, d_model)), lspec((1, HEAD_DIM)),
                  lspec((1, HEAD_DIM)), lspec((width, d_model)), lspec((1, d_model)),
                  lspec((d_model, 2 * d_ff)), lspec((d_ff, d_model)), lspec((1, d_model)),
                  lspec((1, d_model))],
        out_specs=full((nb, d_model)),
        out_shape=jax.ShapeDtypeStruct((nb, d_model), _F32),
        scratch_shapes=[pltpu.VMEM((nb, width), _BF16), pltpu.VMEM((nb, d_ff), _BF16)],
        compiler_params=pltpu.CompilerParams(
            dimension_semantics=("arbitrary",), vmem_limit_bytes=VMEM_LIMIT_BYTES),
        name=f"sample_post_l{layer}",
    )(o, gates, x, gnw, rnw, wout, npost, wgu, wd, fpre, fpost)


def _rope_tables(positions):
    half = HEAD_DIM // 2
    inv = ROPE_BASE ** (-np.arange(half, dtype=np.float64) / half)
    ang = np.asarray(positions, dtype=np.float64)[:, None] * inv[None, :]
    cos, sin = np.cos(ang), np.sin(ang)
    return (jnp.asarray(np.concatenate([cos, cos], axis=-1), dtype=_F32),
            jnp.asarray(np.concatenate([-sin, sin], axis=-1), dtype=_F32))


def _rearranged_w_in_body(w_ref, o_ref):
    small0 = QKV_WIDTH + A_WIDTH
    small1 = small0 + 2 * A_HEADS
    width = w_ref.shape[1]
    o_ref[:, 0:small0] = w_ref[:, 0:small0].astype(_BF16)
    o_ref[:, small0:small0 + width - small1] = w_ref[:, small1:width].astype(_BF16)
    lane = lax.broadcasted_iota(jnp.int32, (w_ref.shape[0], LANES), 1)
    o_ref[:, QKV_WIDTH + SMALL_OFF:IN_WIDTH_PADDED] = jnp.where(
        lane < 2 * A_HEADS, w_ref[:, small0:small0 + LANES], 0.0).astype(_BF16)


def _rearranged_w_in(w_in):
    depth, d_model, width = w_in.shape
    assert width == IN_WIDTH_PADDED - LANES + 2 * A_HEADS
    rb = min(WEIGHT_ROW_BLOCK, d_model)
    assert d_model % rb == 0
    return pl.pallas_call(
        _rearranged_w_in_body,
        grid=(depth, d_model // rb),
        in_specs=[pl.BlockSpec((None, rb, width), lambda l, i: (l, i, 0))],
        out_specs=pl.BlockSpec((None, rb, IN_WIDTH_PADDED), lambda l, i: (l, i, 0)),
        out_shape=jax.ShapeDtypeStruct((depth, d_model, IN_WIDTH_PADDED), _BF16),
        compiler_params=pltpu.CompilerParams(dimension_semantics=("arbitrary", "arbitrary")),
        name="rearranged_w_in",
    )(w_in)


def kernel(x_prompt, x_sample, state_conv, state_gdn, state_ret, w_in, conv_w, a_log, dt_bias, gdn_norm_w, ret_norm_w, w_out, norm_mix_pre, norm_mix_post, norm_ffn_pre, norm_ffn_post, w_gate_up, w_down):
    depth = w_in.shape[0]
    batch, seq, d_model = x_prompt.shape
    nb, seq_s, _ = x_sample.shape
    assert seq_s == 1

    win = _rearranged_w_in(w_in)
    wout = w_out.astype(_BF16)
    wgu = w_gate_up.astype(_BF16)
    wd = w_down.astype(_BF16)
    alog = jnp.pad(a_log, ((0, 0), (A_HEADS, LANES - 2 * A_HEADS)))[:, None, :]
    dtb = jnp.pad(dt_bias, ((0, 0), (A_HEADS, LANES - 2 * A_HEADS)))[:, None, :]
    gnw = gdn_norm_w[:, None, :]
    rnw = ret_norm_w[:, None, :]
    npre = norm_mix_pre[:, None, :]
    npost = norm_mix_post[:, None, :]
    fpre = norm_ffn_pre[:, None, :]
    fpost = norm_ffn_post[:, None, :]
    cos_p, sin_p = _rope_tables(np.arange(seq))
    cos_s, sin_s = _rope_tables(PAST_LEN + np.arange(seq_s))
    conv_state = state_conv.reshape(depth, nb, (CONV_W - 1) * QKV_WIDTH)

    hp = x_prompt
    hs = x_sample.reshape(nb, d_model)
    convs_p, gdns_p, rets_p, convs_s = [], [], [], []
    gdn_s = ret_s = None
    for l in range(depth):
        sq, sk, sv, sc, gates, conv_s = _sample_pre(hs, cos_s, sin_s, win, conv_w, alog, dtb, npre,
                                                    conv_state, l)
        hp, conv_p, gdn_p, ret_p, so, gdn_s, ret_s = _layer_prompt(
            hp, cos_p, sin_p, win, conv_w, alog, dtb, gnw, rnw, wout, npre, npost, wgu, wd, fpre, fpost,
            sq, sk, sv, sc, state_gdn, state_ret, gdn_s, ret_s, l)
        hs = _sample_post(so, gates, hs, gnw, rnw, wout, npost, wgu, wd, fpre, fpost, l)
        convs_p.append(conv_p)
        gdns_p.append(gdn_p)
        rets_p.append(ret_p)
        convs_s.append(conv_s.reshape(nb, CONV_W - 1, QKV_WIDTH))
    return (hp, hs.reshape(nb, seq_s, d_model), jnp.stack(convs_p), jnp.stack(gdns_p),
            jnp.stack(rets_p), jnp.stack(convs_s), gdn_s, ret_s)
```

```python
import functools
import math

import jax
import jax.numpy as jnp
import numpy as np
from jax import lax
from jax.experimental import pallas as pl
from jax.experimental.pallas import tpu as pltpu

HEAD_DIM = 128
A_HEADS = 4
B_HEADS = 4
A_WIDTH = A_HEADS * HEAD_DIM
B_WIDTH = B_HEADS * HEAD_DIM
QKV_WIDTH = 3 * A_WIDTH
CONV_W = 4
CHUNK = 64
ROPE_BASE = 10000.0
EPS = 1e-6
L2_EPS = 1e-6
MASKED_LOG = -1e30
PAST_LEN = 16384
LANES = 128
SUBLANES = 8

REST_WIDTH = A_WIDTH + 4 * B_WIDTH + LANES
Z_OFF = 0
BQ_OFF = A_WIDTH
BK_OFF = BQ_OFF + B_WIDTH
BV_OFF = BK_OFF + B_WIDTH
BG_OFF = BV_OFF + B_WIDTH
SMALL_OFF = BG_OFF + B_WIDTH
IN_WIDTH_PADDED = QKV_WIDTH + REST_WIDTH

GROUP = 256
CHUNKS_PER_GROUP = GROUP // CHUNK
PROMPT_TIME_BLOCK = 256
FFN_COL_BLOCK = 256
WEIGHT_COL_BLOCK = 256
SAMPLE_SCALARS = 4
VMEM_LIMIT_BYTES = 56 * 1024 * 1024

_BF16 = jnp.bfloat16
_F32 = jnp.float32


def _dot(a, b):
    return jnp.dot(a.astype(_BF16), b.astype(_BF16), preferred_element_type=_F32)


def _dot_nt(a, b):
    return lax.dot_general(a.astype(_BF16), b.astype(_BF16), (((1,), (1,)), ((), ())),
                           preferred_element_type=_F32)


def _dot_tn(a, b):
    return lax.dot_general(a.astype(_BF16), b.astype(_BF16), (((0,), (0,)), ((), ())),
                           preferred_element_type=_F32)


def _rmsnorm(x, w):
    return x * lax.rsqrt(jnp.mean(x * x, axis=-1, keepdims=True) + EPS) * w


def _silu(x):
    return x * jax.nn.sigmoid(x)


def _softplus(x):
    return jnp.maximum(x, 0.0) + jnp.log1p(jnp.exp(-jnp.abs(x)))


def _log_gamma(h):
    return math.log1p(-(2.0 ** (-5.0 - h)))


def _rope(x, cos_full, sin_signed):
    return x * cos_full + pltpu.roll(x, HEAD_DIM // 2, axis=1) * sin_signed


def _head_cols(base, h):
    return slice(base + h * HEAD_DIM, base + (h + 1) * HEAD_DIM)


def _gated_head_norm(o, w, gate):
    return o * lax.rsqrt(jnp.mean(o * o, axis=-1, keepdims=True) + EPS) * w * _silu(gate)


class _Filler:
    def __init__(self, pieces):
        self._pieces = list(pieces)
        self._next = 0

    def emit(self, count):
        for _ in range(count):
            if self._next < len(self._pieces):
                self._pieces[self._next]()
                self._next += 1

    def flush(self):
        self.emit(len(self._pieces))


def _fold_rows(m):
    out = m[0:CHUNK]
    for c in range(1, CHUNKS_PER_GROUP):
        out = out + m[c * CHUNK:(c + 1) * CHUNK]
    return out


def _unfold_rows(r, same_chunk):
    return jnp.where(same_chunk, jnp.concatenate([r] * CHUNKS_PER_GROUP, axis=0), 0.0)


def _unit_lower_inverses(a_folded, eye_folded, same_chunk, fill):
    xs = list(a_folded)
    ps = [eye_folded - x for x in xs]
    n = 1
    while n < CHUNK:
        for h in range(len(xs)):
            x_bd = _unfold_rows(xs[h], same_chunk).astype(_BF16)
            if n == 1:
                xs[h] = _dot(xs[h], x_bd)
            elif 2 * n < CHUNK:
                r = _dot(jnp.concatenate([xs[h], ps[h]], axis=0), x_bd)
                xs[h] = r[0:CHUNK]
                ps[h] = ps[h] + r[CHUNK:2 * CHUNK]
            else:
                ps[h] = ps[h] + _dot(ps[h], x_bd)
            fill.emit(h % 2)
        n *= 2
    return [_unfold_rows(p, same_chunk) for p in ps]


def _recurrences_one_group(g, live, fill, act_ref, pr_ref, small_ref, dec_ref, sg_ref, sr_ref, o_ref):
    rows = slice(g * GROUP, (g + 1) * GROUP)
    ri = lax.broadcasted_iota(jnp.int32, (GROUP, GROUP), 0)
    ci = lax.broadcasted_iota(jnp.int32, (GROUP, GROUP), 1)
    same_chunk = (ri // CHUNK) == (ci // CHUNK)
    tril = same_chunk & (ri >= ci)
    off_diag = ri != ci
    rf = lax.broadcasted_iota(jnp.int32, (CHUNK, GROUP), 0)
    cf = lax.broadcasted_iota(jnp.int32, (CHUNK, GROUP), 1)
    eye_folded = jnp.where(rf == cf % CHUNK, 1.0, 0.0).astype(_F32)
    pos = (lax.broadcasted_iota(jnp.int32, (GROUP, HEAD_DIM), 0) % CHUNK).astype(_F32)
    chunk_rows = [slice(c * CHUNK, (c + 1) * CHUNK) for c in range(CHUNKS_PER_GROUP)]

    qs, ks, gccs, a_folded, a_intra, uw_rhs, e_ins = [], [], [], [], [], [], []
    for h in range(A_HEADS):
        q = act_ref[rows, _head_cols(0, h)]
        k = act_ref[rows, _head_cols(A_WIDTH, h)]
        v = act_ref[rows, _head_cols(2 * A_WIDTH, h)]
        beta = jnp.broadcast_to(small_ref[0, rows, h:h + 1], (GROUP, HEAD_DIM))
        gc_col = small_ref[1, rows, A_HEADS + h:A_HEADS + h + 1]
        gcc = jnp.broadcast_to(gc_col, (GROUP, HEAD_DIM))
        decay = dec_ref[h]
        kb = k * beta
        e_in = jnp.exp(gcc)
        kq = _dot_nt(jnp.concatenate([kb, q], axis=0), k)
        a_folded.append(_fold_rows(jnp.where(off_diag, kq[0:GROUP] * decay, 0.0)))
        a_intra.append(kq[GROUP:2 * GROUP] * decay)
        uw_rhs.append(jnp.concatenate([v * beta, kb * e_in], axis=1).astype(_BF16))
        qs.append(q)
        ks.append(k)
        gccs.append(gcc)
        e_ins.append(e_in)
        fill.emit(1)

    r_qe, r_oloc, r_b, r_echunk = [], [], [], []
    for h in range(B_HEADS):
        lg = _log_gamma(h)
        q = pr_ref[rows, _head_cols(BQ_OFF, h)]
        k = pr_ref[rows, _head_cols(BK_OFF, h)]
        v = pr_ref[rows, _head_cols(BV_OFF, h)]
        decay = jnp.exp(jnp.where(tril, (ri - ci).astype(_F32) * lg, MASKED_LOG))
        vb16 = v.astype(_BF16)
        r_oloc.append(_dot(_dot_nt(q, k) * decay, vb16))
        r_qe.append((q * jnp.exp((pos + 1.0) * lg)).astype(_BF16))
        k_out = (k * jnp.exp((CHUNK - 1.0 - pos) * lg)).astype(_BF16)
        r_b.append([_dot_tn(k_out[cr], vb16[cr]) for cr in chunk_rows])
        r_echunk.append(math.exp(CHUNK * lg))

    t_inv = _unit_lower_inverses(a_folded, eye_folded, same_chunk, fill)

    g_lhs, g_oloc, g_b, g_elast = [], [], [], []
    for h in range(A_HEADS):
        uw = _dot(t_inv[h], uw_rhs[h])
        uw16 = uw.astype(_BF16)
        aiuw = _dot(a_intra[h], uw16)
        g_oloc.append(aiuw[:, 0:HEAD_DIM])
        q_eff = qs[h] * e_ins[h] - aiuw[:, HEAD_DIM:2 * HEAD_DIM]
        lhs, bs, elast = [], [], []
        for cr in chunk_rows:
            gcc_c = gccs[h][cr]
            g_last = gcc_c[CHUNK - 1:CHUNK, :]
            k_out = ks[h][cr] * jnp.exp(g_last - gcc_c)
            bg = _dot_tn(k_out, uw16[cr])
            bs.append(bg[:, 0:HEAD_DIM])
            lhs.append(jnp.concatenate([q_eff[cr], bg[:, HEAD_DIM:2 * HEAD_DIM]], axis=0).astype(_BF16))
            elast.append(jnp.exp(g_last))
        g_lhs.append(lhs)
        g_b.append(bs)
        g_elast.append(elast)
        fill.emit(1)

    g_state = [sg_ref[h] for h in range(A_HEADS)]
    r_state = [sr_ref[h] for h in range(B_HEADS)]
    for c, cr in enumerate(chunk_rows):
        out_rows = slice(g * GROUP + c * CHUNK, g * GROUP + (c + 1) * CHUNK)
        for h in range(A_HEADS):
            r = _dot(g_lhs[h][c], g_state[h])
            o_ref[out_rows, _head_cols(0, h)] = r[0:CHUNK] + g_oloc[h][cr]
            g_state[h] = g_state[h] * g_elast[h][c] + g_b[h][c] - r[CHUNK:CHUNK + HEAD_DIM]
        for h in range(B_HEADS):
            o_ref[out_rows, _head_cols(A_WIDTH, h)] = _dot(r_qe[h][cr], r_state[h]) + r_oloc[h][cr]
            r_state[h] = r_state[h] * r_echunk[h] + r_b[h][c]
        fill.emit(1)
    for h in range(A_HEADS):
        sg_ref[h] = jnp.where(live, g_state[h], sg_ref[h])
    for h in range(B_HEADS):
        sr_ref[h] = jnp.where(live, r_state[h], sr_ref[h])


def _ffn_rows(h, wgu_ref, wd_ref, npre, npost, act_ref, d_ff):
    hn = _rmsnorm(h, npre).astype(_BF16)
    for j in range(d_ff // FFN_COL_BLOCK):
        cg = slice(j * FFN_COL_BLOCK, (j + 1) * FFN_COL_BLOCK)
        cu = slice(d_ff + j * FFN_COL_BLOCK, d_ff + (j + 1) * FFN_COL_BLOCK)
        gate = jnp.dot(hn, wgu_ref[:, cg], preferred_element_type=_F32)
        up = jnp.dot(hn, wgu_ref[:, cu], preferred_element_type=_F32)
        act_ref[:, cg] = (_silu(gate) * up).astype(_BF16)
    f = jnp.dot(act_ref[...], wd_ref[...], preferred_element_type=_F32)
    return h + _rmsnorm(f, npost)


def _ffn_pieces(h_ref, hn_ref, wgu_ref, wd_ref, npost, gate_ref, act_ref, f_ref, y_ref, d_ff):
    d_model = f_ref.shape[1]
    k_split = (d_ff // FFN_COL_BLOCK + 1) // 2 * FFN_COL_BLOCK
    pieces = []

    def gate(j):
        def run():
            cg = slice(j * FFN_COL_BLOCK, (j + 1) * FFN_COL_BLOCK)
            gate_ref[j % 2] = _silu(jnp.dot(hn_ref[...], wgu_ref[:, cg], preferred_element_type=_F32))
        return run

    def up(j):
        def run():
            cg = slice(j * FFN_COL_BLOCK, (j + 1) * FFN_COL_BLOCK)
            cu = slice(d_ff + j * FFN_COL_BLOCK, d_ff + (j + 1) * FFN_COL_BLOCK)
            act_ref[:, cg] = (gate_ref[j % 2] * jnp.dot(hn_ref[...], wgu_ref[:, cu],
                                                        preferred_element_type=_F32)).astype(_BF16)
        return run

    def down(j, first):
        def run():
            cs = slice(j * FFN_COL_BLOCK, (j + 1) * FFN_COL_BLOCK)
            if first:
                f_ref[:, cs] = jnp.dot(act_ref[:, 0:k_split], wd_ref[0:k_split, cs],
                                       preferred_element_type=_F32)
            else:
                f_ref[:, cs] = f_ref[:, cs] + jnp.dot(act_ref[:, k_split:d_ff], wd_ref[k_split:d_ff, cs],
                                                      preferred_element_type=_F32)
        return run

    def finish():
        y_ref[...] = h_ref[...] + _rmsnorm(f_ref[...], npost)

    for j in range(d_ff // FFN_COL_BLOCK):
        pieces += [gate(j), up(j)]
    for j in range(d_model // FFN_COL_BLOCK):
        pieces += [down(j, True), down(j, False)]
    pieces.append(finish)
    return pieces


def _column_of_row(row):
    return jnp.broadcast_to(row, (HEAD_DIM, HEAD_DIM)).T


def _sample_state_units(first_seq, per_step, sq_ref, sk_ref, sv_ref, sc_ref, sgi_ref, sri_ref,
                        so_ref, sgo_ref, sro_ref):
    base = pl.multiple_of((first_seq // SUBLANES) * SUBLANES, SUBLANES)
    rows = pl.ds(base, SUBLANES)
    row_id = lax.broadcasted_iota(jnp.int32, (SUBLANES, HEAD_DIM), 0)
    picks = [row_id == (first_seq - base + j) for j in range(per_step)]

    def pick(block, j):
        return jnp.sum(jnp.where(picks[j], block, 0.0), axis=0, keepdims=True)

    def put(cols, outs):
        block = so_ref[rows, cols]
        for j in range(per_step):
            block = jnp.where(picks[j], outs[j], block)
        so_ref[rows, cols] = block

    def deltanet(h):
        def run():
            cols = _head_cols(0, h)
            q8, k8, v8 = sq_ref[rows, cols], sk_ref[rows, cols], sv_ref[rows, cols]
            eg8 = sc_ref[rows, _head_cols(0, h)]
            bt8 = sc_ref[rows, _head_cols(A_WIDTH, h)]
            qk8 = sc_ref[rows, _head_cols(2 * A_WIDTH, h)]
            kq16 = jnp.concatenate([k8, q8], axis=0).astype(_BF16)
            kqs = [jnp.dot(kq16, sgi_ref[j, h].astype(_BF16), preferred_element_type=_F32)
                   for j in range(per_step)]
            outs = []
            for j in range(per_step):
                eg = pick(eg8, j)
                v_new = pick(bt8, j) * (pick(v8, j) - eg * pick(kqs[j][0:SUBLANES], j))
                outs.append(eg * pick(kqs[j][SUBLANES:2 * SUBLANES], j) + pick(qk8, j) * v_new)
                sgo_ref[j, h] = sgi_ref[j, h] * eg + _column_of_row(pick(k8, j)) * v_new
            put(cols, outs)
        return run

    def retention(h):
        def run():
            gamma = math.exp(_log_gamma(h))
            cols = _head_cols(A_WIDTH, h)
            q8, k8, v8 = sq_ref[rows, cols], sk_ref[rows, cols], sv_ref[rows, cols]
            qk8 = sc_ref[rows, _head_cols(3 * A_WIDTH, h)]
            q16 = q8.astype(_BF16)
            qss = [jnp.dot(q16, sri_ref[j, h].astype(_BF16), preferred_element_type=_F32)
                   for j in range(per_step)]
            outs = []
            for j in range(per_step):
                v1 = pick(v8, j)
                outs.append(gamma * pick(qss[j], j) + pick(qk8, j) * v1)
                sro_ref[j, h] = sri_ref[j, h] * gamma + _column_of_row(pick(k8, j)) * v1
            put(cols, outs)
        return run

    return [deltanet(h) for h in range(A_HEADS)] + [retention(h) for h in range(B_HEADS)]


def _layer_prompt_body(*refs, aliased, tb, nt, nblocks, d_ff, per_step):
    (x_ref, cos_ref, sin_ref, win_ref, convw_ref, alog_ref, dtb_ref, gnw_ref, rnw_ref, wout_ref,
     npre_ref, npost_ref, wgu_ref, wd_ref, fpre_ref, fpost_ref,
     sq_ref, sk_ref, sv_ref, sc_ref, sgi_ref, sri_ref) = refs[:22]
    (y_ref, conv_ref, sg_ref, sr_ref, so_ref, sgo_ref, sro_ref,
     pq_ref, pr_ref, act_ref, small_ref, gct_ref, dec_ref, o_ref, h_ref, hn_ref, gate_ref,
     ffn_act_ref, f_ref) = refs[22 + (2 if aliased else 0):]
    s = pl.program_id(0)
    live = s < nblocks
    t = lax.rem(jnp.minimum(s, nblocks - 1), nt)
    pad = SUBLANES

    @pl.when(s == 0)
    def _():
        h_ref[1] = jnp.zeros(h_ref.shape[1:], _F32)
        hn_ref[1] = jnp.zeros(hn_ref.shape[1:], _BF16)
        so_ref[...] = jnp.zeros(so_ref.shape, _F32)

    @pl.when((t == 0) & live)
    def _():
        pq_ref[0:pad, :] = jnp.zeros((pad, QKV_WIDTH), _F32)
        sg_ref[...] = jnp.zeros(sg_ref.shape, _F32)
        sr_ref[...] = jnp.zeros(sr_ref.shape, _F32)

    prev = lax.rem(s + 1, 2)
    fill = _Filler(_ffn_pieces(h_ref.at[prev], hn_ref.at[prev], wgu_ref, wd_ref, fpost_ref[...],
                               gate_ref, ffn_act_ref, f_ref, y_ref, d_ff))
    fill.emit(2)
    sample = _Filler(_sample_state_units(jnp.minimum(s, nblocks - 1) * per_step, per_step, sq_ref, sk_ref,
                                         sv_ref, sc_ref, sgi_ref, sri_ref, so_ref, sgo_ref, sro_ref))

    x = x_ref[...]
    hn = _rmsnorm(x, npre_ref[...]).astype(_BF16)

    def project(lo, hi):
        return jnp.dot(hn, win_ref[:, lo:hi], preferred_element_type=_F32)

    def project_rest(lo, hi):
        pr_ref[:, lo:hi] = project(QKV_WIDTH + lo, QKV_WIDTH + hi)

    ps = project(QKV_WIDTH + SMALL_OFF, IN_WIDTH_PADDED)
    fill.emit(1)
    beta_all = jax.nn.sigmoid(ps)
    gc = -jnp.exp(alog_ref[...]) * _softplus(ps + dtb_ref[...])
    row_in_chunk = lax.broadcasted_iota(jnp.int32, (tb, LANES), 0) % CHUNK
    shift = 1
    while shift < CHUNK:
        gc = gc + jnp.where(row_in_chunk >= shift, pltpu.roll(gc, shift, axis=0), 0.0)
        shift *= 2
    small_ref[0] = beta_all
    small_ref[1] = gc
    gct_ref[...] = gc.T

    half = B_WIDTH // 2
    cos_full = cos_ref[...]
    sin_signed = sin_ref[...]
    ri = lax.broadcasted_iota(jnp.int32, (GROUP, GROUP), 0)
    ci = lax.broadcasted_iota(jnp.int32, (GROUP, GROUP), 1)
    tril = ((ri // CHUNK) == (ci // CHUNK)) & (ri >= ci)
    for pair in range(B_HEADS // 2):
        project_rest(BQ_OFF + pair * half, BQ_OFF + (pair + 1) * half)
        project_rest(BK_OFF + pair * half, BK_OFF + (pair + 1) * half)
        for h in (2 * pair, 2 * pair + 1):
            gc_col = small_ref[1, :, A_HEADS + h:A_HEADS + h + 1]
            gcr = jnp.broadcast_to(gct_ref[A_HEADS + h:A_HEADS + h + 1, :], (GROUP, GROUP))
            diff = jnp.broadcast_to(gc_col, (GROUP, GROUP)) - gcr
            dec_ref[h] = jnp.exp(jnp.where(tril, diff, MASKED_LOG))
            cq = _head_cols(BQ_OFF, h)
            ck = _head_cols(BK_OFF, h)
            pr_ref[:, cq] = _rope(pr_ref[:, cq], cos_full, sin_signed)
            pr_ref[:, ck] = _rope(pr_ref[:, ck], cos_full, sin_signed) * (HEAD_DIM ** -0.5)

    for lo in range(0, QKV_WIDTH, FFN_COL_BLOCK):
        pq_ref[pad:pad + tb, lo:lo + FFN_COL_BLOCK] = project(lo, lo + FFN_COL_BLOCK)
        sample.emit(1)

    rest = [(lo, lo + FFN_COL_BLOCK) for lo in range(Z_OFF, BQ_OFF, FFN_COL_BLOCK)]
    rest += [(lo, lo + FFN_COL_BLOCK) for lo in range(BV_OFF, SMALL_OFF, FFN_COL_BLOCK)]
    for j in range(QKV_WIDTH // LANES):
        if j % 2 == 0 and rest:
            project_rest(*rest.pop(0))
        else:
            fill.emit(1)
        cs = slice(j * LANES, (j + 1) * LANES)
        acc = pq_ref[pad:pad + tb, cs] * convw_ref[3:4, cs]
        for i in range(CONV_W - 1):
            acc = acc + pq_ref[pad - 3 + i:pad - 3 + i + tb, cs] * convw_ref[i:i + 1, cs]
        a = _silu(acc)
        if j < 2 * A_HEADS:
            a = a * lax.rsqrt(jnp.sum(a * a, axis=-1, keepdims=True) + L2_EPS)
        if j < A_HEADS:
            a = a * (HEAD_DIM ** -0.5)
        act_ref[:, cs] = a
    assert not rest

    tail = pq_ref[pad + tb - 3:pad + tb, :]
    conv_ref[...] = tail
    pq_ref[pad - 3:pad, :] = tail

    for g in range(tb // GROUP):
        _recurrences_one_group(g, live, fill, act_ref, pr_ref, small_ref, dec_ref, sg_ref, sr_ref, o_ref)

    for h in range(A_HEADS):
        cs = _head_cols(0, h)
        o_ref[:, cs] = _gated_head_norm(o_ref[:, cs], gnw_ref[...], pr_ref[:, _head_cols(Z_OFF, h)])
    for h in range(B_HEADS):
        cs = _head_cols(A_WIDTH, h)
        o_ref[:, cs] = _gated_head_norm(o_ref[:, cs], rnw_ref[...], pr_ref[:, _head_cols(BG_OFF, h)])
    m = jnp.dot(o_ref[...].astype(_BF16), wout_ref[...], preferred_element_type=_F32)
    fill.emit(12)
    sample.flush()
    h_new = x_ref[...] + _rmsnorm(m, npost_ref[...])
    cur = lax.rem(s, 2)
    h_ref[cur] = h_new
    hn_ref[cur] = _rmsnorm(h_new, fpre_ref[...]).astype(_BF16)
    fill.flush()


def _layer_spec(shape, layer):
    zeros = (0,) * len(shape)
    return pl.BlockSpec((None,) + tuple(shape), lambda i: (layer,) + zeros)


def _layer_prompt(x, cos_full, sin_signed, win, conv_w, alog, dtb, gnw, rnw, wout, npre, npost,
                  wgu, wd, fpre, fpost, sq, sk, sv, sc, state_gdn, state_ret, prev_gdn, prev_ret, layer):
    batch, seq, d_model = x.shape
    nb = sq.shape[0]
    depth = state_gdn.shape[0]
    d_ff = wd.shape[1]
    tb = min(PROMPT_TIME_BLOCK, seq)
    assert seq % tb == 0 and tb == GROUP and d_ff % FFN_COL_BLOCK == 0
    nt = seq // tb
    nblocks = batch * nt
    assert nb % nblocks == 0 and SUBLANES % (nb // nblocks) == 0
    per_step = nb // nblocks
    lspec = functools.partial(_layer_spec, layer=layer)

    def mixer_block(s):
        return jnp.minimum(s, nblocks - 1)

    def ffn_block(s):
        return jnp.maximum(s - 1, 0)

    full = lambda shape: pl.BlockSpec(tuple(shape), lambda s: (0,) * len(shape))
    state_spec = lambda heads: pl.BlockSpec((None, per_step, heads, HEAD_DIM, HEAD_DIM),
                                            lambda s: (layer, mixer_block(s), 0, 0, 0))

    in_specs = [
        pl.BlockSpec((None, tb, d_model), lambda s: (mixer_block(s) // nt, mixer_block(s) % nt, 0)),
        pl.BlockSpec((tb, HEAD_DIM), lambda s: (mixer_block(s) % nt, 0)),
        pl.BlockSpec((tb, HEAD_DIM), lambda s: (mixer_block(s) % nt, 0)),
        lspec((d_model, IN_WIDTH_PADDED)),
        lspec((CONV_W, QKV_WIDTH)),
        lspec((1, LANES)),
        lspec((1, LANES)),
        lspec((1, HEAD_DIM)),
        lspec((1, HEAD_DIM)),
        lspec((A_WIDTH + B_WIDTH, d_model)),
        lspec((1, d_model)),
        lspec((1, d_model)),
        lspec((d_model, 2 * d_ff)),
        lspec((d_ff, d_model)),
        lspec((1, d_model)),
        lspec((1, d_model)),
        full(sq.shape), full(sk.shape), full(sv.shape), full(sc.shape),
        state_spec(A_HEADS), state_spec(B_HEADS),
    ]
    args = [x, cos_full, sin_signed, win, conv_w, alog, dtb, gnw, rnw, wout, npre, npost, wgu, wd, fpre,
            fpost, sq, sk, sv, sc, state_gdn, state_ret]
    aliases = {}
    if prev_gdn is not None:
        in_specs += [pl.BlockSpec(memory_space=pl.ANY), pl.BlockSpec(memory_space=pl.ANY)]
        aliases = {len(args): 5, len(args) + 1: 6}
        args += [prev_gdn, prev_ret]
    out_specs = [
        pl.BlockSpec((None, tb, d_model), lambda s: (ffn_block(s) // nt, ffn_block(s) % nt, 0)),
        pl.BlockSpec((None, CONV_W - 1, QKV_WIDTH), lambda s: (mixer_block(s) // nt, 0, 0)),
        pl.BlockSpec((None, A_HEADS, HEAD_DIM, HEAD_DIM), lambda s: (mixer_block(s) // nt, 0, 0, 0)),
        pl.BlockSpec((None, B_HEADS, HEAD_DIM, HEAD_DIM), lambda s: (mixer_block(s) // nt, 0, 0, 0)),
        full(sq.shape), state_spec(A_HEADS), state_spec(B_HEADS),
    ]
    out_shape = [
        jax.ShapeDtypeStruct((batch, seq, d_model), _F32),
        jax.ShapeDtypeStruct((batch, CONV_W - 1, QKV_WIDTH), _F32),
        jax.ShapeDtypeStruct((batch, A_HEADS, HEAD_DIM, HEAD_DIM), _F32),
        jax.ShapeDtypeStruct((batch, B_HEADS, HEAD_DIM, HEAD_DIM), _F32),
        jax.ShapeDtypeStruct(sq.shape, _F32),
        jax.ShapeDtypeStruct((depth, nb, A_HEADS, HEAD_DIM, HEAD_DIM), _F32),
        jax.ShapeDtypeStruct((depth, nb, B_HEADS, HEAD_DIM, HEAD_DIM), _F32),
    ]
    scratch = [
        pltpu.VMEM((tb + SUBLANES, QKV_WIDTH), _F32),
        pltpu.VMEM((tb, REST_WIDTH), _F32),
        pltpu.VMEM((tb, QKV_WIDTH), _F32),
        pltpu.VMEM((2, tb, LANES), _F32),
        pltpu.VMEM((LANES, tb), _F32),
        pltpu.VMEM((A_HEADS, GROUP, GROUP), _F32),
        pltpu.VMEM((tb, A_WIDTH + B_WIDTH), _F32),
        pltpu.VMEM((2, tb, d_model), _F32),
        pltpu.VMEM((2, tb, d_model), _BF16),
        pltpu.VMEM((2, tb, FFN_COL_BLOCK), _F32),
        pltpu.VMEM((tb, d_ff), _BF16),
        pltpu.VMEM((tb, d_model), _F32),
    ]
    return pl.pallas_call(
        functools.partial(_layer_prompt_body, aliased=prev_gdn is not None, tb=tb, nt=nt, nblocks=nblocks,
                          d_ff=d_ff, per_step=per_step),
        grid=(nblocks + 1,), in_specs=in_specs, out_specs=out_specs, out_shape=out_shape,
        scratch_shapes=scratch, input_output_aliases=aliases,
        compiler_params=pltpu.CompilerParams(
            dimension_semantics=("arbitrary",), vmem_limit_bytes=VMEM_LIMIT_BYTES),
        name=f"layer_prompt_l{layer}",
    )(*args)


def _sample_pre_body(x_ref, cos_ref, sin_ref, win_ref, convw_ref, alog_ref, dtb_ref, npre_ref, convs_ref,
                     q_ref, k_ref, v_ref, sc_ref, gates_ref, convn_ref):
    nb = x_ref.shape[0]
    hn = _rmsnorm(x_ref[...], npre_ref[...]).astype(_BF16)
    pq = jnp.dot(hn, win_ref[:, 0:QKV_WIDTH], preferred_element_type=_F32)
    pr = jnp.dot(hn, win_ref[:, QKV_WIDTH:IN_WIDTH_PADDED], preferred_element_type=_F32)
    outs = (q_ref, k_ref, v_ref)
    for j in range(QKV_WIDTH // LANES):
        cs = slice(j * LANES, (j + 1) * LANES)
        new = pq[:, cs]
        acc = new * convw_ref[3:4, cs]
        for r in range(CONV_W - 1):
            hist = convs_ref[:, r * QKV_WIDTH + j * LANES:r * QKV_WIDTH + (j + 1) * LANES]
            acc = acc + hist * convw_ref[r:r + 1, cs]
            if r > 0:
                convn_ref[:, (r - 1) * QKV_WIDTH + j * LANES:(r - 1) * QKV_WIDTH + (j + 1) * LANES] = hist
        convn_ref[:, (CONV_W - 2) * QKV_WIDTH + j * LANES:(CONV_W - 2) * QKV_WIDTH + (j + 1) * LANES] = new
        a = _silu(acc)
        if j < 2 * A_HEADS:
            a = a * lax.rsqrt(jnp.sum(a * a, axis=-1, keepdims=True) + L2_EPS)
        if j < A_HEADS:
            a = a * (HEAD_DIM ** -0.5)
        outs[j // A_HEADS][:, _head_cols(0, j % A_HEADS)] = a
    ps = pr[:, SMALL_OFF:SMALL_OFF + LANES]
    beta_all = jax.nn.sigmoid(ps)
    eg_all = jnp.exp(-jnp.exp(alog_ref[...]) * _softplus(ps + dtb_ref[...]))
    cos_full = cos_ref[...]
    sin_signed = sin_ref[...]
    for h in range(A_HEADS):
        sc_ref[:, _head_cols(0, h)] = jnp.broadcast_to(eg_all[:, A_HEADS + h:A_HEADS + h + 1], (nb, LANES))
        sc_ref[:, _head_cols(A_WIDTH, h)] = jnp.broadcast_to(beta_all[:, h:h + 1], (nb, LANES))
        qk = jnp.sum(q_ref[:, _head_cols(0, h)] * k_ref[:, _head_cols(0, h)], axis=-1, keepdims=True)
        sc_ref[:, _head_cols(2 * A_WIDTH, h)] = jnp.broadcast_to(qk, (nb, LANES))
    for h in range(B_HEADS):
        q = _rope(pr[:, _head_cols(BQ_OFF, h)], cos_full, sin_signed)
        k = _rope(pr[:, _head_cols(BK_OFF, h)], cos_full, sin_signed) * (HEAD_DIM ** -0.5)
        q_ref[:, _head_cols(A_WIDTH, h)] = q
        k_ref[:, _head_cols(A_WIDTH, h)] = k
        sc_ref[:, _head_cols(3 * A_WIDTH, h)] = jnp.broadcast_to(
            jnp.sum(q * k, axis=-1, keepdims=True), (nb, LANES))
    v_ref[:, A_WIDTH:A_WIDTH + B_WIDTH] = pr[:, BV_OFF:BG_OFF]
    gates_ref[:, 0:A_WIDTH] = pr[:, Z_OFF:BQ_OFF]
    gates_ref[:, A_WIDTH:A_WIDTH + B_WIDTH] = pr[:, BG_OFF:SMALL_OFF]


def _sample_pre(x, cos_full, sin_signed, win, conv_w, alog, dtb, npre, conv_state, layer):
    nb, d_model = x.shape
    width = A_WIDTH + B_WIDTH
    lspec = functools.partial(_layer_spec, layer=layer)
    full = lambda shape: pl.BlockSpec(tuple(shape), lambda i: (0,) * len(shape))
    widths = (width, width, width, SAMPLE_SCALARS * A_WIDTH, width, (CONV_W - 1) * QKV_WIDTH)
    return pl.pallas_call(
        _sample_pre_body,
        grid=(1,),
        in_specs=[full((nb, d_model)), full((1, HEAD_DIM)), full((1, HEAD_DIM)),
                  lspec((d_model, IN_WIDTH_PADDED)), lspec((CONV_W, QKV_WIDTH)), lspec((1, LANES)),
                  lspec((1, LANES)), lspec((1, d_model)), lspec((nb, (CONV_W - 1) * QKV_WIDTH))],
        out_specs=[full((nb, w)) for w in widths],
        out_shape=[jax.ShapeDtypeStruct((nb, w), _F32) for w in widths],
        compiler_params=pltpu.CompilerParams(
            dimension_semantics=("arbitrary",), vmem_limit_bytes=VMEM_LIMIT_BYTES),
        name=f"sample_pre_l{layer}",
    )(x, cos_full, sin_signed, win, conv_w, alog, dtb, npre, conv_state)


def _sample_post_body(o_ref, gates_ref, x_ref, gnw_ref, rnw_ref, wout_ref, npost_ref, wgu_ref, wd_ref,
                      fpre_ref, fpost_ref, y_ref, cat_ref, act_ref, *, d_ff):
    for h in range(A_HEADS + B_HEADS):
        cs = _head_cols(0, h)
        w = gnw_ref[...] if h < A_HEADS else rnw_ref[...]
        cat_ref[:, cs] = _gated_head_norm(o_ref[:, cs], w, gates_ref[:, cs]).astype(_BF16)
    m = jnp.dot(cat_ref[...], wout_ref[...], preferred_element_type=_F32)
    h_new = x_ref[...] + _rmsnorm(m, npost_ref[...])
    y_ref[...] = _ffn_rows(h_new, wgu_ref, wd_ref, fpre_ref[...], fpost_ref[...], act_ref, d_ff)


def _sample_post(o, gates, x, gnw, rnw, wout, npost, wgu, wd, fpre, fpost, layer):
    nb, d_model = x.shape
    d_ff = wd.shape[1]
    width = A_WIDTH + B_WIDTH
    lspec = functools.partial(_layer_spec, layer=layer)
    full = lambda shape: pl.BlockSpec(tuple(shape), lambda i: (0,) * len(shape))
    return pl.pallas_call(
        functools.partial(_sample_post_body, d_ff=d_ff),
        grid=(1,),
        in_specs=[full((nb, width)), full((nb, width)), full((nb, d_model)), lspec((1, HEAD_DIM)),
                  lspec((1, HEAD_DIM)), lspec((width, d_model)), lspec((1, d_model)),
                  lspec((d_model, 2 * d_ff)), lspec((d_ff, d_model)), lspec((1, d_model)),
                  lspec((1, d_model))],
        out_specs=full((nb, d_model)),
        out_shape=jax.ShapeDtypeStruct((nb, d_model), _F32),
        scratch_shapes=[pltpu.VMEM((nb, width), _BF16), pltpu.VMEM((nb, d_ff), _BF16)],
        compiler_params=pltpu.CompilerParams(
            dimension_semantics=("arbitrary",), vmem_limit_bytes=VMEM_LIMIT_BYTES),
        name=f"sample_post_l{layer}",
    )(o, gates, x, gnw, rnw, wout, npost, wgu, wd, fpre, fpost)


def _rope_tables(positions):
    half = HEAD_DIM // 2
    inv = ROPE_BASE ** (-np.arange(half, dtype=np.float64) / half)
    ang = np.asarray(positions, dtype=np.float64)[:, None] * inv[None, :]
    cos, sin = np.cos(ang), np.sin(ang)
    return (jnp.asarray(np.concatenate([cos, cos], axis=-1), dtype=_F32),
            jnp.asarray(np.concatenate([-sin, sin], axis=-1), dtype=_F32))


def _rearranged_w_in_body(wt_ref, o_ref):
    small0 = QKV_WIDTH + A_WIDTH
    small1 = small0 + 2 * A_HEADS
    cb = WEIGHT_COL_BLOCK
    for lo in range(0, small0, cb):
        o_ref[:, lo:lo + cb] = wt_ref[lo:lo + cb, :].T.astype(_BF16)
    for lo in range(small0, QKV_WIDTH + SMALL_OFF, cb):
        src = lo + small1 - small0
        o_ref[:, lo:lo + cb] = wt_ref[src:src + cb, :].T.astype(_BF16)
    lane = lax.broadcasted_iota(jnp.int32, (o_ref.shape[0], LANES), 1)
    o_ref[:, QKV_WIDTH + SMALL_OFF:IN_WIDTH_PADDED] = jnp.where(
        lane < 2 * A_HEADS, wt_ref[small0:small0 + LANES, :].T, 0.0).astype(_BF16)


def _rearranged_w_in(w_in):
    depth, d_model, width = w_in.shape
    assert width == IN_WIDTH_PADDED - LANES + 2 * A_HEADS
    return pl.pallas_call(
        _rearranged_w_in_body,
        grid=(depth,),
        in_specs=[pl.BlockSpec((None, width, d_model), lambda l: (l, 0, 0), pipeline_mode=pl.Buffered(1))],
        out_specs=pl.BlockSpec((None, d_model, IN_WIDTH_PADDED), lambda l: (l, 0, 0)),
        out_shape=jax.ShapeDtypeStruct((depth, d_model, IN_WIDTH_PADDED), _BF16),
        compiler_params=pltpu.CompilerParams(
            dimension_semantics=("arbitrary",), vmem_limit_bytes=VMEM_LIMIT_BYTES),
        name="rearranged_w_in",
    )(jnp.swapaxes(w_in, 1, 2))


def kernel(x_prompt, x_sample, state_conv, state_gdn, state_ret, w_in, conv_w, a_log, dt_bias, gdn_norm_w, ret_norm_w, w_out, norm_mix_pre, norm_mix_post, norm_ffn_pre, norm_ffn_post, w_gate_up, w_down):
    depth = w_in.shape[0]
    batch, seq, d_model = x_prompt.shape
    nb, seq_s, _ = x_sample.shape
    assert seq_s == 1

    win = _rearranged_w_in(w_in)
    wout = w_out.astype(_BF16)
    wgu = w_gate_up.astype(_BF16)
    wd = w_down.astype(_BF16)
    alog = jnp.pad(a_log, ((0, 0), (A_HEADS, LANES - 2 * A_HEADS)))[:, None, :]
    dtb = jnp.pad(dt_bias, ((0, 0), (A_HEADS, LANES - 2 * A_HEADS)))[:, None, :]
    gnw = gdn_norm_w[:, None, :]
    rnw = ret_norm_w[:, None, :]
    npre = norm_mix_pre[:, None, :]
    npost = norm_mix_post[:, None, :]
    fpre = norm_ffn_pre[:, None, :]
    fpost = norm_ffn_post[:, None, :]
    cos_p, sin_p = _rope_tables(np.arange(seq))
    cos_s, sin_s = _rope_tables(PAST_LEN + np.arange(seq_s))
    conv_state = state_conv.reshape(depth, nb, (CONV_W - 1) * QKV_WIDTH)

    hp = x_prompt
    hs = x_sample.reshape(nb, d_model)
    convs_p, gdns_p, rets_p, convs_s = [], [], [], []
    gdn_s = ret_s = None
    for l in range(depth):
        sq, sk, sv, sc, gates, conv_s = _sample_pre(hs, cos_s, sin_s, win, conv_w, alog, dtb, npre,
                                                    conv_state, l)
        hp, conv_p, gdn_p, ret_p, so, gdn_s, ret_s = _layer_prompt(
            hp, cos_p, sin_p, win, conv_w, alog, dtb, gnw, rnw, wout, npre, npost, wgu, wd, fpre, fpost,
            sq, sk, sv, sc, state_gdn, state_ret, gdn_s, ret_s, l)
        hs = _sample_post(so, gates, hs, gnw, rnw, wout, npost, wgu, wd, fpre, fpost, l)
        convs_p.append(conv_p)
        gdns_p.append(gdn_p)
        rets_p.append(ret_p)
        convs_s.append(conv_s.reshape(nb, CONV_W - 1, QKV_WIDTH))
    return (hp, hs.reshape(nb, seq_s, d_model), jnp.stack(convs_p), jnp.stack(gdns_p),
            jnp.stack(rets_p), jnp.stack(convs_s), gdn_s, ret_s)
```

```python
import functools
import math

import jax
import jax.numpy as jnp
import numpy as np
from jax import lax
from jax.experimental import pallas as pl
from jax.experimental.pallas import tpu as pltpu

HEAD_DIM = 128
A_HEADS = 4
B_HEADS = 4
A_WIDTH = A_HEADS * HEAD_DIM
B_WIDTH = B_HEADS * HEAD_DIM
QKV_WIDTH = 3 * A_WIDTH
CONV_W = 4
CHUNK = 64
ROPE_BASE = 10000.0
EPS = 1e-6
L2_EPS = 1e-6
MASKED_LOG = -1e30
PAST_LEN = 16384
LANES = 128
SUBLANES = 8

REST_WIDTH = A_WIDTH + 4 * B_WIDTH + LANES
Z_OFF = 0
BQ_OFF = A_WIDTH
BK_OFF = BQ_OFF + B_WIDTH
BV_OFF = BK_OFF + B_WIDTH
BG_OFF = BV_OFF + B_WIDTH
SMALL_OFF = BG_OFF + B_WIDTH
IN_WIDTH_PADDED = QKV_WIDTH + REST_WIDTH

GROUP = 256
CHUNKS_PER_GROUP = GROUP // CHUNK
PROMPT_TIME_BLOCK = 256
FFN_COL_BLOCK = 256
WEIGHT_COL_BLOCK = 256
SAMPLE_SCALARS = 4
VMEM_LIMIT_BYTES = 56 * 1024 * 1024

_BF16 = jnp.bfloat16
_F32 = jnp.float32


def _dot(a, b):
    return jnp.dot(a.astype(_BF16), b.astype(_BF16), preferred_element_type=_F32)


def _dot_nt(a, b):
    return lax.dot_general(a.astype(_BF16), b.astype(_BF16), (((1,), (1,)), ((), ())),
                           preferred_element_type=_F32)


def _dot_tn(a, b):
    return lax.dot_general(a.astype(_BF16), b.astype(_BF16), (((0,), (0,)), ((), ())),
                           preferred_element_type=_F32)


def _rmsnorm(x, w):
    return x * lax.rsqrt(jnp.mean(x * x, axis=-1, keepdims=True) + EPS) * w


def _silu(x):
    return x * jax.nn.sigmoid(x)


def _softplus(x):
    return jnp.maximum(x, 0.0) + jnp.log1p(jnp.exp(-jnp.abs(x)))


def _log_gamma(h):
    return math.log1p(-(2.0 ** (-5.0 - h)))


def _rope(x, cos_full, sin_signed):
    return x * cos_full + pltpu.roll(x, HEAD_DIM // 2, axis=1) * sin_signed


def _head_cols(base, h):
    return slice(base + h * HEAD_DIM, base + (h + 1) * HEAD_DIM)


def _gated_head_norm(o, w, gate):
    return o * lax.rsqrt(jnp.mean(o * o, axis=-1, keepdims=True) + EPS) * w * _silu(gate)


class _Filler:
    def __init__(self, pieces):
        self._pieces = list(pieces)
        self._next = 0

    def emit(self, count):
        for _ in range(count):
            if self._next < len(self._pieces):
                self._pieces[self._next]()
                self._next += 1

    def flush(self):
        self.emit(len(self._pieces))


def _fold_rows(m):
    out = m[0:CHUNK]
    for c in range(1, CHUNKS_PER_GROUP):
        out = out + m[c * CHUNK:(c + 1) * CHUNK]
    return out


def _unfold_rows(r, same_chunk):
    return jnp.where(same_chunk, jnp.concatenate([r] * CHUNKS_PER_GROUP, axis=0), 0.0)


def _unit_lower_inverses(a_folded, eye_folded, same_chunk, fill):
    xs = list(a_folded)
    ps = [eye_folded - x for x in xs]
    n = 1
    while n < CHUNK:
        for h in range(len(xs)):
            x_bd = _unfold_rows(xs[h], same_chunk).astype(_BF16)
            if n == 1:
                xs[h] = _dot(xs[h], x_bd)
            elif 2 * n < CHUNK:
                r = _dot(jnp.concatenate([xs[h], ps[h]], axis=0), x_bd)
                xs[h] = r[0:CHUNK]
                ps[h] = ps[h] + r[CHUNK:2 * CHUNK]
            else:
                ps[h] = ps[h] + _dot(ps[h], x_bd)
            fill.emit(h % 2)
        n *= 2
    return [_unfold_rows(p, same_chunk) for p in ps]


def _recurrences_one_group(g, live, fill, act_ref, pr_ref, small_ref, dec_ref, sg_ref, sr_ref, o_ref):
    rows = slice(g * GROUP, (g + 1) * GROUP)
    ri = lax.broadcasted_iota(jnp.int32, (GROUP, GROUP), 0)
    ci = lax.broadcasted_iota(jnp.int32, (GROUP, GROUP), 1)
    same_chunk = (ri // CHUNK) == (ci // CHUNK)
    tril = same_chunk & (ri >= ci)
    off_diag = ri != ci
    rf = lax.broadcasted_iota(jnp.int32, (CHUNK, GROUP), 0)
    cf = lax.broadcasted_iota(jnp.int32, (CHUNK, GROUP), 1)
    eye_folded = jnp.where(rf == cf % CHUNK, 1.0, 0.0).astype(_F32)
    pos = (lax.broadcasted_iota(jnp.int32, (GROUP, HEAD_DIM), 0) % CHUNK).astype(_F32)
    chunk_rows = [slice(c * CHUNK, (c + 1) * CHUNK) for c in range(CHUNKS_PER_GROUP)]

    qs, ks, gccs, a_folded, a_intra, uw_rhs, e_ins = [], [], [], [], [], [], []
    for h in range(A_HEADS):
        q = act_ref[rows, _head_cols(0, h)]
        k = act_ref[rows, _head_cols(A_WIDTH, h)]
        v = act_ref[rows, _head_cols(2 * A_WIDTH, h)]
        beta = jnp.broadcast_to(small_ref[0, rows, h:h + 1], (GROUP, HEAD_DIM))
        gc_col = small_ref[1, rows, A_HEADS + h:A_HEADS + h + 1]
        gcc = jnp.broadcast_to(gc_col, (GROUP, HEAD_DIM))
        decay = dec_ref[h]
        kb = k * beta
        e_in = jnp.exp(gcc)
        kq = _dot_nt(jnp.concatenate([kb, q], axis=0), k)
        a_folded.append(_fold_rows(jnp.where(off_diag, kq[0:GROUP] * decay, 0.0)))
        a_intra.append(kq[GROUP:2 * GROUP] * decay)
        uw_rhs.append(jnp.concatenate([v * beta, kb * e_in], axis=1).astype(_BF16))
        qs.append(q)
        ks.append(k)
        gccs.append(gcc)
        e_ins.append(e_in)
        fill.emit(1)

    r_qe, r_oloc, r_b, r_echunk = [], [], [], []
    for h in range(B_HEADS):
        lg = _log_gamma(h)
        q = pr_ref[rows, _head_cols(BQ_OFF, h)]
        k = pr_ref[rows, _head_cols(BK_OFF, h)]
        v = pr_ref[rows, _head_cols(BV_OFF, h)]
        decay = jnp.exp(jnp.where(tril, (ri - ci).astype(_F32) * lg, MASKED_LOG))
        vb16 = v.astype(_BF16)
        r_oloc.append(_dot(_dot_nt(q, k) * decay, vb16))
        r_qe.append((q * jnp.exp((pos + 1.0) * lg)).astype(_BF16))
        k_out = (k * jnp.exp((CHUNK - 1.0 - pos) * lg)).astype(_BF16)
        r_b.append([_dot_tn(k_out[cr], vb16[cr]) for cr in chunk_rows])
        r_echunk.append(math.exp(CHUNK * lg))

    t_inv = _unit_lower_inverses(a_folded, eye_folded, same_chunk, fill)

    g_lhs, g_oloc, g_b, g_elast = [], [], [], []
    for h in range(A_HEADS):
        uw = _dot(t_inv[h], uw_rhs[h])
        uw16 = uw.astype(_BF16)
        aiuw = _dot(a_intra[h], uw16)
        g_oloc.append(aiuw[:, 0:HEAD_DIM])
        q_eff = qs[h] * e_ins[h] - aiuw[:, HEAD_DIM:2 * HEAD_DIM]
        lhs, bs, elast = [], [], []
        for cr in chunk_rows:
            gcc_c = gccs[h][cr]
            g_last = gcc_c[CHUNK - 1:CHUNK, :]
            k_out = ks[h][cr] * jnp.exp(g_last - gcc_c)
            bg = _dot_tn(k_out, uw16[cr])
            bs.append(bg[:, 0:HEAD_DIM])
            lhs.append(jnp.concatenate([q_eff[cr], bg[:, HEAD_DIM:2 * HEAD_DIM]], axis=0).astype(_BF16))
            elast.append(jnp.exp(g_last))
        g_lhs.append(lhs)
        g_b.append(bs)
        g_elast.append(elast)
        fill.emit(1)

    g_state = [sg_ref[h] for h in range(A_HEADS)]
    r_state = [sr_ref[h] for h in range(B_HEADS)]
    for c, cr in enumerate(chunk_rows):
        out_rows = slice(g * GROUP + c * CHUNK, g * GROUP + (c + 1) * CHUNK)
        for h in range(A_HEADS):
            r = _dot(g_lhs[h][c], g_state[h])
            o_ref[out_rows, _head_cols(0, h)] = r[0:CHUNK] + g_oloc[h][cr]
            g_state[h] = g_state[h] * g_elast[h][c] + g_b[h][c] - r[CHUNK:CHUNK + HEAD_DIM]
        for h in range(B_HEADS):
            o_ref[out_rows, _head_cols(A_WIDTH, h)] = _dot(r_qe[h][cr], r_state[h]) + r_oloc[h][cr]
            r_state[h] = r_state[h] * r_echunk[h] + r_b[h][c]
        fill.emit(1)
    for h in range(A_HEADS):
        sg_ref[h] = jnp.where(live, g_state[h], sg_ref[h])
    for h in range(B_HEADS):
        sr_ref[h] = jnp.where(live, r_state[h], sr_ref[h])


def _ffn_rows(h, wgu_ref, wd_ref, npre, npost, act_ref, d_ff):
    hn = _rmsnorm(h, npre).astype(_BF16)
    for j in range(d_ff // FFN_COL_BLOCK):
        cg = slice(j * FFN_COL_BLOCK, (j + 1) * FFN_COL_BLOCK)
        cu = slice(d_ff + j * FFN_COL_BLOCK, d_ff + (j + 1) * FFN_COL_BLOCK)
        gate = jnp.dot(hn, wgu_ref[:, cg], preferred_element_type=_F32)
        up = jnp.dot(hn, wgu_ref[:, cu], preferred_element_type=_F32)
        act_ref[:, cg] = (_silu(gate) * up).astype(_BF16)
    f = jnp.dot(act_ref[...], wd_ref[...], preferred_element_type=_F32)
    return h + _rmsnorm(f, npost)


def _ffn_pieces(h_ref, hn_ref, wgu_ref, wd_ref, npost, gate_ref, act_ref, f_ref, y_ref, d_ff):
    d_model = f_ref.shape[1]
    k_split = (d_ff // FFN_COL_BLOCK + 1) // 2 * FFN_COL_BLOCK
    pieces = []

    def gate(j):
        def run():
            cg = slice(j * FFN_COL_BLOCK, (j + 1) * FFN_COL_BLOCK)
            gate_ref[j % 2] = _silu(jnp.dot(hn_ref[...], wgu_ref[:, cg], preferred_element_type=_F32))
        return run

    def up(j):
        def run():
            cg = slice(j * FFN_COL_BLOCK, (j + 1) * FFN_COL_BLOCK)
            cu = slice(d_ff + j * FFN_COL_BLOCK, d_ff + (j + 1) * FFN_COL_BLOCK)
            act_ref[:, cg] = (gate_ref[j % 2] * jnp.dot(hn_ref[...], wgu_ref[:, cu],
                                                        preferred_element_type=_F32)).astype(_BF16)
        return run

    def down(j, first):
        def run():
            cs = slice(j * FFN_COL_BLOCK, (j + 1) * FFN_COL_BLOCK)
            if first:
                f_ref[:, cs] = jnp.dot(act_ref[:, 0:k_split], wd_ref[0:k_split, cs],
                                       preferred_element_type=_F32)
            else:
                f_ref[:, cs] = f_ref[:, cs] + jnp.dot(act_ref[:, k_split:d_ff], wd_ref[k_split:d_ff, cs],
                                                      preferred_element_type=_F32)
        return run

    def finish():
        y_ref[...] = h_ref[...] + _rmsnorm(f_ref[...], npost)

    for j in range(d_ff // FFN_COL_BLOCK):
        pieces += [gate(j), up(j)]
    for j in range(d_model // FFN_COL_BLOCK):
        pieces += [down(j, True), down(j, False)]
    pieces.append(finish)
    return pieces


def _column_of_row(row):
    return jnp.broadcast_to(row, (HEAD_DIM, HEAD_DIM)).T


def _sample_state_units(first_seq, per_step, sq_ref, sk_ref, sv_ref, sc_ref, sgi_ref, sri_ref,
                        so_ref, sgo_ref, sro_ref):
    base = pl.multiple_of((first_seq // SUBLANES) * SUBLANES, SUBLANES)
    rows = pl.ds(base, SUBLANES)
    row_id = lax.broadcasted_iota(jnp.int32, (SUBLANES, HEAD_DIM), 0)
    picks = [row_id == (first_seq - base + j) for j in range(per_step)]

    def pick(block, j):
        return jnp.sum(jnp.where(picks[j], block, 0.0), axis=0, keepdims=True)

    def put(cols, outs):
        block = so_ref[rows, cols]
        for j in range(per_step):
            block = jnp.where(picks[j], outs[j], block)
        so_ref[rows, cols] = block

    def deltanet(h):
        def run():
            cols = _head_cols(0, h)
            q8, k8, v8 = sq_ref[rows, cols], sk_ref[rows, cols], sv_ref[rows, cols]
            eg8 = sc_ref[rows, _head_cols(0, h)]
            bt8 = sc_ref[rows, _head_cols(A_WIDTH, h)]
            qk8 = sc_ref[rows, _head_cols(2 * A_WIDTH, h)]
            kq16 = jnp.concatenate([k8, q8], axis=0).astype(_BF16)
            kqs = [jnp.dot(kq16, sgi_ref[j, h].astype(_BF16), preferred_element_type=_F32)
                   for j in range(per_step)]
            outs = []
            for j in range(per_step):
                eg = pick(eg8, j)
                v_new = pick(bt8, j) * (pick(v8, j) - eg * pick(kqs[j][0:SUBLANES], j))
                outs.append(eg * pick(kqs[j][SUBLANES:2 * SUBLANES], j) + pick(qk8, j) * v_new)
                sgo_ref[j, h] = sgi_ref[j, h] * eg + _column_of_row(pick(k8, j)) * v_new
            put(cols, outs)
        return run

    def retention(h):
        def run():
            gamma = math.exp(_log_gamma(h))
            cols = _head_cols(A_WIDTH, h)
            q8, k8, v8 = sq_ref[rows, cols], sk_ref[rows, cols], sv_ref[rows, cols]
            qk8 = sc_ref[rows, _head_cols(3 * A_WIDTH, h)]
            q16 = q8.astype(_BF16)
            qss = [jnp.dot(q16, sri_ref[j, h].astype(_BF16), preferred_element_type=_F32)
                   for j in range(per_step)]
            outs = []
            for j in range(per_step):
                v1 = pick(v8, j)
                outs.append(gamma * pick(qss[j], j) + pick(qk8, j) * v1)
                sro_ref[j, h] = sri_ref[j, h] * gamma + _column_of_row(pick(k8, j)) * v1
            put(cols, outs)
        return run

    return [deltanet(h) for h in range(A_HEADS)] + [retention(h) for h in range(B_HEADS)]


def _layer_prompt_body(*refs, aliased, tb, nt, nblocks, d_ff, per_step):
    (x_ref, cos_ref, sin_ref, win_ref, convw_ref, alog_ref, dtb_ref, gnw_ref, rnw_ref, wout_ref,
     npre_ref, npost_ref, wgu_ref, wd_ref, fpre_ref, fpost_ref,
     sq_ref, sk_ref, sv_ref, sc_ref, sgi_ref, sri_ref) = refs[:22]
    (y_ref, conv_ref, sg_ref, sr_ref, so_ref, sgo_ref, sro_ref,
     pq_ref, pr_ref, act_ref, small_ref, gct_ref, dec_ref, o_ref, h_ref, hn_ref, gate_ref,
     ffn_act_ref, f_ref) = refs[22 + (2 if aliased else 0):]
    s = pl.program_id(0)
    live = s < nblocks
    t = lax.rem(jnp.minimum(s, nblocks - 1), nt)
    pad = SUBLANES

    @pl.when(s == 0)
    def _():
        h_ref[1] = jnp.zeros(h_ref.shape[1:], _F32)
        hn_ref[1] = jnp.zeros(hn_ref.shape[1:], _BF16)
        so_ref[...] = jnp.zeros(so_ref.shape, _F32)

    @pl.when((t == 0) & live)
    def _():
        pq_ref[0:pad, :] = jnp.zeros((pad, QKV_WIDTH), _F32)
        sg_ref[...] = jnp.zeros(sg_ref.shape, _F32)
        sr_ref[...] = jnp.zeros(sr_ref.shape, _F32)

    prev = lax.rem(s + 1, 2)
    fill = _Filler(_ffn_pieces(h_ref.at[prev], hn_ref.at[prev], wgu_ref, wd_ref, fpost_ref[...],
                               gate_ref, ffn_act_ref, f_ref, y_ref, d_ff))
    fill.emit(2)
    sample = _Filler(_sample_state_units(jnp.minimum(s, nblocks - 1) * per_step, per_step, sq_ref, sk_ref,
                                         sv_ref, sc_ref, sgi_ref, sri_ref, so_ref, sgo_ref, sro_ref))

    x = x_ref[...]
    hn = _rmsnorm(x, npre_ref[...]).astype(_BF16)

    def project(lo, hi):
        return jnp.dot(hn, win_ref[:, lo:hi], preferred_element_type=_F32)

    def project_rest(lo, hi):
        pr_ref[:, lo:hi] = project(QKV_WIDTH + lo, QKV_WIDTH + hi)

    ps = project(QKV_WIDTH + SMALL_OFF, IN_WIDTH_PADDED)
    fill.emit(1)
    beta_all = jax.nn.sigmoid(ps)
    gc = -jnp.exp(alog_ref[...]) * _softplus(ps + dtb_ref[...])
    row_in_chunk = lax.broadcasted_iota(jnp.int32, (tb, LANES), 0) % CHUNK
    shift = 1
    while shift < CHUNK:
        gc = gc + jnp.where(row_in_chunk >= shift, pltpu.roll(gc, shift, axis=0), 0.0)
        shift *= 2
    small_ref[0] = beta_all
    small_ref[1] = gc
    gct_ref[...] = gc.T

    half = B_WIDTH // 2
    cos_full = cos_ref[...]
    sin_signed = sin_ref[...]
    ri = lax.broadcasted_iota(jnp.int32, (GROUP, GROUP), 0)
    ci = lax.broadcasted_iota(jnp.int32, (GROUP, GROUP), 1)
    tril = ((ri // CHUNK) == (ci // CHUNK)) & (ri >= ci)
    for pair in range(B_HEADS // 2):
        project_rest(BQ_OFF + pair * half, BQ_OFF + (pair + 1) * half)
        fill.emit(1)
        project_rest(BK_OFF + pair * half, BK_OFF + (pair + 1) * half)
        fill.emit(1)
        for h in (2 * pair, 2 * pair + 1):
            gc_col = small_ref[1, :, A_HEADS + h:A_HEADS + h + 1]
            gcr = jnp.broadcast_to(gct_ref[A_HEADS + h:A_HEADS + h + 1, :], (GROUP, GROUP))
            diff = jnp.broadcast_to(gc_col, (GROUP, GROUP)) - gcr
            dec_ref[h] = jnp.exp(jnp.where(tril, diff, MASKED_LOG))
            cq = _head_cols(BQ_OFF, h)
            ck = _head_cols(BK_OFF, h)
            pr_ref[:, cq] = _rope(pr_ref[:, cq], cos_full, sin_signed)
            pr_ref[:, ck] = _rope(pr_ref[:, ck], cos_full, sin_signed) * (HEAD_DIM ** -0.5)

    for lo in range(0, QKV_WIDTH, FFN_COL_BLOCK):
        pq_ref[pad:pad + tb, lo:lo + FFN_COL_BLOCK] = project(lo, lo + FFN_COL_BLOCK)
        fill.emit(1)
        sample.emit(1)

    rest = [(lo, lo + FFN_COL_BLOCK) for lo in range(Z_OFF, BQ_OFF, FFN_COL_BLOCK)]
    rest += [(lo, lo + FFN_COL_BLOCK) for lo in range(BV_OFF, SMALL_OFF, FFN_COL_BLOCK)]
    for j in range(QKV_WIDTH // LANES):
        if j % 2 == 0 and rest:
            project_rest(*rest.pop(0))
        else:
            fill.emit(1)
        cs = slice(j * LANES, (j + 1) * LANES)
        acc = pq_ref[pad:pad + tb, cs] * convw_ref[3:4, cs]
        for i in range(CONV_W - 1):
            acc = acc + pq_ref[pad - 3 + i:pad - 3 + i + tb, cs] * convw_ref[i:i + 1, cs]
        a = _silu(acc)
        if j < 2 * A_HEADS:
            a = a * lax.rsqrt(jnp.sum(a * a, axis=-1, keepdims=True) + L2_EPS)
        if j < A_HEADS:
            a = a * (HEAD_DIM ** -0.5)
        act_ref[:, cs] = a
    assert not rest

    tail = pq_ref[pad + tb - 3:pad + tb, :]
    conv_ref[...] = tail
    pq_ref[pad - 3:pad, :] = tail

    for g in range(tb // GROUP):
        _recurrences_one_group(g, live, fill, act_ref, pr_ref, small_ref, dec_ref, sg_ref, sr_ref, o_ref)

    for h in range(A_HEADS):
        cs = _head_cols(0, h)
        o_ref[:, cs] = _gated_head_norm(o_ref[:, cs], gnw_ref[...], pr_ref[:, _head_cols(Z_OFF, h)])
    for h in range(B_HEADS):
        cs = _head_cols(A_WIDTH, h)
        o_ref[:, cs] = _gated_head_norm(o_ref[:, cs], rnw_ref[...], pr_ref[:, _head_cols(BG_OFF, h)])
    m = jnp.dot(o_ref[...].astype(_BF16), wout_ref[...], preferred_element_type=_F32)
    fill.emit(8)
    sample.flush()
    h_new = x_ref[...] + _rmsnorm(m, npost_ref[...])
    cur = lax.rem(s, 2)
    h_ref[cur] = h_new
    hn_ref[cur] = _rmsnorm(h_new, fpre_ref[...]).astype(_BF16)
    fill.flush()


def _layer_spec(shape, layer):
    zeros = (0,) * len(shape)
    return pl.BlockSpec((None,) + tuple(shape), lambda i: (layer,) + zeros)


def _layer_prompt(x, cos_full, sin_signed, win, conv_w, alog, dtb, gnw, rnw, wout, npre, npost,
                  wgu, wd, fpre, fpost, sq, sk, sv, sc, state_gdn, state_ret, prev_gdn, prev_ret, layer):
    batch, seq, d_model = x.shape
    nb = sq.shape[0]
    depth = state_gdn.shape[0]
    d_ff = wd.shape[1]
    tb = min(PROMPT_TIME_BLOCK, seq)
    assert seq % tb == 0 and tb == GROUP and d_ff % FFN_COL_BLOCK == 0
    nt = seq // tb
    nblocks = batch * nt
    assert nb % nblocks == 0 and SUBLANES % (nb // nblocks) == 0
    per_step = nb // nblocks
    lspec = functools.partial(_layer_spec, layer=layer)

    def mixer_block(s):
        return jnp.minimum(s, nblocks - 1)

    def ffn_block(s):
        return jnp.maximum(s - 1, 0)

    full = lambda shape: pl.BlockSpec(tuple(shape), lambda s: (0,) * len(shape))
    state_spec = lambda heads: pl.BlockSpec((None, per_step, heads, HEAD_DIM, HEAD_DIM),
                                            lambda s: (layer, mixer_block(s), 0, 0, 0))

    in_specs = [
        pl.BlockSpec((None, tb, d_model), lambda s: (mixer_block(s) // nt, mixer_block(s) % nt, 0)),
        pl.BlockSpec((tb, HEAD_DIM), lambda s: (mixer_block(s) % nt, 0)),
        pl.BlockSpec((tb, HEAD_DIM), lambda s: (mixer_block(s) % nt, 0)),
        lspec((d_model, IN_WIDTH_PADDED)),
        lspec((CONV_W, QKV_WIDTH)),
        lspec((1, LANES)),
        lspec((1, LANES)),
        lspec((1, HEAD_DIM)),
        lspec((1, HEAD_DIM)),
        lspec((A_WIDTH + B_WIDTH, d_model)),
        lspec((1, d_model)),
        lspec((1, d_model)),
        lspec((d_model, 2 * d_ff)),
        lspec((d_ff, d_model)),
        lspec((1, d_model)),
        lspec((1, d_model)),
        full(sq.shape), full(sk.shape), full(sv.shape), full(sc.shape),
        state_spec(A_HEADS), state_spec(B_HEADS),
    ]
    args = [x, cos_full, sin_signed, win, conv_w, alog, dtb, gnw, rnw, wout, npre, npost, wgu, wd, fpre,
            fpost, sq, sk, sv, sc, state_gdn, state_ret]
    aliases = {}
    if prev_gdn is not None:
        in_specs += [pl.BlockSpec(memory_space=pl.ANY), pl.BlockSpec(memory_space=pl.ANY)]
        aliases = {len(args): 5, len(args) + 1: 6}
        args += [prev_gdn, prev_ret]
    out_specs = [
        pl.BlockSpec((None, tb, d_model), lambda s: (ffn_block(s) // nt, ffn_block(s) % nt, 0)),
        pl.BlockSpec((None, CONV_W - 1, QKV_WIDTH), lambda s: (mixer_block(s) // nt, 0, 0)),
        pl.BlockSpec((None, A_HEADS, HEAD_DIM, HEAD_DIM), lambda s: (mixer_block(s) // nt, 0, 0, 0)),
        pl.BlockSpec((None, B_HEADS, HEAD_DIM, HEAD_DIM), lambda s: (mixer_block(s) // nt, 0, 0, 0)),
        full(sq.shape), state_spec(A_HEADS), state_spec(B_HEADS),
    ]
    out_shape = [
        jax.ShapeDtypeStruct((batch, seq, d_model), _F32),
        jax.ShapeDtypeStruct((batch, CONV_W - 1, QKV_WIDTH), _F32),
        jax.ShapeDtypeStruct((batch, A_HEADS, HEAD_DIM, HEAD_DIM), _F32),
        jax.ShapeDtypeStruct((batch, B_HEADS, HEAD_DIM, HEAD_DIM), _F32),
        jax.ShapeDtypeStruct(sq.shape, _F32),
        jax.ShapeDtypeStruct((depth, nb, A_HEADS, HEAD_DIM, HEAD_DIM), _F32),
        jax.ShapeDtypeStruct((depth, nb, B_HEADS, HEAD_DIM, HEAD_DIM), _F32),
    ]
    scratch = [
        pltpu.VMEM((tb + SUBLANES, QKV_WIDTH), _F32),
        pltpu.VMEM((tb, REST_WIDTH), _F32),
        pltpu.VMEM((tb, QKV_WIDTH), _F32),
        pltpu.VMEM((2, tb, LANES), _F32),
        pltpu.VMEM((LANES, tb), _F32),
        pltpu.VMEM((A_HEADS, GROUP, GROUP), _F32),
        pltpu.VMEM((tb, A_WIDTH + B_WIDTH), _F32),
        pltpu.VMEM((2, tb, d_model), _F32),
        pltpu.VMEM((2, tb, d_model), _BF16),
        pltpu.VMEM((2, tb, FFN_COL_BLOCK), _F32),
        pltpu.VMEM((tb, d_ff), _BF16),
        pltpu.VMEM((tb, d_model), _F32),
    ]
    return pl.pallas_call(
        functools.partial(_layer_prompt_body, aliased=prev_gdn is not None, tb=tb, nt=nt, nblocks=nblocks,
                          d_ff=d_ff, per_step=per_step),
        grid=(nblocks + 1,), in_specs=in_specs, out_specs=out_specs, out_shape=out_shape,
        scratch_shapes=scratch, input_output_aliases=aliases,
        compiler_params=pltpu.CompilerParams(
            dimension_semantics=("arbitrary",), vmem_limit_bytes=VMEM_LIMIT_BYTES),
        name=f"layer_prompt_l{layer}",
    )(*args)


def _sample_pre_body(x_ref, cos_ref, sin_ref, win_ref, convw_ref, alog_ref, dtb_ref, npre_ref, convs_ref,
                     q_ref, k_ref, v_ref, sc_ref, gates_ref, convn_ref):
    nb = x_ref.shape[0]
    hn = _rmsnorm(x_ref[...], npre_ref[...]).astype(_BF16)
    pq = jnp.dot(hn, win_ref[:, 0:QKV_WIDTH], preferred_element_type=_F32)
    pr = jnp.dot(hn, win_ref[:, QKV_WIDTH:IN_WIDTH_PADDED], preferred_element_type=_F32)
    outs = (q_ref, k_ref, v_ref)
    for j in range(QKV_WIDTH // LANES):
        cs = slice(j * LANES, (j + 1) * LANES)
        new = pq[:, cs]
        acc = new * convw_ref[3:4, cs]
        for r in range(CONV_W - 1):
            hist = convs_ref[:, r * QKV_WIDTH + j * LANES:r * QKV_WIDTH + (j + 1) * LANES]
            acc = acc + hist * convw_ref[r:r + 1, cs]
            if r > 0:
                convn_ref[:, (r - 1) * QKV_WIDTH + j * LANES:(r - 1) * QKV_WIDTH + (j + 1) * LANES] = hist
        convn_ref[:, (CONV_W - 2) * QKV_WIDTH + j * LANES:(CONV_W - 2) * QKV_WIDTH + (j + 1) * LANES] = new
        a = _silu(acc)
        if j < 2 * A_HEADS:
            a = a * lax.rsqrt(jnp.sum(a * a, axis=-1, keepdims=True) + L2_EPS)
        if j < A_HEADS:
            a = a * (HEAD_DIM ** -0.5)
        outs[j // A_HEADS][:, _head_cols(0, j % A_HEADS)] = a
    ps = pr[:, SMALL_OFF:SMALL_OFF + LANES]
    beta_all = jax.nn.sigmoid(ps)
    eg_all = jnp.exp(-jnp.exp(alog_ref[...]) * _softplus(ps + dtb_ref[...]))
    cos_full = cos_ref[...]
    sin_signed = sin_ref[...]
    for h in range(A_HEADS):
        sc_ref[:, _head_cols(0, h)] = jnp.broadcast_to(eg_all[:, A_HEADS + h:A_HEADS + h + 1], (nb, LANES))
        sc_ref[:, _head_cols(A_WIDTH, h)] = jnp.broadcast_to(beta_all[:, h:h + 1], (nb, LANES))
        qk = jnp.sum(q_ref[:, _head_cols(0, h)] * k_ref[:, _head_cols(0, h)], axis=-1, keepdims=True)
        sc_ref[:, _head_cols(2 * A_WIDTH, h)] = jnp.broadcast_to(qk, (nb, LANES))
    for h in range(B_HEADS):
        q = _rope(pr[:, _head_cols(BQ_OFF, h)], cos_full, sin_signed)
        k = _rope(pr[:, _head_cols(BK_OFF, h)], cos_full, sin_signed) * (HEAD_DIM ** -0.5)
        q_ref[:, _head_cols(A_WIDTH, h)] = q
        k_ref[:, _head_cols(A_WIDTH, h)] = k
        sc_ref[:, _head_cols(3 * A_WIDTH, h)] = jnp.broadcast_to(
            jnp.sum(q * k, axis=-1, keepdims=True), (nb, LANES))
    v_ref[:, A_WIDTH:A_WIDTH + B_WIDTH] = pr[:, BV_OFF:BG_OFF]
    gates_ref[:, 0:A_WIDTH] = pr[:, Z_OFF:BQ_OFF]
    gates_ref[:, A_WIDTH:A_WIDTH + B_WIDTH] = pr[:, BG_OFF:SMALL_OFF]


def _sample_pre(x, cos_full, sin_signed, win, conv_w, alog, dtb, npre, conv_state, layer):
    nb, d_model = x.shape
    width = A_WIDTH + B_WIDTH
    lspec = functools.partial(_layer_spec, layer=layer)
    full = lambda shape: pl.BlockSpec(tuple(shape), lambda i: (0,) * len(shape))
    widths = (width, width, width, SAMPLE_SCALARS * A_WIDTH, width, (CONV_W - 1) * QKV_WIDTH)
    return pl.pallas_call(
        _sample_pre_body,
        grid=(1,),
        in_specs=[full((nb, d_model)), full((1, HEAD_DIM)), full((1, HEAD_DIM)),
                  lspec((d_model, IN_WIDTH_PADDED)), lspec((CONV_W, QKV_WIDTH)), lspec((1, LANES)),
                  lspec((1, LANES)), lspec((1, d_model)), lspec((nb, (CONV_W - 1) * QKV_WIDTH))],
        out_specs=[full((nb, w)) for w in widths],
        out_shape=[jax.ShapeDtypeStruct((nb, w), _F32) for w in widths],
        compiler_params=pltpu.CompilerParams(
            dimension_semantics=("arbitrary",), vmem_limit_bytes=VMEM_LIMIT_BYTES),
        name=f"sample_pre_l{layer}",
    )(x, cos_full, sin_signed, win, conv_w, alog, dtb, npre, conv_state)


def _sample_post_body(o_ref, gates_ref, x_ref, gnw_ref, rnw_ref, wout_ref, npost_ref, wgu_ref, wd_ref,
                      fpre_ref, fpost_ref, y_ref, cat_ref, act_ref, *, d_ff):
    for h in range(A_HEADS + B_HEADS):
        cs = _head_cols(0, h)
        w = gnw_ref[...] if h < A_HEADS else rnw_ref[...]
        cat_ref[:, cs] = _gated_head_norm(o_ref[:, cs], w, gates_ref[:, cs]).astype(_BF16)
    m = jnp.dot(cat_ref[...], wout_ref[...], preferred_element_type=_F32)
    h_new = x_ref[...] + _rmsnorm(m, npost_ref[...])
    y_ref[...] = _ffn_rows(h_new, wgu_ref, wd_ref, fpre_ref[...], fpost_ref[...], act_ref, d_ff)


def _sample_post(o, gates, x, gnw, rnw, wout, npost, wgu, wd, fpre, fpost, layer):
    nb, d_model = x.shape
    d_ff = wd.shape[1]
    width = A_WIDTH + B_WIDTH
    lspec = functools.partial(_layer_spec, layer=layer)
    full = lambda shape: pl.BlockSpec(tuple(shape), lambda i: (0,) * len(shape))
    return pl.pallas_call(
        functools.partial(_sample_post_body, d_ff=d_ff),
        grid=(1,),
        in_specs=[full((nb, width)), full((nb, width)), full((nb, d_model)), lspec((1, HEAD_DIM)),
                  lspec((1, HEAD_DIM)), lspec((width, d_model)), lspec((1, d_model)),
                  lspec((d_model, 2 * d_ff)), lspec((d_ff, d_model)), lspec((1, d_model)),
                  lspec((1, d_model))],
        out_specs=full((nb, d_model)),
        out_shape=jax.ShapeDtypeStruct((nb, d_model), _F32),
        scratch_shapes=[pltpu.VMEM((nb, width), _BF16), pltpu.VMEM((nb, d_ff), _BF16)],
        compiler_params=pltpu.CompilerParams(
            dimension_semantics=("arbitrary",), vmem_limit_bytes=VMEM_LIMIT_BYTES),
        name=f"sample_post_l{layer}",
    )(o, gates, x, gnw, rnw, wout, npost, wgu, wd, fpre, fpost)


def _rope_tables(positions):
    half = HEAD_DIM // 2
    inv = ROPE_BASE ** (-np.arange(half, dtype=np.float64) / half)
    ang = np.asarray(positions, dtype=np.float64)[:, None] * inv[None, :]
    cos, sin = np.cos(ang), np.sin(ang)
    return (jnp.asarray(np.concatenate([cos, cos], axis=-1), dtype=_F32),
            jnp.asarray(np.concatenate([-sin, sin], axis=-1), dtype=_F32))


def _rearranged_w_in_body(wt_ref, o_ref):
    small0 = QKV_WIDTH + A_WIDTH
    small1 = small0 + 2 * A_HEADS
    cb = WEIGHT_COL_BLOCK
    for lo in range(0, small0, cb):
        o_ref[:, lo:lo + cb] = wt_ref[lo:lo + cb, :].T.astype(_BF16)
    for lo in range(small0, QKV_WIDTH + SMALL_OFF, cb):
        src = lo + small1 - small0
        o_ref[:, lo:lo + cb] = wt_ref[src:src + cb, :].T.astype(_BF16)
    lane = lax.broadcasted_iota(jnp.int32, (o_ref.shape[0], LANES), 1)
    o_ref[:, QKV_WIDTH + SMALL_OFF:IN_WIDTH_PADDED] = jnp.where(
        lane < 2 * A_HEADS, wt_ref[small0:small0 + LANES, :].T, 0.0).astype(_BF16)


def _rearranged_w_in(w_in):
    depth, d_model, width = w_in.shape
    assert width == IN_WIDTH_PADDED - LANES + 2 * A_HEADS
    return pl.pallas_call(
        _rearranged_w_in_body,
        grid=(depth,),
        in_specs=[pl.BlockSpec((None, width, d_model), lambda l: (l, 0, 0), pipeline_mode=pl.Buffered(1))],
        out_specs=pl.BlockSpec((None, d_model, IN_WIDTH_PADDED), lambda l: (l, 0, 0)),
        out_shape=jax.ShapeDtypeStruct((depth, d_model, IN_WIDTH_PADDED), _BF16),
        compiler_params=pltpu.CompilerParams(
            dimension_semantics=("arbitrary",), vmem_limit_bytes=VMEM_LIMIT_BYTES),
        name="rearranged_w_in",
    )(jnp.swapaxes(w_in, 1, 2))


def kernel(x_prompt, x_sample, state_conv, state_gdn, state_ret, w_in, conv_w, a_log, dt_bias, gdn_norm_w, ret_norm_w, w_out, norm_mix_pre, norm_mix_post, norm_ffn_pre, norm_ffn_post, w_gate_up, w_down):
    depth = w_in.shape[0]
    batch, seq, d_model = x_prompt.shape
    nb, seq_s, _ = x_sample.shape
    assert seq_s == 1

    win = _rearranged_w_in(w_in)
    wout = w_out.astype(_BF16)
    wgu = w_gate_up.astype(_BF16)
    wd = w_down.astype(_BF16)
    alog = jnp.pad(a_log, ((0, 0), (A_HEADS, LANES - 2 * A_HEADS)))[:, None, :]
    dtb = jnp.pad(dt_bias, ((0, 0), (A_HEADS, LANES - 2 * A_HEADS)))[:, None, :]
    gnw = gdn_norm_w[:, None, :]
    rnw = ret_norm_w[:, None, :]
    npre = norm_mix_pre[:, None, :]
    npost = norm_mix_post[:, None, :]
    fpre = norm_ffn_pre[:, None, :]
    fpost = norm_ffn_post[:, None, :]
    cos_p, sin_p = _rope_tables(np.arange(seq))
    cos_s, sin_s = _rope_tables(PAST_LEN + np.arange(seq_s))
    conv_state = state_conv.reshape(depth, nb, (CONV_W - 1) * QKV_WIDTH)

    hp = x_prompt
    hs = x_sample.reshape(nb, d_model)
    convs_p, gdns_p, rets_p, convs_s = [], [], [], []
    gdn_s = ret_s = None
    for l in range(depth):
        sq, sk, sv, sc, gates, conv_s = _sample_pre(hs, cos_s, sin_s, win, conv_w, alog, dtb, npre,
                                                    conv_state, l)
        hp, conv_p, gdn_p, ret_p, so, gdn_s, ret_s = _layer_prompt(
            hp, cos_p, sin_p, win, conv_w, alog, dtb, gnw, rnw, wout, npre, npost, wgu, wd, fpre, fpost,
            sq, sk, sv, sc, state_gdn, state_ret, gdn_s, ret_s, l)
        hs = _sample_post(so, gates, hs, gnw, rnw, wout, npost, wgu, wd, fpre, fpost, l)
        convs_p.append(conv_p)
        gdns_p.append(gdn_p)
        rets_p.append(ret_p)
        convs_s.append(conv_s.reshape(nb, CONV_W - 1, QKV_WIDTH))
    return (hp, hs.reshape(nb, seq_s, d_model), jnp.stack(convs_p), jnp.stack(gdns_p),
            jnp.stack(rets_p), jnp.stack(convs_s), gdn_s, ret_s)
```

```python
import functools
import math

import jax
import jax.numpy as jnp
import numpy as np
from jax import lax
from jax.experimental import pallas as pl
from jax.experimental.pallas import tpu as pltpu

HEAD_DIM = 128
A_HEADS = 4
B_HEADS = 4
A_WIDTH = A_HEADS * HEAD_DIM
B_WIDTH = B_HEADS * HEAD_DIM
QKV_WIDTH = 3 * A_WIDTH
CONV_W = 4
CHUNK = 64
ROPE_BASE = 10000.0
EPS = 1e-6
L2_EPS = 1e-6
MASKED_LOG = -1e30
PAST_LEN = 16384
LANES = 128
SUBLANES = 8

REST_WIDTH = A_WIDTH + 4 * B_WIDTH + LANES
Z_OFF = 0
BQ_OFF = A_WIDTH
BK_OFF = BQ_OFF + B_WIDTH
BV_OFF = BK_OFF + B_WIDTH
BG_OFF = BV_OFF + B_WIDTH
SMALL_OFF = BG_OFF + B_WIDTH
IN_WIDTH_PADDED = QKV_WIDTH + REST_WIDTH

GROUP = 256
CHUNKS_PER_GROUP = GROUP // CHUNK
PROMPT_TIME_BLOCK = 256
FFN_COL_BLOCK = 256
WEIGHT_COL_BLOCK = 256
SAMPLE_SCALARS = 4
STACKED_STATE_OUTPUTS = (2, 3, 5, 6)
VMEM_LIMIT_BYTES = 56 * 1024 * 1024

_BF16 = jnp.bfloat16
_F32 = jnp.float32


def _dot(a, b):
    return jnp.dot(a.astype(_BF16), b.astype(_BF16), preferred_element_type=_F32)


def _dot_nt(a, b):
    return lax.dot_general(a.astype(_BF16), b.astype(_BF16), (((1,), (1,)), ((), ())),
                           preferred_element_type=_F32)


def _dot_tn(a, b):
    return lax.dot_general(a.astype(_BF16), b.astype(_BF16), (((0,), (0,)), ((), ())),
                           preferred_element_type=_F32)


def _rmsnorm(x, w):
    return x * lax.rsqrt(jnp.mean(x * x, axis=-1, keepdims=True) + EPS) * w


def _silu(x):
    return x * jax.nn.sigmoid(x)


def _softplus(x):
    return jnp.maximum(x, 0.0) + jnp.log1p(jnp.exp(-jnp.abs(x)))


def _log_gamma(h):
    return math.log1p(-(2.0 ** (-5.0 - h)))


def _rope(x, cos_full, sin_signed):
    return x * cos_full + pltpu.roll(x, HEAD_DIM // 2, axis=1) * sin_signed


def _head_cols(base, h):
    return slice(base + h * HEAD_DIM, base + (h + 1) * HEAD_DIM)


def _gated_head_norm(o, w, gate):
    return o * lax.rsqrt(jnp.mean(o * o, axis=-1, keepdims=True) + EPS) * w * _silu(gate)


class _Filler:
    def __init__(self, pieces):
        self._pieces = list(pieces)
        self._next = 0

    def emit(self, count):
        for _ in range(count):
            if self._next < len(self._pieces):
                self._pieces[self._next]()
                self._next += 1

    def flush(self):
        self.emit(len(self._pieces))


def _fold_rows(m):
    out = m[0:CHUNK]
    for c in range(1, CHUNKS_PER_GROUP):
        out = out + m[c * CHUNK:(c + 1) * CHUNK]
    return out


def _unfold_rows(r, same_chunk):
    return jnp.where(same_chunk, jnp.concatenate([r] * CHUNKS_PER_GROUP, axis=0), 0.0)


def _unit_lower_inverses(a_folded, eye_folded, same_chunk, fill):
    xs = list(a_folded)
    ps = [eye_folded - x for x in xs]
    n = 1
    while n < CHUNK:
        for h in range(len(xs)):
            x_bd = _unfold_rows(xs[h], same_chunk).astype(_BF16)
            if n == 1:
                xs[h] = _dot(xs[h], x_bd)
            elif 2 * n < CHUNK:
                r = _dot(jnp.concatenate([xs[h], ps[h]], axis=0), x_bd)
                xs[h] = r[0:CHUNK]
                ps[h] = ps[h] + r[CHUNK:2 * CHUNK]
            else:
                ps[h] = ps[h] + _dot(ps[h], x_bd)
            fill.emit(h % 2)
        n *= 2
    return [_unfold_rows(p, same_chunk) for p in ps]


def _recurrences_one_group(g, live, fill, act_ref, pr_ref, small_ref, dec_ref, sg_ref, sr_ref, o_ref):
    rows = slice(g * GROUP, (g + 1) * GROUP)
    ri = lax.broadcasted_iota(jnp.int32, (GROUP, GROUP), 0)
    ci = lax.broadcasted_iota(jnp.int32, (GROUP, GROUP), 1)
    same_chunk = (ri // CHUNK) == (ci // CHUNK)
    tril = same_chunk & (ri >= ci)
    off_diag = ri != ci
    rf = lax.broadcasted_iota(jnp.int32, (CHUNK, GROUP), 0)
    cf = lax.broadcasted_iota(jnp.int32, (CHUNK, GROUP), 1)
    eye_folded = jnp.where(rf == cf % CHUNK, 1.0, 0.0).astype(_F32)
    pos = (lax.broadcasted_iota(jnp.int32, (GROUP, HEAD_DIM), 0) % CHUNK).astype(_F32)
    chunk_rows = [slice(c * CHUNK, (c + 1) * CHUNK) for c in range(CHUNKS_PER_GROUP)]

    qs, ks, gccs, a_folded, a_intra, uw_rhs, e_ins = [], [], [], [], [], [], []
    for h in range(A_HEADS):
        q = act_ref[rows, _head_cols(0, h)]
        k = act_ref[rows, _head_cols(A_WIDTH, h)]
        v = act_ref[rows, _head_cols(2 * A_WIDTH, h)]
        beta = jnp.broadcast_to(small_ref[0, rows, h:h + 1], (GROUP, HEAD_DIM))
        gc_col = small_ref[1, rows, A_HEADS + h:A_HEADS + h + 1]
        gcc = jnp.broadcast_to(gc_col, (GROUP, HEAD_DIM))
        decay = dec_ref[h]
        kb = k * beta
        e_in = jnp.exp(gcc)
        kq = _dot_nt(jnp.concatenate([kb, q], axis=0), k)
        a_folded.append(_fold_rows(jnp.where(off_diag, kq[0:GROUP] * decay, 0.0)))
        a_intra.append(kq[GROUP:2 * GROUP] * decay)
        uw_rhs.append(jnp.concatenate([v * beta, kb * e_in], axis=1).astype(_BF16))
        qs.append(q)
        ks.append(k)
        gccs.append(gcc)
        e_ins.append(e_in)
        fill.emit(1)

    r_qe, r_oloc, r_b, r_echunk = [], [], [], []
    for h in range(B_HEADS):
        lg = _log_gamma(h)
        q = pr_ref[rows, _head_cols(BQ_OFF, h)]
        k = pr_ref[rows, _head_cols(BK_OFF, h)]
        v = pr_ref[rows, _head_cols(BV_OFF, h)]
        decay = jnp.exp(jnp.where(tril, (ri - ci).astype(_F32) * lg, MASKED_LOG))
        vb16 = v.astype(_BF16)
        r_oloc.append(_dot(_dot_nt(q, k) * decay, vb16))
        r_qe.append((q * jnp.exp((pos + 1.0) * lg)).astype(_BF16))
        k_out = (k * jnp.exp((CHUNK - 1.0 - pos) * lg)).astype(_BF16)
        r_b.append([_dot_tn(k_out[cr], vb16[cr]) for cr in chunk_rows])
        r_echunk.append(math.exp(CHUNK * lg))

    t_inv = _unit_lower_inverses(a_folded, eye_folded, same_chunk, fill)

    g_lhs, g_oloc, g_b, g_elast = [], [], [], []
    for h in range(A_HEADS):
        uw = _dot(t_inv[h], uw_rhs[h])
        uw16 = uw.astype(_BF16)
        aiuw = _dot(a_intra[h], uw16)
        g_oloc.append(aiuw[:, 0:HEAD_DIM])
        q_eff = qs[h] * e_ins[h] - aiuw[:, HEAD_DIM:2 * HEAD_DIM]
        lhs, bs, elast = [], [], []
        for cr in chunk_rows:
            gcc_c = gccs[h][cr]
            g_last = gcc_c[CHUNK - 1:CHUNK, :]
            k_out = ks[h][cr] * jnp.exp(g_last - gcc_c)
            bg = _dot_tn(k_out, uw16[cr])
            bs.append(bg[:, 0:HEAD_DIM])
            lhs.append(jnp.concatenate([q_eff[cr], bg[:, HEAD_DIM:2 * HEAD_DIM]], axis=0).astype(_BF16))
            elast.append(jnp.exp(g_last))
        g_lhs.append(lhs)
        g_b.append(bs)
        g_elast.append(elast)
        fill.emit(1)

    g_state = [sg_ref[h] for h in range(A_HEADS)]
    r_state = [sr_ref[h] for h in range(B_HEADS)]
    for c, cr in enumerate(chunk_rows):
        out_rows = slice(g * GROUP + c * CHUNK, g * GROUP + (c + 1) * CHUNK)
        for h in range(A_HEADS):
            r = _dot(g_lhs[h][c], g_state[h])
            o_ref[out_rows, _head_cols(0, h)] = r[0:CHUNK] + g_oloc[h][cr]
            g_state[h] = g_state[h] * g_elast[h][c] + g_b[h][c] - r[CHUNK:CHUNK + HEAD_DIM]
        for h in range(B_HEADS):
            o_ref[out_rows, _head_cols(A_WIDTH, h)] = _dot(r_qe[h][cr], r_state[h]) + r_oloc[h][cr]
            r_state[h] = r_state[h] * r_echunk[h] + r_b[h][c]
        fill.emit(1)
    for h in range(A_HEADS):
        sg_ref[h] = jnp.where(live, g_state[h], sg_ref[h])
    for h in range(B_HEADS):
        sr_ref[h] = jnp.where(live, r_state[h], sr_ref[h])


def _ffn_rows(h, wgu_ref, wd_ref, npre, npost, act_ref, d_ff):
    hn = _rmsnorm(h, npre).astype(_BF16)
    for j in range(d_ff // FFN_COL_BLOCK):
        cg = slice(j * FFN_COL_BLOCK, (j + 1) * FFN_COL_BLOCK)
        cu = slice(d_ff + j * FFN_COL_BLOCK, d_ff + (j + 1) * FFN_COL_BLOCK)
        gate = jnp.dot(hn, wgu_ref[:, cg], preferred_element_type=_F32)
        up = jnp.dot(hn, wgu_ref[:, cu], preferred_element_type=_F32)
        act_ref[:, cg] = (_silu(gate) * up).astype(_BF16)
    f = jnp.dot(act_ref[...], wd_ref[...], preferred_element_type=_F32)
    return h + _rmsnorm(f, npost)


def _ffn_pieces(h_ref, hn_ref, wgu_ref, wd_ref, npost, gate_ref, act_ref, f_ref, y_ref, d_ff):
    d_model = f_ref.shape[1]
    k_split = (d_ff // FFN_COL_BLOCK + 1) // 2 * FFN_COL_BLOCK
    pieces = []

    def gate(j):
        def run():
            cg = slice(j * FFN_COL_BLOCK, (j + 1) * FFN_COL_BLOCK)
            gate_ref[j % 2] = _silu(jnp.dot(hn_ref[...], wgu_ref[:, cg], preferred_element_type=_F32))
        return run

    def up(j):
        def run():
            cg = slice(j * FFN_COL_BLOCK, (j + 1) * FFN_COL_BLOCK)
            cu = slice(d_ff + j * FFN_COL_BLOCK, d_ff + (j + 1) * FFN_COL_BLOCK)
            act_ref[:, cg] = (gate_ref[j % 2] * jnp.dot(hn_ref[...], wgu_ref[:, cu],
                                                        preferred_element_type=_F32)).astype(_BF16)
        return run

    def down(j, first):
        def run():
            cs = slice(j * FFN_COL_BLOCK, (j + 1) * FFN_COL_BLOCK)
            if first:
                f_ref[:, cs] = jnp.dot(act_ref[:, 0:k_split], wd_ref[0:k_split, cs],
                                       preferred_element_type=_F32)
            else:
                f_ref[:, cs] = f_ref[:, cs] + jnp.dot(act_ref[:, k_split:d_ff], wd_ref[k_split:d_ff, cs],
                                                      preferred_element_type=_F32)
        return run

    def finish():
        y_ref[...] = h_ref[...] + _rmsnorm(f_ref[...], npost)

    for j in range(d_ff // FFN_COL_BLOCK):
        pieces += [gate(j), up(j)]
    for j in range(d_model // FFN_COL_BLOCK):
        pieces += [down(j, True), down(j, False)]
    pieces.append(finish)
    return pieces


def _column_of_row(row):
    return jnp.broadcast_to(row, (HEAD_DIM, HEAD_DIM)).T


def _sample_state_units(first_seq, per_step, sq_ref, sk_ref, sv_ref, sc_ref, sgi_ref, sri_ref,
                        so_ref, sgo_ref, sro_ref):
    base = pl.multiple_of((first_seq // SUBLANES) * SUBLANES, SUBLANES)
    rows = pl.ds(base, SUBLANES)
    row_id = lax.broadcasted_iota(jnp.int32, (SUBLANES, HEAD_DIM), 0)
    picks = [row_id == (first_seq - base + j) for j in range(per_step)]

    def pick(block, j):
        return jnp.sum(jnp.where(picks[j], block, 0.0), axis=0, keepdims=True)

    def put(cols, outs):
        block = so_ref[rows, cols]
        for j in range(per_step):
            block = jnp.where(picks[j], outs[j], block)
        so_ref[rows, cols] = block

    def deltanet(h):
        def run():
            cols = _head_cols(0, h)
            q8, k8, v8 = sq_ref[rows, cols], sk_ref[rows, cols], sv_ref[rows, cols]
            eg8 = sc_ref[rows, _head_cols(0, h)]
            bt8 = sc_ref[rows, _head_cols(A_WIDTH, h)]
            qk8 = sc_ref[rows, _head_cols(2 * A_WIDTH, h)]
            kq16 = jnp.concatenate([k8, q8], axis=0).astype(_BF16)
            kqs = [jnp.dot(kq16, sgi_ref[j, h].astype(_BF16), preferred_element_type=_F32)
                   for j in range(per_step)]
            outs = []
            for j in range(per_step):
                eg = pick(eg8, j)
                v_new = pick(bt8, j) * (pick(v8, j) - eg * pick(kqs[j][0:SUBLANES], j))
                outs.append(eg * pick(kqs[j][SUBLANES:2 * SUBLANES], j) + pick(qk8, j) * v_new)
                sgo_ref[j, h] = sgi_ref[j, h] * eg + _column_of_row(pick(k8, j)) * v_new
            put(cols, outs)
        return run

    def retention(h):
        def run():
            gamma = math.exp(_log_gamma(h))
            cols = _head_cols(A_WIDTH, h)
            q8, k8, v8 = sq_ref[rows, cols], sk_ref[rows, cols], sv_ref[rows, cols]
            qk8 = sc_ref[rows, _head_cols(3 * A_WIDTH, h)]
            q16 = q8.astype(_BF16)
            qss = [jnp.dot(q16, sri_ref[j, h].astype(_BF16), preferred_element_type=_F32)
                   for j in range(per_step)]
            outs = []
            for j in range(per_step):
                v1 = pick(v8, j)
                outs.append(gamma * pick(qss[j], j) + pick(qk8, j) * v1)
                sro_ref[j, h] = sri_ref[j, h] * gamma + _column_of_row(pick(k8, j)) * v1
            put(cols, outs)
        return run

    return [deltanet(h) for h in range(A_HEADS)] + [retention(h) for h in range(B_HEADS)]


def _layer_prompt_body(*refs, aliased, tb, nt, nblocks, d_ff, per_step):
    (x_ref, cos_ref, sin_ref, win_ref, convw_ref, alog_ref, dtb_ref, gnw_ref, rnw_ref, wout_ref,
     npre_ref, npost_ref, wgu_ref, wd_ref, fpre_ref, fpost_ref,
     sq_ref, sk_ref, sv_ref, sc_ref, sgi_ref, sri_ref) = refs[:22]
    (y_ref, conv_ref, sg_ref, sr_ref, so_ref, sgo_ref, sro_ref,
     pq_ref, pr_ref, act_ref, small_ref, gct_ref, dec_ref, o_ref, h_ref, hn_ref, gate_ref,
     ffn_act_ref, f_ref) = refs[22 + (len(STACKED_STATE_OUTPUTS) if aliased else 0):]
    s = pl.program_id(0)
    live = s < nblocks
    t = lax.rem(jnp.minimum(s, nblocks - 1), nt)
    pad = SUBLANES

    @pl.when(s == 0)
    def _():
        h_ref[1] = jnp.zeros(h_ref.shape[1:], _F32)
        hn_ref[1] = jnp.zeros(hn_ref.shape[1:], _BF16)
        so_ref[...] = jnp.zeros(so_ref.shape, _F32)

    @pl.when((t == 0) & live)
    def _():
        pq_ref[0:pad, :] = jnp.zeros((pad, QKV_WIDTH), _F32)
        sg_ref[...] = jnp.zeros(sg_ref.shape, _F32)
        sr_ref[...] = jnp.zeros(sr_ref.shape, _F32)

    prev = lax.rem(s + 1, 2)
    fill = _Filler(_ffn_pieces(h_ref.at[prev], hn_ref.at[prev], wgu_ref, wd_ref, fpost_ref[...],
                               gate_ref, ffn_act_ref, f_ref, y_ref, d_ff))
    fill.emit(2)
    sample = _Filler(_sample_state_units(jnp.minimum(s, nblocks - 1) * per_step, per_step, sq_ref, sk_ref,
                                         sv_ref, sc_ref, sgi_ref, sri_ref, so_ref, sgo_ref, sro_ref))

    x = x_ref[...]
    hn = _rmsnorm(x, npre_ref[...]).astype(_BF16)

    def project(lo, hi):
        return jnp.dot(hn, win_ref[:, lo:hi], preferred_element_type=_F32)

    def project_rest(lo, hi):
        pr_ref[:, lo:hi] = project(QKV_WIDTH + lo, QKV_WIDTH + hi)

    ps = project(QKV_WIDTH + SMALL_OFF, IN_WIDTH_PADDED)
    fill.emit(1)
    beta_all = jax.nn.sigmoid(ps)
    gc = -jnp.exp(alog_ref[...]) * _softplus(ps + dtb_ref[...])
    row_in_chunk = lax.broadcasted_iota(jnp.int32, (tb, LANES), 0) % CHUNK
    shift = 1
    while shift < CHUNK:
        gc = gc + jnp.where(row_in_chunk >= shift, pltpu.roll(gc, shift, axis=0), 0.0)
        shift *= 2
    small_ref[0] = beta_all
    small_ref[1] = gc
    gct_ref[...] = gc.T

    half = B_WIDTH // 2
    cos_full = cos_ref[...]
    sin_signed = sin_ref[...]
    ri = lax.broadcasted_iota(jnp.int32, (GROUP, GROUP), 0)
    ci = lax.broadcasted_iota(jnp.int32, (GROUP, GROUP), 1)
    tril = ((ri // CHUNK) == (ci // CHUNK)) & (ri >= ci)
    for pair in range(B_HEADS // 2):
        project_rest(BQ_OFF + pair * half, BQ_OFF + (pair + 1) * half)
        fill.emit(1)
        project_rest(BK_OFF + pair * half, BK_OFF + (pair + 1) * half)
        fill.emit(1)
        for h in (2 * pair, 2 * pair + 1):
            gc_col = small_ref[1, :, A_HEADS + h:A_HEADS + h + 1]
            gcr = jnp.broadcast_to(gct_ref[A_HEADS + h:A_HEADS + h + 1, :], (GROUP, GROUP))
            diff = jnp.broadcast_to(gc_col, (GROUP, GROUP)) - gcr
            dec_ref[h] = jnp.exp(jnp.where(tril, diff, MASKED_LOG))
            cq = _head_cols(BQ_OFF, h)
            ck = _head_cols(BK_OFF, h)
            pr_ref[:, cq] = _rope(pr_ref[:, cq], cos_full, sin_signed)
            pr_ref[:, ck] = _rope(pr_ref[:, ck], cos_full, sin_signed) * (HEAD_DIM ** -0.5)

    for lo in range(0, QKV_WIDTH, FFN_COL_BLOCK):
        pq_ref[pad:pad + tb, lo:lo + FFN_COL_BLOCK] = project(lo, lo + FFN_COL_BLOCK)
        fill.emit(1)
        sample.emit(1)

    rest = [(lo, lo + FFN_COL_BLOCK) for lo in range(Z_OFF, BQ_OFF, FFN_COL_BLOCK)]
    rest += [(lo, lo + FFN_COL_BLOCK) for lo in range(BV_OFF, SMALL_OFF, FFN_COL_BLOCK)]
    for j in range(QKV_WIDTH // LANES):
        if j % 2 == 0 and rest:
            project_rest(*rest.pop(0))
        else:
            fill.emit(1)
        cs = slice(j * LANES, (j + 1) * LANES)
        acc = pq_ref[pad:pad + tb, cs] * convw_ref[3:4, cs]
        for i in range(CONV_W - 1):
            acc = acc + pq_ref[pad - 3 + i:pad - 3 + i + tb, cs] * convw_ref[i:i + 1, cs]
        a = _silu(acc)
        if j < 2 * A_HEADS:
            a = a * lax.rsqrt(jnp.sum(a * a, axis=-1, keepdims=True) + L2_EPS)
        if j < A_HEADS:
            a = a * (HEAD_DIM ** -0.5)
        act_ref[:, cs] = a
    assert not rest

    tail = pq_ref[pad + tb - 3:pad + tb, :]
    conv_ref[...] = tail
    pq_ref[pad - 3:pad, :] = tail

    for g in range(tb // GROUP):
        _recurrences_one_group(g, live, fill, act_ref, pr_ref, small_ref, dec_ref, sg_ref, sr_ref, o_ref)

    for h in range(A_HEADS):
        cs = _head_cols(0, h)
        o_ref[:, cs] = _gated_head_norm(o_ref[:, cs], gnw_ref[...], pr_ref[:, _head_cols(Z_OFF, h)])
    for h in range(B_HEADS):
        cs = _head_cols(A_WIDTH, h)
        o_ref[:, cs] = _gated_head_norm(o_ref[:, cs], rnw_ref[...], pr_ref[:, _head_cols(BG_OFF, h)])
    m = jnp.dot(o_ref[...].astype(_BF16), wout_ref[...], preferred_element_type=_F32)
    fill.emit(8)
    sample.flush()
    h_new = x_ref[...] + _rmsnorm(m, npost_ref[...])
    cur = lax.rem(s, 2)
    h_ref[cur] = h_new
    hn_ref[cur] = _rmsnorm(h_new, fpre_ref[...]).astype(_BF16)
    fill.flush()


def _layer_spec(shape, layer):
    zeros = (0,) * len(shape)
    return pl.BlockSpec((None,) + tuple(shape), lambda i: (layer,) + zeros)


def _layer_prompt(x, cos_full, sin_signed, win, conv_w, alog, dtb, gnw, rnw, wout, npre, npost,
                  wgu, wd, fpre, fpost, sq, sk, sv, sc, state_gdn, state_ret, prev_states, layer):
    batch, seq, d_model = x.shape
    nb = sq.shape[0]
    depth = state_gdn.shape[0]
    d_ff = wd.shape[1]
    tb = min(PROMPT_TIME_BLOCK, seq)
    assert seq % tb == 0 and tb == GROUP and d_ff % FFN_COL_BLOCK == 0
    nt = seq // tb
    nblocks = batch * nt
    assert nb % nblocks == 0 and SUBLANES % (nb // nblocks) == 0
    per_step = nb // nblocks
    lspec = functools.partial(_layer_spec, layer=layer)

    def mixer_block(s):
        return jnp.minimum(s, nblocks - 1)

    def ffn_block(s):
        return jnp.maximum(s - 1, 0)

    full = lambda shape: pl.BlockSpec(tuple(shape), lambda s: (0,) * len(shape))
    state_spec = lambda heads: pl.BlockSpec((None, per_step, heads, HEAD_DIM, HEAD_DIM),
                                            lambda s: (layer, mixer_block(s), 0, 0, 0))

    in_specs = [
        pl.BlockSpec((None, tb, d_model), lambda s: (mixer_block(s) // nt, mixer_block(s) % nt, 0)),
        pl.BlockSpec((tb, HEAD_DIM), lambda s: (mixer_block(s) % nt, 0)),
        pl.BlockSpec((tb, HEAD_DIM), lambda s: (mixer_block(s) % nt, 0)),
        lspec((d_model, IN_WIDTH_PADDED)),
        lspec((CONV_W, QKV_WIDTH)),
        lspec((1, LANES)),
        lspec((1, LANES)),
        lspec((1, HEAD_DIM)),
        lspec((1, HEAD_DIM)),
        lspec((A_WIDTH + B_WIDTH, d_model)),
        lspec((1, d_model)),
        lspec((1, d_model)),
        lspec((d_model, 2 * d_ff)),
        lspec((d_ff, d_model)),
        lspec((1, d_model)),
        lspec((1, d_model)),
        full(sq.shape), full(sk.shape), full(sv.shape), full(sc.shape),
        state_spec(A_HEADS), state_spec(B_HEADS),
    ]
    args = [x, cos_full, sin_signed, win, conv_w, alog, dtb, gnw, rnw, wout, npre, npost, wgu, wd, fpre,
            fpost, sq, sk, sv, sc, state_gdn, state_ret]
    aliases = {}
    if prev_states is not None:
        in_specs += [pl.BlockSpec(memory_space=pl.ANY)] * len(STACKED_STATE_OUTPUTS)
        aliases = {len(args) + i: out for i, out in enumerate(STACKED_STATE_OUTPUTS)}
        args += list(prev_states)
    prompt_state_spec = lambda heads: pl.BlockSpec((None, None, heads, HEAD_DIM, HEAD_DIM),
                                                   lambda s: (layer, mixer_block(s) // nt, 0, 0, 0))
    out_specs = [
        pl.BlockSpec((None, tb, d_model), lambda s: (ffn_block(s) // nt, ffn_block(s) % nt, 0)),
        pl.BlockSpec((None, CONV_W - 1, QKV_WIDTH), lambda s: (mixer_block(s) // nt, 0, 0)),
        prompt_state_spec(A_HEADS), prompt_state_spec(B_HEADS),
        full(sq.shape), state_spec(A_HEADS), state_spec(B_HEADS),
    ]
    out_shape = [
        jax.ShapeDtypeStruct((batch, seq, d_model), _F32),
        jax.ShapeDtypeStruct((batch, CONV_W - 1, QKV_WIDTH), _F32),
        jax.ShapeDtypeStruct((depth, batch, A_HEADS, HEAD_DIM, HEAD_DIM), _F32),
        jax.ShapeDtypeStruct((depth, batch, B_HEADS, HEAD_DIM, HEAD_DIM), _F32),
        jax.ShapeDtypeStruct(sq.shape, _F32),
        jax.ShapeDtypeStruct((depth, nb, A_HEADS, HEAD_DIM, HEAD_DIM), _F32),
        jax.ShapeDtypeStruct((depth, nb, B_HEADS, HEAD_DIM, HEAD_DIM), _F32),
    ]
    scratch = [
        pltpu.VMEM((tb + SUBLANES, QKV_WIDTH), _F32),
        pltpu.VMEM((tb, REST_WIDTH), _F32),
        pltpu.VMEM((tb, QKV_WIDTH), _F32),
        pltpu.VMEM((2, tb, LANES), _F32),
        pltpu.VMEM((LANES, tb), _F32),
        pltpu.VMEM((A_HEADS, GROUP, GROUP), _F32),
        pltpu.VMEM((tb, A_WIDTH + B_WIDTH), _F32),
        pltpu.VMEM((2, tb, d_model), _F32),
        pltpu.VMEM((2, tb, d_model), _BF16),
        pltpu.VMEM((2, tb, FFN_COL_BLOCK), _F32),
        pltpu.VMEM((tb, d_ff), _BF16),
        pltpu.VMEM((tb, d_model), _F32),
    ]
    return pl.pallas_call(
        functools.partial(_layer_prompt_body, aliased=prev_states is not None, tb=tb, nt=nt, nblocks=nblocks,
                          d_ff=d_ff, per_step=per_step),
        grid=(nblocks + 1,), in_specs=in_specs, out_specs=out_specs, out_shape=out_shape,
        scratch_shapes=scratch, input_output_aliases=aliases,
        compiler_params=pltpu.CompilerParams(
            dimension_semantics=("arbitrary",), vmem_limit_bytes=VMEM_LIMIT_BYTES),
        name=f"layer_prompt_l{layer}",
    )(*args)


def _sample_pre_body(x_ref, cos_ref, sin_ref, win_ref, convw_ref, alog_ref, dtb_ref, npre_ref, convs_ref,
                     q_ref, k_ref, v_ref, sc_ref, gates_ref, convn_ref):
    nb = x_ref.shape[0]
    hn = _rmsnorm(x_ref[...], npre_ref[...]).astype(_BF16)
    pq = jnp.dot(hn, win_ref[:, 0:QKV_WIDTH], preferred_element_type=_F32)
    pr = jnp.dot(hn, win_ref[:, QKV_WIDTH:IN_WIDTH_PADDED], preferred_element_type=_F32)
    outs = (q_ref, k_ref, v_ref)
    for j in range(QKV_WIDTH // LANES):
        cs = slice(j * LANES, (j + 1) * LANES)
        new = pq[:, cs]
        acc = new * convw_ref[3:4, cs]
        for r in range(CONV_W - 1):
            hist = convs_ref[r, :, cs]
            acc = acc + hist * convw_ref[r:r + 1, cs]
            if r > 0:
                convn_ref[r - 1, :, cs] = hist
        convn_ref[CONV_W - 2, :, cs] = new
        a = _silu(acc)
        if j < 2 * A_HEADS:
            a = a * lax.rsqrt(jnp.sum(a * a, axis=-1, keepdims=True) + L2_EPS)
        if j < A_HEADS:
            a = a * (HEAD_DIM ** -0.5)
        outs[j // A_HEADS][:, _head_cols(0, j % A_HEADS)] = a
    ps = pr[:, SMALL_OFF:SMALL_OFF + LANES]
    beta_all = jax.nn.sigmoid(ps)
    eg_all = jnp.exp(-jnp.exp(alog_ref[...]) * _softplus(ps + dtb_ref[...]))
    cos_full = cos_ref[...]
    sin_signed = sin_ref[...]
    for h in range(A_HEADS):
        sc_ref[:, _head_cols(0, h)] = jnp.broadcast_to(eg_all[:, A_HEADS + h:A_HEADS + h + 1], (nb, LANES))
        sc_ref[:, _head_cols(A_WIDTH, h)] = jnp.broadcast_to(beta_all[:, h:h + 1], (nb, LANES))
        qk = jnp.sum(q_ref[:, _head_cols(0, h)] * k_ref[:, _head_cols(0, h)], axis=-1, keepdims=True)
        sc_ref[:, _head_cols(2 * A_WIDTH, h)] = jnp.broadcast_to(qk, (nb, LANES))
    for h in range(B_HEADS):
        q = _rope(pr[:, _head_cols(BQ_OFF, h)], cos_full, sin_signed)
        k = _rope(pr[:, _head_cols(BK_OFF, h)], cos_full, sin_signed) * (HEAD_DIM ** -0.5)
        q_ref[:, _head_cols(A_WIDTH, h)] = q
        k_ref[:, _head_cols(A_WIDTH, h)] = k
        sc_ref[:, _head_cols(3 * A_WIDTH, h)] = jnp.broadcast_to(
            jnp.sum(q * k, axis=-1, keepdims=True), (nb, LANES))
    v_ref[:, A_WIDTH:A_WIDTH + B_WIDTH] = pr[:, BV_OFF:BG_OFF]
    gates_ref[:, 0:A_WIDTH] = pr[:, Z_OFF:BQ_OFF]
    gates_ref[:, A_WIDTH:A_WIDTH + B_WIDTH] = pr[:, BG_OFF:SMALL_OFF]


def _sample_pre(x, cos_full, sin_signed, win, conv_w, alog, dtb, npre, conv_state, layer):
    nb, d_model = x.shape
    width = A_WIDTH + B_WIDTH
    lspec = functools.partial(_layer_spec, layer=layer)
    full = lambda shape: pl.BlockSpec(tuple(shape), lambda i: (0,) * len(shape))
    shapes = [(nb, width)] * 3 + [(nb, SAMPLE_SCALARS * A_WIDTH), (nb, width), (CONV_W - 1, nb, QKV_WIDTH)]
    return pl.pallas_call(
        _sample_pre_body,
        grid=(1,),
        in_specs=[full((nb, d_model)), full((1, HEAD_DIM)), full((1, HEAD_DIM)),
                  lspec((d_model, IN_WIDTH_PADDED)), lspec((CONV_W, QKV_WIDTH)), lspec((1, LANES)),
                  lspec((1, LANES)), lspec((1, d_model)), lspec((CONV_W - 1, nb, QKV_WIDTH))],
        out_specs=[full(shape) for shape in shapes],
        out_shape=[jax.ShapeDtypeStruct(shape, _F32) for shape in shapes],
        compiler_params=pltpu.CompilerParams(
            dimension_semantics=("arbitrary",), vmem_limit_bytes=VMEM_LIMIT_BYTES),
        name=f"sample_pre_l{layer}",
    )(x, cos_full, sin_signed, win, conv_w, alog, dtb, npre, conv_state)


def _sample_post_body(o_ref, gates_ref, x_ref, gnw_ref, rnw_ref, wout_ref, npost_ref, wgu_ref, wd_ref,
                      fpre_ref, fpost_ref, y_ref, cat_ref, act_ref, *, d_ff):
    for h in range(A_HEADS + B_HEADS):
        cs = _head_cols(0, h)
        w = gnw_ref[...] if h < A_HEADS else rnw_ref[...]
        cat_ref[:, cs] = _gated_head_norm(o_ref[:, cs], w, gates_ref[:, cs]).astype(_BF16)
    m = jnp.dot(cat_ref[...], wout_ref[...], preferred_element_type=_F32)
    h_new = x_ref[...] + _rmsnorm(m, npost_ref[...])
    y_ref[...] = _ffn_rows(h_new, wgu_ref, wd_ref, fpre_ref[...], fpost_ref[...], act_ref, d_ff)


def _sample_post(o, gates, x, gnw, rnw, wout, npost, wgu, wd, fpre, fpost, layer):
    nb, d_model = x.shape
    d_ff = wd.shape[1]
    width = A_WIDTH + B_WIDTH
    lspec = functools.partial(_layer_spec, layer=layer)
    full = lambda shape: pl.BlockSpec(tuple(shape), lambda i: (0,) * len(shape))
    return pl.pallas_call(
        functools.partial(_sample_post_body, d_ff=d_ff),
        grid=(1,),
        in_specs=[full((nb, width)), full((nb, width)), full((nb, d_model)), lspec((1, HEAD_DIM)),
                  lspec((1, HEAD_DIM)), lspec((width, d_model)), lspec((1, d_model)),
                  lspec((d_model, 2 * d_ff)), lspec((d_ff, d_model)), lspec((1, d_model)),
                  lspec((1, d_model))],
        out_specs=full((nb, d_model)),
        out_shape=jax.ShapeDtypeStruct((nb, d_model), _F32),
        scratch_shapes=[pltpu.VMEM((nb, width), _BF16), pltpu.VMEM((nb, d_ff), _BF16)],
        compiler_params=pltpu.CompilerParams(
            dimension_semantics=("arbitrary",), vmem_limit_bytes=VMEM_LIMIT_BYTES),
        name=f"sample_post_l{layer}",
    )(o, gates, x, gnw, rnw, wout, npost, wgu, wd, fpre, fpost)


def _rope_tables(positions):
    half = HEAD_DIM // 2
    inv = ROPE_BASE ** (-np.arange(half, dtype=np.float64) / half)
    ang = np.asarray(positions, dtype=np.float64)[:, None] * inv[None, :]
    cos, sin = np.cos(ang), np.sin(ang)
    return (jnp.asarray(np.concatenate([cos, cos], axis=-1), dtype=_F32),
            jnp.asarray(np.concatenate([-sin, sin], axis=-1), dtype=_F32))


def _rearranged_w_in_body(wt_ref, o_ref):
    small0 = QKV_WIDTH + A_WIDTH
    small1 = small0 + 2 * A_HEADS
    cb = WEIGHT_COL_BLOCK
    for lo in range(0, small0, cb):
        o_ref[:, lo:lo + cb] = wt_ref[lo:lo + cb, :].T.astype(_BF16)
    for lo in range(small0, QKV_WIDTH + SMALL_OFF, cb):
        src = lo + small1 - small0
        o_ref[:, lo:lo + cb] = wt_ref[src:src + cb, :].T.astype(_BF16)
    lane = lax.broadcasted_iota(jnp.int32, (o_ref.shape[0], LANES), 1)
    o_ref[:, QKV_WIDTH + SMALL_OFF:IN_WIDTH_PADDED] = jnp.where(
        lane < 2 * A_HEADS, wt_ref[small0:small0 + LANES, :].T, 0.0).astype(_BF16)


def _rearranged_w_in(w_in):
    depth, d_model, width = w_in.shape
    assert width == IN_WIDTH_PADDED - LANES + 2 * A_HEADS
    return pl.pallas_call(
        _rearranged_w_in_body,
        grid=(depth,),
        in_specs=[pl.BlockSpec((None, width, d_model), lambda l: (l, 0, 0), pipeline_mode=pl.Buffered(1))],
        out_specs=pl.BlockSpec((None, d_model, IN_WIDTH_PADDED), lambda l: (l, 0, 0)),
        out_shape=jax.ShapeDtypeStruct((depth, d_model, IN_WIDTH_PADDED), _BF16),
        compiler_params=pltpu.CompilerParams(
            dimension_semantics=("arbitrary",), vmem_limit_bytes=VMEM_LIMIT_BYTES),
        name="rearranged_w_in",
    )(jnp.swapaxes(w_in, 1, 2))


def kernel(x_prompt, x_sample, state_conv, state_gdn, state_ret, w_in, conv_w, a_log, dt_bias, gdn_norm_w, ret_norm_w, w_out, norm_mix_pre, norm_mix_post, norm_ffn_pre, norm_ffn_post, w_gate_up, w_down):
    depth = w_in.shape[0]
    batch, seq, d_model = x_prompt.shape
    nb, seq_s, _ = x_sample.shape
    assert seq_s == 1

    win = _rearranged_w_in(w_in)
    wout = w_out.astype(_BF16)
    wgu = w_gate_up.astype(_BF16)
    wd = w_down.astype(_BF16)
    alog = jnp.pad(a_log, ((0, 0), (A_HEADS, LANES - 2 * A_HEADS)))[:, None, :]
    dtb = jnp.pad(dt_bias, ((0, 0), (A_HEADS, LANES - 2 * A_HEADS)))[:, None, :]
    gnw = gdn_norm_w[:, None, :]
    rnw = ret_norm_w[:, None, :]
    npre = norm_mix_pre[:, None, :]
    npost = norm_mix_post[:, None, :]
    fpre = norm_ffn_pre[:, None, :]
    fpost = norm_ffn_post[:, None, :]
    cos_p, sin_p = _rope_tables(np.arange(seq))
    cos_s, sin_s = _rope_tables(PAST_LEN + np.arange(seq_s))
    conv_state = jnp.swapaxes(state_conv, 1, 2)

    hp = x_prompt
    hs = x_sample.reshape(nb, d_model)
    convs_p, convs_s = [], []
    states = None
    for l in range(depth):
        sq, sk, sv, sc, gates, conv_s = _sample_pre(hs, cos_s, sin_s, win, conv_w, alog, dtb, npre,
                                                    conv_state, l)
        hp, conv_p, gdn_p, ret_p, so, gdn_s, ret_s = _layer_prompt(
            hp, cos_p, sin_p, win, conv_w, alog, dtb, gnw, rnw, wout, npre, npost, wgu, wd, fpre, fpost,
            sq, sk, sv, sc, state_gdn, state_ret, states, l)
        states = (gdn_p, ret_p, gdn_s, ret_s)
        hs = _sample_post(so, gates, hs, gnw, rnw, wout, npost, wgu, wd, fpre, fpost, l)
        convs_p.append(conv_p)
        convs_s.append(conv_s)
    gdn_p, ret_p, gdn_s, ret_s = states
    return (hp, hs.reshape(nb, seq_s, d_model), jnp.stack(convs_p), gdn_p, ret_p,
            jnp.swapaxes(jnp.stack(convs_s), 1, 2), gdn_s, ret_s)
```

```python
import functools
import math

import jax
import jax.numpy as jnp
import numpy as np
from jax import lax
from jax.experimental import pallas as pl
from jax.experimental.pallas import tpu as pltpu

HEAD_DIM = 128
A_HEADS = 4
B_HEADS = 4
A_WIDTH = A_HEADS * HEAD_DIM
B_WIDTH = B_HEADS * HEAD_DIM
QKV_WIDTH = 3 * A_WIDTH
CONV_W = 4
CHUNK = 64
ROPE_BASE = 10000.0
EPS = 1e-6
L2_EPS = 1e-6
MASKED_LOG = -1e30
PAST_LEN = 16384
LANES = 128
SUBLANES = 8

REST_WIDTH = A_WIDTH + 4 * B_WIDTH + LANES
Z_OFF = 0
BQ_OFF = A_WIDTH
BK_OFF = BQ_OFF + B_WIDTH
BV_OFF = BK_OFF + B_WIDTH
BG_OFF = BV_OFF + B_WIDTH
SMALL_OFF = BG_OFF + B_WIDTH
IN_WIDTH_PADDED = QKV_WIDTH + REST_WIDTH

GROUP = 256
CHUNKS_PER_GROUP = GROUP // CHUNK
PROMPT_TIME_BLOCK = 256
FFN_COL_BLOCK = 256
SAMPLE_SCALARS = 4
STACKED_STATE_OUTPUTS = (2, 3, 5, 6)
VMEM_LIMIT_BYTES = 56 * 1024 * 1024

_BF16 = jnp.bfloat16
_F32 = jnp.float32


def _dot(a, b):
    return jnp.dot(a.astype(_BF16), b.astype(_BF16), preferred_element_type=_F32)


def _dot_nt(a, b):
    return lax.dot_general(a.astype(_BF16), b.astype(_BF16), (((1,), (1,)), ((), ())),
                           preferred_element_type=_F32)


def _dot_tn(a, b):
    return lax.dot_general(a.astype(_BF16), b.astype(_BF16), (((0,), (0,)), ((), ())),
                           preferred_element_type=_F32)


def _rmsnorm(x, w):
    return x * lax.rsqrt(jnp.mean(x * x, axis=-1, keepdims=True) + EPS) * w


def _silu(x):
    return x * jax.nn.sigmoid(x)


def _softplus(x):
    return jnp.maximum(x, 0.0) + jnp.log1p(jnp.exp(-jnp.abs(x)))


def _log_gamma(h):
    return math.log1p(-(2.0 ** (-5.0 - h)))


def _rope(x, cos_full, sin_signed):
    return x * cos_full + pltpu.roll(x, HEAD_DIM // 2, axis=1) * sin_signed


def _head_cols(base, h):
    return slice(base + h * HEAD_DIM, base + (h + 1) * HEAD_DIM)


def _gated_head_norm(o, w, gate):
    return o * lax.rsqrt(jnp.mean(o * o, axis=-1, keepdims=True) + EPS) * w * _silu(gate)


class _Filler:
    def __init__(self, pieces):
        self._pieces = list(pieces)
        self._next = 0

    def emit(self, count):
        for _ in range(count):
            if self._next < len(self._pieces):
                self._pieces[self._next]()
                self._next += 1

    def flush(self):
        self.emit(len(self._pieces))


def _fold_rows(m):
    out = m[0:CHUNK]
    for c in range(1, CHUNKS_PER_GROUP):
        out = out + m[c * CHUNK:(c + 1) * CHUNK]
    return out


def _unfold_rows(r, same_chunk):
    return jnp.where(same_chunk, jnp.concatenate([r] * CHUNKS_PER_GROUP, axis=0), 0.0)


def _unit_lower_inverses(a_folded, eye_folded, same_chunk, fill):
    xs = list(a_folded)
    ps = [eye_folded - x for x in xs]
    n = 1
    while n < CHUNK:
        for h in range(len(xs)):
            x_bd = _unfold_rows(xs[h], same_chunk).astype(_BF16)
            if n == 1:
                xs[h] = _dot(xs[h], x_bd)
            elif 2 * n < CHUNK:
                r = _dot(jnp.concatenate([xs[h], ps[h]], axis=0), x_bd)
                xs[h] = r[0:CHUNK]
                ps[h] = ps[h] + r[CHUNK:2 * CHUNK]
            else:
                ps[h] = ps[h] + _dot(ps[h], x_bd)
            fill.emit(h % 2)
        n *= 2
    return [_unfold_rows(p, same_chunk) for p in ps]


def _recurrences_one_group(g, live, fill, act_ref, pr_ref, small_ref, dec_ref, sg_ref, sr_ref, o_ref):
    rows = slice(g * GROUP, (g + 1) * GROUP)
    ri = lax.broadcasted_iota(jnp.int32, (GROUP, GROUP), 0)
    ci = lax.broadcasted_iota(jnp.int32, (GROUP, GROUP), 1)
    same_chunk = (ri // CHUNK) == (ci // CHUNK)
    tril = same_chunk & (ri >= ci)
    off_diag = ri != ci
    rf = lax.broadcasted_iota(jnp.int32, (CHUNK, GROUP), 0)
    cf = lax.broadcasted_iota(jnp.int32, (CHUNK, GROUP), 1)
    eye_folded = jnp.where(rf == cf % CHUNK, 1.0, 0.0).astype(_F32)
    pos = (lax.broadcasted_iota(jnp.int32, (GROUP, HEAD_DIM), 0) % CHUNK).astype(_F32)
    chunk_rows = [slice(c * CHUNK, (c + 1) * CHUNK) for c in range(CHUNKS_PER_GROUP)]

    qs, ks, gccs, a_folded, a_intra, uw_rhs, e_ins = [], [], [], [], [], [], []
    for h in range(A_HEADS):
        q = act_ref[rows, _head_cols(0, h)]
        k = act_ref[rows, _head_cols(A_WIDTH, h)]
        v = act_ref[rows, _head_cols(2 * A_WIDTH, h)]
        beta = jnp.broadcast_to(small_ref[0, rows, h:h + 1], (GROUP, HEAD_DIM))
        gc_col = small_ref[1, rows, A_HEADS + h:A_HEADS + h + 1]
        gcc = jnp.broadcast_to(gc_col, (GROUP, HEAD_DIM))
        decay = dec_ref[h]
        kb = k * beta
        e_in = jnp.exp(gcc)
        kq = _dot_nt(jnp.concatenate([kb, q], axis=0), k)
        a_folded.append(_fold_rows(jnp.where(off_diag, kq[0:GROUP] * decay, 0.0)))
        a_intra.append(kq[GROUP:2 * GROUP] * decay)
        uw_rhs.append(jnp.concatenate([v * beta, kb * e_in], axis=1).astype(_BF16))
        qs.append(q)
        ks.append(k)
        gccs.append(gcc)
        e_ins.append(e_in)
        fill.emit(1)

    r_qe, r_oloc, r_b, r_echunk = [], [], [], []
    for h in range(B_HEADS):
        lg = _log_gamma(h)
        q = pr_ref[rows, _head_cols(BQ_OFF, h)]
        k = pr_ref[rows, _head_cols(BK_OFF, h)]
        v = pr_ref[rows, _head_cols(BV_OFF, h)]
        decay = jnp.exp(jnp.where(tril, (ri - ci).astype(_F32) * lg, MASKED_LOG))
        vb16 = v.astype(_BF16)
        r_oloc.append(_dot(_dot_nt(q, k) * decay, vb16))
        r_qe.append((q * jnp.exp((pos + 1.0) * lg)).astype(_BF16))
        k_out = (k * jnp.exp((CHUNK - 1.0 - pos) * lg)).astype(_BF16)
        r_b.append([_dot_tn(k_out[cr], vb16[cr]) for cr in chunk_rows])
        r_echunk.append(math.exp(CHUNK * lg))

    t_inv = _unit_lower_inverses(a_folded, eye_folded, same_chunk, fill)

    g_lhs, g_oloc, g_b, g_elast = [], [], [], []
    for h in range(A_HEADS):
        uw = _dot(t_inv[h], uw_rhs[h])
        uw16 = uw.astype(_BF16)
        aiuw = _dot(a_intra[h], uw16)
        g_oloc.append(aiuw[:, 0:HEAD_DIM])
        q_eff = qs[h] * e_ins[h] - aiuw[:, HEAD_DIM:2 * HEAD_DIM]
        lhs, bs, elast = [], [], []
        for cr in chunk_rows:
            gcc_c = gccs[h][cr]
            g_last = gcc_c[CHUNK - 1:CHUNK, :]
            k_out = ks[h][cr] * jnp.exp(g_last - gcc_c)
            bg = _dot_tn(k_out, uw16[cr])
            bs.append(bg[:, 0:HEAD_DIM])
            lhs.append(jnp.concatenate([q_eff[cr], bg[:, HEAD_DIM:2 * HEAD_DIM]], axis=0).astype(_BF16))
            elast.append(jnp.exp(g_last))
        g_lhs.append(lhs)
        g_b.append(bs)
        g_elast.append(elast)
        fill.emit(1)

    g_state = [sg_ref[h] for h in range(A_HEADS)]
    r_state = [sr_ref[h] for h in range(B_HEADS)]
    for c, cr in enumerate(chunk_rows):
        out_rows = slice(g * GROUP + c * CHUNK, g * GROUP + (c + 1) * CHUNK)
        for h in range(A_HEADS):
            r = _dot(g_lhs[h][c], g_state[h])
            o_ref[out_rows, _head_cols(0, h)] = r[0:CHUNK] + g_oloc[h][cr]
            g_state[h] = g_state[h] * g_elast[h][c] + g_b[h][c] - r[CHUNK:CHUNK + HEAD_DIM]
        for h in range(B_HEADS):
            o_ref[out_rows, _head_cols(A_WIDTH, h)] = _dot(r_qe[h][cr], r_state[h]) + r_oloc[h][cr]
            r_state[h] = r_state[h] * r_echunk[h] + r_b[h][c]
        fill.emit(1)
    for h in range(A_HEADS):
        sg_ref[h] = jnp.where(live, g_state[h], sg_ref[h])
    for h in range(B_HEADS):
        sr_ref[h] = jnp.where(live, r_state[h], sr_ref[h])


def _ffn_rows(h, wgu_ref, wd_ref, npre, npost, act_ref, d_ff):
    hn = _rmsnorm(h, npre).astype(_BF16)
    for j in range(d_ff // FFN_COL_BLOCK):
        cg = slice(j * FFN_COL_BLOCK, (j + 1) * FFN_COL_BLOCK)
        cu = slice(d_ff + j * FFN_COL_BLOCK, d_ff + (j + 1) * FFN_COL_BLOCK)
        gate = jnp.dot(hn, wgu_ref[:, cg].astype(_BF16), preferred_element_type=_F32)
        up = jnp.dot(hn, wgu_ref[:, cu].astype(_BF16), preferred_element_type=_F32)
        act_ref[:, cg] = (_silu(gate) * up).astype(_BF16)
    f = jnp.dot(act_ref[...], wd_ref[...].astype(_BF16), preferred_element_type=_F32)
    return h + _rmsnorm(f, npost)


def _ffn_pieces(h_ref, hn_ref, wgu_ref, wd_ref, npost, gate_ref, act_ref, f_ref, y_ref, d_ff):
    d_model = f_ref.shape[1]
    k_split = (d_ff // FFN_COL_BLOCK + 1) // 2 * FFN_COL_BLOCK
    pieces = []

    def gate(j):
        def run():
            cg = slice(j * FFN_COL_BLOCK, (j + 1) * FFN_COL_BLOCK)
            gate_ref[j % 2] = _silu(jnp.dot(hn_ref[...], wgu_ref[:, cg], preferred_element_type=_F32))
        return run

    def up(j):
        def run():
            cg = slice(j * FFN_COL_BLOCK, (j + 1) * FFN_COL_BLOCK)
            cu = slice(d_ff + j * FFN_COL_BLOCK, d_ff + (j + 1) * FFN_COL_BLOCK)
            act_ref[:, cg] = (gate_ref[j % 2] * jnp.dot(hn_ref[...], wgu_ref[:, cu],
                                                        preferred_element_type=_F32)).astype(_BF16)
        return run

    def down(j, first):
        def run():
            cs = slice(j * FFN_COL_BLOCK, (j + 1) * FFN_COL_BLOCK)
            if first:
                f_ref[:, cs] = jnp.dot(act_ref[:, 0:k_split], wd_ref[0:k_split, cs],
                                       preferred_element_type=_F32)
            else:
                f_ref[:, cs] = f_ref[:, cs] + jnp.dot(act_ref[:, k_split:d_ff], wd_ref[k_split:d_ff, cs],
                                                      preferred_element_type=_F32)
        return run

    def finish():
        y_ref[...] = h_ref[...] + _rmsnorm(f_ref[...], npost)

    for j in range(d_ff // FFN_COL_BLOCK):
        pieces += [gate(j), up(j)]
    for j in range(d_model // FFN_COL_BLOCK):
        pieces += [down(j, True), down(j, False)]
    pieces.append(finish)
    return pieces


def _column_of_row(row):
    return jnp.broadcast_to(row, (HEAD_DIM, HEAD_DIM)).T


def _sample_state_units(first_seq, per_step, sq_ref, sk_ref, sv_ref, sc_ref, sgi_ref, sri_ref,
                        so_ref, sgo_ref, sro_ref):
    base = pl.multiple_of((first_seq // SUBLANES) * SUBLANES, SUBLANES)
    rows = pl.ds(base, SUBLANES)
    row_id = lax.broadcasted_iota(jnp.int32, (SUBLANES, HEAD_DIM), 0)
    picks = [row_id == (first_seq - base + j) for j in range(per_step)]

    def pick(block, j):
        return jnp.sum(jnp.where(picks[j], block, 0.0), axis=0, keepdims=True)

    def put(cols, outs):
        block = so_ref[rows, cols]
        for j in range(per_step):
            block = jnp.where(picks[j], outs[j], block)
        so_ref[rows, cols] = block

    def deltanet(h):
        def run():
            cols = _head_cols(0, h)
            q8, k8, v8 = sq_ref[rows, cols], sk_ref[rows, cols], sv_ref[rows, cols]
            eg8 = sc_ref[rows, _head_cols(0, h)]
            bt8 = sc_ref[rows, _head_cols(A_WIDTH, h)]
            qk8 = sc_ref[rows, _head_cols(2 * A_WIDTH, h)]
            kq16 = jnp.concatenate([k8, q8], axis=0).astype(_BF16)
            kqs = [jnp.dot(kq16, sgi_ref[j, h].astype(_BF16), preferred_element_type=_F32)
                   for j in range(per_step)]
            outs = []
            for j in range(per_step):
                eg = pick(eg8, j)
                v_new = pick(bt8, j) * (pick(v8, j) - eg * pick(kqs[j][0:SUBLANES], j))
                outs.append(eg * pick(kqs[j][SUBLANES:2 * SUBLANES], j) + pick(qk8, j) * v_new)
                sgo_ref[j, h] = sgi_ref[j, h] * eg + _column_of_row(pick(k8, j)) * v_new
            put(cols, outs)
        return run

    def retention(h):
        def run():
            gamma = math.exp(_log_gamma(h))
            cols = _head_cols(A_WIDTH, h)
            q8, k8, v8 = sq_ref[rows, cols], sk_ref[rows, cols], sv_ref[rows, cols]
            qk8 = sc_ref[rows, _head_cols(3 * A_WIDTH, h)]
            q16 = q8.astype(_BF16)
            qss = [jnp.dot(q16, sri_ref[j, h].astype(_BF16), preferred_element_type=_F32)
                   for j in range(per_step)]
            outs = []
            for j in range(per_step):
                v1 = pick(v8, j)
                outs.append(gamma * pick(qss[j], j) + pick(qk8, j) * v1)
                sro_ref[j, h] = sri_ref[j, h] * gamma + _column_of_row(pick(k8, j)) * v1
            put(cols, outs)
        return run

    return [deltanet(h) for h in range(A_HEADS)] + [retention(h) for h in range(B_HEADS)]


def _stage_weights(layer, wint_hbm, wout_hbm, wgu_hbm, wd_hbm, win_ref, wout_ref, wgu_ref, wd_ref,
                   stage_ref, sem):
    rows = stage_ref.shape[1]
    d_model = stage_ref.shape[2]
    small0 = QKV_WIDTH + A_WIDTH
    small1 = small0 + 2 * A_HEADS
    chunks = []

    def w_in_chunk(src_row, dst_col, n):
        def consume(v):
            win_ref[:, dst_col:dst_col + n] = v.T.astype(_BF16)
        return (wint_hbm.at[layer, pl.ds(src_row, n), :], n, d_model, consume)

    for lo in range(0, small0, rows):
        chunks.append(w_in_chunk(lo, lo, rows))
    for lo in range(small0, QKV_WIDTH + SMALL_OFF, rows):
        chunks.append(w_in_chunk(lo + small1 - small0, lo, rows))

    def small_tail(v):
        lane = lax.broadcasted_iota(jnp.int32, (d_model, LANES), 1)
        win_ref[:, QKV_WIDTH + SMALL_OFF:IN_WIDTH_PADDED] = jnp.where(
            lane < 2 * A_HEADS, v.T, 0.0).astype(_BF16)
    chunks.append((wint_hbm.at[layer, pl.ds(small0, LANES), :], LANES, d_model, small_tail))

    def plain(dst_ref, src_hbm, r0, c0, cw):
        def consume(v):
            dst_ref[r0:r0 + rows, c0:c0 + cw] = v.astype(_BF16)
        return (src_hbm.at[layer, pl.ds(r0, rows), pl.ds(c0, cw)], rows, cw, consume)

    for dst_ref, src_hbm in ((wout_ref, wout_hbm), (wgu_ref, wgu_hbm), (wd_ref, wd_hbm)):
        n_rows, n_cols = dst_ref.shape
        assert n_rows % rows == 0
        for r0 in range(0, n_rows, rows):
            for c0 in range(0, n_cols, d_model):
                chunks.append(plain(dst_ref, src_hbm, r0, c0, min(d_model, n_cols - c0)))

    def copy(i):
        src, n, cw, _ = chunks[i]
        slot = i % 2
        return pltpu.make_async_copy(src, stage_ref.at[slot, pl.ds(0, n), pl.ds(0, cw)], sem.at[slot])

    copy(0).start()
    for i in range(len(chunks)):
        if i + 1 < len(chunks):
            copy(i + 1).start()
        copy(i).wait()
        _, n, cw, consume = chunks[i]
        consume(stage_ref[i % 2, 0:n, 0:cw])


def _layer_prompt_body(*refs, layer, aliased, tb, nt, nblocks, d_ff, per_step):
    (x_ref, cos_ref, sin_ref, wint_hbm, convw_ref, alog_ref, dtb_ref, gnw_ref, rnw_ref, wout_hbm,
     npre_ref, npost_ref, wgu_hbm, wd_hbm, fpre_ref, fpost_ref,
     sq_ref, sk_ref, sv_ref, sc_ref, sgi_ref, sri_ref) = refs[:22]
    (y_ref, conv_ref, sg_ref, sr_ref, so_ref, sgo_ref, sro_ref,
     pq_ref, pr_ref, act_ref, small_ref, gct_ref, dec_ref, o_ref, h_ref, hn_ref, gate_ref,
     ffn_act_ref, f_ref, win_ref, wout_ref, wgu_ref, wd_ref, stage_sem) = refs[
         22 + (len(STACKED_STATE_OUTPUTS) if aliased else 0):]
    s = pl.program_id(0)
    live = s < nblocks
    t = lax.rem(jnp.minimum(s, nblocks - 1), nt)
    pad = SUBLANES

    @pl.when(s == 0)
    def _():
        _stage_weights(layer, wint_hbm, wout_hbm, wgu_hbm, wd_hbm, win_ref, wout_ref, wgu_ref, wd_ref,
                       h_ref, stage_sem)
        h_ref[1] = jnp.zeros(h_ref.shape[1:], _F32)
        hn_ref[1] = jnp.zeros(hn_ref.shape[1:], _BF16)
        so_ref[...] = jnp.zeros(so_ref.shape, _F32)

    @pl.when((t == 0) & live)
    def _():
        pq_ref[0:pad, :] = jnp.zeros((pad, QKV_WIDTH), _F32)
        sg_ref[...] = jnp.zeros(sg_ref.shape, _F32)
        sr_ref[...] = jnp.zeros(sr_ref.shape, _F32)

    prev = lax.rem(s + 1, 2)
    fill = _Filler(_ffn_pieces(h_ref.at[prev], hn_ref.at[prev], wgu_ref, wd_ref, fpost_ref[...],
                               gate_ref, ffn_act_ref, f_ref, y_ref, d_ff))
    fill.emit(2)
    sample = _Filler(_sample_state_units(jnp.minimum(s, nblocks - 1) * per_step, per_step, sq_ref, sk_ref,
                                         sv_ref, sc_ref, sgi_ref, sri_ref, so_ref, sgo_ref, sro_ref))

    x = x_ref[...]
    hn = _rmsnorm(x, npre_ref[...]).astype(_BF16)

    def project(lo, hi):
        return jnp.dot(hn, win_ref[:, lo:hi], preferred_element_type=_F32)

    def project_rest(lo, hi):
        pr_ref[:, lo:hi] = project(QKV_WIDTH + lo, QKV_WIDTH + hi)

    ps = project(QKV_WIDTH + SMALL_OFF, IN_WIDTH_PADDED)
    fill.emit(1)
    beta_all = jax.nn.sigmoid(ps)
    gc = -jnp.exp(alog_ref[...]) * _softplus(ps + dtb_ref[...])
    row_in_chunk = lax.broadcasted_iota(jnp.int32, (tb, LANES), 0) % CHUNK
    shift = 1
    while shift < CHUNK:
        gc = gc + jnp.where(row_in_chunk >= shift, pltpu.roll(gc, shift, axis=0), 0.0)
        shift *= 2
    small_ref[0] = beta_all
    small_ref[1] = gc
    gct_ref[...] = gc.T

    half = B_WIDTH // 2
    cos_full = cos_ref[...]
    sin_signed = sin_ref[...]
    ri = lax.broadcasted_iota(jnp.int32, (GROUP, GROUP), 0)
    ci = lax.broadcasted_iota(jnp.int32, (GROUP, GROUP), 1)
    tril = ((ri // CHUNK) == (ci // CHUNK)) & (ri >= ci)
    for pair in range(B_HEADS // 2):
        project_rest(BQ_OFF + pair * half, BQ_OFF + (pair + 1) * half)
        fill.emit(1)
        project_rest(BK_OFF + pair * half, BK_OFF + (pair + 1) * half)
        fill.emit(1)
        for h in (2 * pair, 2 * pair + 1):
            gc_col = small_ref[1, :, A_HEADS + h:A_HEADS + h + 1]
            gcr = jnp.broadcast_to(gct_ref[A_HEADS + h:A_HEADS + h + 1, :], (GROUP, GROUP))
            diff = jnp.broadcast_to(gc_col, (GROUP, GROUP)) - gcr
            dec_ref[h] = jnp.exp(jnp.where(tril, diff, MASKED_LOG))
            cq = _head_cols(BQ_OFF, h)
            ck = _head_cols(BK_OFF, h)
            pr_ref[:, cq] = _rope(pr_ref[:, cq], cos_full, sin_signed)
            pr_ref[:, ck] = _rope(pr_ref[:, ck], cos_full, sin_signed) * (HEAD_DIM ** -0.5)

    for lo in range(0, QKV_WIDTH, FFN_COL_BLOCK):
        pq_ref[pad:pad + tb, lo:lo + FFN_COL_BLOCK] = project(lo, lo + FFN_COL_BLOCK)
        fill.emit(1)
        sample.emit(1)

    rest = [(lo, lo + FFN_COL_BLOCK) for lo in range(Z_OFF, BQ_OFF, FFN_COL_BLOCK)]
    rest += [(lo, lo + FFN_COL_BLOCK) for lo in range(BV_OFF, SMALL_OFF, FFN_COL_BLOCK)]
    for j in range(QKV_WIDTH // LANES):
        if j % 2 == 0 and rest:
            project_rest(*rest.pop(0))
        else:
            fill.emit(1)
        cs = slice(j * LANES, (j + 1) * LANES)
        acc = pq_ref[pad:pad + tb, cs] * convw_ref[3:4, cs]
        for i in range(CONV_W - 1):
            acc = acc + pq_ref[pad - 3 + i:pad - 3 + i + tb, cs] * convw_ref[i:i + 1, cs]
        a = _silu(acc)
        if j < 2 * A_HEADS:
            a = a * lax.rsqrt(jnp.sum(a * a, axis=-1, keepdims=True) + L2_EPS)
        if j < A_HEADS:
            a = a * (HEAD_DIM ** -0.5)
        act_ref[:, cs] = a
    assert not rest

    tail = pq_ref[pad + tb - 3:pad + tb, :]
    conv_ref[...] = tail
    pq_ref[pad - 3:pad, :] = tail

    for g in range(tb // GROUP):
        _recurrences_one_group(g, live, fill, act_ref, pr_ref, small_ref, dec_ref, sg_ref, sr_ref, o_ref)

    for h in range(A_HEADS):
        cs = _head_cols(0, h)
        o_ref[:, cs] = _gated_head_norm(o_ref[:, cs], gnw_ref[...], pr_ref[:, _head_cols(Z_OFF, h)])
    for h in range(B_HEADS):
        cs = _head_cols(A_WIDTH, h)
        o_ref[:, cs] = _gated_head_norm(o_ref[:, cs], rnw_ref[...], pr_ref[:, _head_cols(BG_OFF, h)])
    m = jnp.dot(o_ref[...].astype(_BF16), wout_ref[...], preferred_element_type=_F32)
    fill.emit(8)
    sample.flush()
    h_new = x_ref[...] + _rmsnorm(m, npost_ref[...])
    cur = lax.rem(s, 2)
    h_ref[cur] = h_new
    hn_ref[cur] = _rmsnorm(h_new, fpre_ref[...]).astype(_BF16)
    fill.flush()


def _layer_spec(shape, layer):
    zeros = (0,) * len(shape)
    return pl.BlockSpec((None,) + tuple(shape), lambda i: (layer,) + zeros)


def _layer_prompt(x, cos_full, sin_signed, win, conv_w, alog, dtb, gnw, rnw, wout, npre, npost,
                  wgu, wd, fpre, fpost, sq, sk, sv, sc, state_gdn, state_ret, prev_states, layer):
    batch, seq, d_model = x.shape
    nb = sq.shape[0]
    depth = state_gdn.shape[0]
    d_ff = wd.shape[1]
    tb = min(PROMPT_TIME_BLOCK, seq)
    assert seq % tb == 0 and tb == GROUP and d_ff % FFN_COL_BLOCK == 0
    assert win.shape[1] == IN_WIDTH_PADDED - LANES + 2 * A_HEADS and d_ff % tb == 0 and d_model % tb == 0
    nt = seq // tb
    nblocks = batch * nt
    assert nb % nblocks == 0 and SUBLANES % (nb // nblocks) == 0
    per_step = nb // nblocks
    lspec = functools.partial(_layer_spec, layer=layer)

    def mixer_block(s):
        return jnp.minimum(s, nblocks - 1)

    def ffn_block(s):
        return jnp.maximum(s - 1, 0)

    full = lambda shape: pl.BlockSpec(tuple(shape), lambda s: (0,) * len(shape))
    state_spec = lambda heads: pl.BlockSpec((None, per_step, heads, HEAD_DIM, HEAD_DIM),
                                            lambda s: (layer, mixer_block(s), 0, 0, 0))

    in_specs = [
        pl.BlockSpec((None, tb, d_model), lambda s: (mixer_block(s) // nt, mixer_block(s) % nt, 0)),
        pl.BlockSpec((tb, HEAD_DIM), lambda s: (mixer_block(s) % nt, 0)),
        pl.BlockSpec((tb, HEAD_DIM), lambda s: (mixer_block(s) % nt, 0)),
        pl.BlockSpec(memory_space=pl.ANY),
        lspec((CONV_W, QKV_WIDTH)),
        lspec((1, LANES)),
        lspec((1, LANES)),
        lspec((1, HEAD_DIM)),
        lspec((1, HEAD_DIM)),
        pl.BlockSpec(memory_space=pl.ANY),
        lspec((1, d_model)),
        lspec((1, d_model)),
        pl.BlockSpec(memory_space=pl.ANY),
        pl.BlockSpec(memory_space=pl.ANY),
        lspec((1, d_model)),
        lspec((1, d_model)),
        full(sq.shape), full(sk.shape), full(sv.shape), full(sc.shape),
        state_spec(A_HEADS), state_spec(B_HEADS),
    ]
    args = [x, cos_full, sin_signed, win, conv_w, alog, dtb, gnw, rnw, wout, npre, npost, wgu, wd, fpre,
            fpost, sq, sk, sv, sc, state_gdn, state_ret]
    aliases = {}
    if prev_states is not None:
        in_specs += [pl.BlockSpec(memory_space=pl.ANY)] * len(STACKED_STATE_OUTPUTS)
        aliases = {len(args) + i: out for i, out in enumerate(STACKED_STATE_OUTPUTS)}
        args += list(prev_states)
    prompt_state_spec = lambda heads: pl.BlockSpec((None, None, heads, HEAD_DIM, HEAD_DIM),
                                                   lambda s: (layer, mixer_block(s) // nt, 0, 0, 0))
    out_specs = [
        pl.BlockSpec((None, tb, d_model), lambda s: (ffn_block(s) // nt, ffn_block(s) % nt, 0)),
        pl.BlockSpec((None, CONV_W - 1, QKV_WIDTH), lambda s: (mixer_block(s) // nt, 0, 0)),
        prompt_state_spec(A_HEADS), prompt_state_spec(B_HEADS),
        full(sq.shape), state_spec(A_HEADS), state_spec(B_HEADS),
    ]
    out_shape = [
        jax.ShapeDtypeStruct((batch, seq, d_model), _F32),
        jax.ShapeDtypeStruct((batch, CONV_W - 1, QKV_WIDTH), _F32),
        jax.ShapeDtypeStruct((depth, batch, A_HEADS, HEAD_DIM, HEAD_DIM), _F32),
        jax.ShapeDtypeStruct((depth, batch, B_HEADS, HEAD_DIM, HEAD_DIM), _F32),
        jax.ShapeDtypeStruct(sq.shape, _F32),
        jax.ShapeDtypeStruct((depth, nb, A_HEADS, HEAD_DIM, HEAD_DIM), _F32),
        jax.ShapeDtypeStruct((depth, nb, B_HEADS, HEAD_DIM, HEAD_DIM), _F32),
    ]
    scratch = [
        pltpu.VMEM((tb + SUBLANES, QKV_WIDTH), _F32),
        pltpu.VMEM((tb, REST_WIDTH), _F32),
        pltpu.VMEM((tb, QKV_WIDTH), _F32),
        pltpu.VMEM((2, tb, LANES), _F32),
        pltpu.VMEM((LANES, tb), _F32),
        pltpu.VMEM((A_HEADS, GROUP, GROUP), _F32),
        pltpu.VMEM((tb, A_WIDTH + B_WIDTH), _F32),
        pltpu.VMEM((2, tb, d_model), _F32),
        pltpu.VMEM((2, tb, d_model), _BF16),
        pltpu.VMEM((2, tb, FFN_COL_BLOCK), _F32),
        pltpu.VMEM((tb, d_ff), _BF16),
        pltpu.VMEM((tb, d_model), _F32),
        pltpu.VMEM((d_model, IN_WIDTH_PADDED), _BF16),
        pltpu.VMEM((A_WIDTH + B_WIDTH, d_model), _BF16),
        pltpu.VMEM((d_model, 2 * d_ff), _BF16),
        pltpu.VMEM((d_ff, d_model), _BF16),
        pltpu.SemaphoreType.DMA((2,)),
    ]
    return pl.pallas_call(
        functools.partial(_layer_prompt_body, layer=layer, aliased=prev_states is not None, tb=tb, nt=nt, nblocks=nblocks,
                          d_ff=d_ff, per_step=per_step),
        grid=(nblocks + 1,), in_specs=in_specs, out_specs=out_specs, out_shape=out_shape,
        scratch_shapes=scratch, input_output_aliases=aliases,
        compiler_params=pltpu.CompilerParams(
            dimension_semantics=("arbitrary",), vmem_limit_bytes=VMEM_LIMIT_BYTES),
        name=f"layer_prompt_l{layer}",
    )(*args)


def _sample_pre_body(x_ref, cos_ref, sin_ref, win_ref, convw_ref, alog_ref, dtb_ref, npre_ref, convs_ref,
                     q_ref, k_ref, v_ref, sc_ref, gates_ref, convn_ref):
    nb = x_ref.shape[0]
    hn = _rmsnorm(x_ref[...], npre_ref[...]).astype(_BF16)
    small0 = QKV_WIDTH + A_WIDTH
    small1 = small0 + 2 * A_HEADS

    def project(lo, hi):
        return _dot_nt(hn, win_ref[lo:hi, :])

    pq = project(0, QKV_WIDTH)
    pr = jnp.concatenate([project(QKV_WIDTH, small0), project(small1, win_ref.shape[0]),
                          project(small0, small0 + LANES)], axis=1)
    outs = (q_ref, k_ref, v_ref)
    for j in range(QKV_WIDTH // LANES):
        cs = slice(j * LANES, (j + 1) * LANES)
        new = pq[:, cs]
        acc = new * convw_ref[3:4, cs]
        for r in range(CONV_W - 1):
            hist = convs_ref[r, :, cs]
            acc = acc + hist * convw_ref[r:r + 1, cs]
            if r > 0:
                convn_ref[r - 1, :, cs] = hist
        convn_ref[CONV_W - 2, :, cs] = new
        a = _silu(acc)
        if j < 2 * A_HEADS:
            a = a * lax.rsqrt(jnp.sum(a * a, axis=-1, keepdims=True) + L2_EPS)
        if j < A_HEADS:
            a = a * (HEAD_DIM ** -0.5)
        outs[j // A_HEADS][:, _head_cols(0, j % A_HEADS)] = a
    ps = pr[:, SMALL_OFF:SMALL_OFF + LANES]
    beta_all = jax.nn.sigmoid(ps)
    eg_all = jnp.exp(-jnp.exp(alog_ref[...]) * _softplus(ps + dtb_ref[...]))
    cos_full = cos_ref[...]
    sin_signed = sin_ref[...]
    for h in range(A_HEADS):
        sc_ref[:, _head_cols(0, h)] = jnp.broadcast_to(eg_all[:, A_HEADS + h:A_HEADS + h + 1], (nb, LANES))
        sc_ref[:, _head_cols(A_WIDTH, h)] = jnp.broadcast_to(beta_all[:, h:h + 1], (nb, LANES))
        qk = jnp.sum(q_ref[:, _head_cols(0, h)] * k_ref[:, _head_cols(0, h)], axis=-1, keepdims=True)
        sc_ref[:, _head_cols(2 * A_WIDTH, h)] = jnp.broadcast_to(qk, (nb, LANES))
    for h in range(B_HEADS):
        q = _rope(pr[:, _head_cols(BQ_OFF, h)], cos_full, sin_signed)
        k = _rope(pr[:, _head_cols(BK_OFF, h)], cos_full, sin_signed) * (HEAD_DIM ** -0.5)
        q_ref[:, _head_cols(A_WIDTH, h)] = q
        k_ref[:, _head_cols(A_WIDTH, h)] = k
        sc_ref[:, _head_cols(3 * A_WIDTH, h)] = jnp.broadcast_to(
            jnp.sum(q * k, axis=-1, keepdims=True), (nb, LANES))
    v_ref[:, A_WIDTH:A_WIDTH + B_WIDTH] = pr[:, BV_OFF:BG_OFF]
    gates_ref[:, 0:A_WIDTH] = pr[:, Z_OFF:BQ_OFF]
    gates_ref[:, A_WIDTH:A_WIDTH + B_WIDTH] = pr[:, BG_OFF:SMALL_OFF]


def _sample_pre(x, cos_full, sin_signed, win, conv_w, alog, dtb, npre, conv_state, layer):
    nb, d_model = x.shape
    width = A_WIDTH + B_WIDTH
    lspec = functools.partial(_layer_spec, layer=layer)
    full = lambda shape: pl.BlockSpec(tuple(shape), lambda i: (0,) * len(shape))
    shapes = [(nb, width)] * 3 + [(nb, SAMPLE_SCALARS * A_WIDTH), (nb, width), (CONV_W - 1, nb, QKV_WIDTH)]
    return pl.pallas_call(
        _sample_pre_body,
        grid=(1,),
        in_specs=[full((nb, d_model)), full((1, HEAD_DIM)), full((1, HEAD_DIM)),
                  lspec(win.shape[1:]), lspec((CONV_W, QKV_WIDTH)), lspec((1, LANES)),
                  lspec((1, LANES)), lspec((1, d_model)), lspec((CONV_W - 1, nb, QKV_WIDTH))],
        out_specs=[full(shape) for shape in shapes],
        out_shape=[jax.ShapeDtypeStruct(shape, _F32) for shape in shapes],
        compiler_params=pltpu.CompilerParams(
            dimension_semantics=("arbitrary",), vmem_limit_bytes=VMEM_LIMIT_BYTES),
        name=f"sample_pre_l{layer}",
    )(x, cos_full, sin_signed, win, conv_w, alog, dtb, npre, conv_state)


def _sample_post_body(o_ref, gates_ref, x_ref, gnw_ref, rnw_ref, wout_ref, npost_ref, wgu_ref, wd_ref,
                      fpre_ref, fpost_ref, y_ref, cat_ref, act_ref, *, d_ff):
    for h in range(A_HEADS + B_HEADS):
        cs = _head_cols(0, h)
        w = gnw_ref[...] if h < A_HEADS else rnw_ref[...]
        cat_ref[:, cs] = _gated_head_norm(o_ref[:, cs], w, gates_ref[:, cs]).astype(_BF16)
    m = jnp.dot(cat_ref[...], wout_ref[...].astype(_BF16), preferred_element_type=_F32)
    h_new = x_ref[...] + _rmsnorm(m, npost_ref[...])
    y_ref[...] = _ffn_rows(h_new, wgu_ref, wd_ref, fpre_ref[...], fpost_ref[...], act_ref, d_ff)


def _sample_post(o, gates, x, gnw, rnw, wout, npost, wgu, wd, fpre, fpost, layer):
    nb, d_model = x.shape
    d_ff = wd.shape[1]
    width = A_WIDTH + B_WIDTH
    lspec = functools.partial(_layer_spec, layer=layer)
    full = lambda shape: pl.BlockSpec(tuple(shape), lambda i: (0,) * len(shape))
    return pl.pallas_call(
        functools.partial(_sample_post_body, d_ff=d_ff),
        grid=(1,),
        in_specs=[full((nb, width)), full((nb, width)), full((nb, d_model)), lspec((1, HEAD_DIM)),
                  lspec((1, HEAD_DIM)), lspec((width, d_model)), lspec((1, d_model)),
                  lspec((d_model, 2 * d_ff)), lspec((d_ff, d_model)), lspec((1, d_model)),
                  lspec((1, d_model))],
        out_specs=full((nb, d_model)),
        out_shape=jax.ShapeDtypeStruct((nb, d_model), _F32),
        scratch_shapes=[pltpu.VMEM((nb, width), _BF16), pltpu.VMEM((nb, d_ff), _BF16)],
        compiler_params=pltpu.CompilerParams(
            dimension_semantics=("arbitrary",), vmem_limit_bytes=VMEM_LIMIT_BYTES),
        name=f"sample_post_l{layer}",
    )(o, gates, x, gnw, rnw, wout, npost, wgu, wd, fpre, fpost)


def _rope_tables(positions):
    half = HEAD_DIM // 2
    inv = ROPE_BASE ** (-np.arange(half, dtype=np.float64) / half)
    ang = np.asarray(positions, dtype=np.float64)[:, None] * inv[None, :]
    cos, sin = np.cos(ang), np.sin(ang)
    return (jnp.asarray(np.concatenate([cos, cos], axis=-1), dtype=_F32),
            jnp.asarray(np.concatenate([-sin, sin], axis=-1), dtype=_F32))


def kernel(x_prompt, x_sample, state_conv, state_gdn, state_ret, w_in, conv_w, a_log, dt_bias, gdn_norm_w, ret_norm_w, w_out, norm_mix_pre, norm_mix_post, norm_ffn_pre, norm_ffn_post, w_gate_up, w_down):
    depth = w_in.shape[0]
    batch, seq, d_model = x_prompt.shape
    nb, seq_s, _ = x_sample.shape
    assert seq_s == 1

    win = jnp.swapaxes(w_in, 1, 2)
    wout, wgu, wd = w_out, w_gate_up, w_down
    alog = jnp.pad(a_log, ((0, 0), (A_HEADS, LANES - 2 * A_HEADS)))[:, None, :]
    dtb = jnp.pad(dt_bias, ((0, 0), (A_HEADS, LANES - 2 * A_HEADS)))[:, None, :]
    gnw = gdn_norm_w[:, None, :]
    rnw = ret_norm_w[:, None, :]
    npre = norm_mix_pre[:, None, :]
    npost = norm_mix_post[:, None, :]
    fpre = norm_ffn_pre[:, None, :]
    fpost = norm_ffn_post[:, None, :]
    cos_p, sin_p = _rope_tables(np.arange(seq))
    cos_s, sin_s = _rope_tables(PAST_LEN + np.arange(seq_s))
    conv_state = jnp.swapaxes(state_conv, 1, 2)

    hp = x_prompt
    hs = x_sample.reshape(nb, d_model)
    convs_p, convs_s = [], []
    states = None
    for l in range(depth):
        sq, sk, sv, sc, gates, conv_s = _sample_pre(hs, cos_s, sin_s, win, conv_w, alog, dtb, npre,
                                                    conv_state, l)
        hp, conv_p, gdn_p, ret_p, so, gdn_s, ret_s = _layer_prompt(
            hp, cos_p, sin_p, win, conv_w, alog, dtb, gnw, rnw, wout, npre, npost, wgu, wd, fpre, fpost,
            sq, sk, sv, sc, state_gdn, state_ret, states, l)
        states = (gdn_p, ret_p, gdn_s, ret_s)
        hs = _sample_post(so, gates, hs, gnw, rnw, wout, npost, wgu, wd, fpre, fpost, l)
        convs_p.append(conv_p)
        convs_s.append(conv_s)
    gdn_p, ret_p, gdn_s, ret_s = states
    return (hp, hs.reshape(nb, seq_s, d_model), jnp.stack(convs_p), gdn_p, ret_p,
            jnp.swapaxes(jnp.stack(convs_s), 1, 2), gdn_s, ret_s)
```

```python
import functools
import math

import jax
import jax.numpy as jnp
import numpy as np
from jax import lax
from jax.experimental import pallas as pl
from jax.experimental.pallas import tpu as pltpu

HEAD_DIM = 128
A_HEADS = 4
B_HEADS = 4
A_WIDTH = A_HEADS * HEAD_DIM
B_WIDTH = B_HEADS * HEAD_DIM
QKV_WIDTH = 3 * A_WIDTH
CONV_W = 4
CHUNK = 64
ROPE_BASE = 10000.0
EPS = 1e-6
L2_EPS = 1e-6
MASKED_LOG = -1e30
PAST_LEN = 16384
LANES = 128
SUBLANES = 8

REST_WIDTH = A_WIDTH + 4 * B_WIDTH + LANES
Z_OFF = 0
BQ_OFF = A_WIDTH
BK_OFF = BQ_OFF + B_WIDTH
BV_OFF = BK_OFF + B_WIDTH
BG_OFF = BV_OFF + B_WIDTH
SMALL_OFF = BG_OFF + B_WIDTH
IN_WIDTH_PADDED = QKV_WIDTH + REST_WIDTH

GROUP = 256
CHUNKS_PER_GROUP = GROUP // CHUNK
PROMPT_TIME_BLOCK = 256
FFN_COL_BLOCK = 256
WEIGHT_COL_BLOCK = 256
SAMPLE_SCALARS = 4
STACKED_STATE_OUTPUTS = (2, 3, 5, 6)
VMEM_LIMIT_BYTES = 56 * 1024 * 1024

_BF16 = jnp.bfloat16
_F32 = jnp.float32


def _dot(a, b):
    return jnp.dot(a.astype(_BF16), b.astype(_BF16), preferred_element_type=_F32)


def _dot_nt(a, b):
    return lax.dot_general(a.astype(_BF16), b.astype(_BF16), (((1,), (1,)), ((), ())),
                           preferred_element_type=_F32)


def _dot_tn(a, b):
    return lax.dot_general(a.astype(_BF16), b.astype(_BF16), (((0,), (0,)), ((), ())),
                           preferred_element_type=_F32)


def _rmsnorm(x, w):
    return x * lax.rsqrt(jnp.mean(x * x, axis=-1, keepdims=True) + EPS) * w


def _silu(x):
    return x * jax.nn.sigmoid(x)


def _softplus(x):
    return jnp.maximum(x, 0.0) + jnp.log1p(jnp.exp(-jnp.abs(x)))


def _log_gamma(h):
    return math.log1p(-(2.0 ** (-5.0 - h)))


def _rope(x, cos_full, sin_signed):
    return x * cos_full + pltpu.roll(x, HEAD_DIM // 2, axis=1) * sin_signed


def _head_cols(base, h):
    return slice(base + h * HEAD_DIM, base + (h + 1) * HEAD_DIM)


def _gated_head_norm(o, w, gate):
    return o * lax.rsqrt(jnp.mean(o * o, axis=-1, keepdims=True) + EPS) * w * _silu(gate)


class _Filler:
    def __init__(self, pieces):
        self._pieces = list(pieces)
        self._next = 0

    def emit(self, count):
        for _ in range(count):
            if self._next < len(self._pieces):
                self._pieces[self._next]()
                self._next += 1

    def flush(self):
        self.emit(len(self._pieces))


def _fold_rows(m):
    out = m[0:CHUNK]
    for c in range(1, CHUNKS_PER_GROUP):
        out = out + m[c * CHUNK:(c + 1) * CHUNK]
    return out


def _unfold_rows(r, same_chunk):
    return jnp.where(same_chunk, jnp.concatenate([r] * CHUNKS_PER_GROUP, axis=0), 0.0)


def _unit_lower_inverses(a_folded, eye_folded, same_chunk, fill):
    xs = list(a_folded)
    ps = [eye_folded - x for x in xs]
    n = 1
    while n < CHUNK:
        for h in range(len(xs)):
            x_bd = _unfold_rows(xs[h], same_chunk).astype(_BF16)
            if n == 1:
                xs[h] = _dot(xs[h], x_bd)
            elif 2 * n < CHUNK:
                r = _dot(jnp.concatenate([xs[h], ps[h]], axis=0), x_bd)
                xs[h] = r[0:CHUNK]
                ps[h] = ps[h] + r[CHUNK:2 * CHUNK]
            else:
                ps[h] = ps[h] + _dot(ps[h], x_bd)
            fill.emit(h % 2)
        n *= 2
    return [_unfold_rows(p, same_chunk) for p in ps]


def _recurrences_one_group(g, fill, act_ref, pr_ref, small_ref, dec_ref, sg_ref, sr_ref, o_ref):
    rows = slice(g * GROUP, (g + 1) * GROUP)
    ri = lax.broadcasted_iota(jnp.int32, (GROUP, GROUP), 0)
    ci = lax.broadcasted_iota(jnp.int32, (GROUP, GROUP), 1)
    same_chunk = (ri // CHUNK) == (ci // CHUNK)
    tril = same_chunk & (ri >= ci)
    off_diag = ri != ci
    rf = lax.broadcasted_iota(jnp.int32, (CHUNK, GROUP), 0)
    cf = lax.broadcasted_iota(jnp.int32, (CHUNK, GROUP), 1)
    eye_folded = jnp.where(rf == cf % CHUNK, 1.0, 0.0).astype(_F32)
    pos = (lax.broadcasted_iota(jnp.int32, (GROUP, HEAD_DIM), 0) % CHUNK).astype(_F32)
    chunk_rows = [slice(c * CHUNK, (c + 1) * CHUNK) for c in range(CHUNKS_PER_GROUP)]

    qs, ks, gccs, a_folded, a_intra, uw_rhs, e_ins = [], [], [], [], [], [], []
    for h in range(A_HEADS):
        q = act_ref[rows, _head_cols(0, h)]
        k = act_ref[rows, _head_cols(A_WIDTH, h)]
        v = act_ref[rows, _head_cols(2 * A_WIDTH, h)]
        beta = jnp.broadcast_to(small_ref[0, rows, h:h + 1], (GROUP, HEAD_DIM))
        gc_col = small_ref[1, rows, A_HEADS + h:A_HEADS + h + 1]
        gcc = jnp.broadcast_to(gc_col, (GROUP, HEAD_DIM))
        decay = dec_ref[h]
        kb = k * beta
        e_in = jnp.exp(gcc)
        kq = _dot_nt(jnp.concatenate([kb, q], axis=0), k)
        a_folded.append(_fold_rows(jnp.where(off_diag, kq[0:GROUP] * decay, 0.0)))
        a_intra.append(kq[GROUP:2 * GROUP] * decay)
        uw_rhs.append(jnp.concatenate([v * beta, kb * e_in], axis=1).astype(_BF16))
        qs.append(q)
        ks.append(k)
        gccs.append(gcc)
        e_ins.append(e_in)
        fill.emit(1)

    r_qe, r_oloc, r_b, r_echunk = [], [], [], []
    for h in range(B_HEADS):
        lg = _log_gamma(h)
        q = pr_ref[rows, _head_cols(BQ_OFF, h)]
        k = pr_ref[rows, _head_cols(BK_OFF, h)]
        v = pr_ref[rows, _head_cols(BV_OFF, h)]
        decay = jnp.exp(jnp.where(tril, (ri - ci).astype(_F32) * lg, MASKED_LOG))
        vb16 = v.astype(_BF16)
        r_oloc.append(_dot(_dot_nt(q, k) * decay, vb16))
        r_qe.append((q * jnp.exp((pos + 1.0) * lg)).astype(_BF16))
        k_out = (k * jnp.exp((CHUNK - 1.0 - pos) * lg)).astype(_BF16)
        r_b.append([_dot_tn(k_out[cr], vb16[cr]) for cr in chunk_rows])
        r_echunk.append(math.exp(CHUNK * lg))

    t_inv = _unit_lower_inverses(a_folded, eye_folded, same_chunk, fill)

    g_lhs, g_oloc, g_b, g_elast = [], [], [], []
    for h in range(A_HEADS):
        uw = _dot(t_inv[h], uw_rhs[h])
        uw16 = uw.astype(_BF16)
        aiuw = _dot(a_intra[h], uw16)
        g_oloc.append(aiuw[:, 0:HEAD_DIM])
        q_eff = qs[h] * e_ins[h] - aiuw[:, HEAD_DIM:2 * HEAD_DIM]
        lhs, bs, elast = [], [], []
        for cr in chunk_rows:
            gcc_c = gccs[h][cr]
            g_last = gcc_c[CHUNK - 1:CHUNK, :]
            k_out = ks[h][cr] * jnp.exp(g_last - gcc_c)
            bg = _dot_tn(k_out, uw16[cr])
            bs.append(bg[:, 0:HEAD_DIM])
            lhs.append(jnp.concatenate([q_eff[cr], bg[:, HEAD_DIM:2 * HEAD_DIM]], axis=0).astype(_BF16))
            elast.append(jnp.exp(g_last))
        g_lhs.append(lhs)
        g_b.append(bs)
        g_elast.append(elast)
        fill.emit(1)

    g_state = [sg_ref[h] for h in range(A_HEADS)]
    r_state = [sr_ref[h] for h in range(B_HEADS)]
    for c, cr in enumerate(chunk_rows):
        out_rows = slice(g * GROUP + c * CHUNK, g * GROUP + (c + 1) * CHUNK)
        for h in range(A_HEADS):
            r = _dot(g_lhs[h][c], g_state[h])
            o_ref[out_rows, _head_cols(0, h)] = r[0:CHUNK] + g_oloc[h][cr]
            g_state[h] = g_state[h] * g_elast[h][c] + g_b[h][c] - r[CHUNK:CHUNK + HEAD_DIM]
        for h in range(B_HEADS):
            o_ref[out_rows, _head_cols(A_WIDTH, h)] = _dot(r_qe[h][cr], r_state[h]) + r_oloc[h][cr]
            r_state[h] = r_state[h] * r_echunk[h] + r_b[h][c]
        fill.emit(1)
    for h in range(A_HEADS):
        sg_ref[h] = g_state[h]
    for h in range(B_HEADS):
        sr_ref[h] = r_state[h]


def _ffn_rows(h, wgu_ref, wd_ref, npre, npost, act_ref, d_ff):
    hn = _rmsnorm(h, npre).astype(_BF16)
    for j in range(d_ff // FFN_COL_BLOCK):
        cg = slice(j * FFN_COL_BLOCK, (j + 1) * FFN_COL_BLOCK)
        cu = slice(d_ff + j * FFN_COL_BLOCK, d_ff + (j + 1) * FFN_COL_BLOCK)
        gate = jnp.dot(hn, wgu_ref[:, cg], preferred_element_type=_F32)
        up = jnp.dot(hn, wgu_ref[:, cu], preferred_element_type=_F32)
        act_ref[:, cg] = (_silu(gate) * up).astype(_BF16)
    f = jnp.dot(act_ref[...], wd_ref[...], preferred_element_type=_F32)
    return h + _rmsnorm(f, npost)


def _ffn_pieces(h_ref, hn_ref, wgu_ref, wd_ref, npost, gate_ref, act_ref, f_ref, y_ref, d_ff):
    d_model = f_ref.shape[1]
    k_split = (d_ff // FFN_COL_BLOCK + 1) // 2 * FFN_COL_BLOCK
    pieces = []

    def gate(j):
        def run():
            cg = slice(j * FFN_COL_BLOCK, (j + 1) * FFN_COL_BLOCK)
            gate_ref[j % 2] = _silu(jnp.dot(hn_ref[...], wgu_ref[:, cg], preferred_element_type=_F32))
        return run

    def up(j):
        def run():
            cg = slice(j * FFN_COL_BLOCK, (j + 1) * FFN_COL_BLOCK)
            cu = slice(d_ff + j * FFN_COL_BLOCK, d_ff + (j + 1) * FFN_COL_BLOCK)
            act_ref[:, cg] = (gate_ref[j % 2] * jnp.dot(hn_ref[...], wgu_ref[:, cu],
                                                        preferred_element_type=_F32)).astype(_BF16)
        return run

    def down(j, first):
        def run():
            cs = slice(j * FFN_COL_BLOCK, (j + 1) * FFN_COL_BLOCK)
            if first:
                f_ref[:, cs] = jnp.dot(act_ref[:, 0:k_split], wd_ref[0:k_split, cs],
                                       preferred_element_type=_F32)
            else:
                f_ref[:, cs] = f_ref[:, cs] + jnp.dot(act_ref[:, k_split:d_ff], wd_ref[k_split:d_ff, cs],
                                                      preferred_element_type=_F32)
        return run

    def finish():
        y_ref[...] = h_ref[...] + _rmsnorm(f_ref[...], npost)

    for j in range(d_ff // FFN_COL_BLOCK):
        pieces += [gate(j), up(j)]
    for j in range(d_model // FFN_COL_BLOCK):
        pieces += [down(j, True), down(j, False)]
    pieces.append(finish)
    return pieces


def _column_of_row(row):
    return jnp.broadcast_to(row, (HEAD_DIM, HEAD_DIM)).T


def _sample_state_units(first_seq, per_step, sq_ref, sk_ref, sv_ref, sc_ref, sgi_ref, sri_ref,
                        so_ref, sgo_ref, sro_ref):
    base = pl.multiple_of((first_seq // SUBLANES) * SUBLANES, SUBLANES)
    rows = pl.ds(base, SUBLANES)
    row_id = lax.broadcasted_iota(jnp.int32, (SUBLANES, HEAD_DIM), 0)
    picks = [row_id == (first_seq - base + j) for j in range(per_step)]

    def pick(block, j):
        return jnp.sum(jnp.where(picks[j], block, 0.0), axis=0, keepdims=True)

    def put(cols, outs):
        block = so_ref[rows, cols]
        for j in range(per_step):
            block = jnp.where(picks[j], outs[j], block)
        so_ref[rows, cols] = block

    def deltanet(h):
        def run():
            cols = _head_cols(0, h)
            q8, k8, v8 = sq_ref[rows, cols], sk_ref[rows, cols], sv_ref[rows, cols]
            eg8 = sc_ref[rows, _head_cols(0, h)]
            bt8 = sc_ref[rows, _head_cols(A_WIDTH, h)]
            qk8 = sc_ref[rows, _head_cols(2 * A_WIDTH, h)]
            kq16 = jnp.concatenate([k8, q8], axis=0).astype(_BF16)
            kqs = [jnp.dot(kq16, sgi_ref[j, h].astype(_BF16), preferred_element_type=_F32)
                   for j in range(per_step)]
            outs = []
            for j in range(per_step):
                eg = pick(eg8, j)
                v_new = pick(bt8, j) * (pick(v8, j) - eg * pick(kqs[j][0:SUBLANES], j))
                outs.append(eg * pick(kqs[j][SUBLANES:2 * SUBLANES], j) + pick(qk8, j) * v_new)
                sgo_ref[j, h] = sgi_ref[j, h] * eg + _column_of_row(pick(k8, j)) * v_new
            put(cols, outs)
        return run

    def retention(h):
        def run():
            gamma = math.exp(_log_gamma(h))
            cols = _head_cols(A_WIDTH, h)
            q8, k8, v8 = sq_ref[rows, cols], sk_ref[rows, cols], sv_ref[rows, cols]
            qk8 = sc_ref[rows, _head_cols(3 * A_WIDTH, h)]
            q16 = q8.astype(_BF16)
            qss = [jnp.dot(q16, sri_ref[j, h].astype(_BF16), preferred_element_type=_F32)
                   for j in range(per_step)]
            outs = []
            for j in range(per_step):
                v1 = pick(v8, j)
                outs.append(gamma * pick(qss[j], j) + pick(qk8, j) * v1)
                sro_ref[j, h] = sri_ref[j, h] * gamma + _column_of_row(pick(k8, j)) * v1
            put(cols, outs)
        return run

    return [deltanet(h) for h in range(A_HEADS)] + [retention(h) for h in range(B_HEADS)]


def _layer_prompt_refs(refs, aliased):
    n_in = 22
    return refs[:n_in], refs[n_in + (len(STACKED_STATE_OUTPUTS) if aliased else 0):]


def _layer_prompt_body(*refs, aliased, tb, nt, nblocks, d_ff, per_step):
    s = pl.program_id(0)

    @pl.when(s < nblocks)
    def _():
        _layer_prompt_step(refs, aliased, tb, nt, d_ff, per_step)

    @pl.when(s == nblocks)
    def _():
        ins, rest = _layer_prompt_refs(refs, aliased)
        wgu_ref, wd_ref, fpost_ref = ins[12], ins[13], ins[15]
        y_ref = rest[0]
        h_ref, hn_ref, gate_ref, ffn_act_ref, f_ref = rest[14:19]
        prev = lax.rem(s + 1, 2)
        _Filler(_ffn_pieces(h_ref.at[prev], hn_ref.at[prev], wgu_ref, wd_ref, fpost_ref[...],
                            gate_ref, ffn_act_ref, f_ref, y_ref, d_ff)).flush()


def _layer_prompt_step(refs, aliased, tb, nt, d_ff, per_step):
    ins, rest = _layer_prompt_refs(refs, aliased)
    (x_ref, cos_ref, sin_ref, win_ref, convw_ref, alog_ref, dtb_ref, gnw_ref, rnw_ref, wout_ref,
     npre_ref, npost_ref, wgu_ref, wd_ref, fpre_ref, fpost_ref,
     sq_ref, sk_ref, sv_ref, sc_ref, sgi_ref, sri_ref) = ins
    (y_ref, conv_ref, sg_ref, sr_ref, so_ref, sgo_ref, sro_ref,
     pq_ref, pr_ref, act_ref, small_ref, gct_ref, dec_ref, o_ref, h_ref, hn_ref, gate_ref,
     ffn_act_ref, f_ref) = rest
    s = pl.program_id(0)
    t = lax.rem(s, nt)
    pad = SUBLANES

    @pl.when(s == 0)
    def _():
        h_ref[1] = jnp.zeros(h_ref.shape[1:], _F32)
        hn_ref[1] = jnp.zeros(hn_ref.shape[1:], _BF16)
        so_ref[...] = jnp.zeros(so_ref.shape, _F32)

    @pl.when(t == 0)
    def _():
        pq_ref[0:pad, :] = jnp.zeros((pad, QKV_WIDTH), _F32)
        sg_ref[...] = jnp.zeros(sg_ref.shape, _F32)
        sr_ref[...] = jnp.zeros(sr_ref.shape, _F32)

    prev = lax.rem(s + 1, 2)
    fill = _Filler(_ffn_pieces(h_ref.at[prev], hn_ref.at[prev], wgu_ref, wd_ref, fpost_ref[...],
                               gate_ref, ffn_act_ref, f_ref, y_ref, d_ff))
    fill.emit(2)
    sample = _Filler(_sample_state_units(s * per_step, per_step, sq_ref, sk_ref, sv_ref, sc_ref, sgi_ref,
                                         sri_ref, so_ref, sgo_ref, sro_ref))

    x = x_ref[...]
    hn = _rmsnorm(x, npre_ref[...]).astype(_BF16)

    def project(lo, hi):
        return jnp.dot(hn, win_ref[:, lo:hi], preferred_element_type=_F32)

    def project_rest(lo, hi):
        pr_ref[:, lo:hi] = project(QKV_WIDTH + lo, QKV_WIDTH + hi)

    ps = project(QKV_WIDTH + SMALL_OFF, IN_WIDTH_PADDED)
    fill.emit(1)
    beta_all = jax.nn.sigmoid(ps)
    gc = -jnp.exp(alog_ref[...]) * _softplus(ps + dtb_ref[...])
    row_in_chunk = lax.broadcasted_iota(jnp.int32, (tb, LANES), 0) % CHUNK
    shift = 1
    while shift < CHUNK:
        gc = gc + jnp.where(row_in_chunk >= shift, pltpu.roll(gc, shift, axis=0), 0.0)
        shift *= 2
    small_ref[0] = beta_all
    small_ref[1] = gc
    gct_ref[...] = gc.T

    half = B_WIDTH // 2
    cos_full = cos_ref[...]
    sin_signed = sin_ref[...]
    ri = lax.broadcasted_iota(jnp.int32, (GROUP, GROUP), 0)
    ci = lax.broadcasted_iota(jnp.int32, (GROUP, GROUP), 1)
    tril = ((ri // CHUNK) == (ci // CHUNK)) & (ri >= ci)
    for pair in range(B_HEADS // 2):
        project_rest(BQ_OFF + pair * half, BQ_OFF + (pair + 1) * half)
        fill.emit(1)
        project_rest(BK_OFF + pair * half, BK_OFF + (pair + 1) * half)
        fill.emit(1)
        for h in (2 * pair, 2 * pair + 1):
            gc_col = small_ref[1, :, A_HEADS + h:A_HEADS + h + 1]
            gcr = jnp.broadcast_to(gct_ref[A_HEADS + h:A_HEADS + h + 1, :], (GROUP, GROUP))
            diff = jnp.broadcast_to(gc_col, (GROUP, GROUP)) - gcr
            dec_ref[h] = jnp.exp(jnp.where(tril, diff, MASKED_LOG))
            cq = _head_cols(BQ_OFF, h)
            ck = _head_cols(BK_OFF, h)
            pr_ref[:, cq] = _rope(pr_ref[:, cq], cos_full, sin_signed)
            pr_ref[:, ck] = _rope(pr_ref[:, ck], cos_full, sin_signed) * (HEAD_DIM ** -0.5)

    for lo in range(0, QKV_WIDTH, FFN_COL_BLOCK):
        pq_ref[pad:pad + tb, lo:lo + FFN_COL_BLOCK] = project(lo, lo + FFN_COL_BLOCK)
        fill.emit(1)
        sample.emit(1)

    rest = [(lo, lo + FFN_COL_BLOCK) for lo in range(Z_OFF, BQ_OFF, FFN_COL_BLOCK)]
    rest += [(lo, lo + FFN_COL_BLOCK) for lo in range(BV_OFF, SMALL_OFF, FFN_COL_BLOCK)]
    for j in range(QKV_WIDTH // LANES):
        if j % 2 == 0 and rest:
            project_rest(*rest.pop(0))
        else:
            fill.emit(1)
        cs = slice(j * LANES, (j + 1) * LANES)
        acc = pq_ref[pad:pad + tb, cs] * convw_ref[3:4, cs]
        for i in range(CONV_W - 1):
            acc = acc + pq_ref[pad - 3 + i:pad - 3 + i + tb, cs] * convw_ref[i:i + 1, cs]
        a = _silu(acc)
        if j < 2 * A_HEADS:
            a = a * lax.rsqrt(jnp.sum(a * a, axis=-1, keepdims=True) + L2_EPS)
        if j < A_HEADS:
            a = a * (HEAD_DIM ** -0.5)
        act_ref[:, cs] = a
    assert not rest

    tail = pq_ref[pad + tb - 3:pad + tb, :]
    conv_ref[...] = tail
    pq_ref[pad - 3:pad, :] = tail

    for g in range(tb // GROUP):
        _recurrences_one_group(g, fill, act_ref, pr_ref, small_ref, dec_ref, sg_ref, sr_ref, o_ref)

    for h in range(A_HEADS):
        cs = _head_cols(0, h)
        o_ref[:, cs] = _gated_head_norm(o_ref[:, cs], gnw_ref[...], pr_ref[:, _head_cols(Z_OFF, h)])
    for h in range(B_HEADS):
        cs = _head_cols(A_WIDTH, h)
        o_ref[:, cs] = _gated_head_norm(o_ref[:, cs], rnw_ref[...], pr_ref[:, _head_cols(BG_OFF, h)])
    m = jnp.dot(o_ref[...].astype(_BF16), wout_ref[...], preferred_element_type=_F32)
    fill.emit(8)
    sample.flush()
    h_new = x_ref[...] + _rmsnorm(m, npost_ref[...])
    cur = lax.rem(s, 2)
    h_ref[cur] = h_new
    hn_ref[cur] = _rmsnorm(h_new, fpre_ref[...]).astype(_BF16)
    fill.flush()


def _layer_spec(shape, layer):
    zeros = (0,) * len(shape)
    return pl.BlockSpec((None,) + tuple(shape), lambda i: (layer,) + zeros)


def _layer_prompt(x, cos_full, sin_signed, win, conv_w, alog, dtb, gnw, rnw, wout, npre, npost,
                  wgu, wd, fpre, fpost, sq, sk, sv, sc, state_gdn, state_ret, prev_states, layer):
    batch, seq, d_model = x.shape
    nb = sq.shape[0]
    depth = state_gdn.shape[0]
    d_ff = wd.shape[1]
    tb = min(PROMPT_TIME_BLOCK, seq)
    assert seq % tb == 0 and tb == GROUP and d_ff % FFN_COL_BLOCK == 0
    nt = seq // tb
    nblocks = batch * nt
    assert nb % nblocks == 0 and SUBLANES % (nb // nblocks) == 0
    per_step = nb // nblocks
    lspec = functools.partial(_layer_spec, layer=layer)

    def mixer_block(s):
        return jnp.minimum(s, nblocks - 1)

    def ffn_block(s):
        return jnp.maximum(s - 1, 0)

    full = lambda shape: pl.BlockSpec(tuple(shape), lambda s: (0,) * len(shape))
    state_spec = lambda heads: pl.BlockSpec((None, per_step, heads, HEAD_DIM, HEAD_DIM),
                                            lambda s: (layer, mixer_block(s), 0, 0, 0))

    in_specs = [
        pl.BlockSpec((None, tb, d_model), lambda s: (mixer_block(s) // nt, mixer_block(s) % nt, 0)),
        pl.BlockSpec((tb, HEAD_DIM), lambda s: (mixer_block(s) % nt, 0)),
        pl.BlockSpec((tb, HEAD_DIM), lambda s: (mixer_block(s) % nt, 0)),
        lspec((d_model, IN_WIDTH_PADDED)),
        lspec((CONV_W, QKV_WIDTH)),
        lspec((1, LANES)),
        lspec((1, LANES)),
        lspec((1, HEAD_DIM)),
        lspec((1, HEAD_DIM)),
        lspec((A_WIDTH + B_WIDTH, d_model)),
        lspec((1, d_model)),
        lspec((1, d_model)),
        lspec((d_model, 2 * d_ff)),
        lspec((d_ff, d_model)),
        lspec((1, d_model)),
        lspec((1, d_model)),
        full(sq.shape), full(sk.shape), full(sv.shape), full(sc.shape),
        state_spec(A_HEADS), state_spec(B_HEADS),
    ]
    args = [x, cos_full, sin_signed, win, conv_w, alog, dtb, gnw, rnw, wout, npre, npost, wgu, wd, fpre,
            fpost, sq, sk, sv, sc, state_gdn, state_ret]
    aliases = {}
    if prev_states is not None:
        in_specs += [pl.BlockSpec(memory_space=pl.ANY)] * len(STACKED_STATE_OUTPUTS)
        aliases = {len(args) + i: out for i, out in enumerate(STACKED_STATE_OUTPUTS)}
        args += list(prev_states)
    prompt_state_spec = lambda heads: pl.BlockSpec((None, None, heads, HEAD_DIM, HEAD_DIM),
                                                   lambda s: (layer, mixer_block(s) // nt, 0, 0, 0))
    out_specs = [
        pl.BlockSpec((None, tb, d_model), lambda s: (ffn_block(s) // nt, ffn_block(s) % nt, 0)),
        pl.BlockSpec((None, CONV_W - 1, QKV_WIDTH), lambda s: (mixer_block(s) // nt, 0, 0)),
        prompt_state_spec(A_HEADS), prompt_state_spec(B_HEADS),
        full(sq.shape), state_spec(A_HEADS), state_spec(B_HEADS),
    ]
    out_shape = [
        jax.ShapeDtypeStruct((batch, seq, d_model), _F32),
        jax.ShapeDtypeStruct((batch, CONV_W - 1, QKV_WIDTH), _F32),
        jax.ShapeDtypeStruct((depth, batch, A_HEADS, HEAD_DIM, HEAD_DIM), _F32),
        jax.ShapeDtypeStruct((depth, batch, B_HEADS, HEAD_DIM, HEAD_DIM), _F32),
        jax.ShapeDtypeStruct(sq.shape, _F32),
        jax.ShapeDtypeStruct((depth, nb, A_HEADS, HEAD_DIM, HEAD_DIM), _F32),
        jax.ShapeDtypeStruct((depth, nb, B_HEADS, HEAD_DIM, HEAD_DIM), _F32),
    ]
    scratch = [
        pltpu.VMEM((tb + SUBLANES, QKV_WIDTH), _F32),
        pltpu.VMEM((tb, REST_WIDTH), _F32),
        pltpu.VMEM((tb, QKV_WIDTH), _F32),
        pltpu.VMEM((2, tb, LANES), _F32),
        pltpu.VMEM((LANES, tb), _F32),
        pltpu.VMEM((A_HEADS, GROUP, GROUP), _F32),
        pltpu.VMEM((tb, A_WIDTH + B_WIDTH), _F32),
        pltpu.VMEM((2, tb, d_model), _F32),
        pltpu.VMEM((2, tb, d_model), _BF16),
        pltpu.VMEM((2, tb, FFN_COL_BLOCK), _F32),
        pltpu.VMEM((tb, d_ff), _BF16),
        pltpu.VMEM((tb, d_model), _F32),
    ]
    return pl.pallas_call(
        functools.partial(_layer_prompt_body, aliased=prev_states is not None, tb=tb, nt=nt, nblocks=nblocks,
                          d_ff=d_ff, per_step=per_step),
        grid=(nblocks + 1,), in_specs=in_specs, out_specs=out_specs, out_shape=out_shape,
        scratch_shapes=scratch, input_output_aliases=aliases,
        compiler_params=pltpu.CompilerParams(
            dimension_semantics=("arbitrary",), vmem_limit_bytes=VMEM_LIMIT_BYTES),
        name=f"layer_prompt_l{layer}",
    )(*args)


def _sample_pre_body(x_ref, cos_ref, sin_ref, win_ref, convw_ref, alog_ref, dtb_ref, npre_ref, convs_ref,
                     q_ref, k_ref, v_ref, sc_ref, gates_ref, convn_ref):
    nb = x_ref.shape[0]
    hn = _rmsnorm(x_ref[...], npre_ref[...]).astype(_BF16)
    pq = jnp.dot(hn, win_ref[:, 0:QKV_WIDTH], preferred_element_type=_F32)
    pr = jnp.dot(hn, win_ref[:, QKV_WIDTH:IN_WIDTH_PADDED], preferred_element_type=_F32)
    outs = (q_ref, k_ref, v_ref)
    for j in range(QKV_WIDTH // LANES):
        cs = slice(j * LANES, (j + 1) * LANES)
        new = pq[:, cs]
        acc = new * convw_ref[3:4, cs]
        for r in range(CONV_W - 1):
            hist = convs_ref[r, :, cs]
            acc = acc + hist * convw_ref[r:r + 1, cs]
            if r > 0:
                convn_ref[r - 1, :, cs] = hist
        convn_ref[CONV_W - 2, :, cs] = new
        a = _silu(acc)
        if j < 2 * A_HEADS:
            a = a * lax.rsqrt(jnp.sum(a * a, axis=-1, keepdims=True) + L2_EPS)
        if j < A_HEADS:
            a = a * (HEAD_DIM ** -0.5)
        outs[j // A_HEADS][:, _head_cols(0, j % A_HEADS)] = a
    ps = pr[:, SMALL_OFF:SMALL_OFF + LANES]
    beta_all = jax.nn.sigmoid(ps)
    eg_all = jnp.exp(-jnp.exp(alog_ref[...]) * _softplus(ps + dtb_ref[...]))
    cos_full = cos_ref[...]
    sin_signed = sin_ref[...]
    for h in range(A_HEADS):
        sc_ref[:, _head_cols(0, h)] = jnp.broadcast_to(eg_all[:, A_HEADS + h:A_HEADS + h + 1], (nb, LANES))
        sc_ref[:, _head_cols(A_WIDTH, h)] = jnp.broadcast_to(beta_all[:, h:h + 1], (nb, LANES))
        qk = jnp.sum(q_ref[:, _head_cols(0, h)] * k_ref[:, _head_cols(0, h)], axis=-1, keepdims=True)
        sc_ref[:, _head_cols(2 * A_WIDTH, h)] = jnp.broadcast_to(qk, (nb, LANES))
    for h in range(B_HEADS):
        q = _rope(pr[:, _head_cols(BQ_OFF, h)], cos_full, sin_signed)
        k = _rope(pr[:, _head_cols(BK_OFF, h)], cos_full, sin_signed) * (HEAD_DIM ** -0.5)
        q_ref[:, _head_cols(A_WIDTH, h)] = q
        k_ref[:, _head_cols(A_WIDTH, h)] = k
        sc_ref[:, _head_cols(3 * A_WIDTH, h)] = jnp.broadcast_to(
            jnp.sum(q * k, axis=-1, keepdims=True), (nb, LANES))
    v_ref[:, A_WIDTH:A_WIDTH + B_WIDTH] = pr[:, BV_OFF:BG_OFF]
    gates_ref[:, 0:A_WIDTH] = pr[:, Z_OFF:BQ_OFF]
    gates_ref[:, A_WIDTH:A_WIDTH + B_WIDTH] = pr[:, BG_OFF:SMALL_OFF]


def _sample_pre(x, cos_full, sin_signed, win, conv_w, alog, dtb, npre, conv_state, layer):
    nb, d_model = x.shape
    width = A_WIDTH + B_WIDTH
    lspec = functools.partial(_layer_spec, layer=layer)
    full = lambda shape: pl.BlockSpec(tuple(shape), lambda i: (0,) * len(shape))
    shapes = [(nb, width)] * 3 + [(nb, SAMPLE_SCALARS * A_WIDTH), (nb, width), (CONV_W - 1, nb, QKV_WIDTH)]
    return pl.pallas_call(
        _sample_pre_body,
        grid=(1,),
        in_specs=[full((nb, d_model)), full((1, HEAD_DIM)), full((1, HEAD_DIM)),
                  lspec((d_model, IN_WIDTH_PADDED)), lspec((CONV_W, QKV_WIDTH)), lspec((1, LANES)),
                  lspec((1, LANES)), lspec((1, d_model)), lspec((CONV_W - 1, nb, QKV_WIDTH))],
        out_specs=[full(shape) for shape in shapes],
        out_shape=[jax.ShapeDtypeStruct(shape, _F32) for shape in shapes],
        compiler_params=pltpu.CompilerParams(
            dimension_semantics=("arbitrary",), vmem_limit_bytes=VMEM_LIMIT_BYTES),
        name=f"sample_pre_l{layer}",
    )(x, cos_full, sin_signed, win, conv_w, alog, dtb, npre, conv_state)


def _sample_post_body(o_ref, gates_ref, x_ref, gnw_ref, rnw_ref, wout_ref, npost_ref, wgu_ref, wd_ref,
                      fpre_ref, fpost_ref, y_ref, cat_ref, act_ref, *, d_ff):
    for h in range(A_HEADS + B_HEADS):
        cs = _head_cols(0, h)
        w = gnw_ref[...] if h < A_HEADS else rnw_ref[...]
        cat_ref[:, cs] = _gated_head_norm(o_ref[:, cs], w, gates_ref[:, cs]).astype(_BF16)
    m = jnp.dot(cat_ref[...], wout_ref[...], preferred_element_type=_F32)
    h_new = x_ref[...] + _rmsnorm(m, npost_ref[...])
    y_ref[...] = _ffn_rows(h_new, wgu_ref, wd_ref, fpre_ref[...], fpost_ref[...], act_ref, d_ff)


def _sample_post(o, gates, x, gnw, rnw, wout, npost, wgu, wd, fpre, fpost, layer):
    nb, d_model = x.shape
    d_ff = wd.shape[1]
    width = A_WIDTH + B_WIDTH
    lspec = functools.partial(_layer_spec, layer=layer)
    full = lambda shape: pl.BlockSpec(tuple(shape), lambda i: (0,) * len(shape))
    return pl.pallas_call(
        functools.partial(_sample_post_body, d_ff=d_ff),
        grid=(1,),
        in_specs=[full((nb, width)), full((nb, width)), full((nb, d_model)), lspec((1, HEAD_DIM)),
                  lspec((1, HEAD_DIM)), lspec((width, d_model)), lspec((1, d_model)),
                  lspec((d_model, 2 * d_ff)), lspec((d_ff, d_model)), lspec((1, d_model)),
                  lspec((1, d_model))],
        out_specs=full((nb, d_model)),
        out_shape=jax.ShapeDtypeStruct((nb, d_model), _F32),
        scratch_shapes=[pltpu.VMEM((nb, width), _BF16), pltpu.VMEM((nb, d_ff), _BF16)],
        compiler_params=pltpu.CompilerParams(
            dimension_semantics=("arbitrary",), vmem_limit_bytes=VMEM_LIMIT_BYTES),
        name=f"sample_post_l{layer}",
    )(o, gates, x, gnw, rnw, wout, npost, wgu, wd, fpre, fpost)


def _rope_tables(positions):
    half = HEAD_DIM // 2
    inv = ROPE_BASE ** (-np.arange(half, dtype=np.float64) / half)
    ang = np.asarray(positions, dtype=np.float64)[:, None] * inv[None, :]
    cos, sin = np.cos(ang), np.sin(ang)
    return (jnp.asarray(np.concatenate([cos, cos], axis=-1), dtype=_F32),
            jnp.asarray(np.concatenate([-sin, sin], axis=-1), dtype=_F32))


def _rearranged_w_in_body(wt_ref, o_ref):
    small0 = QKV_WIDTH + A_WIDTH
    small1 = small0 + 2 * A_HEADS
    cb = WEIGHT_COL_BLOCK
    for lo in range(0, small0, cb):
        o_ref[:, lo:lo + cb] = wt_ref[lo:lo + cb, :].T.astype(_BF16)
    for lo in range(small0, QKV_WIDTH + SMALL_OFF, cb):
        src = lo + small1 - small0
        o_ref[:, lo:lo + cb] = wt_ref[src:src + cb, :].T.astype(_BF16)
    lane = lax.broadcasted_iota(jnp.int32, (o_ref.shape[0], LANES), 1)
    o_ref[:, QKV_WIDTH + SMALL_OFF:IN_WIDTH_PADDED] = jnp.where(
        lane < 2 * A_HEADS, wt_ref[small0:small0 + LANES, :].T, 0.0).astype(_BF16)


def _rearranged_w_in(w_in):
    depth, d_model, width = w_in.shape
    assert width == IN_WIDTH_PADDED - LANES + 2 * A_HEADS
    return pl.pallas_call(
        _rearranged_w_in_body,
        grid=(depth,),
        in_specs=[pl.BlockSpec((None, width, d_model), lambda l: (l, 0, 0), pipeline_mode=pl.Buffered(1))],
        out_specs=pl.BlockSpec((None, d_model, IN_WIDTH_PADDED), lambda l: (l, 0, 0)),
        out_shape=jax.ShapeDtypeStruct((depth, d_model, IN_WIDTH_PADDED), _BF16),
        compiler_params=pltpu.CompilerParams(
            dimension_semantics=("arbitrary",), vmem_limit_bytes=VMEM_LIMIT_BYTES),
        name="rearranged_w_in",
    )(jnp.swapaxes(w_in, 1, 2))


def kernel(x_prompt, x_sample, state_conv, state_gdn, state_ret, w_in, conv_w, a_log, dt_bias, gdn_norm_w, ret_norm_w, w_out, norm_mix_pre, norm_mix_post, norm_ffn_pre, norm_ffn_post, w_gate_up, w_down):
    depth = w_in.shape[0]
    batch, seq, d_model = x_prompt.shape
    nb, seq_s, _ = x_sample.shape
    assert seq_s == 1

    win = _rearranged_w_in(w_in)
    wout = w_out.astype(_BF16)
    wgu = w_gate_up.astype(_BF16)
    wd = w_down.astype(_BF16)
    alog = jnp.pad(a_log, ((0, 0), (A_HEADS, LANES - 2 * A_HEADS)))[:, None, :]
    dtb = jnp.pad(dt_bias, ((0, 0), (A_HEADS, LANES - 2 * A_HEADS)))[:, None, :]
    gnw = gdn_norm_w[:, None, :]
    rnw = ret_norm_w[:, None, :]
    npre = norm_mix_pre[:, None, :]
    npost = norm_mix_post[:, None, :]
    fpre = norm_ffn_pre[:, None, :]
    fpost = norm_ffn_post[:, None, :]
    cos_p, sin_p = _rope_tables(np.arange(seq))
    cos_s, sin_s = _rope_tables(PAST_LEN + np.arange(seq_s))
    conv_state = jnp.swapaxes(state_conv, 1, 2)

    hp = x_prompt
    hs = x_sample.reshape(nb, d_model)
    convs_p, convs_s = [], []
    states = None
    for l in range(depth):
        sq, sk, sv, sc, gates, conv_s = _sample_pre(hs, cos_s, sin_s, win, conv_w, alog, dtb, npre,
                                                    conv_state, l)
        hp, conv_p, gdn_p, ret_p, so, gdn_s, ret_s = _layer_prompt(
            hp, cos_p, sin_p, win, conv_w, alog, dtb, gnw, rnw, wout, npre, npost, wgu, wd, fpre, fpost,
            sq, sk, sv, sc, state_gdn, state_ret, states, l)
        states = (gdn_p, ret_p, gdn_s, ret_s)
        hs = _sample_post(so, gates, hs, gnw, rnw, wout, npost, wgu, wd, fpre, fpost, l)
        convs_p.append(conv_p)
        convs_s.append(conv_s)
    gdn_p, ret_p, gdn_s, ret_s = states
    return (hp, hs.reshape(nb, seq_s, d_model), jnp.stack(convs_p), gdn_p, ret_p,
            jnp.swapaxes(jnp.stack(convs_s), 1, 2), gdn_s, ret_s)
```

```python
import collections
import functools
import math

import jax
import jax.numpy as jnp
import numpy as np
from jax import lax
from jax.experimental import pallas as pl
from jax.experimental.pallas import tpu as pltpu

HEAD_DIM = 128
A_HEADS = 4
B_HEADS = 4
A_WIDTH = A_HEADS * HEAD_DIM
B_WIDTH = B_HEADS * HEAD_DIM
QKV_WIDTH = 3 * A_WIDTH
CONV_W = 4
CHUNK = 64
ROPE_BASE = 10000.0
EPS = 1e-6
L2_EPS = 1e-6
MASKED_LOG = -1e30
PAST_LEN = 16384
LANES = 128
SUBLANES = 8

REST_WIDTH = A_WIDTH + 4 * B_WIDTH + LANES
Z_OFF = 0
BQ_OFF = A_WIDTH
BK_OFF = BQ_OFF + B_WIDTH
BV_OFF = BK_OFF + B_WIDTH
BG_OFF = BV_OFF + B_WIDTH
SMALL_OFF = BG_OFF + B_WIDTH
IN_WIDTH_PADDED = QKV_WIDTH + REST_WIDTH

GROUP = 256
CHUNKS_PER_GROUP = GROUP // CHUNK
PROMPT_TIME_BLOCK = 256
FFN_COL_BLOCK = 256
WEIGHT_COL_BLOCK = 256
SAMPLE_SCALARS = 4
STACKED_STATE_OUTPUTS = (2, 3, 5, 6)
VMEM_LIMIT_BYTES = 56 * 1024 * 1024

_BF16 = jnp.bfloat16
_F32 = jnp.float32


def _dot(a, b):
    return jnp.dot(a.astype(_BF16), b.astype(_BF16), preferred_element_type=_F32)


def _dot_nt(a, b):
    return lax.dot_general(a.astype(_BF16), b.astype(_BF16), (((1,), (1,)), ((), ())),
                           preferred_element_type=_F32)


def _dot_tn(a, b):
    return lax.dot_general(a.astype(_BF16), b.astype(_BF16), (((0,), (0,)), ((), ())),
                           preferred_element_type=_F32)


def _rmsnorm(x, w):
    return x * lax.rsqrt(jnp.mean(x * x, axis=-1, keepdims=True) + EPS) * w


def _silu(x):
    return x * jax.nn.sigmoid(x)


def _softplus(x):
    return jnp.maximum(x, 0.0) + jnp.log1p(jnp.exp(-jnp.abs(x)))


def _log_gamma(h):
    return math.log1p(-(2.0 ** (-5.0 - h)))


def _rope(x, cos_full, sin_signed):
    return x * cos_full + pltpu.roll(x, HEAD_DIM // 2, axis=1) * sin_signed


def _head_cols(base, h):
    return slice(base + h * HEAD_DIM, base + (h + 1) * HEAD_DIM)


def _gated_head_norm(o, w, gate):
    return o * lax.rsqrt(jnp.mean(o * o, axis=-1, keepdims=True) + EPS) * w * _silu(gate)


class _Filler:
    def __init__(self, pieces):
        self._pieces = list(pieces)
        self._next = 0

    def emit(self, count):
        for _ in range(count):
            if self._next < len(self._pieces):
                self._pieces[self._next]()
                self._next += 1

    def flush(self):
        self.emit(len(self._pieces))


def _fold_rows(m):
    out = m[0:CHUNK]
    for c in range(1, CHUNKS_PER_GROUP):
        out = out + m[c * CHUNK:(c + 1) * CHUNK]
    return out


def _unfold_rows(r, same_chunk):
    return jnp.where(same_chunk, jnp.concatenate([r] * CHUNKS_PER_GROUP, axis=0), 0.0)


def _unit_lower_inverses(a_folded, eye_folded, same_chunk, fill):
    xs = list(a_folded)
    ps = [eye_folded - x for x in xs]
    n = 1
    while n < CHUNK:
        for h in range(len(xs)):
            x_bd = _unfold_rows(xs[h], same_chunk).astype(_BF16)
            if n == 1:
                xs[h] = _dot(xs[h], x_bd)
            elif 2 * n < CHUNK:
                r = _dot(jnp.concatenate([xs[h], ps[h]], axis=0), x_bd)
                xs[h] = r[0:CHUNK]
                ps[h] = ps[h] + r[CHUNK:2 * CHUNK]
            else:
                ps[h] = ps[h] + _dot(ps[h], x_bd)
            fill.emit(h % 2)
        n *= 2
    return [_unfold_rows(p, same_chunk) for p in ps]


def _recurrences_one_group(g, fill, act_ref, pr_ref, small_ref, dec_ref, sg_ref, sr_ref, o_ref):
    rows = slice(g * GROUP, (g + 1) * GROUP)
    ri = lax.broadcasted_iota(jnp.int32, (GROUP, GROUP), 0)
    ci = lax.broadcasted_iota(jnp.int32, (GROUP, GROUP), 1)
    same_chunk = (ri // CHUNK) == (ci // CHUNK)
    tril = same_chunk & (ri >= ci)
    off_diag = ri != ci
    rf = lax.broadcasted_iota(jnp.int32, (CHUNK, GROUP), 0)
    cf = lax.broadcasted_iota(jnp.int32, (CHUNK, GROUP), 1)
    eye_folded = jnp.where(rf == cf % CHUNK, 1.0, 0.0).astype(_F32)
    pos = (lax.broadcasted_iota(jnp.int32, (GROUP, HEAD_DIM), 0) % CHUNK).astype(_F32)
    chunk_rows = [slice(c * CHUNK, (c + 1) * CHUNK) for c in range(CHUNKS_PER_GROUP)]

    qs, ks, gccs, a_folded, a_intra, uw_rhs, e_ins = [], [], [], [], [], [], []
    for h in range(A_HEADS):
        q = act_ref[rows, _head_cols(0, h)]
        k = act_ref[rows, _head_cols(A_WIDTH, h)]
        v = act_ref[rows, _head_cols(2 * A_WIDTH, h)]
        beta = jnp.broadcast_to(small_ref[0, rows, h:h + 1], (GROUP, HEAD_DIM))
        gc_col = small_ref[1, rows, A_HEADS + h:A_HEADS + h + 1]
        gcc = jnp.broadcast_to(gc_col, (GROUP, HEAD_DIM))
        decay = dec_ref[h]
        kb = k * beta
        e_in = jnp.exp(gcc)
        kq = _dot_nt(jnp.concatenate([kb, q], axis=0), k)
        a_folded.append(_fold_rows(jnp.where(off_diag, kq[0:GROUP] * decay, 0.0)))
        a_intra.append(kq[GROUP:2 * GROUP] * decay)
        uw_rhs.append(jnp.concatenate([v * beta, kb * e_in], axis=1).astype(_BF16))
        qs.append(q)
        ks.append(k)
        gccs.append(gcc)
        e_ins.append(e_in)
        fill.emit(1)

    r_qe, r_oloc, r_b, r_echunk = [], [], [], []
    for h in range(B_HEADS):
        lg = _log_gamma(h)
        q = pr_ref[rows, _head_cols(BQ_OFF, h)]
        k = pr_ref[rows, _head_cols(BK_OFF, h)]
        v = pr_ref[rows, _head_cols(BV_OFF, h)]
        decay = jnp.exp(jnp.where(tril, (ri - ci).astype(_F32) * lg, MASKED_LOG))
        vb16 = v.astype(_BF16)
        r_oloc.append(_dot(_dot_nt(q, k) * decay, vb16))
        r_qe.append((q * jnp.exp((pos + 1.0) * lg)).astype(_BF16))
        k_out = (k * jnp.exp((CHUNK - 1.0 - pos) * lg)).astype(_BF16)
        r_b.append([_dot_tn(k_out[cr], vb16[cr]) for cr in chunk_rows])
        r_echunk.append(math.exp(CHUNK * lg))

    t_inv = _unit_lower_inverses(a_folded, eye_folded, same_chunk, fill)

    g_lhs, g_oloc, g_b, g_elast = [], [], [], []
    for h in range(A_HEADS):
        uw = _dot(t_inv[h], uw_rhs[h])
        uw16 = uw.astype(_BF16)
        aiuw = _dot(a_intra[h], uw16)
        g_oloc.append(aiuw[:, 0:HEAD_DIM])
        q_eff = qs[h] * e_ins[h] - aiuw[:, HEAD_DIM:2 * HEAD_DIM]
        lhs, bs, elast = [], [], []
        for cr in chunk_rows:
            gcc_c = gccs[h][cr]
            g_last = gcc_c[CHUNK - 1:CHUNK, :]
            k_out = ks[h][cr] * jnp.exp(g_last - gcc_c)
            bg = _dot_tn(k_out, uw16[cr])
            bs.append(bg[:, 0:HEAD_DIM])
            lhs.append(jnp.concatenate([q_eff[cr], bg[:, HEAD_DIM:2 * HEAD_DIM]], axis=0).astype(_BF16))
            elast.append(jnp.exp(g_last))
        g_lhs.append(lhs)
        g_b.append(bs)
        g_elast.append(elast)
        fill.emit(1)

    g_state = [sg_ref[h] for h in range(A_HEADS)]
    r_state = [sr_ref[h] for h in range(B_HEADS)]
    for c, cr in enumerate(chunk_rows):
        out_rows = slice(g * GROUP + c * CHUNK, g * GROUP + (c + 1) * CHUNK)
        for h in range(A_HEADS):
            r = _dot(g_lhs[h][c], g_state[h])
            o_ref[out_rows, _head_cols(0, h)] = r[0:CHUNK] + g_oloc[h][cr]
            g_state[h] = g_state[h] * g_elast[h][c] + g_b[h][c] - r[CHUNK:CHUNK + HEAD_DIM]
        for h in range(B_HEADS):
            o_ref[out_rows, _head_cols(A_WIDTH, h)] = _dot(r_qe[h][cr], r_state[h]) + r_oloc[h][cr]
            r_state[h] = r_state[h] * r_echunk[h] + r_b[h][c]
        fill.emit(1)
    for h in range(A_HEADS):
        sg_ref[h] = g_state[h]
    for h in range(B_HEADS):
        sr_ref[h] = r_state[h]


def _ffn_rows(h, wgu_ref, wd_ref, npre, npost, act_ref, d_ff):
    hn = _rmsnorm(h, npre).astype(_BF16)
    for j in range(d_ff // FFN_COL_BLOCK):
        cg = slice(j * FFN_COL_BLOCK, (j + 1) * FFN_COL_BLOCK)
        cu = slice(d_ff + j * FFN_COL_BLOCK, d_ff + (j + 1) * FFN_COL_BLOCK)
        gate = jnp.dot(hn, wgu_ref[:, cg], preferred_element_type=_F32)
        up = jnp.dot(hn, wgu_ref[:, cu], preferred_element_type=_F32)
        act_ref[:, cg] = (_silu(gate) * up).astype(_BF16)
    f = jnp.dot(act_ref[...], wd_ref[...], preferred_element_type=_F32)
    return h + _rmsnorm(f, npost)


def _ffn_pieces(h_ref, hn_ref, wgu_ref, wd_ref, npost, gate_ref, act_ref, f_ref, y_ref, d_ff):
    d_model = f_ref.shape[1]
    k_split = (d_ff // FFN_COL_BLOCK + 1) // 2 * FFN_COL_BLOCK
    pieces = []

    def gate(j):
        def run():
            cg = slice(j * FFN_COL_BLOCK, (j + 1) * FFN_COL_BLOCK)
            gate_ref[j % 2] = _silu(jnp.dot(hn_ref[...], wgu_ref[:, cg], preferred_element_type=_F32))
        return run

    def up(j):
        def run():
            cg = slice(j * FFN_COL_BLOCK, (j + 1) * FFN_COL_BLOCK)
            cu = slice(d_ff + j * FFN_COL_BLOCK, d_ff + (j + 1) * FFN_COL_BLOCK)
            act_ref[:, cg] = (gate_ref[j % 2] * jnp.dot(hn_ref[...], wgu_ref[:, cu],
                                                        preferred_element_type=_F32)).astype(_BF16)
        return run

    def down(j, first):
        def run():
            cs = slice(j * FFN_COL_BLOCK, (j + 1) * FFN_COL_BLOCK)
            if first:
                f_ref[:, cs] = jnp.dot(act_ref[:, 0:k_split], wd_ref[0:k_split, cs],
                                       preferred_element_type=_F32)
            else:
                f_ref[:, cs] = f_ref[:, cs] + jnp.dot(act_ref[:, k_split:d_ff], wd_ref[k_split:d_ff, cs],
                                                      preferred_element_type=_F32)
        return run

    def finish():
        y_ref[...] = h_ref[...] + _rmsnorm(f_ref[...], npost)

    for j in range(d_ff // FFN_COL_BLOCK):
        pieces += [gate(j), up(j)]
    for j in range(d_model // FFN_COL_BLOCK):
        pieces += [down(j, True), down(j, False)]
    pieces.append(finish)
    return pieces


def _column_of_row(row):
    return jnp.broadcast_to(row, (HEAD_DIM, HEAD_DIM)).T


def _sample_state_units(first_seq, per_step, sq_ref, sk_ref, sv_ref, sc_ref, sgi_ref, sri_ref,
                        so_ref, sgo_ref, sro_ref):
    base = pl.multiple_of((first_seq // SUBLANES) * SUBLANES, SUBLANES)
    rows = pl.ds(base, SUBLANES)
    row_id = lax.broadcasted_iota(jnp.int32, (SUBLANES, HEAD_DIM), 0)
    picks = [row_id == (first_seq - base + j) for j in range(per_step)]

    def pick(block, j):
        return jnp.sum(jnp.where(picks[j], block, 0.0), axis=0, keepdims=True)

    def put(cols, outs):
        block = so_ref[rows, cols]
        for j in range(per_step):
            block = jnp.where(picks[j], outs[j], block)
        so_ref[rows, cols] = block

    def deltanet(h):
        def run():
            cols = _head_cols(0, h)
            q8, k8, v8 = sq_ref[rows, cols], sk_ref[rows, cols], sv_ref[rows, cols]
            eg8 = sc_ref[rows, _head_cols(0, h)]
            bt8 = sc_ref[rows, _head_cols(A_WIDTH, h)]
            qk8 = sc_ref[rows, _head_cols(2 * A_WIDTH, h)]
            kq16 = jnp.concatenate([k8, q8], axis=0).astype(_BF16)
            kqs = [jnp.dot(kq16, sgi_ref[j, h].astype(_BF16), preferred_element_type=_F32)
                   for j in range(per_step)]
            outs = []
            for j in range(per_step):
                eg = pick(eg8, j)
                v_new = pick(bt8, j) * (pick(v8, j) - eg * pick(kqs[j][0:SUBLANES], j))
                outs.append(eg * pick(kqs[j][SUBLANES:2 * SUBLANES], j) + pick(qk8, j) * v_new)
                sgo_ref[j, h] = sgi_ref[j, h] * eg + _column_of_row(pick(k8, j)) * v_new
            put(cols, outs)
        return run

    def retention(h):
        def run():
            gamma = math.exp(_log_gamma(h))
            cols = _head_cols(A_WIDTH, h)
            q8, k8, v8 = sq_ref[rows, cols], sk_ref[rows, cols], sv_ref[rows, cols]
            qk8 = sc_ref[rows, _head_cols(3 * A_WIDTH, h)]
            q16 = q8.astype(_BF16)
            qss = [jnp.dot(q16, sri_ref[j, h].astype(_BF16), preferred_element_type=_F32)
                   for j in range(per_step)]
            outs = []
            for j in range(per_step):
                v1 = pick(v8, j)
                outs.append(gamma * pick(qss[j], j) + pick(qk8, j) * v1)
                sro_ref[j, h] = sri_ref[j, h] * gamma + _column_of_row(pick(k8, j)) * v1
            put(cols, outs)
        return run

    return [deltanet(h) for h in range(A_HEADS)] + [retention(h) for h in range(B_HEADS)]


_LayerInputs = collections.namedtuple("_LayerInputs", [
    "x", "cos", "sin", "win", "convw", "alog", "dtb", "gnw", "rnw", "wout", "npre", "npost", "wgu", "wd",
    "fpre", "fpost", "sq", "sk", "sv", "sc", "sgi", "sri"])
_LayerOutputsAndScratch = collections.namedtuple("_LayerOutputsAndScratch", [
    "y", "conv", "sg", "sr", "so", "sgo", "sro",
    "pq", "pr", "act", "small", "gct", "dec", "o", "h", "hn", "gate", "ffn_act", "f"])


def _layer_prompt_refs(refs, aliased):
    n_in = len(_LayerInputs._fields)
    n_skip = len(STACKED_STATE_OUTPUTS) if aliased else 0
    return _LayerInputs(*refs[:n_in]), _LayerOutputsAndScratch(*refs[n_in + n_skip:])


def _layer_prompt_body(*refs, aliased, tb, nt, nblocks, d_ff, per_step):
    s = pl.program_id(0)

    @pl.when(s < nblocks)
    def _():
        _layer_prompt_step(refs, aliased, tb, nt, d_ff, per_step)

    @pl.when(s == nblocks)
    def _():
        ins, rest = _layer_prompt_refs(refs, aliased)
        prev = lax.rem(s + 1, 2)
        _Filler(_ffn_pieces(rest.h.at[prev], rest.hn.at[prev], ins.wgu, ins.wd, ins.fpost[...],
                            rest.gate, rest.ffn_act, rest.f, rest.y, d_ff)).flush()


def _layer_prompt_step(refs, aliased, tb, nt, d_ff, per_step):
    ins, rest = _layer_prompt_refs(refs, aliased)
    (x_ref, cos_ref, sin_ref, win_ref, convw_ref, alog_ref, dtb_ref, gnw_ref, rnw_ref, wout_ref,
     npre_ref, npost_ref, wgu_ref, wd_ref, fpre_ref, fpost_ref,
     sq_ref, sk_ref, sv_ref, sc_ref, sgi_ref, sri_ref) = ins
    (y_ref, conv_ref, sg_ref, sr_ref, so_ref, sgo_ref, sro_ref,
     pq_ref, pr_ref, act_ref, small_ref, gct_ref, dec_ref, o_ref, h_ref, hn_ref, gate_ref,
     ffn_act_ref, f_ref) = rest
    s = pl.program_id(0)
    t = lax.rem(s, nt)
    pad = SUBLANES

    @pl.when(s == 0)
    def _():
        h_ref[1] = jnp.zeros(h_ref.shape[1:], _F32)
        hn_ref[1] = jnp.zeros(hn_ref.shape[1:], _BF16)
        so_ref[...] = jnp.zeros(so_ref.shape, _F32)

    @pl.when(t == 0)
    def _():
        pq_ref[0:pad, :] = jnp.zeros((pad, QKV_WIDTH), _F32)
        sg_ref[...] = jnp.zeros(sg_ref.shape, _F32)
        sr_ref[...] = jnp.zeros(sr_ref.shape, _F32)

    prev = lax.rem(s + 1, 2)
    fill = _Filler(_ffn_pieces(h_ref.at[prev], hn_ref.at[prev], wgu_ref, wd_ref, fpost_ref[...],
                               gate_ref, ffn_act_ref, f_ref, y_ref, d_ff))
    fill.emit(2)
    sample = _Filler(_sample_state_units(s * per_step, per_step, sq_ref, sk_ref, sv_ref, sc_ref, sgi_ref,
                                         sri_ref, so_ref, sgo_ref, sro_ref))

    x = x_ref[...]
    hn = _rmsnorm(x, npre_ref[...]).astype(_BF16)

    def project(lo, hi):
        return jnp.dot(hn, win_ref[:, lo:hi], preferred_element_type=_F32)

    def project_rest(lo, hi):
        pr_ref[:, lo:hi] = project(QKV_WIDTH + lo, QKV_WIDTH + hi)

    ps = project(QKV_WIDTH + SMALL_OFF, IN_WIDTH_PADDED)
    fill.emit(1)
    beta_all = jax.nn.sigmoid(ps)
    gc = -jnp.exp(alog_ref[...]) * _softplus(ps + dtb_ref[...])
    row_in_chunk = lax.broadcasted_iota(jnp.int32, (tb, LANES), 0) % CHUNK
    shift = 1
    while shift < CHUNK:
        gc = gc + jnp.where(row_in_chunk >= shift, pltpu.roll(gc, shift, axis=0), 0.0)
        shift *= 2
    small_ref[0] = beta_all
    small_ref[1] = gc
    gct_ref[...] = gc.T

    half = B_WIDTH // 2
    cos_full = cos_ref[...]
    sin_signed = sin_ref[...]
    ri = lax.broadcasted_iota(jnp.int32, (GROUP, GROUP), 0)
    ci = lax.broadcasted_iota(jnp.int32, (GROUP, GROUP), 1)
    tril = ((ri // CHUNK) == (ci // CHUNK)) & (ri >= ci)
    for pair in range(B_HEADS // 2):
        project_rest(BQ_OFF + pair * half, BQ_OFF + (pair + 1) * half)
        fill.emit(1)
        project_rest(BK_OFF + pair * half, BK_OFF + (pair + 1) * half)
        fill.emit(1)
        for h in (2 * pair, 2 * pair + 1):
            gc_col = small_ref[1, :, A_HEADS + h:A_HEADS + h + 1]
            gcr = jnp.broadcast_to(gct_ref[A_HEADS + h:A_HEADS + h + 1, :], (GROUP, GROUP))
            diff = jnp.broadcast_to(gc_col, (GROUP, GROUP)) - gcr
            dec_ref[h] = jnp.exp(jnp.where(tril, diff, MASKED_LOG))
            cq = _head_cols(BQ_OFF, h)
            ck = _head_cols(BK_OFF, h)
            pr_ref[:, cq] = _rope(pr_ref[:, cq], cos_full, sin_signed)
            pr_ref[:, ck] = _rope(pr_ref[:, ck], cos_full, sin_signed) * (HEAD_DIM ** -0.5)

    for lo in range(0, QKV_WIDTH, FFN_COL_BLOCK):
        pq_ref[pad:pad + tb, lo:lo + FFN_COL_BLOCK] = project(lo, lo + FFN_COL_BLOCK)
        fill.emit(1)
        sample.emit(1)

    rest = [(lo, lo + FFN_COL_BLOCK) for lo in range(Z_OFF, BQ_OFF, FFN_COL_BLOCK)]
    rest += [(lo, lo + FFN_COL_BLOCK) for lo in range(BV_OFF, SMALL_OFF, FFN_COL_BLOCK)]
    for j in range(QKV_WIDTH // LANES):
        if j % 2 == 0 and rest:
            project_rest(*rest.pop(0))
        else:
            fill.emit(1)
        cs = slice(j * LANES, (j + 1) * LANES)
        acc = pq_ref[pad:pad + tb, cs] * convw_ref[3:4, cs]
        for i in range(CONV_W - 1):
            acc = acc + pq_ref[pad - 3 + i:pad - 3 + i + tb, cs] * convw_ref[i:i + 1, cs]
        a = _silu(acc)
        if j < 2 * A_HEADS:
            a = a * lax.rsqrt(jnp.sum(a * a, axis=-1, keepdims=True) + L2_EPS)
        if j < A_HEADS:
            a = a * (HEAD_DIM ** -0.5)
        act_ref[:, cs] = a
    assert not rest

    tail = pq_ref[pad + tb - 3:pad + tb, :]
    conv_ref[...] = tail
    pq_ref[pad - 3:pad, :] = tail

    for g in range(tb // GROUP):
        _recurrences_one_group(g, fill, act_ref, pr_ref, small_ref, dec_ref, sg_ref, sr_ref, o_ref)

    for h in range(A_HEADS):
        cs = _head_cols(0, h)
        o_ref[:, cs] = _gated_head_norm(o_ref[:, cs], gnw_ref[...], pr_ref[:, _head_cols(Z_OFF, h)])
    for h in range(B_HEADS):
        cs = _head_cols(A_WIDTH, h)
        o_ref[:, cs] = _gated_head_norm(o_ref[:, cs], rnw_ref[...], pr_ref[:, _head_cols(BG_OFF, h)])
    m = jnp.dot(o_ref[...].astype(_BF16), wout_ref[...], preferred_element_type=_F32)
    fill.emit(8)
    sample.flush()
    h_new = x_ref[...] + _rmsnorm(m, npost_ref[...])
    cur = lax.rem(s, 2)
    h_ref[cur] = h_new
    hn_ref[cur] = _rmsnorm(h_new, fpre_ref[...]).astype(_BF16)
    fill.flush()


def _layer_spec(shape, layer):
    zeros = (0,) * len(shape)
    return pl.BlockSpec((None,) + tuple(shape), lambda i: (layer,) + zeros)


def _layer_prompt(x, cos_full, sin_signed, win, conv_w, alog, dtb, gnw, rnw, wout, npre, npost,
                  wgu, wd, fpre, fpost, sq, sk, sv, sc, state_gdn, state_ret, prev_states, layer):
    batch, seq, d_model = x.shape
    nb = sq.shape[0]
    depth = state_gdn.shape[0]
    d_ff = wd.shape[1]
    tb = min(PROMPT_TIME_BLOCK, seq)
    assert seq % tb == 0 and tb == GROUP and d_ff % FFN_COL_BLOCK == 0
    nt = seq // tb
    nblocks = batch * nt
    assert nb % nblocks == 0 and SUBLANES % (nb // nblocks) == 0
    per_step = nb // nblocks
    lspec = functools.partial(_layer_spec, layer=layer)

    def mixer_block(s):
        return jnp.minimum(s, nblocks - 1)

    def ffn_block(s):
        return jnp.maximum(s - 1, 0)

    full = lambda shape: pl.BlockSpec(tuple(shape), lambda s: (0,) * len(shape))
    state_spec = lambda heads: pl.BlockSpec((None, per_step, heads, HEAD_DIM, HEAD_DIM),
                                            lambda s: (layer, mixer_block(s), 0, 0, 0))

    in_specs = [
        pl.BlockSpec((None, tb, d_model), lambda s: (mixer_block(s) // nt, mixer_block(s) % nt, 0)),
        pl.BlockSpec((tb, HEAD_DIM), lambda s: (mixer_block(s) % nt, 0)),
        pl.BlockSpec((tb, HEAD_DIM), lambda s: (mixer_block(s) % nt, 0)),
        lspec((d_model, IN_WIDTH_PADDED)),
        lspec((CONV_W, QKV_WIDTH)),
        lspec((1, LANES)),
        lspec((1, LANES)),
        lspec((1, HEAD_DIM)),
        lspec((1, HEAD_DIM)),
        lspec((A_WIDTH + B_WIDTH, d_model)),
        lspec((1, d_model)),
        lspec((1, d_model)),
        lspec((d_model, 2 * d_ff)),
        lspec((d_ff, d_model)),
        lspec((1, d_model)),
        lspec((1, d_model)),
        full(sq.shape), full(sk.shape), full(sv.shape), full(sc.shape),
        state_spec(A_HEADS), state_spec(B_HEADS),
    ]
    args = [x, cos_full, sin_signed, win, conv_w, alog, dtb, gnw, rnw, wout, npre, npost, wgu, wd, fpre,
            fpost, sq, sk, sv, sc, state_gdn, state_ret]
    aliases = {}
    if prev_states is not None:
        in_specs += [pl.BlockSpec(memory_space=pl.ANY)] * len(STACKED_STATE_OUTPUTS)
        aliases = {len(args) + i: out for i, out in enumerate(STACKED_STATE_OUTPUTS)}
        args += list(prev_states)
    prompt_state_spec = lambda heads: pl.BlockSpec((None, None, heads, HEAD_DIM, HEAD_DIM),
                                                   lambda s: (layer, mixer_block(s) // nt, 0, 0, 0))
    out_specs = [
        pl.BlockSpec((None, tb, d_model), lambda s: (ffn_block(s) // nt, ffn_block(s) % nt, 0)),
        pl.BlockSpec((None, CONV_W - 1, QKV_WIDTH), lambda s: (mixer_block(s) // nt, 0, 0)),
        prompt_state_spec(A_HEADS), prompt_state_spec(B_HEADS),
        full(sq.shape), state_spec(A_HEADS), state_spec(B_HEADS),
    ]
    out_shape = [
        jax.ShapeDtypeStruct((batch, seq, d_model), _F32),
        jax.ShapeDtypeStruct((batch, CONV_W - 1, QKV_WIDTH), _F32),
        jax.ShapeDtypeStruct((depth, batch, A_HEADS, HEAD_DIM, HEAD_DIM), _F32),
        jax.ShapeDtypeStruct((depth, batch, B_HEADS, HEAD_DIM, HEAD_DIM), _F32),
        jax.ShapeDtypeStruct(sq.shape, _F32),
        jax.ShapeDtypeStruct((depth, nb, A_HEADS, HEAD_DIM, HEAD_DIM), _F32),
        jax.ShapeDtypeStruct((depth, nb, B_HEADS, HEAD_DIM, HEAD_DIM), _F32),
    ]
    scratch = [
        pltpu.VMEM((tb + SUBLANES, QKV_WIDTH), _F32),
        pltpu.VMEM((tb, REST_WIDTH), _F32),
        pltpu.VMEM((tb, QKV_WIDTH), _F32),
        pltpu.VMEM((2, tb, LANES), _F32),
        pltpu.VMEM((LANES, tb), _F32),
        pltpu.VMEM((A_HEADS, GROUP, GROUP), _F32),
        pltpu.VMEM((tb, A_WIDTH + B_WIDTH), _F32),
        pltpu.VMEM((2, tb, d_model), _F32),
        pltpu.VMEM((2, tb, d_model), _BF16),
        pltpu.VMEM((2, tb, FFN_COL_BLOCK), _F32),
        pltpu.VMEM((tb, d_ff), _BF16),
        pltpu.VMEM((tb, d_model), _F32),
    ]
    return pl.pallas_call(
        functools.partial(_layer_prompt_body, aliased=prev_states is not None, tb=tb, nt=nt, nblocks=nblocks,
                          d_ff=d_ff, per_step=per_step),
        grid=(nblocks + 1,), in_specs=in_specs, out_specs=out_specs, out_shape=out_shape,
        scratch_shapes=scratch, input_output_aliases=aliases,
        compiler_params=pltpu.CompilerParams(
            dimension_semantics=("arbitrary",), vmem_limit_bytes=VMEM_LIMIT_BYTES),
        name=f"layer_prompt_l{layer}",
    )(*args)


def _sample_pre_body(x_ref, cos_ref, sin_ref, win_ref, convw_ref, alog_ref, dtb_ref, npre_ref, convs_ref,
                     q_ref, k_ref, v_ref, sc_ref, gates_ref, convn_ref):
    nb = x_ref.shape[0]
    hn = _rmsnorm(x_ref[...], npre_ref[...]).astype(_BF16)
    pq = jnp.dot(hn, win_ref[:, 0:QKV_WIDTH], preferred_element_type=_F32)
    pr = jnp.dot(hn, win_ref[:, QKV_WIDTH:IN_WIDTH_PADDED], preferred_element_type=_F32)
    outs = (q_ref, k_ref, v_ref)
    for j in range(QKV_WIDTH // LANES):
        cs = slice(j * LANES, (j + 1) * LANES)
        new = pq[:, cs]
        acc = new * convw_ref[3:4, cs]
        for r in range(CONV_W - 1):
            hist = convs_ref[r, :, cs]
            acc = acc + hist * convw_ref[r:r + 1, cs]
            if r > 0:
                convn_ref[r - 1, :, cs] = hist
        convn_ref[CONV_W - 2, :, cs] = new
        a = _silu(acc)
        if j < 2 * A_HEADS:
            a = a * lax.rsqrt(jnp.sum(a * a, axis=-1, keepdims=True) + L2_EPS)
        if j < A_HEADS:
            a = a * (HEAD_DIM ** -0.5)
        outs[j // A_HEADS][:, _head_cols(0, j % A_HEADS)] = a
    ps = pr[:, SMALL_OFF:SMALL_OFF + LANES]
    beta_all = jax.nn.sigmoid(ps)
    eg_all = jnp.exp(-jnp.exp(alog_ref[...]) * _softplus(ps + dtb_ref[...]))
    cos_full = cos_ref[...]
    sin_signed = sin_ref[...]
    for h in range(A_HEADS):
        sc_ref[:, _head_cols(0, h)] = jnp.broadcast_to(eg_all[:, A_HEADS + h:A_HEADS + h + 1], (nb, LANES))
        sc_ref[:, _head_cols(A_WIDTH, h)] = jnp.broadcast_to(beta_all[:, h:h + 1], (nb, LANES))
        qk = jnp.sum(q_ref[:, _head_cols(0, h)] * k_ref[:, _head_cols(0, h)], axis=-1, keepdims=True)
        sc_ref[:, _head_cols(2 * A_WIDTH, h)] = jnp.broadcast_to(qk, (nb, LANES))
    for h in range(B_HEADS):
        q = _rope(pr[:, _head_cols(BQ_OFF, h)], cos_full, sin_signed)
        k = _rope(pr[:, _head_cols(BK_OFF, h)], cos_full, sin_signed) * (HEAD_DIM ** -0.5)
        q_ref[:, _head_cols(A_WIDTH, h)] = q
        k_ref[:, _head_cols(A_WIDTH, h)] = k
        sc_ref[:, _head_cols(3 * A_WIDTH, h)] = jnp.broadcast_to(
            jnp.sum(q * k, axis=-1, keepdims=True), (nb, LANES))
    v_ref[:, A_WIDTH:A_WIDTH + B_WIDTH] = pr[:, BV_OFF:BG_OFF]
    gates_ref[:, 0:A_WIDTH] = pr[:, Z_OFF:BQ_OFF]
    gates_ref[:, A_WIDTH:A_WIDTH + B_WIDTH] = pr[:, BG_OFF:SMALL_OFF]


def _sample_pre(x, cos_full, sin_signed, win, conv_w, alog, dtb, npre, conv_state, layer):
    nb, d_model = x.shape
    width = A_WIDTH + B_WIDTH
    lspec = functools.partial(_layer_spec, layer=layer)
    full = lambda shape: pl.BlockSpec(tuple(shape), lambda i: (0,) * len(shape))
    shapes = [(nb, width)] * 3 + [(nb, SAMPLE_SCALARS * A_WIDTH), (nb, width), (CONV_W - 1, nb, QKV_WIDTH)]
    return pl.pallas_call(
        _sample_pre_body,
        grid=(1,),
        in_specs=[full((nb, d_model)), full((1, HEAD_DIM)), full((1, HEAD_DIM)),
                  lspec((d_model, IN_WIDTH_PADDED)), lspec((CONV_W, QKV_WIDTH)), lspec((1, LANES)),
                  lspec((1, LANES)), lspec((1, d_model)), lspec((CONV_W - 1, nb, QKV_WIDTH))],
        out_specs=[full(shape) for shape in shapes],
        out_shape=[jax.ShapeDtypeStruct(shape, _F32) for shape in shapes],
        compiler_params=pltpu.CompilerParams(
            dimension_semantics=("arbitrary",), vmem_limit_bytes=VMEM_LIMIT_BYTES),
        name=f"sample_pre_l{layer}",
    )(x, cos_full, sin_signed, win, conv_w, alog, dtb, npre, conv_state)


def _sample_post_body(o_ref, gates_ref, x_ref, gnw_ref, rnw_ref, wout_ref, npost_ref, wgu_ref, wd_ref,
                      fpre_ref, fpost_ref, y_ref, cat_ref, act_ref, *, d_ff):
    for h in range(A_HEADS + B_HEADS):
        cs = _head_cols(0, h)
        w = gnw_ref[...] if h < A_HEADS else rnw_ref[...]
        cat_ref[:, cs] = _gated_head_norm(o_ref[:, cs], w, gates_ref[:, cs]).astype(_BF16)
    m = jnp.dot(cat_ref[...], wout_ref[...], preferred_element_type=_F32)
    h_new = x_ref[...] + _rmsnorm(m, npost_ref[...])
    y_ref[...] = _ffn_rows(h_new, wgu_ref, wd_ref, fpre_ref[...], fpost_ref[...], act_ref, d_ff)


def _sample_post(o, gates, x, gnw, rnw, wout, npost, wgu, wd, fpre, fpost, layer):
    nb, d_model = x.shape
    d_ff = wd.shape[1]
    width = A_WIDTH + B_WIDTH
    lspec = functools.partial(_layer_spec, layer=layer)
    full = lambda shape: pl.BlockSpec(tuple(shape), lambda i: (0,) * len(shape))
    return pl.pallas_call(
        functools.partial(_sample_post_body, d_ff=d_ff),
        grid=(1,),
        in_specs=[full((nb, width)), full((nb, width)), full((nb, d_model)), lspec((1, HEAD_DIM)),
                  lspec((1, HEAD_DIM)), lspec((width, d_model)), lspec((1, d_model)),
                  lspec((d_model, 2 * d_ff)), lspec((d_ff, d_model)), lspec((1, d_model)),
                  lspec((1, d_model))],
        out_specs=full((nb, d_model)),
        out_shape=jax.ShapeDtypeStruct((nb, d_model), _F32),
        scratch_shapes=[pltpu.VMEM((nb, width), _BF16), pltpu.VMEM((nb, d_ff), _BF16)],
        compiler_params=pltpu.CompilerParams(
            dimension_semantics=("arbitrary",), vmem_limit_bytes=VMEM_LIMIT_BYTES),
        name=f"sample_post_l{layer}",
    )(o, gates, x, gnw, rnw, wout, npost, wgu, wd, fpre, fpost)


def _rope_tables(positions):
    half = HEAD_DIM // 2
    inv = ROPE_BASE ** (-np.arange(half, dtype=np.float64) / half)
    ang = np.asarray(positions, dtype=np.float64)[:, None] * inv[None, :]
    cos, sin = np.cos(ang), np.sin(ang)
    return (jnp.asarray(np.concatenate([cos, cos], axis=-1), dtype=_F32),
            jnp.asarray(np.concatenate([-sin, sin], axis=-1), dtype=_F32))


def _rearranged_w_in_body(wt_ref, o_ref):
    small0 = QKV_WIDTH + A_WIDTH
    small1 = small0 + 2 * A_HEADS
    cb = WEIGHT_COL_BLOCK
    for lo in range(0, small0, cb):
        o_ref[:, lo:lo + cb] = wt_ref[lo:lo + cb, :].T.astype(_BF16)
    for lo in range(small0, QKV_WIDTH + SMALL_OFF, cb):
        src = lo + small1 - small0
        o_ref[:, lo:lo + cb] = wt_ref[src:src + cb, :].T.astype(_BF16)
    lane = lax.broadcasted_iota(jnp.int32, (o_ref.shape[0], LANES), 1)
    o_ref[:, QKV_WIDTH + SMALL_OFF:IN_WIDTH_PADDED] = jnp.where(
        lane < 2 * A_HEADS, wt_ref[small0:small0 + LANES, :].T, 0.0).astype(_BF16)


def _rearranged_w_in(w_in):
    depth, d_model, width = w_in.shape
    assert width == IN_WIDTH_PADDED - LANES + 2 * A_HEADS
    return pl.pallas_call(
        _rearranged_w_in_body,
        grid=(depth,),
        in_specs=[pl.BlockSpec((None, width, d_model), lambda l: (l, 0, 0), pipeline_mode=pl.Buffered(1))],
        out_specs=pl.BlockSpec((None, d_model, IN_WIDTH_PADDED), lambda l: (l, 0, 0)),
        out_shape=jax.ShapeDtypeStruct((depth, d_model, IN_WIDTH_PADDED), _BF16),
        compiler_params=pltpu.CompilerParams(
            dimension_semantics=("arbitrary",), vmem_limit_bytes=VMEM_LIMIT_BYTES),
        name="rearranged_w_in",
    )(jnp.swapaxes(w_in, 1, 2))


def kernel(x_prompt, x_sample, state_conv, state_gdn, state_ret, w_in, conv_w, a_log, dt_bias, gdn_norm_w, ret_norm_w, w_out, norm_mix_pre, norm_mix_post, norm_ffn_pre, norm_ffn_post, w_gate_up, w_down):
    depth = w_in.shape[0]
    batch, seq, d_model = x_prompt.shape
    nb, seq_s, _ = x_sample.shape
    assert seq_s == 1

    win = _rearranged_w_in(w_in)
    wout = w_out.astype(_BF16)
    wgu = w_gate_up.astype(_BF16)
    wd = w_down.astype(_BF16)
    alog = jnp.pad(a_log, ((0, 0), (A_HEADS, LANES - 2 * A_HEADS)))[:, None, :]
    dtb = jnp.pad(dt_bias, ((0, 0), (A_HEADS, LANES - 2 * A_HEADS)))[:, None, :]
    gnw = gdn_norm_w[:, None, :]
    rnw = ret_norm_w[:, None, :]
    npre = norm_mix_pre[:, None, :]
    npost = norm_mix_post[:, None, :]
    fpre = norm_ffn_pre[:, None, :]
    fpost = norm_ffn_post[:, None, :]
    cos_p, sin_p = _rope_tables(np.arange(seq))
    cos_s, sin_s = _rope_tables(PAST_LEN + np.arange(seq_s))
    conv_state = jnp.swapaxes(state_conv, 1, 2)

    hp = x_prompt
    hs = x_sample.reshape(nb, d_model)
    convs_p, convs_s = [], []
    states = None
    for l in range(depth):
        sq, sk, sv, sc, gates, conv_s = _sample_pre(hs, cos_s, sin_s, win, conv_w, alog, dtb, npre,
                                                    conv_state, l)
        hp, conv_p, gdn_p, ret_p, so, gdn_s, ret_s = _layer_prompt(
            hp, cos_p, sin_p, win, conv_w, alog, dtb, gnw, rnw, wout, npre, npost, wgu, wd, fpre, fpost,
            sq, sk, sv, sc, state_gdn, state_ret, states, l)
        states = (gdn_p, ret_p, gdn_s, ret_s)
        hs = _sample_post(so, gates, hs, gnw, rnw, wout, npost, wgu, wd, fpre, fpost, l)
        convs_p.append(conv_p)
        convs_s.append(conv_s)
    gdn_p, ret_p, gdn_s, ret_s = states
    return (hp, hs.reshape(nb, seq_s, d_model), jnp.stack(convs_p), gdn_p, ret_p,
            jnp.swapaxes(jnp.stack(convs_s), 1, 2), gdn_s, ret_s)
```

```python
import collections
import functools
import math

import jax
import jax.numpy as jnp
import numpy as np
from jax import lax
from jax.experimental import pallas as pl
from jax.experimental.pallas import tpu as pltpu

HEAD_DIM = 128
A_HEADS = 4
B_HEADS = 4
A_WIDTH = A_HEADS * HEAD_DIM
B_WIDTH = B_HEADS * HEAD_DIM
QKV_WIDTH = 3 * A_WIDTH
CONV_W = 4
CHUNK = 64
ROPE_BASE = 10000.0
EPS = 1e-6
L2_EPS = 1e-6
MASKED_LOG = -1e30
PAST_LEN = 16384
LANES = 128
SUBLANES = 8

REST_WIDTH = A_WIDTH + 4 * B_WIDTH + LANES
Z_OFF = 0
BQ_OFF = A_WIDTH
BK_OFF = BQ_OFF + B_WIDTH
BV_OFF = BK_OFF + B_WIDTH
BG_OFF = BV_OFF + B_WIDTH
SMALL_OFF = BG_OFF + B_WIDTH
IN_WIDTH_PADDED = QKV_WIDTH + REST_WIDTH

GROUP = 256
CHUNKS_PER_GROUP = GROUP // CHUNK
PROMPT_TIME_BLOCK = 256
FFN_COL_BLOCK = 256
WEIGHT_COL_BLOCK = 256
SAMPLE_SCALARS = 4
STACKED_STATE_OUTPUTS = (2, 3, 5, 6)
VMEM_LIMIT_BYTES = 56 * 1024 * 1024

_BF16 = jnp.bfloat16
_F32 = jnp.float32


def _dot(a, b):
    return jnp.dot(a.astype(_BF16), b.astype(_BF16), preferred_element_type=_F32)


def _dot_nt(a, b):
    return lax.dot_general(a.astype(_BF16), b.astype(_BF16), (((1,), (1,)), ((), ())),
                           preferred_element_type=_F32)


def _dot_tn(a, b):
    return lax.dot_general(a.astype(_BF16), b.astype(_BF16), (((0,), (0,)), ((), ())),
                           preferred_element_type=_F32)


def _rmsnorm(x, w):
    return x * lax.rsqrt(jnp.mean(x * x, axis=-1, keepdims=True) + EPS) * w


def _silu(x):
    return x * jax.nn.sigmoid(x)


def _softplus(x):
    return jnp.maximum(x, 0.0) + jnp.log1p(jnp.exp(-jnp.abs(x)))


def _log_gamma(h):
    return math.log1p(-(2.0 ** (-5.0 - h)))


def _rope(x, cos_full, sin_signed):
    return x * cos_full + pltpu.roll(x, HEAD_DIM // 2, axis=1) * sin_signed


def _head_cols(base, h):
    return slice(base + h * HEAD_DIM, base + (h + 1) * HEAD_DIM)


def _gated_head_norm(o, w, gate):
    return o * lax.rsqrt(jnp.mean(o * o, axis=-1, keepdims=True) + EPS) * w * _silu(gate)


class _Filler:
    def __init__(self, pieces):
        self._pieces = list(pieces)
        self._next = 0

    def emit(self, count):
        for _ in range(count):
            if self._next < len(self._pieces):
                self._pieces[self._next]()
                self._next += 1

    def flush(self):
        self.emit(len(self._pieces))


def _fold_rows(m):
    out = m[0:CHUNK]
    for c in range(1, CHUNKS_PER_GROUP):
        out = out + m[c * CHUNK:(c + 1) * CHUNK]
    return out


def _unfold_rows(r, same_chunk):
    return jnp.where(same_chunk, jnp.concatenate([r] * CHUNKS_PER_GROUP, axis=0), 0.0)


def _unit_lower_inverses(a_folded, eye_folded, same_chunk, fill):
    xs = list(a_folded)
    ps = [eye_folded - x for x in xs]
    n = 1
    while n < CHUNK:
        for h in range(len(xs)):
            x_bd = _unfold_rows(xs[h], same_chunk).astype(_BF16)
            if n == 1:
                xs[h] = _dot(xs[h], x_bd)
            elif 2 * n < CHUNK:
                r = _dot(jnp.concatenate([xs[h], ps[h]], axis=0), x_bd)
                xs[h] = r[0:CHUNK]
                ps[h] = ps[h] + r[CHUNK:2 * CHUNK]
            else:
                ps[h] = ps[h] + _dot(ps[h], x_bd)
            fill.emit(h % 2)
        n *= 2
    return [_unfold_rows(p, same_chunk) for p in ps]


def _recurrences_one_group(g, fill, act_ref, pr_ref, small_ref, dec_ref, sg_ref, sr_ref, o_ref, gnw, rnw):
    rows = slice(g * GROUP, (g + 1) * GROUP)
    ri = lax.broadcasted_iota(jnp.int32, (GROUP, GROUP), 0)
    ci = lax.broadcasted_iota(jnp.int32, (GROUP, GROUP), 1)
    same_chunk = (ri // CHUNK) == (ci // CHUNK)
    tril = same_chunk & (ri >= ci)
    off_diag = ri != ci
    rf = lax.broadcasted_iota(jnp.int32, (CHUNK, GROUP), 0)
    cf = lax.broadcasted_iota(jnp.int32, (CHUNK, GROUP), 1)
    eye_folded = jnp.where(rf == cf % CHUNK, 1.0, 0.0).astype(_F32)
    pos = (lax.broadcasted_iota(jnp.int32, (GROUP, HEAD_DIM), 0) % CHUNK).astype(_F32)
    chunk_rows = [slice(c * CHUNK, (c + 1) * CHUNK) for c in range(CHUNKS_PER_GROUP)]

    qs, ks, gccs, a_folded, a_intra, uw_rhs, e_ins = [], [], [], [], [], [], []
    for h in range(A_HEADS):
        q = act_ref[rows, _head_cols(0, h)]
        k = act_ref[rows, _head_cols(A_WIDTH, h)]
        v = act_ref[rows, _head_cols(2 * A_WIDTH, h)]
        beta = jnp.broadcast_to(small_ref[0, rows, h:h + 1], (GROUP, HEAD_DIM))
        gc_col = small_ref[1, rows, A_HEADS + h:A_HEADS + h + 1]
        gcc = jnp.broadcast_to(gc_col, (GROUP, HEAD_DIM))
        decay = dec_ref[h]
        kb = k * beta
        e_in = jnp.exp(gcc)
        kq = _dot_nt(jnp.concatenate([kb, q], axis=0), k)
        a_folded.append(_fold_rows(jnp.where(off_diag, kq[0:GROUP] * decay, 0.0)))
        a_intra.append(kq[GROUP:2 * GROUP] * decay)
        uw_rhs.append(jnp.concatenate([v * beta, kb * e_in], axis=1).astype(_BF16))
        qs.append(q)
        ks.append(k)
        gccs.append(gcc)
        e_ins.append(e_in)
        fill.emit(1)

    r_qe, r_oloc, r_b, r_echunk = [], [], [], []
    for h in range(B_HEADS):
        lg = _log_gamma(h)
        q = pr_ref[rows, _head_cols(BQ_OFF, h)]
        k = pr_ref[rows, _head_cols(BK_OFF, h)]
        v = pr_ref[rows, _head_cols(BV_OFF, h)]
        decay = jnp.exp(jnp.where(tril, (ri - ci).astype(_F32) * lg, MASKED_LOG))
        vb16 = v.astype(_BF16)
        r_oloc.append(_dot(_dot_nt(q, k) * decay, vb16))
        r_qe.append((q * jnp.exp((pos + 1.0) * lg)).astype(_BF16))
        k_out = (k * jnp.exp((CHUNK - 1.0 - pos) * lg)).astype(_BF16)
        r_b.append([_dot_tn(k_out[cr], vb16[cr]) for cr in chunk_rows])
        r_echunk.append(math.exp(CHUNK * lg))

    t_inv = _unit_lower_inverses(a_folded, eye_folded, same_chunk, fill)

    g_lhs, g_oloc, g_b, g_elast = [], [], [], []
    for h in range(A_HEADS):
        uw = _dot(t_inv[h], uw_rhs[h])
        uw16 = uw.astype(_BF16)
        aiuw = _dot(a_intra[h], uw16)
        g_oloc.append(aiuw[:, 0:HEAD_DIM])
        q_eff = qs[h] * e_ins[h] - aiuw[:, HEAD_DIM:2 * HEAD_DIM]
        lhs, bs, elast = [], [], []
        for cr in chunk_rows:
            gcc_c = gccs[h][cr]
            g_last = gcc_c[CHUNK - 1:CHUNK, :]
            k_out = ks[h][cr] * jnp.exp(g_last - gcc_c)
            bg = _dot_tn(k_out, uw16[cr])
            bs.append(bg[:, 0:HEAD_DIM])
            lhs.append(jnp.concatenate([q_eff[cr], bg[:, HEAD_DIM:2 * HEAD_DIM]], axis=0).astype(_BF16))
            elast.append(jnp.exp(g_last))
        g_lhs.append(lhs)
        g_b.append(bs)
        g_elast.append(elast)
        fill.emit(1)

    def normalize_chunk(c):
        out_rows = slice(g * GROUP + c * CHUNK, g * GROUP + (c + 1) * CHUNK)
        for h in range(A_HEADS):
            cs = _head_cols(0, h)
            o_ref[out_rows, cs] = _gated_head_norm(o_ref[out_rows, cs], gnw,
                                                   pr_ref[out_rows, _head_cols(Z_OFF, h)])
        for h in range(B_HEADS):
            cs = _head_cols(A_WIDTH, h)
            o_ref[out_rows, cs] = _gated_head_norm(o_ref[out_rows, cs], rnw,
                                                   pr_ref[out_rows, _head_cols(BG_OFF, h)])

    g_state = [sg_ref[h] for h in range(A_HEADS)]
    r_state = [sr_ref[h] for h in range(B_HEADS)]
    for c, cr in enumerate(chunk_rows):
        out_rows = slice(g * GROUP + c * CHUNK, g * GROUP + (c + 1) * CHUNK)
        for h in range(A_HEADS):
            r = _dot(g_lhs[h][c], g_state[h])
            o_ref[out_rows, _head_cols(0, h)] = r[0:CHUNK] + g_oloc[h][cr]
            g_state[h] = g_state[h] * g_elast[h][c] + g_b[h][c] - r[CHUNK:CHUNK + HEAD_DIM]
        for h in range(B_HEADS):
            o_ref[out_rows, _head_cols(A_WIDTH, h)] = _dot(r_qe[h][cr], r_state[h]) + r_oloc[h][cr]
            r_state[h] = r_state[h] * r_echunk[h] + r_b[h][c]
        if c > 0:
            normalize_chunk(c - 1)
        fill.emit(1)
    normalize_chunk(CHUNKS_PER_GROUP - 1)
    for h in range(A_HEADS):
        sg_ref[h] = g_state[h]
    for h in range(B_HEADS):
        sr_ref[h] = r_state[h]


def _ffn_rows(h, wgu_ref, wd_ref, npre, npost, act_ref, d_ff):
    hn = _rmsnorm(h, npre).astype(_BF16)
    for j in range(d_ff // FFN_COL_BLOCK):
        cg = slice(j * FFN_COL_BLOCK, (j + 1) * FFN_COL_BLOCK)
        cu = slice(d_ff + j * FFN_COL_BLOCK, d_ff + (j + 1) * FFN_COL_BLOCK)
        gate = jnp.dot(hn, wgu_ref[:, cg], preferred_element_type=_F32)
        up = jnp.dot(hn, wgu_ref[:, cu], preferred_element_type=_F32)
        act_ref[:, cg] = (_silu(gate) * up).astype(_BF16)
    f = jnp.dot(act_ref[...], wd_ref[...], preferred_element_type=_F32)
    return h + _rmsnorm(f, npost)


def _ffn_pieces(h_ref, hn_ref, wgu_ref, wd_ref, npost, gate_ref, act_ref, f_ref, y_ref, d_ff):
    d_model = f_ref.shape[1]
    k_split = (d_ff // FFN_COL_BLOCK + 1) // 2 * FFN_COL_BLOCK
    pieces = []

    def gate(j):
        def run():
            cg = slice(j * FFN_COL_BLOCK, (j + 1) * FFN_COL_BLOCK)
            gate_ref[j % 2] = _silu(jnp.dot(hn_ref[...], wgu_ref[:, cg], preferred_element_type=_F32))
        return run

    def up(j):
        def run():
            cg = slice(j * FFN_COL_BLOCK, (j + 1) * FFN_COL_BLOCK)
            cu = slice(d_ff + j * FFN_COL_BLOCK, d_ff + (j + 1) * FFN_COL_BLOCK)
            act_ref[:, cg] = (gate_ref[j % 2] * jnp.dot(hn_ref[...], wgu_ref[:, cu],
                                                        preferred_element_type=_F32)).astype(_BF16)
        return run

    def down(j, first):
        def run():
            cs = slice(j * FFN_COL_BLOCK, (j + 1) * FFN_COL_BLOCK)
            if first:
                f_ref[:, cs] = jnp.dot(act_ref[:, 0:k_split], wd_ref[0:k_split, cs],
                                       preferred_element_type=_F32)
            else:
                f_ref[:, cs] = f_ref[:, cs] + jnp.dot(act_ref[:, k_split:d_ff], wd_ref[k_split:d_ff, cs],
                                                      preferred_element_type=_F32)
        return run

    def finish():
        y_ref[...] = h_ref[...] + _rmsnorm(f_ref[...], npost)

    for j in range(d_ff // FFN_COL_BLOCK):
        pieces += [gate(j), up(j)]
    for j in range(d_model // FFN_COL_BLOCK):
        pieces += [down(j, True), down(j, False)]
    pieces.append(finish)
    return pieces


def _column_of_row(row):
    return jnp.broadcast_to(row, (HEAD_DIM, HEAD_DIM)).T


def _sample_state_units(first_seq, per_step, sq_ref, sk_ref, sv_ref, sc_ref, sgi_ref, sri_ref,
                        so_ref, sgo_ref, sro_ref):
    base = pl.multiple_of((first_seq // SUBLANES) * SUBLANES, SUBLANES)
    rows = pl.ds(base, SUBLANES)
    row_id = lax.broadcasted_iota(jnp.int32, (SUBLANES, HEAD_DIM), 0)
    picks = [row_id == (first_seq - base + j) for j in range(per_step)]

    def pick(block, j):
        return jnp.sum(jnp.where(picks[j], block, 0.0), axis=0, keepdims=True)

    def put(cols, outs):
        block = so_ref[rows, cols]
        for j in range(per_step):
            block = jnp.where(picks[j], outs[j], block)
        so_ref[rows, cols] = block

    def deltanet(h):
        def run():
            cols = _head_cols(0, h)
            q8, k8, v8 = sq_ref[rows, cols], sk_ref[rows, cols], sv_ref[rows, cols]
            eg8 = sc_ref[rows, _head_cols(0, h)]
            bt8 = sc_ref[rows, _head_cols(A_WIDTH, h)]
            qk8 = sc_ref[rows, _head_cols(2 * A_WIDTH, h)]
            kq16 = jnp.concatenate([k8, q8], axis=0).astype(_BF16)
            kqs = [jnp.dot(kq16, sgi_ref[j, h].astype(_BF16), preferred_element_type=_F32)
                   for j in range(per_step)]
            outs = []
            for j in range(per_step):
                eg = pick(eg8, j)
                v_new = pick(bt8, j) * (pick(v8, j) - eg * pick(kqs[j][0:SUBLANES], j))
                outs.append(eg * pick(kqs[j][SUBLANES:2 * SUBLANES], j) + pick(qk8, j) * v_new)
                sgo_ref[j, h] = sgi_ref[j, h] * eg + _column_of_row(pick(k8, j)) * v_new
            put(cols, outs)
        return run

    def retention(h):
        def run():
            gamma = math.exp(_log_gamma(h))
            cols = _head_cols(A_WIDTH, h)
            q8, k8, v8 = sq_ref[rows, cols], sk_ref[rows, cols], sv_ref[rows, cols]
            qk8 = sc_ref[rows, _head_cols(3 * A_WIDTH, h)]
            q16 = q8.astype(_BF16)
            qss = [jnp.dot(q16, sri_ref[j, h].astype(_BF16), preferred_element_type=_F32)
                   for j in range(per_step)]
            outs = []
            for j in range(per_step):
                v1 = pick(v8, j)
                outs.append(gamma * pick(qss[j], j) + pick(qk8, j) * v1)
                sro_ref[j, h] = sri_ref[j, h] * gamma + _column_of_row(pick(k8, j)) * v1
            put(cols, outs)
        return run

    return [deltanet(h) for h in range(A_HEADS)] + [retention(h) for h in range(B_HEADS)]


_LayerInputs = collections.namedtuple("_LayerInputs", [
    "x", "cos", "sin", "win", "convw", "alog", "dtb", "gnw", "rnw", "wout", "npre", "npost", "wgu", "wd",
    "fpre", "fpost", "sq", "sk", "sv", "sc", "sgi", "sri"])
_LayerOutputsAndScratch = collections.namedtuple("_LayerOutputsAndScratch", [
    "y", "conv", "sg", "sr", "so", "sgo", "sro",
    "pq", "pr", "act", "small", "gct", "dec", "o", "h", "hn", "gate", "ffn_act", "f"])


def _layer_prompt_refs(refs, aliased):
    n_in = len(_LayerInputs._fields)
    n_skip = len(STACKED_STATE_OUTPUTS) if aliased else 0
    return _LayerInputs(*refs[:n_in]), _LayerOutputsAndScratch(*refs[n_in + n_skip:])


def _layer_prompt_body(*refs, aliased, tb, nt, nblocks, d_ff, per_step):
    s = pl.program_id(0)

    @pl.when(s < nblocks)
    def _():
        _layer_prompt_step(refs, aliased, tb, nt, d_ff, per_step)

    @pl.when(s == nblocks)
    def _():
        ins, rest = _layer_prompt_refs(refs, aliased)
        prev = lax.rem(s + 1, 2)
        _Filler(_ffn_pieces(rest.h.at[prev], rest.hn.at[prev], ins.wgu, ins.wd, ins.fpost[...],
                            rest.gate, rest.ffn_act, rest.f, rest.y, d_ff)).flush()


def _layer_prompt_step(refs, aliased, tb, nt, d_ff, per_step):
    ins, rest = _layer_prompt_refs(refs, aliased)
    (x_ref, cos_ref, sin_ref, win_ref, convw_ref, alog_ref, dtb_ref, gnw_ref, rnw_ref, wout_ref,
     npre_ref, npost_ref, wgu_ref, wd_ref, fpre_ref, fpost_ref,
     sq_ref, sk_ref, sv_ref, sc_ref, sgi_ref, sri_ref) = ins
    (y_ref, conv_ref, sg_ref, sr_ref, so_ref, sgo_ref, sro_ref,
     pq_ref, pr_ref, act_ref, small_ref, gct_ref, dec_ref, o_ref, h_ref, hn_ref, gate_ref,
     ffn_act_ref, f_ref) = rest
    s = pl.program_id(0)
    t = lax.rem(s, nt)
    pad = SUBLANES

    @pl.when(s == 0)
    def _():
        h_ref[1] = jnp.zeros(h_ref.shape[1:], _F32)
        hn_ref[1] = jnp.zeros(hn_ref.shape[1:], _BF16)
        so_ref[...] = jnp.zeros(so_ref.shape, _F32)

    @pl.when(t == 0)
    def _():
        pq_ref[0:pad, :] = jnp.zeros((pad, QKV_WIDTH), _F32)
        sg_ref[...] = jnp.zeros(sg_ref.shape, _F32)
        sr_ref[...] = jnp.zeros(sr_ref.shape, _F32)

    prev = lax.rem(s + 1, 2)
    fill = _Filler(_ffn_pieces(h_ref.at[prev], hn_ref.at[prev], wgu_ref, wd_ref, fpost_ref[...],
                               gate_ref, ffn_act_ref, f_ref, y_ref, d_ff))
    fill.emit(2)
    sample = _Filler(_sample_state_units(s * per_step, per_step, sq_ref, sk_ref, sv_ref, sc_ref, sgi_ref,
                                         sri_ref, so_ref, sgo_ref, sro_ref))

    x = x_ref[...]
    hn = _rmsnorm(x, npre_ref[...]).astype(_BF16)

    def project(lo, hi):
        return jnp.dot(hn, win_ref[:, lo:hi], preferred_element_type=_F32)

    def project_rest(lo, hi):
        pr_ref[:, lo:hi] = project(QKV_WIDTH + lo, QKV_WIDTH + hi)

    ps = project(QKV_WIDTH + SMALL_OFF, IN_WIDTH_PADDED)
    fill.emit(1)
    beta_all = jax.nn.sigmoid(ps)
    gc = -jnp.exp(alog_ref[...]) * _softplus(ps + dtb_ref[...])
    row_in_chunk = lax.broadcasted_iota(jnp.int32, (tb, LANES), 0) % CHUNK
    shift = 1
    while shift < CHUNK:
        gc = gc + jnp.where(row_in_chunk >= shift, pltpu.roll(gc, shift, axis=0), 0.0)
        shift *= 2
    small_ref[0] = beta_all
    small_ref[1] = gc
    gct_ref[...] = gc.T

    half = B_WIDTH // 2
    cos_full = cos_ref[...]
    sin_signed = sin_ref[...]
    ri = lax.broadcasted_iota(jnp.int32, (GROUP, GROUP), 0)
    ci = lax.broadcasted_iota(jnp.int32, (GROUP, GROUP), 1)
    tril = ((ri // CHUNK) == (ci // CHUNK)) & (ri >= ci)
    for pair in range(B_HEADS // 2):
        project_rest(BQ_OFF + pair * half, BQ_OFF + (pair + 1) * half)
        fill.emit(1)
        project_rest(BK_OFF + pair * half, BK_OFF + (pair + 1) * half)
        fill.emit(1)
        for h in (2 * pair, 2 * pair + 1):
            gc_col = small_ref[1, :, A_HEADS + h:A_HEADS + h + 1]
            gcr = jnp.broadcast_to(gct_ref[A_HEADS + h:A_HEADS + h + 1, :], (GROUP, GROUP))
            diff = jnp.broadcast_to(gc_col, (GROUP, GROUP)) - gcr
            dec_ref[h] = jnp.exp(jnp.where(tril, diff, MASKED_LOG))
            cq = _head_cols(BQ_OFF, h)
            ck = _head_cols(BK_OFF, h)
            pr_ref[:, cq] = _rope(pr_ref[:, cq], cos_full, sin_signed)
            pr_ref[:, ck] = _rope(pr_ref[:, ck], cos_full, sin_signed) * (HEAD_DIM ** -0.5)

    for lo in range(0, QKV_WIDTH, FFN_COL_BLOCK):
        pq_ref[pad:pad + tb, lo:lo + FFN_COL_BLOCK] = project(lo, lo + FFN_COL_BLOCK)
        fill.emit(1)
        sample.emit(1)

    rest = [(lo, lo + FFN_COL_BLOCK) for lo in range(Z_OFF, BQ_OFF, FFN_COL_BLOCK)]
    rest += [(lo, lo + FFN_COL_BLOCK) for lo in range(BV_OFF, SMALL_OFF, FFN_COL_BLOCK)]
    for j in range(QKV_WIDTH // LANES):
        if j % 2 == 0 and rest:
            project_rest(*rest.pop(0))
        else:
            fill.emit(1)
        cs = slice(j * LANES, (j + 1) * LANES)
        acc = pq_ref[pad:pad + tb, cs] * convw_ref[3:4, cs]
        for i in range(CONV_W - 1):
            acc = acc + pq_ref[pad - 3 + i:pad - 3 + i + tb, cs] * convw_ref[i:i + 1, cs]
        a = _silu(acc)
        if j < 2 * A_HEADS:
            a = a * lax.rsqrt(jnp.sum(a * a, axis=-1, keepdims=True) + L2_EPS)
        if j < A_HEADS:
            a = a * (HEAD_DIM ** -0.5)
        act_ref[:, cs] = a
    assert not rest

    tail = pq_ref[pad + tb - 3:pad + tb, :]
    conv_ref[...] = tail
    pq_ref[pad - 3:pad, :] = tail

    for g in range(tb // GROUP):
        _recurrences_one_group(g, fill, act_ref, pr_ref, small_ref, dec_ref, sg_ref, sr_ref, o_ref,
                               gnw_ref[...], rnw_ref[...])

    m = jnp.dot(o_ref[...].astype(_BF16), wout_ref[...], preferred_element_type=_F32)
    fill.emit(8)
    sample.flush()
    h_new = x_ref[...] + _rmsnorm(m, npost_ref[...])
    cur = lax.rem(s, 2)
    h_ref[cur] = h_new
    hn_ref[cur] = _rmsnorm(h_new, fpre_ref[...]).astype(_BF16)
    fill.flush()


def _layer_spec(shape, layer):
    zeros = (0,) * len(shape)
    return pl.BlockSpec((None,) + tuple(shape), lambda i: (layer,) + zeros)


def _layer_prompt(x, cos_full, sin_signed, win, conv_w, alog, dtb, gnw, rnw, wout, npre, npost,
                  wgu, wd, fpre, fpost, sq, sk, sv, sc, state_gdn, state_ret, prev_states, layer):
    batch, seq, d_model = x.shape
    nb = sq.shape[0]
    depth = state_gdn.shape[0]
    d_ff = wd.shape[1]
    tb = min(PROMPT_TIME_BLOCK, seq)
    assert seq % tb == 0 and tb == GROUP and d_ff % FFN_COL_BLOCK == 0
    nt = seq // tb
    nblocks = batch * nt
    assert nb % nblocks == 0 and SUBLANES % (nb // nblocks) == 0
    per_step = nb // nblocks
    lspec = functools.partial(_layer_spec, layer=layer)

    def mixer_block(s):
        return jnp.minimum(s, nblocks - 1)

    def ffn_block(s):
        return jnp.maximum(s - 1, 0)

    full = lambda shape: pl.BlockSpec(tuple(shape), lambda s: (0,) * len(shape))
    state_spec = lambda heads: pl.BlockSpec((None, per_step, heads, HEAD_DIM, HEAD_DIM),
                                            lambda s: (layer, mixer_block(s), 0, 0, 0))

    in_specs = [
        pl.BlockSpec((None, tb, d_model), lambda s: (mixer_block(s) // nt, mixer_block(s) % nt, 0)),
        pl.BlockSpec((tb, HEAD_DIM), lambda s: (mixer_block(s) % nt, 0)),
        pl.BlockSpec((tb, HEAD_DIM), lambda s: (mixer_block(s) % nt, 0)),
        lspec((d_model, IN_WIDTH_PADDED)),
        lspec((CONV_W, QKV_WIDTH)),
        lspec((1, LANES)),
        lspec((1, LANES)),
        lspec((1, HEAD_DIM)),
        lspec((1, HEAD_DIM)),
        lspec((A_WIDTH + B_WIDTH, d_model)),
        lspec((1, d_model)),
        lspec((1, d_model)),
        lspec((d_model, 2 * d_ff)),
        lspec((d_ff, d_model)),
        lspec((1, d_model)),
        lspec((1, d_model)),
        full(sq.shape), full(sk.shape), full(sv.shape), full(sc.shape),
        state_spec(A_HEADS), state_spec(B_HEADS),
    ]
    args = [x, cos_full, sin_signed, win, conv_w, alog, dtb, gnw, rnw, wout, npre, npost, wgu, wd, fpre,
            fpost, sq, sk, sv, sc, state_gdn, state_ret]
    aliases = {}
    if prev_states is not None:
        in_specs += [pl.BlockSpec(memory_space=pl.ANY)] * len(STACKED_STATE_OUTPUTS)
        aliases = {len(args) + i: out for i, out in enumerate(STACKED_STATE_OUTPUTS)}
        args += list(prev_states)
    prompt_state_spec = lambda heads: pl.BlockSpec((None, None, heads, HEAD_DIM, HEAD_DIM),
                                                   lambda s: (layer, mixer_block(s) // nt, 0, 0, 0))
    out_specs = [
        pl.BlockSpec((None, tb, d_model), lambda s: (ffn_block(s) // nt, ffn_block(s) % nt, 0)),
        pl.BlockSpec((None, CONV_W - 1, QKV_WIDTH), lambda s: (mixer_block(s) // nt, 0, 0)),
        prompt_state_spec(A_HEADS), prompt_state_spec(B_HEADS),
        full(sq.shape), state_spec(A_HEADS), state_spec(B_HEADS),
    ]
    out_shape = [
        jax.ShapeDtypeStruct((batch, seq, d_model), _F32),
        jax.ShapeDtypeStruct((batch, CONV_W - 1, QKV_WIDTH), _F32),
        jax.ShapeDtypeStruct((depth, batch, A_HEADS, HEAD_DIM, HEAD_DIM), _F32),
        jax.ShapeDtypeStruct((depth, batch, B_HEADS, HEAD_DIM, HEAD_DIM), _F32),
        jax.ShapeDtypeStruct(sq.shape, _F32),
        jax.ShapeDtypeStruct((depth, nb, A_HEADS, HEAD_DIM, HEAD_DIM), _F32),
        jax.ShapeDtypeStruct((depth, nb, B_HEADS, HEAD_DIM, HEAD_DIM), _F32),
    ]
    scratch = [
        pltpu.VMEM((tb + SUBLANES, QKV_WIDTH), _F32),
        pltpu.VMEM((tb, REST_WIDTH), _F32),
        pltpu.VMEM((tb, QKV_WIDTH), _F32),
        pltpu.VMEM((2, tb, LANES), _F32),
        pltpu.VMEM((LANES, tb), _F32),
        pltpu.VMEM((A_HEADS, GROUP, GROUP), _F32),
        pltpu.VMEM((tb, A_WIDTH + B_WIDTH), _F32),
        pltpu.VMEM((2, tb, d_model), _F32),
        pltpu.VMEM((2, tb, d_model), _BF16),
        pltpu.VMEM((2, tb, FFN_COL_BLOCK), _F32),
        pltpu.VMEM((tb, d_ff), _BF16),
        pltpu.VMEM((tb, d_model), _F32),
    ]
    return pl.pallas_call(
        functools.partial(_layer_prompt_body, aliased=prev_states is not None, tb=tb, nt=nt, nblocks=nblocks,
                          d_ff=d_ff, per_step=per_step),
        grid=(nblocks + 1,), in_specs=in_specs, out_specs=out_specs, out_shape=out_shape,
        scratch_shapes=scratch, input_output_aliases=aliases,
        compiler_params=pltpu.CompilerParams(
            dimension_semantics=("arbitrary",), vmem_limit_bytes=VMEM_LIMIT_BYTES),
        name=f"layer_prompt_l{layer}",
    )(*args)


def _sample_pre_body(x_ref, cos_ref, sin_ref, win_ref, convw_ref, alog_ref, dtb_ref, npre_ref, convs_ref,
                     q_ref, k_ref, v_ref, sc_ref, gates_ref, convn_ref):
    nb = x_ref.shape[0]
    hn = _rmsnorm(x_ref[...], npre_ref[...]).astype(_BF16)
    pq = jnp.dot(hn, win_ref[:, 0:QKV_WIDTH], preferred_element_type=_F32)
    pr = jnp.dot(hn, win_ref[:, QKV_WIDTH:IN_WIDTH_PADDED], preferred_element_type=_F32)
    outs = (q_ref, k_ref, v_ref)
    for j in range(QKV_WIDTH // LANES):
        cs = slice(j * LANES, (j + 1) * LANES)
        new = pq[:, cs]
        acc = new * convw_ref[3:4, cs]
        for r in range(CONV_W - 1):
            hist = convs_ref[r, :, cs]
            acc = acc + hist * convw_ref[r:r + 1, cs]
            if r > 0:
                convn_ref[r - 1, :, cs] = hist
        convn_ref[CONV_W - 2, :, cs] = new
        a = _silu(acc)
        if j < 2 * A_HEADS:
            a = a * lax.rsqrt(jnp.sum(a * a, axis=-1, keepdims=True) + L2_EPS)
        if j < A_HEADS:
            a = a * (HEAD_DIM ** -0.5)
        outs[j // A_HEADS][:, _head_cols(0, j % A_HEADS)] = a
    ps = pr[:, SMALL_OFF:SMALL_OFF + LANES]
    beta_all = jax.nn.sigmoid(ps)
    eg_all = jnp.exp(-jnp.exp(alog_ref[...]) * _softplus(ps + dtb_ref[...]))
    cos_full = cos_ref[...]
    sin_signed = sin_ref[...]
    for h in range(A_HEADS):
        sc_ref[:, _head_cols(0, h)] = jnp.broadcast_to(eg_all[:, A_HEADS + h:A_HEADS + h + 1], (nb, LANES))
        sc_ref[:, _head_cols(A_WIDTH, h)] = jnp.broadcast_to(beta_all[:, h:h + 1], (nb, LANES))
        qk = jnp.sum(q_ref[:, _head_cols(0, h)] * k_ref[:, _head_cols(0, h)], axis=-1, keepdims=True)
        sc_ref[:, _head_cols(2 * A_WIDTH, h)] = jnp.broadcast_to(qk, (nb, LANES))
    for h in range(B_HEADS):
        q = _rope(pr[:, _head_cols(BQ_OFF, h)], cos_full, sin_signed)
        k = _rope(pr[:, _head_cols(BK_OFF, h)], cos_full, sin_signed) * (HEAD_DIM ** -0.5)
        q_ref[:, _head_cols(A_WIDTH, h)] = q
        k_ref[:, _head_cols(A_WIDTH, h)] = k
        sc_ref[:, _head_cols(3 * A_WIDTH, h)] = jnp.broadcast_to(
            jnp.sum(q * k, axis=-1, keepdims=True), (nb, LANES))
    v_ref[:, A_WIDTH:A_WIDTH + B_WIDTH] = pr[:, BV_OFF:BG_OFF]
    gates_ref[:, 0:A_WIDTH] = pr[:, Z_OFF:BQ_OFF]
    gates_ref[:, A_WIDTH:A_WIDTH + B_WIDTH] = pr[:, BG_OFF:SMALL_OFF]


def _sample_pre(x, cos_full, sin_signed, win, conv_w, alog, dtb, npre, conv_state, layer):
    nb, d_model = x.shape
    width = A_WIDTH + B_WIDTH
    lspec = functools.partial(_layer_spec, layer=layer)
    full = lambda shape: pl.BlockSpec(tuple(shape), lambda i: (0,) * len(shape))
    shapes = [(nb, width)] * 3 + [(nb, SAMPLE_SCALARS * A_WIDTH), (nb, width), (CONV_W - 1, nb, QKV_WIDTH)]
    return pl.pallas_call(
        _sample_pre_body,
        grid=(1,),
        in_specs=[full((nb, d_model)), full((1, HEAD_DIM)), full((1, HEAD_DIM)),
                  lspec((d_model, IN_WIDTH_PADDED)), lspec((CONV_W, QKV_WIDTH)), lspec((1, LANES)),
                  lspec((1, LANES)), lspec((1, d_model)), lspec((CONV_W - 1, nb, QKV_WIDTH))],
        out_specs=[full(shape) for shape in shapes],
        out_shape=[jax.ShapeDtypeStruct(shape, _F32) for shape in shapes],
        compiler_params=pltpu.CompilerParams(
            dimension_semantics=("arbitrary",), vmem_limit_bytes=VMEM_LIMIT_BYTES),
        name=f"sample_pre_l{layer}",
    )(x, cos_full, sin_signed, win, conv_w, alog, dtb, npre, conv_state)


def _sample_post_body(o_ref, gates_ref, x_ref, gnw_ref, rnw_ref, wout_ref, npost_ref, wgu_ref, wd_ref,
                      fpre_ref, fpost_ref, y_ref, cat_ref, act_ref, *, d_ff):
    for h in range(A_HEADS + B_HEADS):
        cs = _head_cols(0, h)
        w = gnw_ref[...] if h < A_HEADS else rnw_ref[...]
        cat_ref[:, cs] = _gated_head_norm(o_ref[:, cs], w, gates_ref[:, cs]).astype(_BF16)
    m = jnp.dot(cat_ref[...], wout_ref[...], preferred_element_type=_F32)
    h_new = x_ref[...] + _rmsnorm(m, npost_ref[...])
    y_ref[...] = _ffn_rows(h_new, wgu_ref, wd_ref, fpre_ref[...], fpost_ref[...], act_ref, d_ff)


def _sample_post(o, gates, x, gnw, rnw, wout, npost, wgu, wd, fpre, fpost, layer):
    nb, d_model = x.shape
    d_ff = wd.shape[1]
    width = A_WIDTH + B_WIDTH
    lspec = functools.partial(_layer_spec, layer=layer)
    full = lambda shape: pl.BlockSpec(tuple(shape), lambda i: (0,) * len(shape))
    return pl.pallas_call(
        functools.partial(_sample_post_body, d_ff=d_ff),
        grid=(1,),
        in_specs=[full((nb, width)), full((nb, width)), full((nb, d_model)), lspec((1, HEAD_DIM)),
                  lspec((1, HEAD_DIM)), lspec((width, d_model)), lspec((1, d_model)),
                  lspec((d_model, 2 * d_ff)), lspec((d_ff, d_model)), lspec((1, d_model)),
                  lspec((1, d_model))],
        out_specs=full((nb, d_model)),
        out_shape=jax.ShapeDtypeStruct((nb, d_model), _F32),
        scratch_shapes=[pltpu.VMEM((nb, width), _BF16), pltpu.VMEM((nb, d_ff), _BF16)],
        compiler_params=pltpu.CompilerParams(
            dimension_semantics=("arbitrary",), vmem_limit_bytes=VMEM_LIMIT_BYTES),
        name=f"sample_post_l{layer}",
    )(o, gates, x, gnw, rnw, wout, npost, wgu, wd, fpre, fpost)


def _rope_tables(positions):
    half = HEAD_DIM // 2
    inv = ROPE_BASE ** (-np.arange(half, dtype=np.float64) / half)
    ang = np.asarray(positions, dtype=np.float64)[:, None] * inv[None, :]
    cos, sin = np.cos(ang), np.sin(ang)
    return (jnp.asarray(np.concatenate([cos, cos], axis=-1), dtype=_F32),
            jnp.asarray(np.concatenate([-sin, sin], axis=-1), dtype=_F32))


def _rearranged_w_in_body(wt_ref, o_ref):
    small0 = QKV_WIDTH + A_WIDTH
    small1 = small0 + 2 * A_HEADS
    cb = WEIGHT_COL_BLOCK
    for lo in range(0, small0, cb):
        o_ref[:, lo:lo + cb] = wt_ref[lo:lo + cb, :].T.astype(_BF16)
    for lo in range(small0, QKV_WIDTH + SMALL_OFF, cb):
        src = lo + small1 - small0
        o_ref[:, lo:lo + cb] = wt_ref[src:src + cb, :].T.astype(_BF16)
    lane = lax.broadcasted_iota(jnp.int32, (o_ref.shape[0], LANES), 1)
    o_ref[:, QKV_WIDTH + SMALL_OFF:IN_WIDTH_PADDED] = jnp.where(
        lane < 2 * A_HEADS, wt_ref[small0:small0 + LANES, :].T, 0.0).astype(_BF16)


def _rearranged_w_in(w_in):
    depth, d_model, width = w_in.shape
    assert width == IN_WIDTH_PADDED - LANES + 2 * A_HEADS
    return pl.pallas_call(
        _rearranged_w_in_body,
        grid=(depth,),
        in_specs=[pl.BlockSpec((None, width, d_model), lambda l: (l, 0, 0), pipeline_mode=pl.Buffered(1))],
        out_specs=pl.BlockSpec((None, d_model, IN_WIDTH_PADDED), lambda l: (l, 0, 0)),
        out_shape=jax.ShapeDtypeStruct((depth, d_model, IN_WIDTH_PADDED), _BF16),
        compiler_params=pltpu.CompilerParams(
            dimension_semantics=("arbitrary",), vmem_limit_bytes=VMEM_LIMIT_BYTES),
        name="rearranged_w_in",
    )(jnp.swapaxes(w_in, 1, 2))


def kernel(x_prompt, x_sample, state_conv, state_gdn, state_ret, w_in, conv_w, a_log, dt_bias, gdn_norm_w, ret_norm_w, w_out, norm_mix_pre, norm_mix_post, norm_ffn_pre, norm_ffn_post, w_gate_up, w_down):
    depth = w_in.shape[0]
    batch, seq, d_model = x_prompt.shape
    nb, seq_s, _ = x_sample.shape
    assert seq_s == 1

    win = _rearranged_w_in(w_in)
    wout = w_out.astype(_BF16)
    wgu = w_gate_up.astype(_BF16)
    wd = w_down.astype(_BF16)
    alog = jnp.pad(a_log, ((0, 0), (A_HEADS, LANES - 2 * A_HEADS)))[:, None, :]
    dtb = jnp.pad(dt_bias, ((0, 0), (A_HEADS, LANES - 2 * A_HEADS)))[:, None, :]
    gnw = gdn_norm_w[:, None, :]
    rnw = ret_norm_w[:, None, :]
    npre = norm_mix_pre[:, None, :]
    npost = norm_mix_post[:, None, :]
    fpre = norm_ffn_pre[:, None, :]
    fpost = norm_ffn_post[:, None, :]
    cos_p, sin_p = _rope_tables(np.arange(seq))
    cos_s, sin_s = _rope_tables(PAST_LEN + np.arange(seq_s))
    conv_state = jnp.swapaxes(state_conv, 1, 2)

    hp = x_prompt
    hs = x_sample.reshape(nb, d_model)
    convs_p, convs_s = [], []
    states = None
    for l in range(depth):
        sq, sk, sv, sc, gates, conv_s = _sample_pre(hs, cos_s, sin_s, win, conv_w, alog, dtb, npre,
                                                    conv_state, l)
        hp, conv_p, gdn_p, ret_p, so, gdn_s, ret_s = _layer_prompt(
            hp, cos_p, sin_p, win, conv_w, alog, dtb, gnw, rnw, wout, npre, npost, wgu, wd, fpre, fpost,
            sq, sk, sv, sc, state_gdn, state_ret, states, l)
        states = (gdn_p, ret_p, gdn_s, ret_s)
        hs = _sample_post(so, gates, hs, gnw, rnw, wout, npost, wgu, wd, fpre, fpost, l)
        convs_p.append(conv_p)
        convs_s.append(conv_s)
    gdn_p, ret_p, gdn_s, ret_s = states
    return (hp, hs.reshape(nb, seq_s, d_model), jnp.stack(convs_p), gdn_p, ret_p,
            jnp.swapaxes(jnp.stack(convs_s), 1, 2), gdn_s, ret_s)
```

```python
import collections
import functools
import math

import jax
import jax.numpy as jnp
import numpy as np
from jax import lax
from jax.experimental import pallas as pl
from jax.experimental.pallas import tpu as pltpu

HEAD_DIM = 128
A_HEADS = 4
B_HEADS = 4
A_WIDTH = A_HEADS * HEAD_DIM
B_WIDTH = B_HEADS * HEAD_DIM
QKV_WIDTH = 3 * A_WIDTH
CONV_W = 4
CHUNK = 64
ROPE_BASE = 10000.0
EPS = 1e-6
L2_EPS = 1e-6
MASKED_LOG = -1e30
PAST_LEN = 16384
LANES = 128
SUBLANES = 8

REST_WIDTH = A_WIDTH + 4 * B_WIDTH + LANES
Z_OFF = 0
BQ_OFF = A_WIDTH
BK_OFF = BQ_OFF + B_WIDTH
BV_OFF = BK_OFF + B_WIDTH
BG_OFF = BV_OFF + B_WIDTH
SMALL_OFF = BG_OFF + B_WIDTH
IN_WIDTH_PADDED = QKV_WIDTH + REST_WIDTH

GROUP = 256
CHUNKS_PER_GROUP = GROUP // CHUNK
PROMPT_TIME_BLOCK = 256
FFN_COL_BLOCK = 256
FFN_TAIL_PIECES = 5
WEIGHT_COL_BLOCK = 256
SAMPLE_SCALARS = 4
STACKED_STATE_OUTPUTS = (2, 3, 5, 6)
VMEM_LIMIT_BYTES = 56 * 1024 * 1024

_BF16 = jnp.bfloat16
_F32 = jnp.float32


def _dot(a, b):
    return jnp.dot(a.astype(_BF16), b.astype(_BF16), preferred_element_type=_F32)


def _dot_nt(a, b):
    return lax.dot_general(a.astype(_BF16), b.astype(_BF16), (((1,), (1,)), ((), ())),
                           preferred_element_type=_F32)


def _dot_tn(a, b):
    return lax.dot_general(a.astype(_BF16), b.astype(_BF16), (((0,), (0,)), ((), ())),
                           preferred_element_type=_F32)


def _rmsnorm(x, w):
    return x * lax.rsqrt(jnp.mean(x * x, axis=-1, keepdims=True) + EPS) * w


def _silu(x):
    return x * jax.nn.sigmoid(x)


def _softplus(x):
    return jnp.maximum(x, 0.0) + jnp.log1p(jnp.exp(-jnp.abs(x)))


def _log_gamma(h):
    return math.log1p(-(2.0 ** (-5.0 - h)))


def _rope(x, cos_full, sin_signed):
    return x * cos_full + pltpu.roll(x, HEAD_DIM // 2, axis=1) * sin_signed


def _head_cols(base, h):
    return slice(base + h * HEAD_DIM, base + (h + 1) * HEAD_DIM)


def _gated_head_norm(o, w, gate):
    return o * lax.rsqrt(jnp.mean(o * o, axis=-1, keepdims=True) + EPS) * w * _silu(gate)


class _Filler:
    def __init__(self, pieces, reserve=0):
        self._pieces = list(pieces)
        self._next = 0
        self._limit = len(self._pieces) - reserve

    def emit(self, count):
        for _ in range(count):
            if self._next < self._limit:
                self._pieces[self._next]()
                self._next += 1

    def flush(self):
        self._limit = len(self._pieces)
        self.emit(len(self._pieces))


def _fold_rows(m):
    out = m[0:CHUNK]
    for c in range(1, CHUNKS_PER_GROUP):
        out = out + m[c * CHUNK:(c + 1) * CHUNK]
    return out


def _unfold_rows(r, same_chunk):
    return jnp.where(same_chunk, jnp.concatenate([r] * CHUNKS_PER_GROUP, axis=0), 0.0)


def _unit_lower_inverses(a_folded, eye_folded, same_chunk, fill):
    xs = list(a_folded)
    ps = [eye_folded - x for x in xs]
    n = 1
    while n < CHUNK:
        for h in range(len(xs)):
            x_bd = _unfold_rows(xs[h], same_chunk).astype(_BF16)
            if n == 1:
                xs[h] = _dot(xs[h], x_bd)
            elif 2 * n < CHUNK:
                r = _dot(jnp.concatenate([xs[h], ps[h]], axis=0), x_bd)
                xs[h] = r[0:CHUNK]
                ps[h] = ps[h] + r[CHUNK:2 * CHUNK]
            else:
                ps[h] = ps[h] + _dot(ps[h], x_bd)
            fill.emit(h % 2)
        n *= 2
    return [_unfold_rows(p, same_chunk) for p in ps]


def _recurrences_one_group(g, fill, act_ref, pr_ref, small_ref, dec_ref, sg_ref, sr_ref, o_ref, gnw, rnw):
    rows = slice(g * GROUP, (g + 1) * GROUP)
    ri = lax.broadcasted_iota(jnp.int32, (GROUP, GROUP), 0)
    ci = lax.broadcasted_iota(jnp.int32, (GROUP, GROUP), 1)
    same_chunk = (ri // CHUNK) == (ci // CHUNK)
    tril = same_chunk & (ri >= ci)
    off_diag = ri != ci
    rf = lax.broadcasted_iota(jnp.int32, (CHUNK, GROUP), 0)
    cf = lax.broadcasted_iota(jnp.int32, (CHUNK, GROUP), 1)
    eye_folded = jnp.where(rf == cf % CHUNK, 1.0, 0.0).astype(_F32)
    pos = (lax.broadcasted_iota(jnp.int32, (GROUP, HEAD_DIM), 0) % CHUNK).astype(_F32)
    chunk_rows = [slice(c * CHUNK, (c + 1) * CHUNK) for c in range(CHUNKS_PER_GROUP)]

    qs, ks, gccs, a_folded, a_intra, uw_rhs, e_ins = [], [], [], [], [], [], []
    for h in range(A_HEADS):
        q = act_ref[rows, _head_cols(0, h)]
        k = act_ref[rows, _head_cols(A_WIDTH, h)]
        v = act_ref[rows, _head_cols(2 * A_WIDTH, h)]
        beta = jnp.broadcast_to(small_ref[0, rows, h:h + 1], (GROUP, HEAD_DIM))
        gc_col = small_ref[1, rows, A_HEADS + h:A_HEADS + h + 1]
        gcc = jnp.broadcast_to(gc_col, (GROUP, HEAD_DIM))
        decay = dec_ref[h]
        kb = k * beta
        e_in = jnp.exp(gcc)
        kq = _dot_nt(jnp.concatenate([kb, q], axis=0), k)
        a_folded.append(_fold_rows(jnp.where(off_diag, kq[0:GROUP] * decay, 0.0)))
        a_intra.append(kq[GROUP:2 * GROUP] * decay)
        uw_rhs.append(jnp.concatenate([v * beta, kb * e_in], axis=1).astype(_BF16))
        qs.append(q)
        ks.append(k)
        gccs.append(gcc)
        e_ins.append(e_in)
        fill.emit(1)

    r_qe, r_oloc, r_b, r_echunk = [], [], [], []
    for h in range(B_HEADS):
        lg = _log_gamma(h)
        q = pr_ref[rows, _head_cols(BQ_OFF, h)]
        k = pr_ref[rows, _head_cols(BK_OFF, h)]
        v = pr_ref[rows, _head_cols(BV_OFF, h)]
        decay = jnp.exp(jnp.where(tril, (ri - ci).astype(_F32) * lg, MASKED_LOG))
        vb16 = v.astype(_BF16)
        r_oloc.append(_dot(_dot_nt(q, k) * decay, vb16))
        r_qe.append((q * jnp.exp((pos + 1.0) * lg)).astype(_BF16))
        k_out = (k * jnp.exp((CHUNK - 1.0 - pos) * lg)).astype(_BF16)
        r_b.append([_dot_tn(k_out[cr], vb16[cr]) for cr in chunk_rows])
        r_echunk.append(math.exp(CHUNK * lg))

    t_inv = _unit_lower_inverses(a_folded, eye_folded, same_chunk, fill)

    g_lhs, g_oloc, g_b, g_elast = [], [], [], []
    for h in range(A_HEADS):
        uw = _dot(t_inv[h], uw_rhs[h])
        uw16 = uw.astype(_BF16)
        aiuw = _dot(a_intra[h], uw16)
        g_oloc.append(aiuw[:, 0:HEAD_DIM])
        q_eff = qs[h] * e_ins[h] - aiuw[:, HEAD_DIM:2 * HEAD_DIM]
        lhs, bs, elast = [], [], []
        for cr in chunk_rows:
            gcc_c = gccs[h][cr]
            g_last = gcc_c[CHUNK - 1:CHUNK, :]
            k_out = ks[h][cr] * jnp.exp(g_last - gcc_c)
            bg = _dot_tn(k_out, uw16[cr])
            bs.append(bg[:, 0:HEAD_DIM])
            lhs.append(jnp.concatenate([q_eff[cr], bg[:, HEAD_DIM:2 * HEAD_DIM]], axis=0).astype(_BF16))
            elast.append(jnp.exp(g_last))
        g_lhs.append(lhs)
        g_b.append(bs)
        g_elast.append(elast)
        fill.emit(1)

    def normalize_chunk(c):
        out_rows = slice(g * GROUP + c * CHUNK, g * GROUP + (c + 1) * CHUNK)
        for h in range(A_HEADS):
            cs = _head_cols(0, h)
            o_ref[out_rows, cs] = _gated_head_norm(o_ref[out_rows, cs], gnw,
                                                   pr_ref[out_rows, _head_cols(Z_OFF, h)])
        for h in range(B_HEADS):
            cs = _head_cols(A_WIDTH, h)
            o_ref[out_rows, cs] = _gated_head_norm(o_ref[out_rows, cs], rnw,
                                                   pr_ref[out_rows, _head_cols(BG_OFF, h)])

    g_state = [sg_ref[h] for h in range(A_HEADS)]
    r_state = [sr_ref[h] for h in range(B_HEADS)]
    for c, cr in enumerate(chunk_rows):
        out_rows = slice(g * GROUP + c * CHUNK, g * GROUP + (c + 1) * CHUNK)
        for h in range(A_HEADS):
            r = _dot(g_lhs[h][c], g_state[h])
            o_ref[out_rows, _head_cols(0, h)] = r[0:CHUNK] + g_oloc[h][cr]
            g_state[h] = g_state[h] * g_elast[h][c] + g_b[h][c] - r[CHUNK:CHUNK + HEAD_DIM]
        for h in range(B_HEADS):
            o_ref[out_rows, _head_cols(A_WIDTH, h)] = _dot(r_qe[h][cr], r_state[h]) + r_oloc[h][cr]
            r_state[h] = r_state[h] * r_echunk[h] + r_b[h][c]
        if c > 0:
            normalize_chunk(c - 1)
        fill.emit(1)
    normalize_chunk(CHUNKS_PER_GROUP - 1)
    for h in range(A_HEADS):
        sg_ref[h] = g_state[h]
    for h in range(B_HEADS):
        sr_ref[h] = r_state[h]


def _ffn_rows(h, wgu_ref, wd_ref, npre, npost, act_ref, d_ff):
    hn = _rmsnorm(h, npre).astype(_BF16)
    for j in range(d_ff // FFN_COL_BLOCK):
        cg = slice(j * FFN_COL_BLOCK, (j + 1) * FFN_COL_BLOCK)
        cu = slice(d_ff + j * FFN_COL_BLOCK, d_ff + (j + 1) * FFN_COL_BLOCK)
        gate = jnp.dot(hn, wgu_ref[:, cg], preferred_element_type=_F32)
        up = jnp.dot(hn, wgu_ref[:, cu], preferred_element_type=_F32)
        act_ref[:, cg] = (_silu(gate) * up).astype(_BF16)
    f = jnp.dot(act_ref[...], wd_ref[...], preferred_element_type=_F32)
    return h + _rmsnorm(f, npost)


def _ffn_pieces(h_ref, hn_ref, wgu_ref, wd_ref, npost, gate_ref, act_ref, f_ref, y_ref, d_ff):
    d_model = f_ref.shape[1]
    k_split = (d_ff // FFN_COL_BLOCK + 1) // 2 * FFN_COL_BLOCK
    pieces = []

    def gate(j):
        def run():
            cg = slice(j * FFN_COL_BLOCK, (j + 1) * FFN_COL_BLOCK)
            gate_ref[j % 2] = _silu(jnp.dot(hn_ref[...], wgu_ref[:, cg], preferred_element_type=_F32))
        return run

    def up(j):
        def run():
            cg = slice(j * FFN_COL_BLOCK, (j + 1) * FFN_COL_BLOCK)
            cu = slice(d_ff + j * FFN_COL_BLOCK, d_ff + (j + 1) * FFN_COL_BLOCK)
            act_ref[:, cg] = (gate_ref[j % 2] * jnp.dot(hn_ref[...], wgu_ref[:, cu],
                                                        preferred_element_type=_F32)).astype(_BF16)
        return run

    def down(j, first):
        def run():
            cs = slice(j * FFN_COL_BLOCK, (j + 1) * FFN_COL_BLOCK)
            if first:
                f_ref[:, cs] = jnp.dot(act_ref[:, 0:k_split], wd_ref[0:k_split, cs],
                                       preferred_element_type=_F32)
            else:
                f_ref[:, cs] = f_ref[:, cs] + jnp.dot(act_ref[:, k_split:d_ff], wd_ref[k_split:d_ff, cs],
                                                      preferred_element_type=_F32)
        return run

    def finish():
        y_ref[...] = h_ref[...] + _rmsnorm(f_ref[...], npost)

    for j in range(d_ff // FFN_COL_BLOCK):
        pieces += [gate(j), up(j)]
    for j in range(d_model // FFN_COL_BLOCK):
        pieces += [down(j, True), down(j, False)]
    pieces.append(finish)
    return pieces


def _column_of_row(row):
    return jnp.broadcast_to(row, (HEAD_DIM, HEAD_DIM)).T


def _sample_state_units(first_seq, per_step, sq_ref, sk_ref, sv_ref, sc_ref, sgi_ref, sri_ref,
                        so_ref, sgo_ref, sro_ref):
    base = pl.multiple_of((first_seq // SUBLANES) * SUBLANES, SUBLANES)
    rows = pl.ds(base, SUBLANES)
    row_id = lax.broadcasted_iota(jnp.int32, (SUBLANES, HEAD_DIM), 0)
    picks = [row_id == (first_seq - base + j) for j in range(per_step)]

    def pick(block, j):
        return jnp.sum(jnp.where(picks[j], block, 0.0), axis=0, keepdims=True)

    def put(cols, outs):
        block = so_ref[rows, cols]
        for j in range(per_step):
            block = jnp.where(picks[j], outs[j], block)
        so_ref[rows, cols] = block

    def deltanet(h):
        def run():
            cols = _head_cols(0, h)
            q8, k8, v8 = sq_ref[rows, cols], sk_ref[rows, cols], sv_ref[rows, cols]
            eg8 = sc_ref[rows, _head_cols(0, h)]
            bt8 = sc_ref[rows, _head_cols(A_WIDTH, h)]
            qk8 = sc_ref[rows, _head_cols(2 * A_WIDTH, h)]
            kq16 = jnp.concatenate([k8, q8], axis=0).astype(_BF16)
            kqs = [jnp.dot(kq16, sgi_ref[j, h].astype(_BF16), preferred_element_type=_F32)
                   for j in range(per_step)]
            outs = []
            for j in range(per_step):
                eg = pick(eg8, j)
                v_new = pick(bt8, j) * (pick(v8, j) - eg * pick(kqs[j][0:SUBLANES], j))
                outs.append(eg * pick(kqs[j][SUBLANES:2 * SUBLANES], j) + pick(qk8, j) * v_new)
                sgo_ref[j, h] = sgi_ref[j, h] * eg + _column_of_row(pick(k8, j)) * v_new
            put(cols, outs)
        return run

    def retention(h):
        def run():
            gamma = math.exp(_log_gamma(h))
            cols = _head_cols(A_WIDTH, h)
            q8, k8, v8 = sq_ref[rows, cols], sk_ref[rows, cols], sv_ref[rows, cols]
            qk8 = sc_ref[rows, _head_cols(3 * A_WIDTH, h)]
            q16 = q8.astype(_BF16)
            qss = [jnp.dot(q16, sri_ref[j, h].astype(_BF16), preferred_element_type=_F32)
                   for j in range(per_step)]
            outs = []
            for j in range(per_step):
                v1 = pick(v8, j)
                outs.append(gamma * pick(qss[j], j) + pick(qk8, j) * v1)
                sro_ref[j, h] = sri_ref[j, h] * gamma + _column_of_row(pick(k8, j)) * v1
            put(cols, outs)
        return run

    return [deltanet(h) for h in range(A_HEADS)] + [retention(h) for h in range(B_HEADS)]


_LayerInputs = collections.namedtuple("_LayerInputs", [
    "x", "cos", "sin", "win", "convw", "alog", "dtb", "gnw", "rnw", "wout", "npre", "npost", "wgu", "wd",
    "fpre", "fpost", "sq", "sk", "sv", "sc", "sgi", "sri"])
_LayerOutputsAndScratch = collections.namedtuple("_LayerOutputsAndScratch", [
    "y", "conv", "sg", "sr", "so", "sgo", "sro",
    "pq", "pr", "act", "small", "gct", "dec", "o", "h", "hn", "gate", "ffn_act", "f"])


def _layer_prompt_refs(refs, aliased):
    n_in = len(_LayerInputs._fields)
    n_skip = len(STACKED_STATE_OUTPUTS) if aliased else 0
    return _LayerInputs(*refs[:n_in]), _LayerOutputsAndScratch(*refs[n_in + n_skip:])


def _layer_prompt_body(*refs, aliased, tb, nt, nblocks, d_ff, per_step):
    s = pl.program_id(0)

    @pl.when(s < nblocks)
    def _():
        _layer_prompt_step(refs, aliased, tb, nt, d_ff, per_step)

    @pl.when(s == nblocks)
    def _():
        ins, rest = _layer_prompt_refs(refs, aliased)
        prev = lax.rem(s + 1, 2)
        _Filler(_ffn_pieces(rest.h.at[prev], rest.hn.at[prev], ins.wgu, ins.wd, ins.fpost[...],
                            rest.gate, rest.ffn_act, rest.f, rest.y, d_ff)).flush()


def _layer_prompt_step(refs, aliased, tb, nt, d_ff, per_step):
    ins, rest = _layer_prompt_refs(refs, aliased)
    (x_ref, cos_ref, sin_ref, win_ref, convw_ref, alog_ref, dtb_ref, gnw_ref, rnw_ref, wout_ref,
     npre_ref, npost_ref, wgu_ref, wd_ref, fpre_ref, fpost_ref,
     sq_ref, sk_ref, sv_ref, sc_ref, sgi_ref, sri_ref) = ins
    (y_ref, conv_ref, sg_ref, sr_ref, so_ref, sgo_ref, sro_ref,
     pq_ref, pr_ref, act_ref, small_ref, gct_ref, dec_ref, o_ref, h_ref, hn_ref, gate_ref,
     ffn_act_ref, f_ref) = rest
    s = pl.program_id(0)
    t = lax.rem(s, nt)
    pad = SUBLANES

    @pl.when(s == 0)
    def _():
        h_ref[1] = jnp.zeros(h_ref.shape[1:], _F32)
        hn_ref[1] = jnp.zeros(hn_ref.shape[1:], _BF16)
        so_ref[...] = jnp.zeros(so_ref.shape, _F32)

    @pl.when(t == 0)
    def _():
        pq_ref[0:pad, :] = jnp.zeros((pad, QKV_WIDTH), _F32)
        sg_ref[...] = jnp.zeros(sg_ref.shape, _F32)
        sr_ref[...] = jnp.zeros(sr_ref.shape, _F32)

    prev = lax.rem(s + 1, 2)
    fill = _Filler(_ffn_pieces(h_ref.at[prev], hn_ref.at[prev], wgu_ref, wd_ref, fpost_ref[...],
                               gate_ref, ffn_act_ref, f_ref, y_ref, d_ff), reserve=FFN_TAIL_PIECES)
    fill.emit(2)
    sample = _Filler(_sample_state_units(s * per_step, per_step, sq_ref, sk_ref, sv_ref, sc_ref, sgi_ref,
                                         sri_ref, so_ref, sgo_ref, sro_ref))

    x = x_ref[...]
    hn = _rmsnorm(x, npre_ref[...]).astype(_BF16)

    def project(lo, hi):
        return jnp.dot(hn, win_ref[:, lo:hi], preferred_element_type=_F32)

    def project_rest(lo, hi):
        pr_ref[:, lo:hi] = project(QKV_WIDTH + lo, QKV_WIDTH + hi)

    ps = project(QKV_WIDTH + SMALL_OFF, IN_WIDTH_PADDED)
    fill.emit(1)
    beta_all = jax.nn.sigmoid(ps)
    gc = -jnp.exp(alog_ref[...]) * _softplus(ps + dtb_ref[...])
    row_in_chunk = lax.broadcasted_iota(jnp.int32, (tb, LANES), 0) % CHUNK
    shift = 1
    while shift < CHUNK:
        gc = gc + jnp.where(row_in_chunk >= shift, pltpu.roll(gc, shift, axis=0), 0.0)
        shift *= 2
    small_ref[0] = beta_all
    small_ref[1] = gc
    gct_ref[...] = gc.T

    half = B_WIDTH // 2
    cos_full = cos_ref[...]
    sin_signed = sin_ref[...]
    ri = lax.broadcasted_iota(jnp.int32, (GROUP, GROUP), 0)
    ci = lax.broadcasted_iota(jnp.int32, (GROUP, GROUP), 1)
    tril = ((ri // CHUNK) == (ci // CHUNK)) & (ri >= ci)
    for pair in range(B_HEADS // 2):
        project_rest(BQ_OFF + pair * half, BQ_OFF + (pair + 1) * half)
        fill.emit(1)
        project_rest(BK_OFF + pair * half, BK_OFF + (pair + 1) * half)
        fill.emit(1)
        for h in (2 * pair, 2 * pair + 1):
            gc_col = small_ref[1, :, A_HEADS + h:A_HEADS + h + 1]
            gcr = jnp.broadcast_to(gct_ref[A_HEADS + h:A_HEADS + h + 1, :], (GROUP, GROUP))
            diff = jnp.broadcast_to(gc_col, (GROUP, GROUP)) - gcr
            dec_ref[h] = jnp.exp(jnp.where(tril, diff, MASKED_LOG))
            cq = _head_cols(BQ_OFF, h)
            ck = _head_cols(BK_OFF, h)
            pr_ref[:, cq] = _rope(pr_ref[:, cq], cos_full, sin_signed)
            pr_ref[:, ck] = _rope(pr_ref[:, ck], cos_full, sin_signed) * (HEAD_DIM ** -0.5)

    for lo in range(0, QKV_WIDTH, FFN_COL_BLOCK):
        pq_ref[pad:pad + tb, lo:lo + FFN_COL_BLOCK] = project(lo, lo + FFN_COL_BLOCK)
        fill.emit(1)
        sample.emit(1)

    rest = [(lo, lo + FFN_COL_BLOCK) for lo in range(Z_OFF, BQ_OFF, FFN_COL_BLOCK)]
    rest += [(lo, lo + FFN_COL_BLOCK) for lo in range(BV_OFF, SMALL_OFF, FFN_COL_BLOCK)]
    for j in range(QKV_WIDTH // LANES):
        if j % 2 == 0 and rest:
            project_rest(*rest.pop(0))
        else:
            fill.emit(1)
        cs = slice(j * LANES, (j + 1) * LANES)
        acc = pq_ref[pad:pad + tb, cs] * convw_ref[3:4, cs]
        for i in range(CONV_W - 1):
            acc = acc + pq_ref[pad - 3 + i:pad - 3 + i + tb, cs] * convw_ref[i:i + 1, cs]
        a = _silu(acc)
        if j < 2 * A_HEADS:
            a = a * lax.rsqrt(jnp.sum(a * a, axis=-1, keepdims=True) + L2_EPS)
        if j < A_HEADS:
            a = a * (HEAD_DIM ** -0.5)
        act_ref[:, cs] = a
    assert not rest

    tail = pq_ref[pad + tb - 3:pad + tb, :]
    conv_ref[...] = tail
    pq_ref[pad - 3:pad, :] = tail

    for g in range(tb // GROUP):
        _recurrences_one_group(g, fill, act_ref, pr_ref, small_ref, dec_ref, sg_ref, sr_ref, o_ref,
                               gnw_ref[...], rnw_ref[...])

    m = jnp.dot(o_ref[...].astype(_BF16), wout_ref[...], preferred_element_type=_F32)
    sample.flush()
    fill.flush()
    h_new = x_ref[...] + _rmsnorm(m, npost_ref[...])
    cur = lax.rem(s, 2)
    h_ref[cur] = h_new
    hn_ref[cur] = _rmsnorm(h_new, fpre_ref[...]).astype(_BF16)
    fill.flush()


def _layer_spec(shape, layer):
    zeros = (0,) * len(shape)
    return pl.BlockSpec((None,) + tuple(shape), lambda i: (layer,) + zeros)


def _layer_prompt(x, cos_full, sin_signed, win, conv_w, alog, dtb, gnw, rnw, wout, npre, npost,
                  wgu, wd, fpre, fpost, sq, sk, sv, sc, state_gdn, state_ret, prev_states, layer):
    batch, seq, d_model = x.shape
    nb = sq.shape[0]
    depth = state_gdn.shape[0]
    d_ff = wd.shape[1]
    tb = min(PROMPT_TIME_BLOCK, seq)
    assert seq % tb == 0 and tb == GROUP and d_ff % FFN_COL_BLOCK == 0
    nt = seq // tb
    nblocks = batch * nt
    assert nb % nblocks == 0 and SUBLANES % (nb // nblocks) == 0
    per_step = nb // nblocks
    lspec = functools.partial(_layer_spec, layer=layer)

    def mixer_block(s):
        return jnp.minimum(s, nblocks - 1)

    def ffn_block(s):
        return jnp.maximum(s - 1, 0)

    full = lambda shape: pl.BlockSpec(tuple(shape), lambda s: (0,) * len(shape))
    state_spec = lambda heads: pl.BlockSpec((None, per_step, heads, HEAD_DIM, HEAD_DIM),
                                            lambda s: (layer, mixer_block(s), 0, 0, 0))

    in_specs = [
        pl.BlockSpec((None, tb, d_model), lambda s: (mixer_block(s) // nt, mixer_block(s) % nt, 0)),
        pl.BlockSpec((tb, HEAD_DIM), lambda s: (mixer_block(s) % nt, 0)),
        pl.BlockSpec((tb, HEAD_DIM), lambda s: (mixer_block(s) % nt, 0)),
        lspec((d_model, IN_WIDTH_PADDED)),
        lspec((CONV_W, QKV_WIDTH)),
        lspec((1, LANES)),
        lspec((1, LANES)),
        lspec((1, HEAD_DIM)),
        lspec((1, HEAD_DIM)),
        lspec((A_WIDTH + B_WIDTH, d_model)),
        lspec((1, d_model)),
        lspec((1, d_model)),
        lspec((d_model, 2 * d_ff)),
        lspec((d_ff, d_model)),
        lspec((1, d_model)),
        lspec((1, d_model)),
        full(sq.shape), full(sk.shape), full(sv.shape), full(sc.shape),
        state_spec(A_HEADS), state_spec(B_HEADS),
    ]
    args = [x, cos_full, sin_signed, win, conv_w, alog, dtb, gnw, rnw, wout, npre, npost, wgu, wd, fpre,
            fpost, sq, sk, sv, sc, state_gdn, state_ret]
    aliases = {}
    if prev_states is not None:
        in_specs += [pl.BlockSpec(memory_space=pl.ANY)] * len(STACKED_STATE_OUTPUTS)
        aliases = {len(args) + i: out for i, out in enumerate(STACKED_STATE_OUTPUTS)}
        args += list(prev_states)
    prompt_state_spec = lambda heads: pl.BlockSpec((None, None, heads, HEAD_DIM, HEAD_DIM),
                                                   lambda s: (layer, mixer_block(s) // nt, 0, 0, 0))
    out_specs = [
        pl.BlockSpec((None, tb, d_model), lambda s: (ffn_block(s) // nt, ffn_block(s) % nt, 0)),
        pl.BlockSpec((None, CONV_W - 1, QKV_WIDTH), lambda s: (mixer_block(s) // nt, 0, 0)),
        prompt_state_spec(A_HEADS), prompt_state_spec(B_HEADS),
        full(sq.shape), state_spec(A_HEADS), state_spec(B_HEADS),
    ]
    out_shape = [
        jax.ShapeDtypeStruct((batch, seq, d_model), _F32),
        jax.ShapeDtypeStruct((batch, CONV_W - 1, QKV_WIDTH), _F32),
        jax.ShapeDtypeStruct((depth, batch, A_HEADS, HEAD_DIM, HEAD_DIM), _F32),
        jax.ShapeDtypeStruct((depth, batch, B_HEADS, HEAD_DIM, HEAD_DIM), _F32),
        jax.ShapeDtypeStruct(sq.shape, _F32),
        jax.ShapeDtypeStruct((depth, nb, A_HEADS, HEAD_DIM, HEAD_DIM), _F32),
        jax.ShapeDtypeStruct((depth, nb, B_HEADS, HEAD_DIM, HEAD_DIM), _F32),
    ]
    scratch = [
        pltpu.VMEM((tb + SUBLANES, QKV_WIDTH), _F32),
        pltpu.VMEM((tb, REST_WIDTH), _F32),
        pltpu.VMEM((tb, QKV_WIDTH), _F32),
        pltpu.VMEM((2, tb, LANES), _F32),
        pltpu.VMEM((LANES, tb), _F32),
        pltpu.VMEM((A_HEADS, GROUP, GROUP), _F32),
        pltpu.VMEM((tb, A_WIDTH + B_WIDTH), _F32),
        pltpu.VMEM((2, tb, d_model), _F32),
        pltpu.VMEM((2, tb, d_model), _BF16),
        pltpu.VMEM((2, tb, FFN_COL_BLOCK), _F32),
        pltpu.VMEM((tb, d_ff), _BF16),
        pltpu.VMEM((tb, d_model), _F32),
    ]
    return pl.pallas_call(
        functools.partial(_layer_prompt_body, aliased=prev_states is not None, tb=tb, nt=nt, nblocks=nblocks,
                          d_ff=d_ff, per_step=per_step),
        grid=(nblocks + 1,), in_specs=in_specs, out_specs=out_specs, out_shape=out_shape,
        scratch_shapes=scratch, input_output_aliases=aliases,
        compiler_params=pltpu.CompilerParams(
            dimension_semantics=("arbitrary",), vmem_limit_bytes=VMEM_LIMIT_BYTES),
        name=f"layer_prompt_l{layer}",
    )(*args)


def _sample_pre_body(x_ref, cos_ref, sin_ref, win_ref, convw_ref, alog_ref, dtb_ref, npre_ref, convs_ref,
                     q_ref, k_ref, v_ref, sc_ref, gates_ref, convn_ref):
    nb = x_ref.shape[0]
    hn = _rmsnorm(x_ref[...], npre_ref[...]).astype(_BF16)
    pq = jnp.dot(hn, win_ref[:, 0:QKV_WIDTH], preferred_element_type=_F32)
    pr = jnp.dot(hn, win_ref[:, QKV_WIDTH:IN_WIDTH_PADDED], preferred_element_type=_F32)
    outs = (q_ref, k_ref, v_ref)
    for j in range(QKV_WIDTH // LANES):
        cs = slice(j * LANES, (j + 1) * LANES)
        new = pq[:, cs]
        acc = new * convw_ref[3:4, cs]
        for r in range(CONV_W - 1):
            hist = convs_ref[r, :, cs]
            acc = acc + hist * convw_ref[r:r + 1, cs]
            if r > 0:
                convn_ref[r - 1, :, cs] = hist
        convn_ref[CONV_W - 2, :, cs] = new
        a = _silu(acc)
        if j < 2 * A_HEADS:
            a = a * lax.rsqrt(jnp.sum(a * a, axis=-1, keepdims=True) + L2_EPS)
        if j < A_HEADS:
            a = a * (HEAD_DIM ** -0.5)
        outs[j // A_HEADS][:, _head_cols(0, j % A_HEADS)] = a
    ps = pr[:, SMALL_OFF:SMALL_OFF + LANES]
    beta_all = jax.nn.sigmoid(ps)
    eg_all = jnp.exp(-jnp.exp(alog_ref[...]) * _softplus(ps + dtb_ref[...]))
    cos_full = cos_ref[...]
    sin_signed = sin_ref[...]
    for h in range(A_HEADS):
        sc_ref[:, _head_cols(0, h)] = jnp.broadcast_to(eg_all[:, A_HEADS + h:A_HEADS + h + 1], (nb, LANES))
        sc_ref[:, _head_cols(A_WIDTH, h)] = jnp.broadcast_to(beta_all[:, h:h + 1], (nb, LANES))
        qk = jnp.sum(q_ref[:, _head_cols(0, h)] * k_ref[:, _head_cols(0, h)], axis=-1, keepdims=True)
        sc_ref[:, _head_cols(2 * A_WIDTH, h)] = jnp.broadcast_to(qk, (nb, LANES))
    for h in range(B_HEADS):
        q = _rope(pr[:, _head_cols(BQ_OFF, h)], cos_full, sin_signed)
        k = _rope(pr[:, _head_cols(BK_OFF, h)], cos_full, sin_signed) * (HEAD_DIM ** -0.5)
        q_ref[:, _head_cols(A_WIDTH, h)] = q
        k_ref[:, _head_cols(A_WIDTH, h)] = k
        sc_ref[:, _head_cols(3 * A_WIDTH, h)] = jnp.broadcast_to(
            jnp.sum(q * k, axis=-1, keepdims=True), (nb, LANES))
    v_ref[:, A_WIDTH:A_WIDTH + B_WIDTH] = pr[:, BV_OFF:BG_OFF]
    gates_ref[:, 0:A_WIDTH] = pr[:, Z_OFF:BQ_OFF]
    gates_ref[:, A_WIDTH:A_WIDTH + B_WIDTH] = pr[:, BG_OFF:SMALL_OFF]


def _sample_pre(x, cos_full, sin_signed, win, conv_w, alog, dtb, npre, conv_state, layer):
    nb, d_model = x.shape
    width = A_WIDTH + B_WIDTH
    lspec = functools.partial(_layer_spec, layer=layer)
    full = lambda shape: pl.BlockSpec(tuple(shape), lambda i: (0,) * len(shape))
    shapes = [(nb, width)] * 3 + [(nb, SAMPLE_SCALARS * A_WIDTH), (nb, width), (CONV_W - 1, nb, QKV_WIDTH)]
    return pl.pallas_call(
        _sample_pre_body,
        grid=(1,),
        in_specs=[full((nb, d_model)), full((1, HEAD_DIM)), full((1, HEAD_DIM)),
                  lspec((d_model, IN_WIDTH_PADDED)), lspec((CONV_W, QKV_WIDTH)), lspec((1, LANES)),
                  lspec((1, LANES)), lspec((1, d_model)), lspec((CONV_W - 1, nb, QKV_WIDTH))],
        out_specs=[full(shape) for shape in shapes],
        out_shape=[jax.ShapeDtypeStruct(shape, _F32) for shape in shapes],
        compiler_params=pltpu.CompilerParams(
            dimension_semantics=("arbitrary",), vmem_limit_bytes=VMEM_LIMIT_BYTES),
        name=f"sample_pre_l{layer}",
    )(x, cos_full, sin_signed, win, conv_w, alog, dtb, npre, conv_state)


def _sample_post_body(o_ref, gates_ref, x_ref, gnw_ref, rnw_ref, wout_ref, npost_ref, wgu_ref, wd_ref,
                      fpre_ref, fpost_ref, y_ref, cat_ref, act_ref, *, d_ff):
    for h in range(A_HEADS + B_HEADS):
        cs = _head_cols(0, h)
        w = gnw_ref[...] if h < A_HEADS else rnw_ref[...]
        cat_ref[:, cs] = _gated_head_norm(o_ref[:, cs], w, gates_ref[:, cs]).astype(_BF16)
    m = jnp.dot(cat_ref[...], wout_ref[...], preferred_element_type=_F32)
    h_new = x_ref[...] + _rmsnorm(m, npost_ref[...])
    y_ref[...] = _ffn_rows(h_new, wgu_ref, wd_ref, fpre_ref[...], fpost_ref[...], act_ref, d_ff)


def _sample_post(o, gates, x, gnw, rnw, wout, npost, wgu, wd, fpre, fpost, layer):
    nb, d_model = x.shape
    d_ff = wd.shape[1]
    width = A_WIDTH + B_WIDTH
    lspec = functools.partial(_layer_spec, layer=layer)
    full = lambda shape: pl.BlockSpec(tuple(shape), lambda i: (0,) * len(shape))
    return pl.pallas_call(
        functools.partial(_sample_post_body, d_ff=d_ff),
        grid=(1,),
        in_specs=[full((nb, width)), full((nb, width)), full((nb, d_model)), lspec((1, HEAD_DIM)),
                  lspec((1, HEAD_DIM)), lspec((width, d_model)), lspec((1, d_model)),
                  lspec((d_model, 2 * d_ff)), lspec((d_ff, d_model)), lspec((1, d_model)),
                  lspec((1, d_model))],
        out_specs=full((nb, d_model)),
        out_shape=jax.ShapeDtypeStruct((nb, d_model), _F32),
        scratch_shapes=[pltpu.VMEM((nb, width), _BF16), pltpu.VMEM((nb, d_ff), _BF16)],
        compiler_params=pltpu.CompilerParams(
            dimension_semantics=("arbitrary",), vmem_limit_bytes=VMEM_LIMIT_BYTES),
        name=f"sample_post_l{layer}",
    )(o, gates, x, gnw, rnw, wout, npost, wgu, wd, fpre, fpost)


def _rope_tables(positions):
    half = HEAD_DIM // 2
    inv = ROPE_BASE ** (-np.arange(half, dtype=np.float64) / half)
    ang = np.asarray(positions, dtype=np.float64)[:, None] * inv[None, :]
    cos, sin = np.cos(ang), np.sin(ang)
    return (jnp.asarray(np.concatenate([cos, cos], axis=-1), dtype=_F32),
            jnp.asarray(np.concatenate([-sin, sin], axis=-1), dtype=_F32))


def _rearranged_w_in_body(wt_ref, o_ref):
    small0 = QKV_WIDTH + A_WIDTH
    small1 = small0 + 2 * A_HEADS
    cb = WEIGHT_COL_BLOCK
    for lo in range(0, small0, cb):
        o_ref[:, lo:lo + cb] = wt_ref[lo:lo + cb, :].T.astype(_BF16)
    for lo in range(small0, QKV_WIDTH + SMALL_OFF, cb):
        src = lo + small1 - small0
        o_ref[:, lo:lo + cb] = wt_ref[src:src + cb, :].T.astype(_BF16)
    lane = lax.broadcasted_iota(jnp.int32, (o_ref.shape[0], LANES), 1)
    o_ref[:, QKV_WIDTH + SMALL_OFF:IN_WIDTH_PADDED] = jnp.where(
        lane < 2 * A_HEADS, wt_ref[small0:small0 + LANES, :].T, 0.0).astype(_BF16)


def _rearranged_w_in(w_in):
    depth, d_model, width = w_in.shape
    assert width == IN_WIDTH_PADDED - LANES + 2 * A_HEADS
    return pl.pallas_call(
        _rearranged_w_in_body,
        grid=(depth,),
        in_specs=[pl.BlockSpec((None, width, d_model), lambda l: (l, 0, 0), pipeline_mode=pl.Buffered(1))],
        out_specs=pl.BlockSpec((None, d_model, IN_WIDTH_PADDED), lambda l: (l, 0, 0)),
        out_shape=jax.ShapeDtypeStruct((depth, d_model, IN_WIDTH_PADDED), _BF16),
        compiler_params=pltpu.CompilerParams(
            dimension_semantics=("arbitrary",), vmem_limit_bytes=VMEM_LIMIT_BYTES),
        name="rearranged_w_in",
    )(jnp.swapaxes(w_in, 1, 2))


def kernel(x_prompt, x_sample, state_conv, state_gdn, state_ret, w_in, conv_w, a_log, dt_bias, gdn_norm_w, ret_norm_w, w_out, norm_mix_pre, norm_mix_post, norm_ffn_pre, norm_ffn_post, w_gate_up, w_down):
    depth = w_in.shape[0]
    batch, seq, d_model = x_prompt.shape
    nb, seq_s, _ = x_sample.shape
    assert seq_s == 1

    win = _rearranged_w_in(w_in)
    wout = w_out.astype(_BF16)
    wgu = w_gate_up.astype(_BF16)
    wd = w_down.astype(_BF16)
    alog = jnp.pad(a_log, ((0, 0), (A_HEADS, LANES - 2 * A_HEADS)))[:, None, :]
    dtb = jnp.pad(dt_bias, ((0, 0), (A_HEADS, LANES - 2 * A_HEADS)))[:, None, :]
    gnw = gdn_norm_w[:, None, :]
    rnw = ret_norm_w[:, None, :]
    npre = norm_mix_pre[:, None, :]
    npost = norm_mix_post[:, None, :]
    fpre = norm_ffn_pre[:, None, :]
    fpost = norm_ffn_post[:, None, :]
    cos_p, sin_p = _rope_tables(np.arange(seq))
    cos_s, sin_s = _rope_tables(PAST_LEN + np.arange(seq_s))
    conv_state = jnp.swapaxes(state_conv, 1, 2)

    hp = x_prompt
    hs = x_sample.reshape(nb, d_model)
    convs_p, convs_s = [], []
    states = None
    for l in range(depth):
        sq, sk, sv, sc, gates, conv_s = _sample_pre(hs, cos_s, sin_s, win, conv_w, alog, dtb, npre,
                                                    conv_state, l)
        hp, conv_p, gdn_p, ret_p, so, gdn_s, ret_s = _layer_prompt(
            hp, cos_p, sin_p, win, conv_w, alog, dtb, gnw, rnw, wout, npre, npost, wgu, wd, fpre, fpost,
            sq, sk, sv, sc, state_gdn, state_ret, states, l)
        states = (gdn_p, ret_p, gdn_s, ret_s)
        hs = _sample_post(so, gates, hs, gnw, rnw, wout, npost, wgu, wd, fpre, fpost, l)
        convs_p.append(conv_p)
        convs_s.append(conv_s)
    gdn_p, ret_p, gdn_s, ret_s = states
    return (hp, hs.reshape(nb, seq_s, d_model), jnp.stack(convs_p), gdn_p, ret_p,
            jnp.swapaxes(jnp.stack(convs_s), 1, 2), gdn_s, ret_s)
```

```python
import collections
import functools
import math

import jax
import jax.numpy as jnp
import numpy as np
from jax import lax
from jax.experimental import pallas as pl
from jax.experimental.pallas import tpu as pltpu

HEAD_DIM = 128
A_HEADS = 4
B_HEADS = 4
A_WIDTH = A_HEADS * HEAD_DIM
B_WIDTH = B_HEADS * HEAD_DIM
QKV_WIDTH = 3 * A_WIDTH
CONV_W = 4
CHUNK = 64
ROPE_BASE = 10000.0
EPS = 1e-6
L2_EPS = 1e-6
MASKED_LOG = -1e30
PAST_LEN = 16384
LANES = 128
SUBLANES = 8

REST_WIDTH = A_WIDTH + 4 * B_WIDTH + LANES
Z_OFF = 0
BQ_OFF = A_WIDTH
BK_OFF = BQ_OFF + B_WIDTH
BV_OFF = BK_OFF + B_WIDTH
BG_OFF = BV_OFF + B_WIDTH
SMALL_OFF = BG_OFF + B_WIDTH
IN_WIDTH_PADDED = QKV_WIDTH + REST_WIDTH

GROUP = 256
CHUNKS_PER_GROUP = GROUP // CHUNK
PROMPT_TIME_BLOCK = 256
FFN_COL_BLOCK = 256
FFN_TAIL_PIECES = 0
WEIGHT_COL_BLOCK = 256
SAMPLE_SCALARS = 4
STACKED_STATE_OUTPUTS = (2, 3, 5, 6)
VMEM_LIMIT_BYTES = 56 * 1024 * 1024

_BF16 = jnp.bfloat16
_F32 = jnp.float32


def _dot(a, b):
    return jnp.dot(a.astype(_BF16), b.astype(_BF16), preferred_element_type=_F32)


def _dot_nt(a, b):
    return lax.dot_general(a.astype(_BF16), b.astype(_BF16), (((1,), (1,)), ((), ())),
                           preferred_element_type=_F32)


def _dot_tn(a, b):
    return lax.dot_general(a.astype(_BF16), b.astype(_BF16), (((0,), (0,)), ((), ())),
                           preferred_element_type=_F32)


def _rmsnorm(x, w):
    return x * lax.rsqrt(jnp.mean(x * x, axis=-1, keepdims=True) + EPS) * w


def _silu(x):
    return x * jax.nn.sigmoid(x)


def _softplus(x):
    return jnp.maximum(x, 0.0) + jnp.log1p(jnp.exp(-jnp.abs(x)))


def _log_gamma(h):
    return math.log1p(-(2.0 ** (-5.0 - h)))


def _rope(x, cos_full, sin_signed):
    return x * cos_full + pltpu.roll(x, HEAD_DIM // 2, axis=1) * sin_signed


def _head_cols(base, h):
    return slice(base + h * HEAD_DIM, base + (h + 1) * HEAD_DIM)


def _gated_head_norm(o, w, gate):
    return o * lax.rsqrt(jnp.mean(o * o, axis=-1, keepdims=True) + EPS) * w * _silu(gate)


class _Filler:
    def __init__(self, pieces, reserve=0):
        self._pieces = list(pieces)
        self._next = 0
        self._limit = len(self._pieces) - reserve

    def emit(self, count):
        for _ in range(count):
            if self._next < self._limit:
                self._pieces[self._next]()
                self._next += 1

    def flush(self):
        self._limit = len(self._pieces)
        self.emit(len(self._pieces))


def _fold_rows(m):
    out = m[0:CHUNK]
    for c in range(1, CHUNKS_PER_GROUP):
        out = out + m[c * CHUNK:(c + 1) * CHUNK]
    return out


def _unfold_rows(r, same_chunk):
    return jnp.where(same_chunk, jnp.concatenate([r] * CHUNKS_PER_GROUP, axis=0), 0.0)


def _unit_lower_inverses(a_folded, eye_folded, same_chunk, fill):
    xs = list(a_folded)
    ps = [eye_folded - x for x in xs]
    n = 1
    while n < CHUNK:
        for h in range(len(xs)):
            x_bd = _unfold_rows(xs[h], same_chunk).astype(_BF16)
            if n == 1:
                xs[h] = _dot(xs[h], x_bd)
            elif 2 * n < CHUNK:
                r = _dot(jnp.concatenate([xs[h], ps[h]], axis=0), x_bd)
                xs[h] = r[0:CHUNK]
                ps[h] = ps[h] + r[CHUNK:2 * CHUNK]
            else:
                ps[h] = ps[h] + _dot(ps[h], x_bd)
            fill.emit(h % 2)
        n *= 2
    return [_unfold_rows(p, same_chunk) for p in ps]


def _recurrences_one_group(g, fill, act_ref, pr_ref, small_ref, dec_ref, sg_ref, sr_ref, o_ref, gnw, rnw):
    rows = slice(g * GROUP, (g + 1) * GROUP)
    ri = lax.broadcasted_iota(jnp.int32, (GROUP, GROUP), 0)
    ci = lax.broadcasted_iota(jnp.int32, (GROUP, GROUP), 1)
    same_chunk = (ri // CHUNK) == (ci // CHUNK)
    tril = same_chunk & (ri >= ci)
    off_diag = ri != ci
    rf = lax.broadcasted_iota(jnp.int32, (CHUNK, GROUP), 0)
    cf = lax.broadcasted_iota(jnp.int32, (CHUNK, GROUP), 1)
    eye_folded = jnp.where(rf == cf % CHUNK, 1.0, 0.0).astype(_F32)
    pos = (lax.broadcasted_iota(jnp.int32, (GROUP, HEAD_DIM), 0) % CHUNK).astype(_F32)
    chunk_rows = [slice(c * CHUNK, (c + 1) * CHUNK) for c in range(CHUNKS_PER_GROUP)]

    qs, ks, gccs, a_folded, a_intra, uw_rhs, e_ins = [], [], [], [], [], [], []
    for h in range(A_HEADS):
        q = act_ref[rows, _head_cols(0, h)]
        k = act_ref[rows, _head_cols(A_WIDTH, h)]
        v = act_ref[rows, _head_cols(2 * A_WIDTH, h)]
        beta = jnp.broadcast_to(small_ref[0, rows, h:h + 1], (GROUP, HEAD_DIM))
        gc_col = small_ref[1, rows, A_HEADS + h:A_HEADS + h + 1]
        gcc = jnp.broadcast_to(gc_col, (GROUP, HEAD_DIM))
        decay = dec_ref[h]
        kb = k * beta
        e_in = jnp.exp(gcc)
        kq = _dot_nt(jnp.concatenate([kb, q], axis=0), k)
        a_folded.append(_fold_rows(jnp.where(off_diag, kq[0:GROUP] * decay, 0.0)))
        a_intra.append(kq[GROUP:2 * GROUP] * decay)
        uw_rhs.append(jnp.concatenate([v * beta, kb * e_in], axis=1).astype(_BF16))
        qs.append(q)
        ks.append(k)
        gccs.append(gcc)
        e_ins.append(e_in)
        fill.emit(1)

    r_qe, r_oloc, r_b, r_echunk = [], [], [], []
    for h in range(B_HEADS):
        lg = _log_gamma(h)
        q = pr_ref[rows, _head_cols(BQ_OFF, h)]
        k = pr_ref[rows, _head_cols(BK_OFF, h)]
        v = pr_ref[rows, _head_cols(BV_OFF, h)]
        decay = jnp.exp(jnp.where(tril, (ri - ci).astype(_F32) * lg, MASKED_LOG))
        vb16 = v.astype(_BF16)
        r_oloc.append(_dot(_dot_nt(q, k) * decay, vb16))
        r_qe.append((q * jnp.exp((pos + 1.0) * lg)).astype(_BF16))
        k_out = (k * jnp.exp((CHUNK - 1.0 - pos) * lg)).astype(_BF16)
        r_b.append([_dot_tn(k_out[cr], vb16[cr]) for cr in chunk_rows])
        r_echunk.append(math.exp(CHUNK * lg))

    t_inv = _unit_lower_inverses(a_folded, eye_folded, same_chunk, fill)

    g_lhs, g_oloc, g_b, g_elast = [], [], [], []
    for h in range(A_HEADS):
        uw = _dot(t_inv[h], uw_rhs[h])
        uw16 = uw.astype(_BF16)
        aiuw = _dot(a_intra[h], uw16)
        g_oloc.append(aiuw[:, 0:HEAD_DIM])
        q_eff = qs[h] * e_ins[h] - aiuw[:, HEAD_DIM:2 * HEAD_DIM]
        lhs, bs, elast = [], [], []
        for cr in chunk_rows:
            gcc_c = gccs[h][cr]
            g_last = gcc_c[CHUNK - 1:CHUNK, :]
            k_out = ks[h][cr] * jnp.exp(g_last - gcc_c)
            bg = _dot_tn(k_out, uw16[cr])
            bs.append(bg[:, 0:HEAD_DIM])
            lhs.append(jnp.concatenate([q_eff[cr], bg[:, HEAD_DIM:2 * HEAD_DIM]], axis=0).astype(_BF16))
            elast.append(jnp.exp(g_last))
        g_lhs.append(lhs)
        g_b.append(bs)
        g_elast.append(elast)
        fill.emit(1)

    def normalize_chunk(c):
        out_rows = slice(g * GROUP + c * CHUNK, g * GROUP + (c + 1) * CHUNK)
        for h in range(A_HEADS):
            cs = _head_cols(0, h)
            o_ref[out_rows, cs] = _gated_head_norm(o_ref[out_rows, cs], gnw,
                                                   pr_ref[out_rows, _head_cols(Z_OFF, h)])
        for h in range(B_HEADS):
            cs = _head_cols(A_WIDTH, h)
            o_ref[out_rows, cs] = _gated_head_norm(o_ref[out_rows, cs], rnw,
                                                   pr_ref[out_rows, _head_cols(BG_OFF, h)])

    g_state = [sg_ref[h] for h in range(A_HEADS)]
    r_state = [sr_ref[h] for h in range(B_HEADS)]
    for c, cr in enumerate(chunk_rows):
        out_rows = slice(g * GROUP + c * CHUNK, g * GROUP + (c + 1) * CHUNK)
        for h in range(A_HEADS):
            r = _dot(g_lhs[h][c], g_state[h])
            o_ref[out_rows, _head_cols(0, h)] = r[0:CHUNK] + g_oloc[h][cr]
            g_state[h] = g_state[h] * g_elast[h][c] + g_b[h][c] - r[CHUNK:CHUNK + HEAD_DIM]
        for h in range(B_HEADS):
            o_ref[out_rows, _head_cols(A_WIDTH, h)] = _dot(r_qe[h][cr], r_state[h]) + r_oloc[h][cr]
            r_state[h] = r_state[h] * r_echunk[h] + r_b[h][c]
        if c > 0:
            normalize_chunk(c - 1)
        fill.emit(1)
    normalize_chunk(CHUNKS_PER_GROUP - 1)
    for h in range(A_HEADS):
        sg_ref[h] = g_state[h]
    for h in range(B_HEADS):
        sr_ref[h] = r_state[h]


def _ffn_rows(h, wgu_ref, wd_ref, npre, npost, act_ref, d_ff):
    hn = _rmsnorm(h, npre).astype(_BF16)
    for j in range(d_ff // FFN_COL_BLOCK):
        cg = slice(j * FFN_COL_BLOCK, (j + 1) * FFN_COL_BLOCK)
        cu = slice(d_ff + j * FFN_COL_BLOCK, d_ff + (j + 1) * FFN_COL_BLOCK)
        gate = jnp.dot(hn, wgu_ref[:, cg], preferred_element_type=_F32)
        up = jnp.dot(hn, wgu_ref[:, cu], preferred_element_type=_F32)
        act_ref[:, cg] = (_silu(gate) * up).astype(_BF16)
    f = jnp.dot(act_ref[...], wd_ref[...], preferred_element_type=_F32)
    return h + _rmsnorm(f, npost)


def _ffn_pieces(h_ref, hn_ref, wgu_ref, wd_ref, npost, gate_ref, act_ref, f_ref, y_ref, d_ff):
    d_model = f_ref.shape[1]
    k_split = (d_ff // FFN_COL_BLOCK + 1) // 2 * FFN_COL_BLOCK
    pieces = []

    def gate(j):
        def run():
            cg = slice(j * FFN_COL_BLOCK, (j + 1) * FFN_COL_BLOCK)
            gate_ref[j % 2] = _silu(jnp.dot(hn_ref[...], wgu_ref[:, cg], preferred_element_type=_F32))
        return run

    def up(j):
        def run():
            cg = slice(j * FFN_COL_BLOCK, (j + 1) * FFN_COL_BLOCK)
            cu = slice(d_ff + j * FFN_COL_BLOCK, d_ff + (j + 1) * FFN_COL_BLOCK)
            act_ref[:, cg] = (gate_ref[j % 2] * jnp.dot(hn_ref[...], wgu_ref[:, cu],
                                                        preferred_element_type=_F32)).astype(_BF16)
        return run

    def down(j, first):
        def run():
            cs = slice(j * FFN_COL_BLOCK, (j + 1) * FFN_COL_BLOCK)
            if first:
                f_ref[:, cs] = jnp.dot(act_ref[:, 0:k_split], wd_ref[0:k_split, cs],
                                       preferred_element_type=_F32)
            else:
                f_ref[:, cs] = f_ref[:, cs] + jnp.dot(act_ref[:, k_split:d_ff], wd_ref[k_split:d_ff, cs],
                                                      preferred_element_type=_F32)
        return run

    def finish():
        y_ref[...] = h_ref[...] + _rmsnorm(f_ref[...], npost)

    for j in range(d_ff // FFN_COL_BLOCK):
        pieces += [gate(j), up(j)]
    for j in range(d_model // FFN_COL_BLOCK):
        pieces += [down(j, True), down(j, False)]
    pieces.append(finish)
    return pieces


def _column_of_row(row):
    return jnp.broadcast_to(row, (HEAD_DIM, HEAD_DIM)).T


def _sample_state_units(first_seq, per_step, sq_ref, sk_ref, sv_ref, sc_ref, sgi_ref, sri_ref,
                        so_ref, sgo_ref, sro_ref):
    base = pl.multiple_of((first_seq // SUBLANES) * SUBLANES, SUBLANES)
    rows = pl.ds(base, SUBLANES)
    row_id = lax.broadcasted_iota(jnp.int32, (SUBLANES, HEAD_DIM), 0)
    picks = [row_id == (first_seq - base + j) for j in range(per_step)]

    def pick(block, j):
        return jnp.sum(jnp.where(picks[j], block, 0.0), axis=0, keepdims=True)

    def put(cols, outs):
        block = so_ref[rows, cols]
        for j in range(per_step):
            block = jnp.where(picks[j], outs[j], block)
        so_ref[rows, cols] = block

    def deltanet(h):
        def run():
            cols = _head_cols(0, h)
            q8, k8, v8 = sq_ref[rows, cols], sk_ref[rows, cols], sv_ref[rows, cols]
            eg8 = sc_ref[rows, _head_cols(0, h)]
            bt8 = sc_ref[rows, _head_cols(A_WIDTH, h)]
            qk8 = sc_ref[rows, _head_cols(2 * A_WIDTH, h)]
            kq16 = jnp.concatenate([k8, q8], axis=0).astype(_BF16)
            kqs = [jnp.dot(kq16, sgi_ref[j, h].astype(_BF16), preferred_element_type=_F32)
                   for j in range(per_step)]
            outs = []
            for j in range(per_step):
                eg = pick(eg8, j)
                v_new = pick(bt8, j) * (pick(v8, j) - eg * pick(kqs[j][0:SUBLANES], j))
                outs.append(eg * pick(kqs[j][SUBLANES:2 * SUBLANES], j) + pick(qk8, j) * v_new)
                sgo_ref[j, h] = sgi_ref[j, h] * eg + _column_of_row(pick(k8, j)) * v_new
            put(cols, outs)
        return run

    def retention(h):
        def run():
            gamma = math.exp(_log_gamma(h))
            cols = _head_cols(A_WIDTH, h)
            q8, k8, v8 = sq_ref[rows, cols], sk_ref[rows, cols], sv_ref[rows, cols]
            qk8 = sc_ref[rows, _head_cols(3 * A_WIDTH, h)]
            q16 = q8.astype(_BF16)
            qss = [jnp.dot(q16, sri_ref[j, h].astype(_BF16), preferred_element_type=_F32)
                   for j in range(per_step)]
            outs = []
            for j in range(per_step):
                v1 = pick(v8, j)
                outs.append(gamma * pick(qss[j], j) + pick(qk8, j) * v1)
                sro_ref[j, h] = sri_ref[j, h] * gamma + _column_of_row(pick(k8, j)) * v1
            put(cols, outs)
        return run

    return [deltanet(h) for h in range(A_HEADS)] + [retention(h) for h in range(B_HEADS)]


_LayerInputs = collections.namedtuple("_LayerInputs", [
    "x", "cos", "sin", "win", "convw", "alog", "dtb", "gnw", "rnw", "wout", "npre", "npost", "wgu", "wd",
    "fpre", "fpost", "sq", "sk", "sv", "sc", "sgi", "sri"])
_LayerOutputsAndScratch = collections.namedtuple("_LayerOutputsAndScratch", [
    "y", "conv", "sg", "sr", "so", "sgo", "sro",
    "pq", "pr", "act", "small", "gct", "dec", "o", "h", "hn", "gate", "ffn_act", "f"])


def _layer_prompt_refs(refs, aliased):
    n_in = len(_LayerInputs._fields)
    n_skip = len(STACKED_STATE_OUTPUTS) if aliased else 0
    return _LayerInputs(*refs[:n_in]), _LayerOutputsAndScratch(*refs[n_in + n_skip:])


def _layer_prompt_body(*refs, aliased, tb, nt, nblocks, d_ff, per_step):
    s = pl.program_id(0)

    @pl.when(s < nblocks)
    def _():
        _layer_prompt_step(refs, aliased, tb, nt, d_ff, per_step)

    @pl.when(s == nblocks)
    def _():
        ins, rest = _layer_prompt_refs(refs, aliased)
        prev = lax.rem(s + 1, 2)
        _Filler(_ffn_pieces(rest.h.at[prev], rest.hn.at[prev], ins.wgu, ins.wd, ins.fpost[...],
                            rest.gate, rest.ffn_act, rest.f, rest.y, d_ff)).flush()


def _layer_prompt_step(refs, aliased, tb, nt, d_ff, per_step):
    ins, rest = _layer_prompt_refs(refs, aliased)
    (x_ref, cos_ref, sin_ref, win_ref, convw_ref, alog_ref, dtb_ref, gnw_ref, rnw_ref, wout_ref,
     npre_ref, npost_ref, wgu_ref, wd_ref, fpre_ref, fpost_ref,
     sq_ref, sk_ref, sv_ref, sc_ref, sgi_ref, sri_ref) = ins
    (y_ref, conv_ref, sg_ref, sr_ref, so_ref, sgo_ref, sro_ref,
     pq_ref, pr_ref, act_ref, small_ref, gct_ref, dec_ref, o_ref, h_ref, hn_ref, gate_ref,
     ffn_act_ref, f_ref) = rest
    s = pl.program_id(0)
    t = lax.rem(s, nt)
    pad = SUBLANES

    @pl.when(s == 0)
    def _():
        h_ref[1] = jnp.zeros(h_ref.shape[1:], _F32)
        hn_ref[1] = jnp.zeros(hn_ref.shape[1:], _BF16)
        so_ref[...] = jnp.zeros(so_ref.shape, _F32)

    @pl.when(t == 0)
    def _():
        pq_ref[0:pad, :] = jnp.zeros((pad, QKV_WIDTH), _F32)
        sg_ref[...] = jnp.zeros(sg_ref.shape, _F32)
        sr_ref[...] = jnp.zeros(sr_ref.shape, _F32)

    prev = lax.rem(s + 1, 2)
    fill = _Filler(_ffn_pieces(h_ref.at[prev], hn_ref.at[prev], wgu_ref, wd_ref, fpost_ref[...],
                               gate_ref, ffn_act_ref, f_ref, y_ref, d_ff), reserve=FFN_TAIL_PIECES)
    fill.emit(2)
    sample = _Filler(_sample_state_units(s * per_step, per_step, sq_ref, sk_ref, sv_ref, sc_ref, sgi_ref,
                                         sri_ref, so_ref, sgo_ref, sro_ref))

    x = x_ref[...]
    hn = _rmsnorm(x, npre_ref[...]).astype(_BF16)

    def project(lo, hi):
        return jnp.dot(hn, win_ref[:, lo:hi], preferred_element_type=_F32)

    def project_rest(lo, hi):
        pr_ref[:, lo:hi] = project(QKV_WIDTH + lo, QKV_WIDTH + hi)

    ps = project(QKV_WIDTH + SMALL_OFF, IN_WIDTH_PADDED)
    fill.emit(1)
    beta_all = jax.nn.sigmoid(ps)
    gc = -jnp.exp(alog_ref[...]) * _softplus(ps + dtb_ref[...])
    row_in_chunk = lax.broadcasted_iota(jnp.int32, (tb, LANES), 0) % CHUNK
    shift = 1
    while shift < CHUNK:
        gc = gc + jnp.where(row_in_chunk >= shift, pltpu.roll(gc, shift, axis=0), 0.0)
        shift *= 2
    small_ref[0] = beta_all
    small_ref[1] = gc
    gct_ref[...] = gc.T

    half = B_WIDTH // 2
    cos_full = cos_ref[...]
    sin_signed = sin_ref[...]
    ri = lax.broadcasted_iota(jnp.int32, (GROUP, GROUP), 0)
    ci = lax.broadcasted_iota(jnp.int32, (GROUP, GROUP), 1)
    tril = ((ri // CHUNK) == (ci // CHUNK)) & (ri >= ci)
    for pair in range(B_HEADS // 2):
        project_rest(BQ_OFF + pair * half, BQ_OFF + (pair + 1) * half)
        fill.emit(1)
        project_rest(BK_OFF + pair * half, BK_OFF + (pair + 1) * half)
        fill.emit(1)
        for h in (2 * pair, 2 * pair + 1):
            gc_col = small_ref[1, :, A_HEADS + h:A_HEADS + h + 1]
            gcr = jnp.broadcast_to(gct_ref[A_HEADS + h:A_HEADS + h + 1, :], (GROUP, GROUP))
            diff = jnp.broadcast_to(gc_col, (GROUP, GROUP)) - gcr
            dec_ref[h] = jnp.exp(jnp.where(tril, diff, MASKED_LOG))
            cq = _head_cols(BQ_OFF, h)
            ck = _head_cols(BK_OFF, h)
            pr_ref[:, cq] = _rope(pr_ref[:, cq], cos_full, sin_signed)
            pr_ref[:, ck] = _rope(pr_ref[:, ck], cos_full, sin_signed) * (HEAD_DIM ** -0.5)

    for lo in range(0, QKV_WIDTH, FFN_COL_BLOCK):
        pq_ref[pad:pad + tb, lo:lo + FFN_COL_BLOCK] = project(lo, lo + FFN_COL_BLOCK)
        fill.emit(1)
        sample.emit(1)

    rest = [(lo, lo + FFN_COL_BLOCK) for lo in range(Z_OFF, BQ_OFF, FFN_COL_BLOCK)]
    rest += [(lo, lo + FFN_COL_BLOCK) for lo in range(BV_OFF, SMALL_OFF, FFN_COL_BLOCK)]
    for j in range(QKV_WIDTH // LANES):
        if j % 2 == 0 and rest:
            project_rest(*rest.pop(0))
        fill.emit(1)
        cs = slice(j * LANES, (j + 1) * LANES)
        acc = pq_ref[pad:pad + tb, cs] * convw_ref[3:4, cs]
        for i in range(CONV_W - 1):
            acc = acc + pq_ref[pad - 3 + i:pad - 3 + i + tb, cs] * convw_ref[i:i + 1, cs]
        a = _silu(acc)
        if j < 2 * A_HEADS:
            a = a * lax.rsqrt(jnp.sum(a * a, axis=-1, keepdims=True) + L2_EPS)
        if j < A_HEADS:
            a = a * (HEAD_DIM ** -0.5)
        act_ref[:, cs] = a
    assert not rest

    tail = pq_ref[pad + tb - 3:pad + tb, :]
    conv_ref[...] = tail
    pq_ref[pad - 3:pad, :] = tail

    for g in range(tb // GROUP):
        _recurrences_one_group(g, fill, act_ref, pr_ref, small_ref, dec_ref, sg_ref, sr_ref, o_ref,
                               gnw_ref[...], rnw_ref[...])

    m = jnp.dot(o_ref[...].astype(_BF16), wout_ref[...], preferred_element_type=_F32)
    sample.flush()
    fill.flush()
    h_new = x_ref[...] + _rmsnorm(m, npost_ref[...])
    cur = lax.rem(s, 2)
    h_ref[cur] = h_new
    hn_ref[cur] = _rmsnorm(h_new, fpre_ref[...]).astype(_BF16)
    fill.flush()


def _layer_spec(shape, layer):
    zeros = (0,) * len(shape)
    return pl.BlockSpec((None,) + tuple(shape), lambda i: (layer,) + zeros)


def _layer_prompt(x, cos_full, sin_signed, win, conv_w, alog, dtb, gnw, rnw, wout, npre, npost,
                  wgu, wd, fpre, fpost, sq, sk, sv, sc, state_gdn, state_ret, prev_states, layer):
    batch, seq, d_model = x.shape
    nb = sq.shape[0]
    depth = state_gdn.shape[0]
    d_ff = wd.shape[1]
    tb = min(PROMPT_TIME_BLOCK, seq)
    assert seq % tb == 0 and tb == GROUP and d_ff % FFN_COL_BLOCK == 0
    nt = seq // tb
    nblocks = batch * nt
    assert nb % nblocks == 0 and SUBLANES % (nb // nblocks) == 0
    per_step = nb // nblocks
    lspec = functools.partial(_layer_spec, layer=layer)

    def mixer_block(s):
        return jnp.minimum(s, nblocks - 1)

    def ffn_block(s):
        return jnp.maximum(s - 1, 0)

    full = lambda shape: pl.BlockSpec(tuple(shape), lambda s: (0,) * len(shape))
    state_spec = lambda heads: pl.BlockSpec((None, per_step, heads, HEAD_DIM, HEAD_DIM),
                                            lambda s: (layer, mixer_block(s), 0, 0, 0))

    in_specs = [
        pl.BlockSpec((None, tb, d_model), lambda s: (mixer_block(s) // nt, mixer_block(s) % nt, 0)),
        pl.BlockSpec((tb, HEAD_DIM), lambda s: (mixer_block(s) % nt, 0)),
        pl.BlockSpec((tb, HEAD_DIM), lambda s: (mixer_block(s) % nt, 0)),
        lspec((d_model, IN_WIDTH_PADDED)),
        lspec((CONV_W, QKV_WIDTH)),
        lspec((1, LANES)),
        lspec((1, LANES)),
        lspec((1, HEAD_DIM)),
        lspec((1, HEAD_DIM)),
        lspec((A_WIDTH + B_WIDTH, d_model)),
        lspec((1, d_model)),
        lspec((1, d_model)),
        lspec((d_model, 2 * d_ff)),
        lspec((d_ff, d_model)),
        lspec((1, d_model)),
        lspec((1, d_model)),
        full(sq.shape), full(sk.shape), full(sv.shape), full(sc.shape),
        state_spec(A_HEADS), state_spec(B_HEADS),
    ]
    args = [x, cos_full, sin_signed, win, conv_w, alog, dtb, gnw, rnw, wout, npre, npost, wgu, wd, fpre,
            fpost, sq, sk, sv, sc, state_gdn, state_ret]
    aliases = {}
    if prev_states is not None:
        in_specs += [pl.BlockSpec(memory_space=pl.ANY)] * len(STACKED_STATE_OUTPUTS)
        aliases = {len(args) + i: out for i, out in enumerate(STACKED_STATE_OUTPUTS)}
        args += list(prev_states)
    prompt_state_spec = lambda heads: pl.BlockSpec((None, None, heads, HEAD_DIM, HEAD_DIM),
                                                   lambda s: (layer, mixer_block(s) // nt, 0, 0, 0))
    out_specs = [
        pl.BlockSpec((None, tb, d_model), lambda s: (ffn_block(s) // nt, ffn_block(s) % nt, 0)),
        pl.BlockSpec((None, CONV_W - 1, QKV_WIDTH), lambda s: (mixer_block(s) // nt, 0, 0)),
        prompt_state_spec(A_HEADS), prompt_state_spec(B_HEADS),
        full(sq.shape), state_spec(A_HEADS), state_spec(B_HEADS),
    ]
    out_shape = [
        jax.ShapeDtypeStruct((batch, seq, d_model), _F32),
        jax.ShapeDtypeStruct((batch, CONV_W - 1, QKV_WIDTH), _F32),
        jax.ShapeDtypeStruct((depth, batch, A_HEADS, HEAD_DIM, HEAD_DIM), _F32),
        jax.ShapeDtypeStruct((depth, batch, B_HEADS, HEAD_DIM, HEAD_DIM), _F32),
        jax.ShapeDtypeStruct(sq.shape, _F32),
        jax.ShapeDtypeStruct((depth, nb, A_HEADS, HEAD_DIM, HEAD_DIM), _F32),
        jax.ShapeDtypeStruct((depth, nb, B_HEADS, HEAD_DIM, HEAD_DIM), _F32),
    ]
    scratch = [
        pltpu.VMEM((tb + SUBLANES, QKV_WIDTH), _F32),
        pltpu.VMEM((tb, REST_WIDTH), _F32),
        pltpu.VMEM((tb, QKV_WIDTH), _F32),
        pltpu.VMEM((2, tb, LANES), _F32),
        pltpu.VMEM((LANES, tb), _F32),
        pltpu.VMEM((A_HEADS, GROUP, GROUP), _F32),
        pltpu.VMEM((tb, A_WIDTH + B_WIDTH), _F32),
        pltpu.VMEM((2, tb, d_model), _F32),
        pltpu.VMEM((2, tb, d_model), _BF16),
        pltpu.VMEM((2, tb, FFN_COL_BLOCK), _F32),
        pltpu.VMEM((tb, d_ff), _BF16),
        pltpu.VMEM((tb, d_model), _F32),
    ]
    return pl.pallas_call(
        functools.partial(_layer_prompt_body, aliased=prev_states is not None, tb=tb, nt=nt, nblocks=nblocks,
                          d_ff=d_ff, per_step=per_step),
        grid=(nblocks + 1,), in_specs=in_specs, out_specs=out_specs, out_shape=out_shape,
        scratch_shapes=scratch, input_output_aliases=aliases,
        compiler_params=pltpu.CompilerParams(
            dimension_semantics=("arbitrary",), vmem_limit_bytes=VMEM_LIMIT_BYTES),
        name=f"layer_prompt_l{layer}",
    )(*args)


def _sample_pre_body(x_ref, cos_ref, sin_ref, win_ref, convw_ref, alog_ref, dtb_ref, npre_ref, convs_ref,
                     q_ref, k_ref, v_ref, sc_ref, gates_ref, convn_ref):
    nb = x_ref.shape[0]
    hn = _rmsnorm(x_ref[...], npre_ref[...]).astype(_BF16)
    pq = jnp.dot(hn, win_ref[:, 0:QKV_WIDTH], preferred_element_type=_F32)
    pr = jnp.dot(hn, win_ref[:, QKV_WIDTH:IN_WIDTH_PADDED], preferred_element_type=_F32)
    outs = (q_ref, k_ref, v_ref)
    for j in range(QKV_WIDTH // LANES):
        cs = slice(j * LANES, (j + 1) * LANES)
        new = pq[:, cs]
        acc = new * convw_ref[3:4, cs]
        for r in range(CONV_W - 1):
            hist = convs_ref[r, :, cs]
            acc = acc + hist * convw_ref[r:r + 1, cs]
            if r > 0:
                convn_ref[r - 1, :, cs] = hist
        convn_ref[CONV_W - 2, :, cs] = new
        a = _silu(acc)
        if j < 2 * A_HEADS:
            a = a * lax.rsqrt(jnp.sum(a * a, axis=-1, keepdims=True) + L2_EPS)
        if j < A_HEADS:
            a = a * (HEAD_DIM ** -0.5)
        outs[j // A_HEADS][:, _head_cols(0, j % A_HEADS)] = a
    ps = pr[:, SMALL_OFF:SMALL_OFF + LANES]
    beta_all = jax.nn.sigmoid(ps)
    eg_all = jnp.exp(-jnp.exp(alog_ref[...]) * _softplus(ps + dtb_ref[...]))
    cos_full = cos_ref[...]
    sin_signed = sin_ref[...]
    for h in range(A_HEADS):
        sc_ref[:, _head_cols(0, h)] = jnp.broadcast_to(eg_all[:, A_HEADS + h:A_HEADS + h + 1], (nb, LANES))
        sc_ref[:, _head_cols(A_WIDTH, h)] = jnp.broadcast_to(beta_all[:, h:h + 1], (nb, LANES))
        qk = jnp.sum(q_ref[:, _head_cols(0, h)] * k_ref[:, _head_cols(0, h)], axis=-1, keepdims=True)
        sc_ref[:, _head_cols(2 * A_WIDTH, h)] = jnp.broadcast_to(qk, (nb, LANES))
    for h in range(B_HEADS):
        q = _rope(pr[:, _head_cols(BQ_OFF, h)], cos_full, sin_signed)
        k = _rope(pr[:, _head_cols(BK_OFF, h)], cos_full, sin_signed) * (HEAD_DIM ** -0.5)
        q_ref[:, _head_cols(A_WIDTH, h)] = q
        k_ref[:, _head_cols(A_WIDTH, h)] = k
        sc_ref[:, _head_cols(3 * A_WIDTH, h)] = jnp.broadcast_to(
            jnp.sum(q * k, axis=-1, keepdims=True), (nb, LANES))
    v_ref[:, A_WIDTH:A_WIDTH + B_WIDTH] = pr[:, BV_OFF:BG_OFF]
    gates_ref[:, 0:A_WIDTH] = pr[:, Z_OFF:BQ_OFF]
    gates_ref[:, A_WIDTH:A_WIDTH + B_WIDTH] = pr[:, BG_OFF:SMALL_OFF]


def _sample_pre(x, cos_full, sin_signed, win, conv_w, alog, dtb, npre, conv_state, layer):
    nb, d_model = x.shape
    width = A_WIDTH + B_WIDTH
    lspec = functools.partial(_layer_spec, layer=layer)
    full = lambda shape: pl.BlockSpec(tuple(shape), lambda i: (0,) * len(shape))
    shapes = [(nb, width)] * 3 + [(nb, SAMPLE_SCALARS * A_WIDTH), (nb, width), (CONV_W - 1, nb, QKV_WIDTH)]
    return pl.pallas_call(
        _sample_pre_body,
        grid=(1,),
        in_specs=[full((nb, d_model)), full((1, HEAD_DIM)), full((1, HEAD_DIM)),
                  lspec((d_model, IN_WIDTH_PADDED)), lspec((CONV_W, QKV_WIDTH)), lspec((1, LANES)),
                  lspec((1, LANES)), lspec((1, d_model)), lspec((CONV_W - 1, nb, QKV_WIDTH))],
        out_specs=[full(shape) for shape in shapes],
        out_shape=[jax.ShapeDtypeStruct(shape, _F32) for shape in shapes],
        compiler_params=pltpu.CompilerParams(
            dimension_semantics=("arbitrary",), vmem_limit_bytes=VMEM_LIMIT_BYTES),
        name=f"sample_pre_l{layer}",
    )(x, cos_full, sin_signed, win, conv_w, alog, dtb, npre, conv_state)


def _sample_post_body(o_ref, gates_ref, x_ref, gnw_ref, rnw_ref, wout_ref, npost_ref, wgu_ref, wd_ref,
                      fpre_ref, fpost_ref, y_ref, cat_ref, act_ref, *, d_ff):
    for h in range(A_HEADS + B_HEADS):
        cs = _head_cols(0, h)
        w = gnw_ref[...] if h < A_HEADS else rnw_ref[...]
        cat_ref[:, cs] = _gated_head_norm(o_ref[:, cs], w, gates_ref[:, cs]).astype(_BF16)
    m = jnp.dot(cat_ref[...], wout_ref[...], preferred_element_type=_F32)
    h_new = x_ref[...] + _rmsnorm(m, npost_ref[...])
    y_ref[...] = _ffn_rows(h_new, wgu_ref, wd_ref, fpre_ref[...], fpost_ref[...], act_ref, d_ff)


def _sample_post(o, gates, x, gnw, rnw, wout, npost, wgu, wd, fpre, fpost, layer):
    nb, d_model = x.shape
    d_ff = wd.shape[1]
    width = A_WIDTH + B_WIDTH
    lspec = functools.partial(_layer_spec, layer=layer)
    full = lambda shape: pl.BlockSpec(tuple(shape), lambda i: (0,) * len(shape))
    return pl.pallas_call(
        functools.partial(_sample_post_body, d_ff=d_ff),
        grid=(1,),
        in_specs=[full((nb, width)), full((nb, width)), full((nb, d_model)), lspec((1, HEAD_DIM)),
                  lspec((1, HEAD_DIM)), lspec((width, d_model)), lspec((1, d_model)),
                  lspec((d_model, 2 * d_ff)), lspec((d_ff, d_model)), lspec((1, d_model)),
                  lspec((1, d_model))],
        out_specs=full((nb, d_model)),
        out_shape=jax.ShapeDtypeStruct((nb, d_model), _F32),
        scratch_shapes=[pltpu.VMEM((nb, width), _BF16), pltpu.VMEM((nb, d_ff), _BF16)],
        compiler_params=pltpu.CompilerParams(
            dimension_semantics=("arbitrary",), vmem_limit_bytes=VMEM_LIMIT_BYTES),
        name=f"sample_post_l{layer}",
    )(o, gates, x, gnw, rnw, wout, npost, wgu, wd, fpre, fpost)


def _rope_tables(positions):
    half = HEAD_DIM // 2
    inv = ROPE_BASE ** (-np.arange(half, dtype=np.float64) / half)
    ang = np.asarray(positions, dtype=np.float64)[:, None] * inv[None, :]
    cos, sin = np.cos(ang), np.sin(ang)
    return (jnp.asarray(np.concatenate([cos, cos], axis=-1), dtype=_F32),
            jnp.asarray(np.concatenate([-sin, sin], axis=-1), dtype=_F32))


def _rearranged_w_in_body(wt_ref, o_ref):
    small0 = QKV_WIDTH + A_WIDTH
    small1 = small0 + 2 * A_HEADS
    cb = WEIGHT_COL_BLOCK
    for lo in range(0, small0, cb):
        o_ref[:, lo:lo + cb] = wt_ref[lo:lo + cb, :].T.astype(_BF16)
    for lo in range(small0, QKV_WIDTH + SMALL_OFF, cb):
        src = lo + small1 - small0
        o_ref[:, lo:lo + cb] = wt_ref[src:src + cb, :].T.astype(_BF16)
    lane = lax.broadcasted_iota(jnp.int32, (o_ref.shape[0], LANES), 1)
    o_ref[:, QKV_WIDTH + SMALL_OFF:IN_WIDTH_PADDED] = jnp.where(
        lane < 2 * A_HEADS, wt_ref[small0:small0 + LANES, :].T, 0.0).astype(_BF16)


def _rearranged_w_in(w_in):
    depth, d_model, width = w_in.shape
    assert width == IN_WIDTH_PADDED - LANES + 2 * A_HEADS
    return pl.pallas_call(
        _rearranged_w_in_body,
        grid=(depth,),
        in_specs=[pl.BlockSpec((None, width, d_model), lambda l: (l, 0, 0), pipeline_mode=pl.Buffered(1))],
        out_specs=pl.BlockSpec((None, d_model, IN_WIDTH_PADDED), lambda l: (l, 0, 0)),
        out_shape=jax.ShapeDtypeStruct((depth, d_model, IN_WIDTH_PADDED), _BF16),
        compiler_params=pltpu.CompilerParams(
            dimension_semantics=("arbitrary",), vmem_limit_bytes=VMEM_LIMIT_BYTES),
        name="rearranged_w_in",
    )(jnp.swapaxes(w_in, 1, 2))


def kernel(x_prompt, x_sample, state_conv, state_gdn, state_ret, w_in, conv_w, a_log, dt_bias, gdn_norm_w, ret_norm_w, w_out, norm_mix_pre, norm_mix_post, norm_ffn_pre, norm_ffn_post, w_gate_up, w_down):
    depth = w_in.shape[0]
    batch, seq, d_model = x_prompt.shape
    nb, seq_s, _ = x_sample.shape
    assert seq_s == 1

    win = _rearranged_w_in(w_in)
    wout = w_out.astype(_BF16)
    wgu = w_gate_up.astype(_BF16)
    wd = w_down.astype(_BF16)
    alog = jnp.pad(a_log, ((0, 0), (A_HEADS, LANES - 2 * A_HEADS)))[:, None, :]
    dtb = jnp.pad(dt_bias, ((0, 0), (A_HEADS, LANES - 2 * A_HEADS)))[:, None, :]
    gnw = gdn_norm_w[:, None, :]
    rnw = ret_norm_w[:, None, :]
    npre = norm_mix_pre[:, None, :]
    npost = norm_mix_post[:, None, :]
    fpre = norm_ffn_pre[:, None, :]
    fpost = norm_ffn_post[:, None, :]
    cos_p, sin_p = _rope_tables(np.arange(seq))
    cos_s, sin_s = _rope_tables(PAST_LEN + np.arange(seq_s))
    conv_state = jnp.swapaxes(state_conv, 1, 2)

    hp = x_prompt
    hs = x_sample.reshape(nb, d_model)
    convs_p, convs_s = [], []
    states = None
    for l in range(depth):
        sq, sk, sv, sc, gates, conv_s = _sample_pre(hs, cos_s, sin_s, win, conv_w, alog, dtb, npre,
                                                    conv_state, l)
        hp, conv_p, gdn_p, ret_p, so, gdn_s, ret_s = _layer_prompt(
            hp, cos_p, sin_p, win, conv_w, alog, dtb, gnw, rnw, wout, npre, npost, wgu, wd, fpre, fpost,
            sq, sk, sv, sc, state_gdn, state_ret, states, l)
        states = (gdn_p, ret_p, gdn_s, ret_s)
        hs = _sample_post(so, gates, hs, gnw, rnw, wout, npost, wgu, wd, fpre, fpost, l)
        convs_p.append(conv_p)
        convs_s.append(conv_s)
    gdn_p, ret_p, gdn_s, ret_s = states
    return (hp, hs.reshape(nb, seq_s, d_model), jnp.stack(convs_p), gdn_p, ret_p,
            jnp.swapaxes(jnp.stack(convs_s), 1, 2), gdn_s, ret_s)
```

```python
import collections
import functools
import math

import jax
import jax.numpy as jnp
import numpy as np
from jax import lax
from jax.experimental import pallas as pl
from jax.experimental.pallas import tpu as pltpu

HEAD_DIM = 128
A_HEADS = 4
B_HEADS = 4
A_WIDTH = A_HEADS * HEAD_DIM
B_WIDTH = B_HEADS * HEAD_DIM
QKV_WIDTH = 3 * A_WIDTH
CONV_W = 4
CHUNK = 64
ROPE_BASE = 10000.0
EPS = 1e-6
L2_EPS = 1e-6
MASKED_LOG = -1e30
PAST_LEN = 16384
LANES = 128
SUBLANES = 8

REST_WIDTH = A_WIDTH + 4 * B_WIDTH + LANES
Z_OFF = 0
BQ_OFF = A_WIDTH
BK_OFF = BQ_OFF + B_WIDTH
BV_OFF = BK_OFF + B_WIDTH
BG_OFF = BV_OFF + B_WIDTH
SMALL_OFF = BG_OFF + B_WIDTH
IN_WIDTH_PADDED = QKV_WIDTH + REST_WIDTH

GROUP = 256
CHUNKS_PER_GROUP = GROUP // CHUNK
PROMPT_TIME_BLOCK = 256
FFN_COL_BLOCK = 256
WEIGHT_COL_BLOCK = 256
SAMPLE_SCALARS = 4
STACKED_STATE_OUTPUTS = (2, 3, 5, 6)
VMEM_LIMIT_BYTES = 56 * 1024 * 1024

_BF16 = jnp.bfloat16
_F32 = jnp.float32


def _dot(a, b):
    return jnp.dot(a.astype(_BF16), b.astype(_BF16), preferred_element_type=_F32)


def _dot_nt(a, b):
    return lax.dot_general(a.astype(_BF16), b.astype(_BF16), (((1,), (1,)), ((), ())),
                           preferred_element_type=_F32)


def _dot_tn(a, b):
    return lax.dot_general(a.astype(_BF16), b.astype(_BF16), (((0,), (0,)), ((), ())),
                           preferred_element_type=_F32)


def _rmsnorm(x, w):
    return x * lax.rsqrt(jnp.mean(x * x, axis=-1, keepdims=True) + EPS) * w


def _silu(x):
    return x * jax.nn.sigmoid(x)


def _softplus(x):
    return jnp.maximum(x, 0.0) + jnp.log1p(jnp.exp(-jnp.abs(x)))


def _log_gamma(h):
    return math.log1p(-(2.0 ** (-5.0 - h)))


def _rope(x, cos_full, sin_signed):
    return x * cos_full + pltpu.roll(x, HEAD_DIM // 2, axis=1) * sin_signed


def _head_cols(base, h):
    return slice(base + h * HEAD_DIM, base + (h + 1) * HEAD_DIM)


def _gated_head_norm(o, w, gate):
    return o * lax.rsqrt(jnp.mean(o * o, axis=-1, keepdims=True) + EPS) * w * _silu(gate)


class _Filler:
    def __init__(self, pieces):
        self._pieces = list(pieces)
        self._next = 0

    def emit(self, count):
        for _ in range(count):
            if self._next < len(self._pieces):
                self._pieces[self._next]()
                self._next += 1

    def flush(self):
        self.emit(len(self._pieces))


def _fold_rows(m):
    out = m[0:CHUNK]
    for c in range(1, CHUNKS_PER_GROUP):
        out = out + m[c * CHUNK:(c + 1) * CHUNK]
    return out


def _unfold_rows(r, same_chunk):
    return jnp.where(same_chunk, jnp.concatenate([r] * CHUNKS_PER_GROUP, axis=0), 0.0)


def _unit_lower_inverses(a_folded, eye_folded, same_chunk, fill):
    xs = list(a_folded)
    ps = [eye_folded - x for x in xs]
    n = 1
    while n < CHUNK:
        for h in range(len(xs)):
            x_bd = _unfold_rows(xs[h], same_chunk).astype(_BF16)
            if n == 1:
                xs[h] = _dot(xs[h], x_bd)
            elif 2 * n < CHUNK:
                r = _dot(jnp.concatenate([xs[h], ps[h]], axis=0), x_bd)
                xs[h] = r[0:CHUNK]
                ps[h] = ps[h] + r[CHUNK:2 * CHUNK]
            else:
                ps[h] = ps[h] + _dot(ps[h], x_bd)
            fill.emit(h % 2)
        n *= 2
    return [_unfold_rows(p, same_chunk) for p in ps]


def _recurrences_one_group(g, fill, act_ref, pr_ref, small_ref, dec_ref, sg_ref, sr_ref, o_ref, gnw, rnw):
    rows = slice(g * GROUP, (g + 1) * GROUP)
    ri = lax.broadcasted_iota(jnp.int32, (GROUP, GROUP), 0)
    ci = lax.broadcasted_iota(jnp.int32, (GROUP, GROUP), 1)
    same_chunk = (ri // CHUNK) == (ci // CHUNK)
    tril = same_chunk & (ri >= ci)
    off_diag = ri != ci
    rf = lax.broadcasted_iota(jnp.int32, (CHUNK, GROUP), 0)
    cf = lax.broadcasted_iota(jnp.int32, (CHUNK, GROUP), 1)
    eye_folded = jnp.where(rf == cf % CHUNK, 1.0, 0.0).astype(_F32)
    pos = (lax.broadcasted_iota(jnp.int32, (GROUP, HEAD_DIM), 0) % CHUNK).astype(_F32)
    chunk_rows = [slice(c * CHUNK, (c + 1) * CHUNK) for c in range(CHUNKS_PER_GROUP)]

    qs, ks, gccs, a_folded, a_intra, uw_rhs, e_ins = [], [], [], [], [], [], []
    for h in range(A_HEADS):
        q = act_ref[rows, _head_cols(0, h)]
        k = act_ref[rows, _head_cols(A_WIDTH, h)]
        v = act_ref[rows, _head_cols(2 * A_WIDTH, h)]
        beta = jnp.broadcast_to(small_ref[0, rows, h:h + 1], (GROUP, HEAD_DIM))
        gc_col = small_ref[1, rows, A_HEADS + h:A_HEADS + h + 1]
        gcc = jnp.broadcast_to(gc_col, (GROUP, HEAD_DIM))
        decay = dec_ref[h]
        kb = k * beta
        e_in = jnp.exp(gcc)
        kq = _dot_nt(jnp.concatenate([kb, q], axis=0), k)
        a_folded.append(_fold_rows(jnp.where(off_diag, kq[0:GROUP] * decay, 0.0)))
        a_intra.append(kq[GROUP:2 * GROUP] * decay)
        uw_rhs.append(jnp.concatenate([v * beta, kb * e_in], axis=1).astype(_BF16))
        qs.append(q)
        ks.append(k)
        gccs.append(gcc)
        e_ins.append(e_in)
        fill.emit(1)

    r_qe, r_oloc, r_b, r_echunk = [], [], [], []
    for h in range(B_HEADS):
        lg = _log_gamma(h)
        q = pr_ref[rows, _head_cols(BQ_OFF, h)]
        k = pr_ref[rows, _head_cols(BK_OFF, h)]
        v = pr_ref[rows, _head_cols(BV_OFF, h)]
        decay = jnp.exp(jnp.where(tril, (ri - ci).astype(_F32) * lg, MASKED_LOG))
        vb16 = v.astype(_BF16)
        r_oloc.append(_dot(_dot_nt(q, k) * decay, vb16))
        r_qe.append((q * jnp.exp((pos + 1.0) * lg)).astype(_BF16))
        k_out = (k * jnp.exp((CHUNK - 1.0 - pos) * lg)).astype(_BF16)
        r_b.append([_dot_tn(k_out[cr], vb16[cr]) for cr in chunk_rows])
        r_echunk.append(math.exp(CHUNK * lg))

    t_inv = _unit_lower_inverses(a_folded, eye_folded, same_chunk, fill)

    g_lhs, g_oloc, g_b, g_elast = [], [], [], []
    for h in range(A_HEADS):
        uw = _dot(t_inv[h], uw_rhs[h])
        uw16 = uw.astype(_BF16)
        aiuw = _dot(a_intra[h], uw16)
        g_oloc.append(aiuw[:, 0:HEAD_DIM])
        q_eff = qs[h] * e_ins[h] - aiuw[:, HEAD_DIM:2 * HEAD_DIM]
        lhs, bs, elast = [], [], []
        for cr in chunk_rows:
            gcc_c = gccs[h][cr]
            g_last = gcc_c[CHUNK - 1:CHUNK, :]
            k_out = ks[h][cr] * jnp.exp(g_last - gcc_c)
            bg = _dot_tn(k_out, uw16[cr])
            bs.append(bg[:, 0:HEAD_DIM])
            lhs.append(jnp.concatenate([q_eff[cr], bg[:, HEAD_DIM:2 * HEAD_DIM]], axis=0).astype(_BF16))
            elast.append(jnp.exp(g_last))
        g_lhs.append(lhs)
        g_b.append(bs)
        g_elast.append(elast)
        fill.emit(1)

    def normalize_chunk(c):
        out_rows = slice(g * GROUP + c * CHUNK, g * GROUP + (c + 1) * CHUNK)
        for h in range(A_HEADS):
            cs = _head_cols(0, h)
            o_ref[out_rows, cs] = _gated_head_norm(o_ref[out_rows, cs], gnw,
                                                   pr_ref[out_rows, _head_cols(Z_OFF, h)])
        for h in range(B_HEADS):
            cs = _head_cols(A_WIDTH, h)
            o_ref[out_rows, cs] = _gated_head_norm(o_ref[out_rows, cs], rnw,
                                                   pr_ref[out_rows, _head_cols(BG_OFF, h)])

    g_state = [sg_ref[h] for h in range(A_HEADS)]
    r_state = [sr_ref[h] for h in range(B_HEADS)]
    for c, cr in enumerate(chunk_rows):
        out_rows = slice(g * GROUP + c * CHUNK, g * GROUP + (c + 1) * CHUNK)
        for h in range(A_HEADS):
            r = _dot(g_lhs[h][c], g_state[h])
            o_ref[out_rows, _head_cols(0, h)] = r[0:CHUNK] + g_oloc[h][cr]
            g_state[h] = g_state[h] * g_elast[h][c] + g_b[h][c] - r[CHUNK:CHUNK + HEAD_DIM]
        for h in range(B_HEADS):
            o_ref[out_rows, _head_cols(A_WIDTH, h)] = _dot(r_qe[h][cr], r_state[h]) + r_oloc[h][cr]
            r_state[h] = r_state[h] * r_echunk[h] + r_b[h][c]
        if c > 0:
            normalize_chunk(c - 1)
        fill.emit(1)
    normalize_chunk(CHUNKS_PER_GROUP - 1)
    for h in range(A_HEADS):
        sg_ref[h] = g_state[h]
    for h in range(B_HEADS):
        sr_ref[h] = r_state[h]


def _ffn_rows(h, wgu_ref, wd_ref, npre, npost, act_ref, d_ff):
    hn = _rmsnorm(h, npre).astype(_BF16)
    for j in range(d_ff // FFN_COL_BLOCK):
        cg = slice(j * FFN_COL_BLOCK, (j + 1) * FFN_COL_BLOCK)
        cu = slice(d_ff + j * FFN_COL_BLOCK, d_ff + (j + 1) * FFN_COL_BLOCK)
        gate = jnp.dot(hn, wgu_ref[:, cg], preferred_element_type=_F32)
        up = jnp.dot(hn, wgu_ref[:, cu], preferred_element_type=_F32)
        act_ref[:, cg] = (_silu(gate) * up).astype(_BF16)
    f = jnp.dot(act_ref[...], wd_ref[...], preferred_element_type=_F32)
    return h + _rmsnorm(f, npost)


def _ffn_pieces(h_ref, hn_ref, wgu_ref, wd_ref, npost, gate_ref, act_ref, f_ref, y_ref, d_ff):
    d_model = f_ref.shape[1]
    k_split = (d_ff // FFN_COL_BLOCK + 1) // 2 * FFN_COL_BLOCK
    pieces = []

    def gate(j):
        def run():
            cg = slice(j * FFN_COL_BLOCK, (j + 1) * FFN_COL_BLOCK)
            gate_ref[j % 2] = _silu(jnp.dot(hn_ref[...], wgu_ref[:, cg], preferred_element_type=_F32))
        return run

    def up(j):
        def run():
            cg = slice(j * FFN_COL_BLOCK, (j + 1) * FFN_COL_BLOCK)
            cu = slice(d_ff + j * FFN_COL_BLOCK, d_ff + (j + 1) * FFN_COL_BLOCK)
            act_ref[:, cg] = (gate_ref[j % 2] * jnp.dot(hn_ref[...], wgu_ref[:, cu],
                                                        preferred_element_type=_F32)).astype(_BF16)
        return run

    def down(j, first):
        def run():
            cs = slice(j * FFN_COL_BLOCK, (j + 1) * FFN_COL_BLOCK)
            if first:
                f_ref[:, cs] = jnp.dot(act_ref[:, 0:k_split], wd_ref[0:k_split, cs],
                                       preferred_element_type=_F32)
            else:
                f_ref[:, cs] = f_ref[:, cs] + jnp.dot(act_ref[:, k_split:d_ff], wd_ref[k_split:d_ff, cs],
                                                      preferred_element_type=_F32)
        return run

    def finish():
        y_ref[...] = h_ref[...] + _rmsnorm(f_ref[...], npost)

    for j in range(d_ff // FFN_COL_BLOCK):
        pieces += [gate(j), up(j)]
    for j in range(d_model // FFN_COL_BLOCK):
        pieces += [down(j, True), down(j, False)]
    pieces.append(finish)
    return pieces


def _column_of_row(row):
    return jnp.broadcast_to(row, (HEAD_DIM, HEAD_DIM)).T


def _sample_state_units(first_seq, per_step, sq_ref, sk_ref, sv_ref, sc_ref, sgi_ref, sri_ref,
                        so_ref, sgo_ref, sro_ref):
    base = pl.multiple_of((first_seq // SUBLANES) * SUBLANES, SUBLANES)
    rows = pl.ds(base, SUBLANES)
    row_id = lax.broadcasted_iota(jnp.int32, (SUBLANES, HEAD_DIM), 0)
    picks = [row_id == (first_seq - base + j) for j in range(per_step)]

    def pick(block, j):
        return jnp.sum(jnp.where(picks[j], block, 0.0), axis=0, keepdims=True)

    def put(cols, outs):
        block = so_ref[rows, cols]
        for j in range(per_step):
            block = jnp.where(picks[j], outs[j], block)
        so_ref[rows, cols] = block

    def deltanet(h):
        def run():
            cols = _head_cols(0, h)
            q8, k8, v8 = sq_ref[rows, cols], sk_ref[rows, cols], sv_ref[rows, cols]
            eg8 = sc_ref[rows, _head_cols(0, h)]
            bt8 = sc_ref[rows, _head_cols(A_WIDTH, h)]
            qk8 = sc_ref[rows, _head_cols(2 * A_WIDTH, h)]
            kq16 = jnp.concatenate([k8, q8], axis=0).astype(_BF16)
            kqs = [jnp.dot(kq16, sgi_ref[j, h].astype(_BF16), preferred_element_type=_F32)
                   for j in range(per_step)]
            outs = []
            for j in range(per_step):
                eg = pick(eg8, j)
                v_new = pick(bt8, j) * (pick(v8, j) - eg * pick(kqs[j][0:SUBLANES], j))
                outs.append(eg * pick(kqs[j][SUBLANES:2 * SUBLANES], j) + pick(qk8, j) * v_new)
                sgo_ref[j, h] = sgi_ref[j, h] * eg + _column_of_row(pick(k8, j)) * v_new
            put(cols, outs)
        return run

    def retention(h):
        def run():
            gamma = math.exp(_log_gamma(h))
            cols = _head_cols(A_WIDTH, h)
            q8, k8, v8 = sq_ref[rows, cols], sk_ref[rows, cols], sv_ref[rows, cols]
            qk8 = sc_ref[rows, _head_cols(3 * A_WIDTH, h)]
            q16 = q8.astype(_BF16)
            qss = [jnp.dot(q16, sri_ref[j, h].astype(_BF16), preferred_element_type=_F32)
                   for j in range(per_step)]
            outs = []
            for j in range(per_step):
                v1 = pick(v8, j)
                outs.append(gamma * pick(qss[j], j) + pick(qk8, j) * v1)
                sro_ref[j, h] = sri_ref[j, h] * gamma + _column_of_row(pick(k8, j)) * v1
            put(cols, outs)
        return run

    return [deltanet(h) for h in range(A_HEADS)] + [retention(h) for h in range(B_HEADS)]


_LayerInputs = collections.namedtuple("_LayerInputs", [
    "x", "cos", "sin", "win", "convw", "alog", "dtb", "gnw", "rnw", "wout", "npre", "npost", "wgu", "wd",
    "fpre", "fpost", "sq", "sk", "sv", "sc", "sgi", "sri"])
_LayerOutputsAndScratch = collections.namedtuple("_LayerOutputsAndScratch", [
    "y", "conv", "sg", "sr", "so", "sgo", "sro",
    "pq", "pr", "act", "small", "gct", "dec", "o", "h", "hn", "gate", "ffn_act", "f"])


def _layer_prompt_refs(refs, aliased):
    n_in = len(_LayerInputs._fields)
    n_skip = len(STACKED_STATE_OUTPUTS) if aliased else 0
    return _LayerInputs(*refs[:n_in]), _LayerOutputsAndScratch(*refs[n_in + n_skip:])


def _layer_prompt_body(*refs, aliased, tb, nt, nblocks, d_ff, per_step):
    s = pl.program_id(0)

    @pl.when(s < nblocks)
    def _():
        _layer_prompt_step(refs, aliased, tb, nt, d_ff, per_step)

    @pl.when(s == nblocks)
    def _():
        ins, rest = _layer_prompt_refs(refs, aliased)
        prev = lax.rem(s + 1, 2)
        _Filler(_ffn_pieces(rest.h.at[prev], rest.hn.at[prev], ins.wgu, ins.wd, ins.fpost[...],
                            rest.gate, rest.ffn_act, rest.f, rest.y, d_ff)).flush()


def _layer_prompt_step(refs, aliased, tb, nt, d_ff, per_step):
    ins, rest = _layer_prompt_refs(refs, aliased)
    (x_ref, cos_ref, sin_ref, win_ref, convw_ref, alog_ref, dtb_ref, gnw_ref, rnw_ref, wout_ref,
     npre_ref, npost_ref, wgu_ref, wd_ref, fpre_ref, fpost_ref,
     sq_ref, sk_ref, sv_ref, sc_ref, sgi_ref, sri_ref) = ins
    (y_ref, conv_ref, sg_ref, sr_ref, so_ref, sgo_ref, sro_ref,
     pq_ref, pr_ref, act_ref, small_ref, gct_ref, dec_ref, o_ref, h_ref, hn_ref, gate_ref,
     ffn_act_ref, f_ref) = rest
    s = pl.program_id(0)
    t = lax.rem(s, nt)
    pad = SUBLANES

    @pl.when(s == 0)
    def _():
        h_ref[1] = jnp.zeros(h_ref.shape[1:], _F32)
        hn_ref[1] = jnp.zeros(hn_ref.shape[1:], _BF16)
        so_ref[...] = jnp.zeros(so_ref.shape, _F32)

    @pl.when(t == 0)
    def _():
        pq_ref[0:pad, :] = jnp.zeros((pad, QKV_WIDTH), _F32)
        sg_ref[...] = jnp.zeros(sg_ref.shape, _F32)
        sr_ref[...] = jnp.zeros(sr_ref.shape, _F32)

    prev = lax.rem(s + 1, 2)
    fill = _Filler(_ffn_pieces(h_ref.at[prev], hn_ref.at[prev], wgu_ref, wd_ref, fpost_ref[...],
                               gate_ref, ffn_act_ref, f_ref, y_ref, d_ff))
    fill.emit(2)
    sample = _Filler(_sample_state_units(s * per_step, per_step, sq_ref, sk_ref, sv_ref, sc_ref, sgi_ref,
                                         sri_ref, so_ref, sgo_ref, sro_ref))

    x = x_ref[...]
    hn = _rmsnorm(x, npre_ref[...]).astype(_BF16)

    def project(lo, hi):
        return jnp.dot(hn, win_ref[:, lo:hi], preferred_element_type=_F32)

    def project_rest(lo, hi):
        pr_ref[:, lo:hi] = project(QKV_WIDTH + lo, QKV_WIDTH + hi)

    ps = project(QKV_WIDTH + SMALL_OFF, IN_WIDTH_PADDED)
    fill.emit(1)
    beta_all = jax.nn.sigmoid(ps)
    gc = -jnp.exp(alog_ref[...]) * _softplus(ps + dtb_ref[...])
    row_in_chunk = lax.broadcasted_iota(jnp.int32, (tb, LANES), 0) % CHUNK
    shift = 1
    while shift < CHUNK:
        gc = gc + jnp.where(row_in_chunk >= shift, pltpu.roll(gc, shift, axis=0), 0.0)
        shift *= 2
    small_ref[0] = beta_all
    small_ref[1] = gc
    gct_ref[...] = gc.T

    half = B_WIDTH // 2
    cos_full = cos_ref[...]
    sin_signed = sin_ref[...]
    ri = lax.broadcasted_iota(jnp.int32, (GROUP, GROUP), 0)
    ci = lax.broadcasted_iota(jnp.int32, (GROUP, GROUP), 1)
    tril = ((ri // CHUNK) == (ci // CHUNK)) & (ri >= ci)
    for pair in range(B_HEADS // 2):
        project_rest(BQ_OFF + pair * half, BQ_OFF + (pair + 1) * half)
        fill.emit(1)
        project_rest(BK_OFF + pair * half, BK_OFF + (pair + 1) * half)
        fill.emit(1)
        sample.emit(1)
        for h in (2 * pair, 2 * pair + 1):
            gc_col = small_ref[1, :, A_HEADS + h:A_HEADS + h + 1]
            gcr = jnp.broadcast_to(gct_ref[A_HEADS + h:A_HEADS + h + 1, :], (GROUP, GROUP))
            diff = jnp.broadcast_to(gc_col, (GROUP, GROUP)) - gcr
            dec_ref[h] = jnp.exp(jnp.where(tril, diff, MASKED_LOG))
            cq = _head_cols(BQ_OFF, h)
            ck = _head_cols(BK_OFF, h)
            pr_ref[:, cq] = _rope(pr_ref[:, cq], cos_full, sin_signed)
            pr_ref[:, ck] = _rope(pr_ref[:, ck], cos_full, sin_signed) * (HEAD_DIM ** -0.5)

    for lo in range(0, QKV_WIDTH, FFN_COL_BLOCK):
        pq_ref[pad:pad + tb, lo:lo + FFN_COL_BLOCK] = project(lo, lo + FFN_COL_BLOCK)
        fill.emit(1)
        sample.emit(1)

    rest = [(lo, lo + FFN_COL_BLOCK) for lo in range(Z_OFF, BQ_OFF, FFN_COL_BLOCK)]
    rest += [(lo, lo + FFN_COL_BLOCK) for lo in range(BV_OFF, SMALL_OFF, FFN_COL_BLOCK)]
    for j in range(QKV_WIDTH // LANES):
        if j % 2 == 0 and rest:
            project_rest(*rest.pop(0))
        else:
            fill.emit(1)
        cs = slice(j * LANES, (j + 1) * LANES)
        acc = pq_ref[pad:pad + tb, cs] * convw_ref[3:4, cs]
        for i in range(CONV_W - 1):
            acc = acc + pq_ref[pad - 3 + i:pad - 3 + i + tb, cs] * convw_ref[i:i + 1, cs]
        a = _silu(acc)
        if j < 2 * A_HEADS:
            a = a * lax.rsqrt(jnp.sum(a * a, axis=-1, keepdims=True) + L2_EPS)
        if j < A_HEADS:
            a = a * (HEAD_DIM ** -0.5)
        act_ref[:, cs] = a
    assert not rest

    tail = pq_ref[pad + tb - 3:pad + tb, :]
    conv_ref[...] = tail
    pq_ref[pad - 3:pad, :] = tail

    for g in range(tb // GROUP):
        _recurrences_one_group(g, fill, act_ref, pr_ref, small_ref, dec_ref, sg_ref, sr_ref, o_ref,
                               gnw_ref[...], rnw_ref[...])

    m = jnp.dot(o_ref[...].astype(_BF16), wout_ref[...], preferred_element_type=_F32)
    fill.emit(8)
    sample.flush()
    h_new = x_ref[...] + _rmsnorm(m, npost_ref[...])
    cur = lax.rem(s, 2)
    h_ref[cur] = h_new
    hn_ref[cur] = _rmsnorm(h_new, fpre_ref[...]).astype(_BF16)
    fill.flush()


def _layer_spec(shape, layer):
    zeros = (0,) * len(shape)
    return pl.BlockSpec((None,) + tuple(shape), lambda i: (layer,) + zeros)


def _layer_prompt(x, cos_full, sin_signed, win, conv_w, alog, dtb, gnw, rnw, wout, npre, npost,
                  wgu, wd, fpre, fpost, sq, sk, sv, sc, state_gdn, state_ret, prev_states, layer):
    batch, seq, d_model = x.shape
    nb = sq.shape[0]
    depth = state_gdn.shape[0]
    d_ff = wd.shape[1]
    tb = min(PROMPT_TIME_BLOCK, seq)
    assert seq % tb == 0 and tb == GROUP and d_ff % FFN_COL_BLOCK == 0
    nt = seq // tb
    nblocks = batch * nt
    assert nb % nblocks == 0 and SUBLANES % (nb // nblocks) == 0
    per_step = nb // nblocks
    lspec = functools.partial(_layer_spec, layer=layer)

    def mixer_block(s):
        return jnp.minimum(s, nblocks - 1)

    def ffn_block(s):
        return jnp.maximum(s - 1, 0)

    full = lambda shape: pl.BlockSpec(tuple(shape), lambda s: (0,) * len(shape))
    state_spec = lambda heads: pl.BlockSpec((None, per_step, heads, HEAD_DIM, HEAD_DIM),
                                            lambda s: (layer, mixer_block(s), 0, 0, 0))

    in_specs = [
        pl.BlockSpec((None, tb, d_model), lambda s: (mixer_block(s) // nt, mixer_block(s) % nt, 0)),
        pl.BlockSpec((tb, HEAD_DIM), lambda s: (mixer_block(s) % nt, 0)),
        pl.BlockSpec((tb, HEAD_DIM), lambda s: (mixer_block(s) % nt, 0)),
        lspec((d_model, IN_WIDTH_PADDED)),
        lspec((CONV_W, QKV_WIDTH)),
        lspec((1, LANES)),
        lspec((1, LANES)),
        lspec((1, HEAD_DIM)),
        lspec((1, HEAD_DIM)),
        lspec((A_WIDTH + B_WIDTH, d_model)),
        lspec((1, d_model)),
        lspec((1, d_model)),
        lspec((d_model, 2 * d_ff)),
        lspec((d_ff, d_model)),
        lspec((1, d_model)),
        lspec((1, d_model)),
        full(sq.shape), full(sk.shape), full(sv.shape), full(sc.shape),
        state_spec(A_HEADS), state_spec(B_HEADS),
    ]
    args = [x, cos_full, sin_signed, win, conv_w, alog, dtb, gnw, rnw, wout, npre, npost, wgu, wd, fpre,
            fpost, sq, sk, sv, sc, state_gdn, state_ret]
    aliases = {}
    if prev_states is not None:
        in_specs += [pl.BlockSpec(memory_space=pl.ANY)] * len(STACKED_STATE_OUTPUTS)
        aliases = {len(args) + i: out for i, out in enumerate(STACKED_STATE_OUTPUTS)}
        args += list(prev_states)
    prompt_state_spec = lambda heads: pl.BlockSpec((None, None, heads, HEAD_DIM, HEAD_DIM),
                                                   lambda s: (layer, mixer_block(s) // nt, 0, 0, 0))
    out_specs = [
        pl.BlockSpec((None, tb, d_model), lambda s: (ffn_block(s) // nt, ffn_block(s) % nt, 0)),
        pl.BlockSpec((None, CONV_W - 1, QKV_WIDTH), lambda s: (mixer_block(s) // nt, 0, 0)),
        prompt_state_spec(A_HEADS), prompt_state_spec(B_HEADS),
        full(sq.shape), state_spec(A_HEADS), state_spec(B_HEADS),
    ]
    out_shape = [
        jax.ShapeDtypeStruct((batch, seq, d_model), _F32),
        jax.ShapeDtypeStruct((batch, CONV_W - 1, QKV_WIDTH), _F32),
        jax.ShapeDtypeStruct((depth, batch, A_HEADS, HEAD_DIM, HEAD_DIM), _F32),
        jax.ShapeDtypeStruct((depth, batch, B_HEADS, HEAD_DIM, HEAD_DIM), _F32),
        jax.ShapeDtypeStruct(sq.shape, _F32),
        jax.ShapeDtypeStruct((depth, nb, A_HEADS, HEAD_DIM, HEAD_DIM), _F32),
        jax.ShapeDtypeStruct((depth, nb, B_HEADS, HEAD_DIM, HEAD_DIM), _F32),
    ]
    scratch = [
        pltpu.VMEM((tb + SUBLANES, QKV_WIDTH), _F32),
        pltpu.VMEM((tb, REST_WIDTH), _F32),
        pltpu.VMEM((tb, QKV_WIDTH), _F32),
        pltpu.VMEM((2, tb, LANES), _F32),
        pltpu.VMEM((LANES, tb), _F32),
        pltpu.VMEM((A_HEADS, GROUP, GROUP), _F32),
        pltpu.VMEM((tb, A_WIDTH + B_WIDTH), _F32),
        pltpu.VMEM((2, tb, d_model), _F32),
        pltpu.VMEM((2, tb, d_model), _BF16),
        pltpu.VMEM((2, tb, FFN_COL_BLOCK), _F32),
        pltpu.VMEM((tb, d_ff), _BF16),
        pltpu.VMEM((tb, d_model), _F32),
    ]
    return pl.pallas_call(
        functools.partial(_layer_prompt_body, aliased=prev_states is not None, tb=tb, nt=nt, nblocks=nblocks,
                          d_ff=d_ff, per_step=per_step),
        grid=(nblocks + 1,), in_specs=in_specs, out_specs=out_specs, out_shape=out_shape,
        scratch_shapes=scratch, input_output_aliases=aliases,
        compiler_params=pltpu.CompilerParams(
            dimension_semantics=("arbitrary",), vmem_limit_bytes=VMEM_LIMIT_BYTES),
        name=f"layer_prompt_l{layer}",
    )(*args)


def _sample_pre_body(x_ref, cos_ref, sin_ref, win_ref, convw_ref, alog_ref, dtb_ref, npre_ref, convs_ref,
                     q_ref, k_ref, v_ref, sc_ref, gates_ref, convn_ref):
    nb = x_ref.shape[0]
    hn = _rmsnorm(x_ref[...], npre_ref[...]).astype(_BF16)
    pq = jnp.dot(hn, win_ref[:, 0:QKV_WIDTH], preferred_element_type=_F32)
    pr = jnp.dot(hn, win_ref[:, QKV_WIDTH:IN_WIDTH_PADDED], preferred_element_type=_F32)
    outs = (q_ref, k_ref, v_ref)
    for j in range(QKV_WIDTH // LANES):
        cs = slice(j * LANES, (j + 1) * LANES)
        new = pq[:, cs]
        acc = new * convw_ref[3:4, cs]
        for r in range(CONV_W - 1):
            hist = convs_ref[r, :, cs]
            acc = acc + hist * convw_ref[r:r + 1, cs]
            if r > 0:
                convn_ref[r - 1, :, cs] = hist
        convn_ref[CONV_W - 2, :, cs] = new
        a = _silu(acc)
        if j < 2 * A_HEADS:
            a = a * lax.rsqrt(jnp.sum(a * a, axis=-1, keepdims=True) + L2_EPS)
        if j < A_HEADS:
            a = a * (HEAD_DIM ** -0.5)
        outs[j // A_HEADS][:, _head_cols(0, j % A_HEADS)] = a
    ps = pr[:, SMALL_OFF:SMALL_OFF + LANES]
    beta_all = jax.nn.sigmoid(ps)
    eg_all = jnp.exp(-jnp.exp(alog_ref[...]) * _softplus(ps + dtb_ref[...]))
    cos_full = cos_ref[...]
    sin_signed = sin_ref[...]
    for h in range(A_HEADS):
        sc_ref[:, _head_cols(0, h)] = jnp.broadcast_to(eg_all[:, A_HEADS + h:A_HEADS + h + 1], (nb, LANES))
        sc_ref[:, _head_cols(A_WIDTH, h)] = jnp.broadcast_to(beta_all[:, h:h + 1], (nb, LANES))
        qk = jnp.sum(q_ref[:, _head_cols(0, h)] * k_ref[:, _head_cols(0, h)], axis=-1, keepdims=True)
        sc_ref[:, _head_cols(2 * A_WIDTH, h)] = jnp.broadcast_to(qk, (nb, LANES))
    for h in range(B_HEADS):
        q = _rope(pr[:, _head_cols(BQ_OFF, h)], cos_full, sin_signed)
        k = _rope(pr[:, _head_cols(BK_OFF, h)], cos_full, sin_signed) * (HEAD_DIM ** -0.5)
        q_ref[:, _head_cols(A_WIDTH, h)] = q
        k_ref[:, _head_cols(A_WIDTH, h)] = k
        sc_ref[:, _head_cols(3 * A_WIDTH, h)] = jnp.broadcast_to(
            jnp.sum(q * k, axis=-1, keepdims=True), (nb, LANES))
    v_ref[:, A_WIDTH:A_WIDTH + B_WIDTH] = pr[:, BV_OFF:BG_OFF]
    gates_ref[:, 0:A_WIDTH] = pr[:, Z_OFF:BQ_OFF]
    gates_ref[:, A_WIDTH:A_WIDTH + B_WIDTH] = pr[:, BG_OFF:SMALL_OFF]


def _sample_pre(x, cos_full, sin_signed, win, conv_w, alog, dtb, npre, conv_state, layer):
    nb, d_model = x.shape
    width = A_WIDTH + B_WIDTH
    lspec = functools.partial(_layer_spec, layer=layer)
    full = lambda shape: pl.BlockSpec(tuple(shape), lambda i: (0,) * len(shape))
    shapes = [(nb, width)] * 3 + [(nb, SAMPLE_SCALARS * A_WIDTH), (nb, width), (CONV_W - 1, nb, QKV_WIDTH)]
    return pl.pallas_call(
        _sample_pre_body,
        grid=(1,),
        in_specs=[full((nb, d_model)), full((1, HEAD_DIM)), full((1, HEAD_DIM)),
                  lspec((d_model, IN_WIDTH_PADDED)), lspec((CONV_W, QKV_WIDTH)), lspec((1, LANES)),
                  lspec((1, LANES)), lspec((1, d_model)), lspec((CONV_W - 1, nb, QKV_WIDTH))],
        out_specs=[full(shape) for shape in shapes],
        out_shape=[jax.ShapeDtypeStruct(shape, _F32) for shape in shapes],
        compiler_params=pltpu.CompilerParams(
            dimension_semantics=("arbitrary",), vmem_limit_bytes=VMEM_LIMIT_BYTES),
        name=f"sample_pre_l{layer}",
    )(x, cos_full, sin_signed, win, conv_w, alog, dtb, npre, conv_state)


def _sample_post_body(o_ref, gates_ref, x_ref, gnw_ref, rnw_ref, wout_ref, npost_ref, wgu_ref, wd_ref,
                      fpre_ref, fpost_ref, y_ref, cat_ref, act_ref, *, d_ff):
    for h in range(A_HEADS + B_HEADS):
        cs = _head_cols(0, h)
        w = gnw_ref[...] if h < A_HEADS else rnw_ref[...]
        cat_ref[:, cs] = _gated_head_norm(o_ref[:, cs], w, gates_ref[:, cs]).astype(_BF16)
    m = jnp.dot(cat_ref[...], wout_ref[...], preferred_element_type=_F32)
    h_new = x_ref[...] + _rmsnorm(m, npost_ref[...])
    y_ref[...] = _ffn_rows(h_new, wgu_ref, wd_ref, fpre_ref[...], fpost_ref[...], act_ref, d_ff)


def _sample_post(o, gates, x, gnw, rnw, wout, npost, wgu, wd, fpre, fpost, layer):
    nb, d_model = x.shape
    d_ff = wd.shape[1]
    width = A_WIDTH + B_WIDTH
    lspec = functools.partial(_layer_spec, layer=layer)
    full = lambda shape: pl.BlockSpec(tuple(shape), lambda i: (0,) * len(shape))
    return pl.pallas_call(
        functools.partial(_sample_post_body, d_ff=d_ff),
        grid=(1,),
        in_specs=[full((nb, width)), full((nb, width)), full((nb, d_model)), lspec((1, HEAD_DIM)),
                  lspec((1, HEAD_DIM)), lspec((width, d_model)), lspec((1, d_model)),
                  lspec((d_model, 2 * d_ff)), lspec((d_ff, d_model)), lspec((1, d_model)),
                  lspec((1, d_model))],
        out_specs=full((nb, d_model)),
        out_shape=jax.ShapeDtypeStruct((nb, d_model), _F32),
        scratch_shapes=[pltpu.VMEM((nb, width), _BF16), pltpu.VMEM((nb, d_ff), _BF16)],
        compiler_params=pltpu.CompilerParams(
            dimension_semantics=("arbitrary",), vmem_limit_bytes=VMEM_LIMIT_BYTES),
        name=f"sample_post_l{layer}",
    )(o, gates, x, gnw, rnw, wout, npost, wgu, wd, fpre, fpost)


def _rope_tables(positions):
    half = HEAD_DIM // 2
    inv = ROPE_BASE ** (-np.arange(half, dtype=np.float64) / half)
    ang = np.asarray(positions, dtype=np.float64)[:, None] * inv[None, :]
    cos, sin = np.cos(ang), np.sin(ang)
    return (jnp.asarray(np.concatenate([cos, cos], axis=-1), dtype=_F32),
            jnp.asarray(np.concatenate([-sin, sin], axis=-1), dtype=_F32))


def _rearranged_w_in_body(wt_ref, o_ref):
    small0 = QKV_WIDTH + A_WIDTH
    small1 = small0 + 2 * A_HEADS
    cb = WEIGHT_COL_BLOCK
    for lo in range(0, small0, cb):
        o_ref[:, lo:lo + cb] = wt_ref[lo:lo + cb, :].T.astype(_BF16)
    for lo in range(small0, QKV_WIDTH + SMALL_OFF, cb):
        src = lo + small1 - small0
        o_ref[:, lo:lo + cb] = wt_ref[src:src + cb, :].T.astype(_BF16)
    lane = lax.broadcasted_iota(jnp.int32, (o_ref.shape[0], LANES), 1)
    o_ref[:, QKV_WIDTH + SMALL_OFF:IN_WIDTH_PADDED] = jnp.where(
        lane < 2 * A_HEADS, wt_ref[small0:small0 + LANES, :].T, 0.0).astype(_BF16)


def _rearranged_w_in(w_in):
    depth, d_model, width = w_in.shape
    assert width == IN_WIDTH_PADDED - LANES + 2 * A_HEADS
    return pl.pallas_call(
        _rearranged_w_in_body,
        grid=(depth,),
        in_specs=[pl.BlockSpec((None, width, d_model), lambda l: (l, 0, 0), pipeline_mode=pl.Buffered(1))],
        out_specs=pl.BlockSpec((None, d_model, IN_WIDTH_PADDED), lambda l: (l, 0, 0)),
        out_shape=jax.ShapeDtypeStruct((depth, d_model, IN_WIDTH_PADDED), _BF16),
        compiler_params=pltpu.CompilerParams(
            dimension_semantics=("arbitrary",), vmem_limit_bytes=VMEM_LIMIT_BYTES),
        name="rearranged_w_in",
    )(jnp.swapaxes(w_in, 1, 2))


def kernel(x_prompt, x_sample, state_conv, state_gdn, state_ret, w_in, conv_w, a_log, dt_bias, gdn_norm_w, ret_norm_w, w_out, norm_mix_pre, norm_mix_post, norm_ffn_pre, norm_ffn_post, w_gate_up, w_down):
    depth = w_in.shape[0]
    batch, seq, d_model = x_prompt.shape
    nb, seq_s, _ = x_sample.shape
    assert seq_s == 1

    win = _rearranged_w_in(w_in)
    wout = w_out.astype(_BF16)
    wgu = w_gate_up.astype(_BF16)
    wd = w_down.astype(_BF16)
    alog = jnp.pad(a_log, ((0, 0), (A_HEADS, LANES - 2 * A_HEADS)))[:, None, :]
    dtb = jnp.pad(dt_bias, ((0, 0), (A_HEADS, LANES - 2 * A_HEADS)))[:, None, :]
    gnw = gdn_norm_w[:, None, :]
    rnw = ret_norm_w[:, None, :]
    npre = norm_mix_pre[:, None, :]
    npost = norm_mix_post[:, None, :]
    fpre = norm_ffn_pre[:, None, :]
    fpost = norm_ffn_post[:, None, :]
    cos_p, sin_p = _rope_tables(np.arange(seq))
    cos_s, sin_s = _rope_tables(PAST_LEN + np.arange(seq_s))
    conv_state = jnp.swapaxes(state_conv, 1, 2)

    hp = x_prompt
    hs = x_sample.reshape(nb, d_model)
    convs_p, convs_s = [], []
    states = None
    for l in range(depth):
        sq, sk, sv, sc, gates, conv_s = _sample_pre(hs, cos_s, sin_s, win, conv_w, alog, dtb, npre,
                                                    conv_state, l)
        hp, conv_p, gdn_p, ret_p, so, gdn_s, ret_s = _layer_prompt(
            hp, cos_p, sin_p, win, conv_w, alog, dtb, gnw, rnw, wout, npre, npost, wgu, wd, fpre, fpost,
            sq, sk, sv, sc, state_gdn, state_ret, states, l)
        states = (gdn_p, ret_p, gdn_s, ret_s)
        hs = _sample_post(so, gates, hs, gnw, rnw, wout, npost, wgu, wd, fpre, fpost, l)
        convs_p.append(conv_p)
        convs_s.append(conv_s)
    gdn_p, ret_p, gdn_s, ret_s = states
    return (hp, hs.reshape(nb, seq_s, d_model), jnp.stack(convs_p), gdn_p, ret_p,
            jnp.swapaxes(jnp.stack(convs_s), 1, 2), gdn_s, ret_s)
```

```python
import collections
import functools
import math

import jax
import jax.numpy as jnp
import numpy as np
from jax import lax
from jax.experimental import pallas as pl
from jax.experimental.pallas import tpu as pltpu

HEAD_DIM = 128
A_HEADS = 4
B_HEADS = 4
A_WIDTH = A_HEADS * HEAD_DIM
B_WIDTH = B_HEADS * HEAD_DIM
QKV_WIDTH = 3 * A_WIDTH
CONV_W = 4
CHUNK = 64
ROPE_BASE = 10000.0
EPS = 1e-6
L2_EPS = 1e-6
MASKED_LOG = -1e30
PAST_LEN = 16384
LANES = 128
SUBLANES = 8

REST_WIDTH = A_WIDTH + 4 * B_WIDTH + LANES
Z_OFF = 0
BQ_OFF = A_WIDTH
BK_OFF = BQ_OFF + B_WIDTH
BV_OFF = BK_OFF + B_WIDTH
BG_OFF = BV_OFF + B_WIDTH
SMALL_OFF = BG_OFF + B_WIDTH
IN_WIDTH_PADDED = QKV_WIDTH + REST_WIDTH

GROUP = 256
CHUNKS_PER_GROUP = GROUP // CHUNK
PROMPT_TIME_BLOCK = 256
FFN_COL_BLOCK = 256
WEIGHT_COL_BLOCK = 256
SAMPLE_SCALARS = 4
STACKED_STATE_OUTPUTS = (2, 3, 5, 6)
VMEM_LIMIT_BYTES = 56 * 1024 * 1024

_BF16 = jnp.bfloat16
_F32 = jnp.float32


def _dot(a, b):
    return jnp.dot(a.astype(_BF16), b.astype(_BF16), preferred_element_type=_F32)


def _dot_nt(a, b):
    return lax.dot_general(a.astype(_BF16), b.astype(_BF16), (((1,), (1,)), ((), ())),
                           preferred_element_type=_F32)


def _dot_tn(a, b):
    return lax.dot_general(a.astype(_BF16), b.astype(_BF16), (((0,), (0,)), ((), ())),
                           preferred_element_type=_F32)


def _rmsnorm(x, w):
    return x * lax.rsqrt(jnp.mean(x * x, axis=-1, keepdims=True) + EPS) * w


def _silu(x):
    return x * jax.nn.sigmoid(x)


def _softplus(x):
    return jnp.maximum(x, 0.0) + jnp.log1p(jnp.exp(-jnp.abs(x)))


def _log_gamma(h):
    return math.log1p(-(2.0 ** (-5.0 - h)))


def _rope(x, cos_full, sin_signed):
    return x * cos_full + pltpu.roll(x, HEAD_DIM // 2, axis=1) * sin_signed


def _head_cols(base, h):
    return slice(base + h * HEAD_DIM, base + (h + 1) * HEAD_DIM)


def _gated_head_norm(o, w, gate):
    return o * lax.rsqrt(jnp.mean(o * o, axis=-1, keepdims=True) + EPS) * w * _silu(gate)


class _Filler:
    def __init__(self, pieces):
        self._pieces = list(pieces)
        self._next = 0

    def emit(self, count):
        for _ in range(count):
            if self._next < len(self._pieces):
                self._pieces[self._next]()
                self._next += 1

    def flush(self):
        self.emit(len(self._pieces))


def _fold_rows(m):
    out = m[0:CHUNK]
    for c in range(1, CHUNKS_PER_GROUP):
        out = out + m[c * CHUNK:(c + 1) * CHUNK]
    return out


def _unfold_rows(r, same_chunk):
    return jnp.where(same_chunk, jnp.concatenate([r] * CHUNKS_PER_GROUP, axis=0), 0.0)


def _unit_lower_inverses(a_folded, eye_folded, same_chunk, fill):
    xs = list(a_folded)
    ps = [eye_folded - x for x in xs]
    n = 1
    while n < CHUNK:
        for h in range(len(xs)):
            x_bd = _unfold_rows(xs[h], same_chunk).astype(_BF16)
            if n == 1:
                xs[h] = _dot(xs[h], x_bd)
            elif 2 * n < CHUNK:
                r = _dot(jnp.concatenate([xs[h], ps[h]], axis=0), x_bd)
                xs[h] = r[0:CHUNK]
                ps[h] = ps[h] + r[CHUNK:2 * CHUNK]
            else:
                ps[h] = ps[h] + _dot(ps[h], x_bd)
            fill.emit(h % 2)
        n *= 2
    return [_unfold_rows(p, same_chunk) for p in ps]


def _recurrences_one_group(g, fill, act_ref, pr_ref, small_ref, dec_ref, sg_ref, sr_ref, o_ref, gnw, rnw):
    rows = slice(g * GROUP, (g + 1) * GROUP)
    ri = lax.broadcasted_iota(jnp.int32, (GROUP, GROUP), 0)
    ci = lax.broadcasted_iota(jnp.int32, (GROUP, GROUP), 1)
    same_chunk = (ri // CHUNK) == (ci // CHUNK)
    tril = same_chunk & (ri >= ci)
    off_diag = ri != ci
    rf = lax.broadcasted_iota(jnp.int32, (CHUNK, GROUP), 0)
    cf = lax.broadcasted_iota(jnp.int32, (CHUNK, GROUP), 1)
    eye_folded = jnp.where(rf == cf % CHUNK, 1.0, 0.0).astype(_F32)
    pos = (lax.broadcasted_iota(jnp.int32, (GROUP, HEAD_DIM), 0) % CHUNK).astype(_F32)
    chunk_rows = [slice(c * CHUNK, (c + 1) * CHUNK) for c in range(CHUNKS_PER_GROUP)]

    qs, ks, gccs, a_folded, a_intra, uw_rhs, e_ins = [], [], [], [], [], [], []
    for h in range(A_HEADS):
        q = act_ref[rows, _head_cols(0, h)]
        k = act_ref[rows, _head_cols(A_WIDTH, h)]
        v = act_ref[rows, _head_cols(2 * A_WIDTH, h)]
        beta = jnp.broadcast_to(small_ref[0, rows, h:h + 1], (GROUP, HEAD_DIM))
        gc_col = small_ref[1, rows, A_HEADS + h:A_HEADS + h + 1]
        gcc = jnp.broadcast_to(gc_col, (GROUP, HEAD_DIM))
        decay = dec_ref[h]
        kb = k * beta
        e_in = jnp.exp(gcc)
        kq = _dot_nt(jnp.concatenate([kb, q], axis=0), k)
        a_folded.append(_fold_rows(jnp.where(off_diag, kq[0:GROUP] * decay, 0.0)))
        a_intra.append(kq[GROUP:2 * GROUP] * decay)
        uw_rhs.append(jnp.concatenate([v * beta, kb * e_in], axis=1).astype(_BF16))
        qs.append(q)
        ks.append(k)
        gccs.append(gcc)
        e_ins.append(e_in)
        fill.emit(1)

    r_qe, r_oloc, r_b, r_echunk = [], [], [], []
    for h in range(B_HEADS):
        lg = _log_gamma(h)
        q = pr_ref[rows, _head_cols(BQ_OFF, h)]
        k = pr_ref[rows, _head_cols(BK_OFF, h)]
        v = pr_ref[rows, _head_cols(BV_OFF, h)]
        decay = jnp.exp(jnp.where(tril, (ri - ci).astype(_F32) * lg, MASKED_LOG))
        vb16 = v.astype(_BF16)
        r_oloc.append(_dot(_dot_nt(q, k) * decay, vb16))
        r_qe.append((q * jnp.exp((pos + 1.0) * lg)).astype(_BF16))
        k_out = (k * jnp.exp((CHUNK - 1.0 - pos) * lg)).astype(_BF16)
        r_b.append([_dot_tn(k_out[cr], vb16[cr]) for cr in chunk_rows])
        r_echunk.append(math.exp(CHUNK * lg))

    t_inv = _unit_lower_inverses(a_folded, eye_folded, same_chunk, fill)

    g_lhs, g_oloc, g_b, g_elast = [], [], [], []
    for h in range(A_HEADS):
        uw = _dot(t_inv[h], uw_rhs[h])
        uw16 = uw.astype(_BF16)
        aiuw = _dot(a_intra[h], uw16)
        g_oloc.append(aiuw[:, 0:HEAD_DIM])
        q_eff = qs[h] * e_ins[h] - aiuw[:, HEAD_DIM:2 * HEAD_DIM]
        lhs, bs, elast = [], [], []
        for cr in chunk_rows:
            gcc_c = gccs[h][cr]
            g_last = gcc_c[CHUNK - 1:CHUNK, :]
            k_out = ks[h][cr] * jnp.exp(g_last - gcc_c)
            bg = _dot_tn(k_out, uw16[cr])
            bs.append(bg[:, 0:HEAD_DIM])
            lhs.append(jnp.concatenate([q_eff[cr], bg[:, HEAD_DIM:2 * HEAD_DIM]], axis=0).astype(_BF16))
            elast.append(jnp.exp(g_last))
        g_lhs.append(lhs)
        g_b.append(bs)
        g_elast.append(elast)
        fill.emit(1)

    def normalize_chunk(c):
        out_rows = slice(g * GROUP + c * CHUNK, g * GROUP + (c + 1) * CHUNK)
        for h in range(A_HEADS):
            cs = _head_cols(0, h)
            o_ref[out_rows, cs] = _gated_head_norm(o_ref[out_rows, cs], gnw,
                                                   pr_ref[out_rows, _head_cols(Z_OFF, h)])
        for h in range(B_HEADS):
            cs = _head_cols(A_WIDTH, h)
            o_ref[out_rows, cs] = _gated_head_norm(o_ref[out_rows, cs], rnw,
                                                   pr_ref[out_rows, _head_cols(BG_OFF, h)])

    g_state = [sg_ref[h] for h in range(A_HEADS)]
    r_state = [sr_ref[h] for h in range(B_HEADS)]
    for c, cr in enumerate(chunk_rows):
        out_rows = slice(g * GROUP + c * CHUNK, g * GROUP + (c + 1) * CHUNK)
        for h in range(A_HEADS):
            r = _dot(g_lhs[h][c], g_state[h])
            o_ref[out_rows, _head_cols(0, h)] = r[0:CHUNK] + g_oloc[h][cr]
            g_state[h] = g_state[h] * g_elast[h][c] + g_b[h][c] - r[CHUNK:CHUNK + HEAD_DIM]
        for h in range(B_HEADS):
            o_ref[out_rows, _head_cols(A_WIDTH, h)] = _dot(r_qe[h][cr], r_state[h]) + r_oloc[h][cr]
            r_state[h] = r_state[h] * r_echunk[h] + r_b[h][c]
        if c > 0:
            normalize_chunk(c - 1)
        fill.emit(1)
    normalize_chunk(CHUNKS_PER_GROUP - 1)
    for h in range(A_HEADS):
        sg_ref[h] = g_state[h]
    for h in range(B_HEADS):
        sr_ref[h] = r_state[h]


def _ffn_rows(h, wgu_ref, wd_ref, npre, npost, act_ref, d_ff):
    hn = _rmsnorm(h, npre).astype(_BF16)
    for j in range(d_ff // FFN_COL_BLOCK):
        cg = slice(j * FFN_COL_BLOCK, (j + 1) * FFN_COL_BLOCK)
        cu = slice(d_ff + j * FFN_COL_BLOCK, d_ff + (j + 1) * FFN_COL_BLOCK)
        gate = jnp.dot(hn, wgu_ref[:, cg], preferred_element_type=_F32)
        up = jnp.dot(hn, wgu_ref[:, cu], preferred_element_type=_F32)
        act_ref[:, cg] = (_silu(gate) * up).astype(_BF16)
    f = jnp.dot(act_ref[...], wd_ref[...], preferred_element_type=_F32)
    return h + _rmsnorm(f, npost)


def _ffn_pieces(h_ref, hn_ref, wgu_ref, wd_ref, npost, gate_ref, act_ref, f_ref, y_ref, d_ff):
    d_model = f_ref.shape[1]
    k_split = (d_ff // FFN_COL_BLOCK + 1) // 2 * FFN_COL_BLOCK
    pieces = []

    def gate(j):
        def run():
            cg = slice(j * FFN_COL_BLOCK, (j + 1) * FFN_COL_BLOCK)
            gate_ref[j % 2] = _silu(jnp.dot(hn_ref[...], wgu_ref[:, cg], preferred_element_type=_F32))
        return run

    def up(j):
        def run():
            cg = slice(j * FFN_COL_BLOCK, (j + 1) * FFN_COL_BLOCK)
            cu = slice(d_ff + j * FFN_COL_BLOCK, d_ff + (j + 1) * FFN_COL_BLOCK)
            act_ref[:, cg] = (gate_ref[j % 2] * jnp.dot(hn_ref[...], wgu_ref[:, cu],
                                                        preferred_element_type=_F32)).astype(_BF16)
        return run

    def down(j, first):
        def run():
            cs = slice(j * FFN_COL_BLOCK, (j + 1) * FFN_COL_BLOCK)
            if first:
                f_ref[:, cs] = jnp.dot(act_ref[:, 0:k_split], wd_ref[0:k_split, cs],
                                       preferred_element_type=_F32)
            else:
                f_ref[:, cs] = f_ref[:, cs] + jnp.dot(act_ref[:, k_split:d_ff], wd_ref[k_split:d_ff, cs],
                                                      preferred_element_type=_F32)
        return run

    def finish():
        y_ref[...] = h_ref[...] + _rmsnorm(f_ref[...], npost)

    for j in range(d_ff // FFN_COL_BLOCK):
        pieces += [gate(j), up(j)]
    for j in range(d_model // FFN_COL_BLOCK):
        pieces += [down(j, True), down(j, False)]
    pieces.append(finish)
    return pieces


def _column_of_row(row):
    return jnp.broadcast_to(row, (HEAD_DIM, HEAD_DIM)).T


def _sample_state_units(first_seq, per_step, sq_ref, sk_ref, sv_ref, sc_ref, sgi_ref, sri_ref,
                        so_ref, sgo_ref, sro_ref):
    base = pl.multiple_of((first_seq // SUBLANES) * SUBLANES, SUBLANES)
    rows = pl.ds(base, SUBLANES)
    row_id = lax.broadcasted_iota(jnp.int32, (SUBLANES, HEAD_DIM), 0)
    picks = [row_id == (first_seq - base + j) for j in range(per_step)]

    def pick(block, j):
        return jnp.sum(jnp.where(picks[j], block, 0.0), axis=0, keepdims=True)

    def put(cols, outs):
        block = so_ref[rows, cols]
        for j in range(per_step):
            block = jnp.where(picks[j], outs[j], block)
        so_ref[rows, cols] = block

    def deltanet(h):
        def run():
            cols = _head_cols(0, h)
            q8, k8, v8 = sq_ref[rows, cols], sk_ref[rows, cols], sv_ref[rows, cols]
            eg8 = sc_ref[rows, _head_cols(0, h)]
            bt8 = sc_ref[rows, _head_cols(A_WIDTH, h)]
            qk8 = sc_ref[rows, _head_cols(2 * A_WIDTH, h)]
            kq16 = jnp.concatenate([k8, q8], axis=0).astype(_BF16)
            kqs = [jnp.dot(kq16, sgi_ref[j, h].astype(_BF16), preferred_element_type=_F32)
                   for j in range(per_step)]
            outs = []
            for j in range(per_step):
                eg = pick(eg8, j)
                v_new = pick(bt8, j) * (pick(v8, j) - eg * pick(kqs[j][0:SUBLANES], j))
                outs.append(eg * pick(kqs[j][SUBLANES:2 * SUBLANES], j) + pick(qk8, j) * v_new)
                sgo_ref[j, h] = sgi_ref[j, h] * eg + _column_of_row(pick(k8, j)) * v_new
            put(cols, outs)
        return run

    def retention(h):
        def run():
            gamma = math.exp(_log_gamma(h))
            cols = _head_cols(A_WIDTH, h)
            q8, k8, v8 = sq_ref[rows, cols], sk_ref[rows, cols], sv_ref[rows, cols]
            qk8 = sc_ref[rows, _head_cols(3 * A_WIDTH, h)]
            q16 = q8.astype(_BF16)
            qss = [jnp.dot(q16, sri_ref[j, h].astype(_BF16), preferred_element_type=_F32)
                   for j in range(per_step)]
            outs = []
            for j in range(per_step):
                v1 = pick(v8, j)
                outs.append(gamma * pick(qss[j], j) + pick(qk8, j) * v1)
                sro_ref[j, h] = sri_ref[j, h] * gamma + _column_of_row(pick(k8, j)) * v1
            put(cols, outs)
        return run

    return [deltanet(h) for h in range(A_HEADS)] + [retention(h) for h in range(B_HEADS)]


_LayerInputs = collections.namedtuple("_LayerInputs", [
    "x", "cos", "sin", "win", "convw", "alog", "dtb", "gnw", "rnw", "wout", "npre", "npost", "wgu", "wd",
    "fpre", "fpost", "sq", "sk", "sv", "sc", "sgi", "sri"])
_LayerOutputsAndScratch = collections.namedtuple("_LayerOutputsAndScratch", [
    "y", "conv", "sg", "sr", "so", "sgo", "sro",
    "pq", "pr", "act", "small", "gct", "dec", "o", "h", "hn", "gate", "ffn_act", "f"])


def _layer_prompt_refs(refs, aliased):
    n_in = len(_LayerInputs._fields)
    n_skip = len(STACKED_STATE_OUTPUTS) if aliased else 0
    return _LayerInputs(*refs[:n_in]), _LayerOutputsAndScratch(*refs[n_in + n_skip:])


def _layer_prompt_body(*refs, aliased, tb, nt, nblocks, d_ff, per_step):
    s = pl.program_id(0)

    @pl.when(s < nblocks)
    def _():
        _layer_prompt_step(refs, aliased, tb, nt, d_ff, per_step)

    @pl.when(s == nblocks)
    def _():
        ins, rest = _layer_prompt_refs(refs, aliased)
        prev = lax.rem(s + 1, 2)
        _Filler(_ffn_pieces(rest.h.at[prev], rest.hn.at[prev], ins.wgu, ins.wd, ins.fpost[...],
                            rest.gate, rest.ffn_act, rest.f, rest.y, d_ff)).flush()


def _layer_prompt_step(refs, aliased, tb, nt, d_ff, per_step):
    ins, rest = _layer_prompt_refs(refs, aliased)
    (x_ref, cos_ref, sin_ref, win_ref, convw_ref, alog_ref, dtb_ref, gnw_ref, rnw_ref, wout_ref,
     npre_ref, npost_ref, wgu_ref, wd_ref, fpre_ref, fpost_ref,
     sq_ref, sk_ref, sv_ref, sc_ref, sgi_ref, sri_ref) = ins
    (y_ref, conv_ref, sg_ref, sr_ref, so_ref, sgo_ref, sro_ref,
     pq_ref, pr_ref, act_ref, small_ref, gct_ref, dec_ref, o_ref, h_ref, hn_ref, gate_ref,
     ffn_act_ref, f_ref) = rest
    s = pl.program_id(0)
    t = lax.rem(s, nt)
    pad = SUBLANES

    @pl.when(s == 0)
    def _():
        h_ref[1] = jnp.zeros(h_ref.shape[1:], _F32)
        hn_ref[1] = jnp.zeros(hn_ref.shape[1:], _BF16)
        so_ref[...] = jnp.zeros(so_ref.shape, _F32)

    @pl.when(t == 0)
    def _():
        pq_ref[0:pad, :] = jnp.zeros((pad, QKV_WIDTH), _F32)
        sg_ref[...] = jnp.zeros(sg_ref.shape, _F32)
        sr_ref[...] = jnp.zeros(sr_ref.shape, _F32)

    prev = lax.rem(s + 1, 2)
    fill = _Filler(_ffn_pieces(h_ref.at[prev], hn_ref.at[prev], wgu_ref, wd_ref, fpost_ref[...],
                               gate_ref, ffn_act_ref, f_ref, y_ref, d_ff))
    fill.emit(3)
    sample = _Filler(_sample_state_units(s * per_step, per_step, sq_ref, sk_ref, sv_ref, sc_ref, sgi_ref,
                                         sri_ref, so_ref, sgo_ref, sro_ref))

    x = x_ref[...]
    hn = _rmsnorm(x, npre_ref[...]).astype(_BF16)

    def project(lo, hi):
        return jnp.dot(hn, win_ref[:, lo:hi], preferred_element_type=_F32)

    def project_rest(lo, hi):
        pr_ref[:, lo:hi] = project(QKV_WIDTH + lo, QKV_WIDTH + hi)

    ps = project(QKV_WIDTH + SMALL_OFF, IN_WIDTH_PADDED)
    beta_all = jax.nn.sigmoid(ps)
    gc = -jnp.exp(alog_ref[...]) * _softplus(ps + dtb_ref[...])
    row_in_chunk = lax.broadcasted_iota(jnp.int32, (tb, LANES), 0) % CHUNK
    shift = 1
    while shift < CHUNK:
        gc = gc + jnp.where(row_in_chunk >= shift, pltpu.roll(gc, shift, axis=0), 0.0)
        shift *= 2
    small_ref[0] = beta_all
    small_ref[1] = gc
    gct_ref[...] = gc.T

    half = B_WIDTH // 2
    cos_full = cos_ref[...]
    sin_signed = sin_ref[...]
    ri = lax.broadcasted_iota(jnp.int32, (GROUP, GROUP), 0)
    ci = lax.broadcasted_iota(jnp.int32, (GROUP, GROUP), 1)
    tril = ((ri // CHUNK) == (ci // CHUNK)) & (ri >= ci)
    for pair in range(B_HEADS // 2):
        project_rest(BQ_OFF + pair * half, BQ_OFF + (pair + 1) * half)
        fill.emit(1)
        project_rest(BK_OFF + pair * half, BK_OFF + (pair + 1) * half)
        fill.emit(1)
        sample.emit(1)
        for h in (2 * pair, 2 * pair + 1):
            gc_col = small_ref[1, :, A_HEADS + h:A_HEADS + h + 1]
            gcr = jnp.broadcast_to(gct_ref[A_HEADS + h:A_HEADS + h + 1, :], (GROUP, GROUP))
            diff = jnp.broadcast_to(gc_col, (GROUP, GROUP)) - gcr
            dec_ref[h] = jnp.exp(jnp.where(tril, diff, MASKED_LOG))
            cq = _head_cols(BQ_OFF, h)
            ck = _head_cols(BK_OFF, h)
            pr_ref[:, cq] = _rope(pr_ref[:, cq], cos_full, sin_signed)
            pr_ref[:, ck] = _rope(pr_ref[:, ck], cos_full, sin_signed) * (HEAD_DIM ** -0.5)

    for lo in range(0, QKV_WIDTH, FFN_COL_BLOCK):
        pq_ref[pad:pad + tb, lo:lo + FFN_COL_BLOCK] = project(lo, lo + FFN_COL_BLOCK)
        fill.emit(1)
        sample.emit(1)

    rest = [(lo, lo + FFN_COL_BLOCK) for lo in range(Z_OFF, BQ_OFF, FFN_COL_BLOCK)]
    rest += [(lo, lo + FFN_COL_BLOCK) for lo in range(BV_OFF, SMALL_OFF, FFN_COL_BLOCK)]
    for j in range(QKV_WIDTH // LANES):
        if j % 2 == 0 and rest:
            project_rest(*rest.pop(0))
        else:
            fill.emit(1)
        cs = slice(j * LANES, (j + 1) * LANES)
        acc = pq_ref[pad:pad + tb, cs] * convw_ref[3:4, cs]
        for i in range(CONV_W - 1):
            acc = acc + pq_ref[pad - 3 + i:pad - 3 + i + tb, cs] * convw_ref[i:i + 1, cs]
        a = _silu(acc)
        if j < 2 * A_HEADS:
            a = a * lax.rsqrt(jnp.sum(a * a, axis=-1, keepdims=True) + L2_EPS)
        if j < A_HEADS:
            a = a * (HEAD_DIM ** -0.5)
        act_ref[:, cs] = a
    assert not rest

    tail = pq_ref[pad + tb - 3:pad + tb, :]
    conv_ref[...] = tail
    pq_ref[pad - 3:pad, :] = tail

    for g in range(tb // GROUP):
        _recurrences_one_group(g, fill, act_ref, pr_ref, small_ref, dec_ref, sg_ref, sr_ref, o_ref,
                               gnw_ref[...], rnw_ref[...])

    m = jnp.dot(o_ref[...].astype(_BF16), wout_ref[...], preferred_element_type=_F32)
    fill.emit(8)
    sample.flush()
    h_new = x_ref[...] + _rmsnorm(m, npost_ref[...])
    cur = lax.rem(s, 2)
    h_ref[cur] = h_new
    hn_ref[cur] = _rmsnorm(h_new, fpre_ref[...]).astype(_BF16)
    fill.flush()


def _layer_spec(shape, layer):
    zeros = (0,) * len(shape)
    return pl.BlockSpec((None,) + tuple(shape), lambda i: (layer,) + zeros)


def _layer_prompt(x, cos_full, sin_signed, win, conv_w, alog, dtb, gnw, rnw, wout, npre, npost,
                  wgu, wd, fpre, fpost, sq, sk, sv, sc, state_gdn, state_ret, prev_states, layer):
    batch, seq, d_model = x.shape
    nb = sq.shape[0]
    depth = state_gdn.shape[0]
    d_ff = wd.shape[1]
    tb = min(PROMPT_TIME_BLOCK, seq)
    assert seq % tb == 0 and tb == GROUP and d_ff % FFN_COL_BLOCK == 0
    nt = seq // tb
    nblocks = batch * nt
    assert nb % nblocks == 0 and SUBLANES % (nb // nblocks) == 0
    per_step = nb // nblocks
    lspec = functools.partial(_layer_spec, layer=layer)

    def mixer_block(s):
        return jnp.minimum(s, nblocks - 1)

    def ffn_block(s):
        return jnp.maximum(s - 1, 0)

    full = lambda shape: pl.BlockSpec(tuple(shape), lambda s: (0,) * len(shape))
    state_spec = lambda heads: pl.BlockSpec((None, per_step, heads, HEAD_DIM, HEAD_DIM),
                                            lambda s: (layer, mixer_block(s), 0, 0, 0))

    in_specs = [
        pl.BlockSpec((None, tb, d_model), lambda s: (mixer_block(s) // nt, mixer_block(s) % nt, 0)),
        pl.BlockSpec((tb, HEAD_DIM), lambda s: (mixer_block(s) % nt, 0)),
        pl.BlockSpec((tb, HEAD_DIM), lambda s: (mixer_block(s) % nt, 0)),
        lspec((d_model, IN_WIDTH_PADDED)),
        lspec((CONV_W, QKV_WIDTH)),
        lspec((1, LANES)),
        lspec((1, LANES)),
        lspec((1, HEAD_DIM)),
        lspec((1, HEAD_DIM)),
        lspec((A_WIDTH + B_WIDTH, d_model)),
        lspec((1, d_model)),
        lspec((1, d_model)),
        lspec((d_model, 2 * d_ff)),
        lspec((d_ff, d_model)),
        lspec((1, d_model)),
        lspec((1, d_model)),
        full(sq.shape), full(sk.shape), full(sv.shape), full(sc.shape),
        state_spec(A_HEADS), state_spec(B_HEADS),
    ]
    args = [x, cos_full, sin_signed, win, conv_w, alog, dtb, gnw, rnw, wout, npre, npost, wgu, wd, fpre,
            fpost, sq, sk, sv, sc, state_gdn, state_ret]
    aliases = {}
    if prev_states is not None:
        in_specs += [pl.BlockSpec(memory_space=pl.ANY)] * len(STACKED_STATE_OUTPUTS)
        aliases = {len(args) + i: out for i, out in enumerate(STACKED_STATE_OUTPUTS)}
        args += list(prev_states)
    prompt_state_spec = lambda heads: pl.BlockSpec((None, None, heads, HEAD_DIM, HEAD_DIM),
                                                   lambda s: (layer, mixer_block(s) // nt, 0, 0, 0))
    out_specs = [
        pl.BlockSpec((None, tb, d_model), lambda s: (ffn_block(s) // nt, ffn_block(s) % nt, 0)),
        pl.BlockSpec((None, CONV_W - 1, QKV_WIDTH), lambda s: (mixer_block(s) // nt, 0, 0)),
        prompt_state_spec(A_HEADS), prompt_state_spec(B_HEADS),
        full(sq.shape), state_spec(A_HEADS), state_spec(B_HEADS),
    ]
    out_shape = [
        jax.ShapeDtypeStruct((batch, seq, d_model), _F32),
        jax.ShapeDtypeStruct((batch, CONV_W - 1, QKV_WIDTH), _F32),
        jax.ShapeDtypeStruct((depth, batch, A_HEADS, HEAD_DIM, HEAD_DIM), _F32),
        jax.ShapeDtypeStruct((depth, batch, B_HEADS, HEAD_DIM, HEAD_DIM), _F32),
        jax.ShapeDtypeStruct(sq.shape, _F32),
        jax.ShapeDtypeStruct((depth, nb, A_HEADS, HEAD_DIM, HEAD_DIM), _F32),
        jax.ShapeDtypeStruct((depth, nb, B_HEADS, HEAD_DIM, HEAD_DIM), _F32),
    ]
    scratch = [
        pltpu.VMEM((tb + SUBLANES, QKV_WIDTH), _F32),
        pltpu.VMEM((tb, REST_WIDTH), _F32),
        pltpu.VMEM((tb, QKV_WIDTH), _F32),
        pltpu.VMEM((2, tb, LANES), _F32),
        pltpu.VMEM((LANES, tb), _F32),
        pltpu.VMEM((A_HEADS, GROUP, GROUP), _F32),
        pltpu.VMEM((tb, A_WIDTH + B_WIDTH), _F32),
        pltpu.VMEM((2, tb, d_model), _F32),
        pltpu.VMEM((2, tb, d_model), _BF16),
        pltpu.VMEM((2, tb, FFN_COL_BLOCK), _F32),
        pltpu.VMEM((tb, d_ff), _BF16),
        pltpu.VMEM((tb, d_model), _F32),
    ]
    return pl.pallas_call(
        functools.partial(_layer_prompt_body, aliased=prev_states is not None, tb=tb, nt=nt, nblocks=nblocks,
                          d_ff=d_ff, per_step=per_step),
        grid=(nblocks + 1,), in_specs=in_specs, out_specs=out_specs, out_shape=out_shape,
        scratch_shapes=scratch, input_output_aliases=aliases,
        compiler_params=pltpu.CompilerParams(
            dimension_semantics=("arbitrary",), vmem_limit_bytes=VMEM_LIMIT_BYTES),
        name=f"layer_prompt_l{layer}",
    )(*args)


def _sample_pre_body(x_ref, cos_ref, sin_ref, win_ref, convw_ref, alog_ref, dtb_ref, npre_ref, convs_ref,
                     q_ref, k_ref, v_ref, sc_ref, gates_ref, convn_ref):
    nb = x_ref.shape[0]
    hn = _rmsnorm(x_ref[...], npre_ref[...]).astype(_BF16)
    pq = jnp.dot(hn, win_ref[:, 0:QKV_WIDTH], preferred_element_type=_F32)
    pr = jnp.dot(hn, win_ref[:, QKV_WIDTH:IN_WIDTH_PADDED], preferred_element_type=_F32)
    outs = (q_ref, k_ref, v_ref)
    for j in range(QKV_WIDTH // LANES):
        cs = slice(j * LANES, (j + 1) * LANES)
        new = pq[:, cs]
        acc = new * convw_ref[3:4, cs]
        for r in range(CONV_W - 1):
            hist = convs_ref[r, :, cs]
            acc = acc + hist * convw_ref[r:r + 1, cs]
            if r > 0:
                convn_ref[r - 1, :, cs] = hist
        convn_ref[CONV_W - 2, :, cs] = new
        a = _silu(acc)
        if j < 2 * A_HEADS:
            a = a * lax.rsqrt(jnp.sum(a * a, axis=-1, keepdims=True) + L2_EPS)
        if j < A_HEADS:
            a = a * (HEAD_DIM ** -0.5)
        outs[j // A_HEADS][:, _head_cols(0, j % A_HEADS)] = a
    ps = pr[:, SMALL_OFF:SMALL_OFF + LANES]
    beta_all = jax.nn.sigmoid(ps)
    eg_all = jnp.exp(-jnp.exp(alog_ref[...]) * _softplus(ps + dtb_ref[...]))
    cos_full = cos_ref[...]
    sin_signed = sin_ref[...]
    for h in range(A_HEADS):
        sc_ref[:, _head_cols(0, h)] = jnp.broadcast_to(eg_all[:, A_HEADS + h:A_HEADS + h + 1], (nb, LANES))
        sc_ref[:, _head_cols(A_WIDTH, h)] = jnp.broadcast_to(beta_all[:, h:h + 1], (nb, LANES))
        qk = jnp.sum(q_ref[:, _head_cols(0, h)] * k_ref[:, _head_cols(0, h)], axis=-1, keepdims=True)
        sc_ref[:, _head_cols(2 * A_WIDTH, h)] = jnp.broadcast_to(qk, (nb, LANES))
    for h in range(B_HEADS):
        q = _rope(pr[:, _head_cols(BQ_OFF, h)], cos_full, sin_signed)
        k = _rope(pr[:, _head_cols(BK_OFF, h)], cos_full, sin_signed) * (HEAD_DIM ** -0.5)
        q_ref[:, _head_cols(A_WIDTH, h)] = q
        k_ref[:, _head_cols(A_WIDTH, h)] = k
        sc_ref[:, _head_cols(3 * A_WIDTH, h)] = jnp.broadcast_to(
            jnp.sum(q * k, axis=-1, keepdims=True), (nb, LANES))
    v_ref[:, A_WIDTH:A_WIDTH + B_WIDTH] = pr[:, BV_OFF:BG_OFF]
    gates_ref[:, 0:A_WIDTH] = pr[:, Z_OFF:BQ_OFF]
    gates_ref[:, A_WIDTH:A_WIDTH + B_WIDTH] = pr[:, BG_OFF:SMALL_OFF]


def _sample_pre(x, cos_full, sin_signed, win, conv_w, alog, dtb, npre, conv_state, layer):
    nb, d_model = x.shape
    width = A_WIDTH + B_WIDTH
    lspec = functools.partial(_layer_spec, layer=layer)
    full = lambda shape: pl.BlockSpec(tuple(shape), lambda i: (0,) * len(shape))
    shapes = [(nb, width)] * 3 + [(nb, SAMPLE_SCALARS * A_WIDTH), (nb, width), (CONV_W - 1, nb, QKV_WIDTH)]
    return pl.pallas_call(
        _sample_pre_body,
        grid=(1,),
        in_specs=[full((nb, d_model)), full((1, HEAD_DIM)), full((1, HEAD_DIM)),
                  lspec((d_model, IN_WIDTH_PADDED)), lspec((CONV_W, QKV_WIDTH)), lspec((1, LANES)),
                  lspec((1, LANES)), lspec((1, d_model)), lspec((CONV_W - 1, nb, QKV_WIDTH))],
        out_specs=[full(shape) for shape in shapes],
        out_shape=[jax.ShapeDtypeStruct(shape, _F32) for shape in shapes],
        compiler_params=pltpu.CompilerParams(
            dimension_semantics=("arbitrary",), vmem_limit_bytes=VMEM_LIMIT_BYTES),
        name=f"sample_pre_l{layer}",
    )(x, cos_full, sin_signed, win, conv_w, alog, dtb, npre, conv_state)


def _sample_post_body(o_ref, gates_ref, x_ref, gnw_ref, rnw_ref, wout_ref, npost_ref, wgu_ref, wd_ref,
                      fpre_ref, fpost_ref, y_ref, cat_ref, act_ref, *, d_ff):
    for h in range(A_HEADS + B_HEADS):
        cs = _head_cols(0, h)
        w = gnw_ref[...] if h < A_HEADS else rnw_ref[...]
        cat_ref[:, cs] = _gated_head_norm(o_ref[:, cs], w, gates_ref[:, cs]).astype(_BF16)
    m = jnp.dot(cat_ref[...], wout_ref[...], preferred_element_type=_F32)
    h_new = x_ref[...] + _rmsnorm(m, npost_ref[...])
    y_ref[...] = _ffn_rows(h_new, wgu_ref, wd_ref, fpre_ref[...], fpost_ref[...], act_ref, d_ff)


def _sample_post(o, gates, x, gnw, rnw, wout, npost, wgu, wd, fpre, fpost, layer):
    nb, d_model = x.shape
    d_ff = wd.shape[1]
    width = A_WIDTH + B_WIDTH
    lspec = functools.partial(_layer_spec, layer=layer)
    full = lambda shape: pl.BlockSpec(tuple(shape), lambda i: (0,) * len(shape))
    return pl.pallas_call(
        functools.partial(_sample_post_body, d_ff=d_ff),
        grid=(1,),
        in_specs=[full((nb, width)), full((nb, width)), full((nb, d_model)), lspec((1, HEAD_DIM)),
                  lspec((1, HEAD_DIM)), lspec((width, d_model)), lspec((1, d_model)),
                  lspec((d_model, 2 * d_ff)), lspec((d_ff, d_model)), lspec((1, d_model)),
                  lspec((1, d_model))],
        out_specs=full((nb, d_model)),
        out_shape=jax.ShapeDtypeStruct((nb, d_model), _F32),
        scratch_shapes=[pltpu.VMEM((nb, width), _BF16), pltpu.VMEM((nb, d_ff), _BF16)],
        compiler_params=pltpu.CompilerParams(
            dimension_semantics=("arbitrary",), vmem_limit_bytes=VMEM_LIMIT_BYTES),
        name=f"sample_post_l{layer}",
    )(o, gates, x, gnw, rnw, wout, npost, wgu, wd, fpre, fpost)


def _rope_tables(positions):
    half = HEAD_DIM // 2
    inv = ROPE_BASE ** (-np.arange(half, dtype=np.float64) / half)
    ang = np.asarray(positions, dtype=np.float64)[:, None] * inv[None, :]
    cos, sin = np.cos(ang), np.sin(ang)
    return (jnp.asarray(np.concatenate([cos, cos], axis=-1), dtype=_F32),
            jnp.asarray(np.concatenate([-sin, sin], axis=-1), dtype=_F32))


def _rearranged_w_in_body(wt_ref, o_ref):
    small0 = QKV_WIDTH + A_WIDTH
    small1 = small0 + 2 * A_HEADS
    cb = WEIGHT_COL_BLOCK
    for lo in range(0, small0, cb):
        o_ref[:, lo:lo + cb] = wt_ref[lo:lo + cb, :].T.astype(_BF16)
    for lo in range(small0, QKV_WIDTH + SMALL_OFF, cb):
        src = lo + small1 - small0
        o_ref[:, lo:lo + cb] = wt_ref[src:src + cb, :].T.astype(_BF16)
    lane = lax.broadcasted_iota(jnp.int32, (o_ref.shape[0], LANES), 1)
    o_ref[:, QKV_WIDTH + SMALL_OFF:IN_WIDTH_PADDED] = jnp.where(
        lane < 2 * A_HEADS, wt_ref[small0:small0 + LANES, :].T, 0.0).astype(_BF16)


def _rearranged_w_in(w_in):
    depth, d_model, width = w_in.shape
    assert width == IN_WIDTH_PADDED - LANES + 2 * A_HEADS
    return pl.pallas_call(
        _rearranged_w_in_body,
        grid=(depth,),
        in_specs=[pl.BlockSpec((None, width, d_model), lambda l: (l, 0, 0), pipeline_mode=pl.Buffered(1))],
        out_specs=pl.BlockSpec((None, d_model, IN_WIDTH_PADDED), lambda l: (l, 0, 0)),
        out_shape=jax.ShapeDtypeStruct((depth, d_model, IN_WIDTH_PADDED), _BF16),
        compiler_params=pltpu.CompilerParams(
            dimension_semantics=("arbitrary",), vmem_limit_bytes=VMEM_LIMIT_BYTES),
        name="rearranged_w_in",
    )(jnp.swapaxes(w_in, 1, 2))


def kernel(x_prompt, x_sample, state_conv, state_gdn, state_ret, w_in, conv_w, a_log, dt_bias, gdn_norm_w, ret_norm_w, w_out, norm_mix_pre, norm_mix_post, norm_ffn_pre, norm_ffn_post, w_gate_up, w_down):
    depth = w_in.shape[0]
    batch, seq, d_model = x_prompt.shape
    nb, seq_s, _ = x_sample.shape
    assert seq_s == 1

    win = _rearranged_w_in(w_in)
    wout = w_out.astype(_BF16)
    wgu = w_gate_up.astype(_BF16)
    wd = w_down.astype(_BF16)
    alog = jnp.pad(a_log, ((0, 0), (A_HEADS, LANES - 2 * A_HEADS)))[:, None, :]
    dtb = jnp.pad(dt_bias, ((0, 0), (A_HEADS, LANES - 2 * A_HEADS)))[:, None, :]
    gnw = gdn_norm_w[:, None, :]
    rnw = ret_norm_w[:, None, :]
    npre = norm_mix_pre[:, None, :]
    npost = norm_mix_post[:, None, :]
    fpre = norm_ffn_pre[:, None, :]
    fpost = norm_ffn_post[:, None, :]
    cos_p, sin_p = _rope_tables(np.arange(seq))
    cos_s, sin_s = _rope_tables(PAST_LEN + np.arange(seq_s))
    conv_state = jnp.swapaxes(state_conv, 1, 2)

    hp = x_prompt
    hs = x_sample.reshape(nb, d_model)
    convs_p, convs_s = [], []
    states = None
    for l in range(depth):
        sq, sk, sv, sc, gates, conv_s = _sample_pre(hs, cos_s, sin_s, win, conv_w, alog, dtb, npre,
                                                    conv_state, l)
        hp, conv_p, gdn_p, ret_p, so, gdn_s, ret_s = _layer_prompt(
            hp, cos_p, sin_p, win, conv_w, alog, dtb, gnw, rnw, wout, npre, npost, wgu, wd, fpre, fpost,
            sq, sk, sv, sc, state_gdn, state_ret, states, l)
        states = (gdn_p, ret_p, gdn_s, ret_s)
        hs = _sample_post(so, gates, hs, gnw, rnw, wout, npost, wgu, wd, fpre, fpost, l)
        convs_p.append(conv_p)
        convs_s.append(conv_s)
    gdn_p, ret_p, gdn_s, ret_s = states
    return (hp, hs.reshape(nb, seq_s, d_model), jnp.stack(convs_p), gdn_p, ret_p,
            jnp.swapaxes(jnp.stack(convs_s), 1, 2), gdn_s, ret_s)
```

```python
import collections
import functools
import math

import jax
import jax.numpy as jnp
import numpy as np
from jax import lax
from jax.experimental import pallas as pl
from jax.experimental.pallas import tpu as pltpu

HEAD_DIM = 128
A_HEADS = 4
B_HEADS = 4
A_WIDTH = A_HEADS * HEAD_DIM
B_WIDTH = B_HEADS * HEAD_DIM
QKV_WIDTH = 3 * A_WIDTH
CONV_W = 4
CHUNK = 64
ROPE_BASE = 10000.0
EPS = 1e-6
L2_EPS = 1e-6
MASKED_LOG = -1e30
PAST_LEN = 16384
LANES = 128
SUBLANES = 8

REST_WIDTH = A_WIDTH + 4 * B_WIDTH + LANES
Z_OFF = 0
BQ_OFF = A_WIDTH
BK_OFF = BQ_OFF + B_WIDTH
BV_OFF = BK_OFF + B_WIDTH
BG_OFF = BV_OFF + B_WIDTH
SMALL_OFF = BG_OFF + B_WIDTH
IN_WIDTH_PADDED = QKV_WIDTH + REST_WIDTH

GROUP = 256
CHUNKS_PER_GROUP = GROUP // CHUNK
PROMPT_TIME_BLOCK = 256
FFN_COL_BLOCK = 256
WEIGHT_COL_BLOCK = 256
SAMPLE_SCALARS = 4
STACKED_STATE_OUTPUTS = (2, 3, 5, 6)
VMEM_LIMIT_BYTES = 56 * 1024 * 1024

_BF16 = jnp.bfloat16
_F32 = jnp.float32


def _dot(a, b):
    return jnp.dot(a.astype(_BF16), b.astype(_BF16), preferred_element_type=_F32)


def _dot_nt(a, b):
    return lax.dot_general(a.astype(_BF16), b.astype(_BF16), (((1,), (1,)), ((), ())),
                           preferred_element_type=_F32)


def _dot_tn(a, b):
    return lax.dot_general(a.astype(_BF16), b.astype(_BF16), (((0,), (0,)), ((), ())),
                           preferred_element_type=_F32)


def _rmsnorm(x, w):
    return x * lax.rsqrt(jnp.mean(x * x, axis=-1, keepdims=True) + EPS) * w


def _silu(x):
    return x * jax.nn.sigmoid(x)


def _softplus(x):
    return jnp.maximum(x, 0.0) + jnp.log1p(jnp.exp(-jnp.abs(x)))


def _log_gamma(h):
    return math.log1p(-(2.0 ** (-5.0 - h)))


def _rope(x, cos_full, sin_signed):
    return x * cos_full + pltpu.roll(x, HEAD_DIM // 2, axis=1) * sin_signed


def _head_cols(base, h):
    return slice(base + h * HEAD_DIM, base + (h + 1) * HEAD_DIM)


def _gated_head_norm(o, w, gate):
    return o * lax.rsqrt(jnp.mean(o * o, axis=-1, keepdims=True) + EPS) * w * _silu(gate)


class _Filler:
    def __init__(self, pieces):
        self._pieces = list(pieces)
        self._next = 0

    def emit(self, count):
        for _ in range(count):
            if self._next < len(self._pieces):
                self._pieces[self._next]()
                self._next += 1

    def flush(self):
        self.emit(len(self._pieces))


def _fold_rows(m):
    out = m[0:CHUNK]
    for c in range(1, CHUNKS_PER_GROUP):
        out = out + m[c * CHUNK:(c + 1) * CHUNK]
    return out


def _unfold_rows(r, same_chunk):
    return jnp.where(same_chunk, jnp.concatenate([r] * CHUNKS_PER_GROUP, axis=0), 0.0)


def _unit_lower_inverses(a_folded, eye_folded, same_chunk, fill):
    xs = list(a_folded)
    ps = [eye_folded - x for x in xs]
    n = 1
    while n < CHUNK:
        for h in range(len(xs)):
            x_bd = _unfold_rows(xs[h], same_chunk).astype(_BF16)
            if n == 1:
                xs[h] = _dot(xs[h], x_bd)
            elif 2 * n < CHUNK:
                r = _dot(jnp.concatenate([xs[h], ps[h]], axis=0), x_bd)
                xs[h] = r[0:CHUNK]
                ps[h] = ps[h] + r[CHUNK:2 * CHUNK]
            else:
                ps[h] = ps[h] + _dot(ps[h], x_bd)
            fill.emit(h % 2)
        n *= 2
    return [_unfold_rows(p, same_chunk) for p in ps]


def _recurrences_one_group(g, fill, act_ref, pr_ref, small_ref, dec_ref, sg_ref, sr_ref, o_ref, gnw, rnw):
    rows = slice(g * GROUP, (g + 1) * GROUP)
    ri = lax.broadcasted_iota(jnp.int32, (GROUP, GROUP), 0)
    ci = lax.broadcasted_iota(jnp.int32, (GROUP, GROUP), 1)
    same_chunk = (ri // CHUNK) == (ci // CHUNK)
    tril = same_chunk & (ri >= ci)
    off_diag = ri != ci
    rf = lax.broadcasted_iota(jnp.int32, (CHUNK, GROUP), 0)
    cf = lax.broadcasted_iota(jnp.int32, (CHUNK, GROUP), 1)
    eye_folded = jnp.where(rf == cf % CHUNK, 1.0, 0.0).astype(_F32)
    pos = (lax.broadcasted_iota(jnp.int32, (GROUP, HEAD_DIM), 0) % CHUNK).astype(_F32)
    chunk_rows = [slice(c * CHUNK, (c + 1) * CHUNK) for c in range(CHUNKS_PER_GROUP)]

    qs, ks, gccs, a_folded, a_intra, uw_rhs, e_ins = [], [], [], [], [], [], []
    for h in range(A_HEADS):
        q = act_ref[rows, _head_cols(0, h)]
        k = act_ref[rows, _head_cols(A_WIDTH, h)]
        v = act_ref[rows, _head_cols(2 * A_WIDTH, h)]
        beta = jnp.broadcast_to(small_ref[0, rows, h:h + 1], (GROUP, HEAD_DIM))
        gc_col = small_ref[1, rows, A_HEADS + h:A_HEADS + h + 1]
        gcc = jnp.broadcast_to(gc_col, (GROUP, HEAD_DIM))
        decay = dec_ref[h]
        kb = k * beta
        e_in = jnp.exp(gcc)
        kq = _dot_nt(jnp.concatenate([kb, q], axis=0), k)
        a_folded.append(_fold_rows(jnp.where(off_diag, kq[0:GROUP] * decay, 0.0)))
        a_intra.append(kq[GROUP:2 * GROUP] * decay)
        uw_rhs.append(jnp.concatenate([v * beta, kb * e_in], axis=1).astype(_BF16))
        qs.append(q)
        ks.append(k)
        gccs.append(gcc)
        e_ins.append(e_in)
        fill.emit(1)

    r_qe, r_oloc, r_b, r_echunk = [], [], [], []
    for h in range(B_HEADS):
        lg = _log_gamma(h)
        q = pr_ref[rows, _head_cols(BQ_OFF, h)]
        k = pr_ref[rows, _head_cols(BK_OFF, h)]
        v = pr_ref[rows, _head_cols(BV_OFF, h)]
        decay = jnp.exp(jnp.where(tril, (ri - ci).astype(_F32) * lg, MASKED_LOG))
        vb16 = v.astype(_BF16)
        r_oloc.append(_dot(_dot_nt(q, k) * decay, vb16))
        r_qe.append((q * jnp.exp((pos + 1.0) * lg)).astype(_BF16))
        k_out = (k * jnp.exp((CHUNK - 1.0 - pos) * lg)).astype(_BF16)
        r_b.append([_dot_tn(k_out[cr], vb16[cr]) for cr in chunk_rows])
        r_echunk.append(math.exp(CHUNK * lg))

    t_inv = _unit_lower_inverses(a_folded, eye_folded, same_chunk, fill)

    g_lhs, g_oloc, g_b, g_elast = [], [], [], []
    for h in range(A_HEADS):
        uw = _dot(t_inv[h], uw_rhs[h])
        uw16 = uw.astype(_BF16)
        aiuw = _dot(a_intra[h], uw16)
        g_oloc.append(aiuw[:, 0:HEAD_DIM])
        q_eff = qs[h] * e_ins[h] - aiuw[:, HEAD_DIM:2 * HEAD_DIM]
        lhs, bs, elast = [], [], []
        for cr in chunk_rows:
            gcc_c = gccs[h][cr]
            g_last = gcc_c[CHUNK - 1:CHUNK, :]
            k_out = ks[h][cr] * jnp.exp(g_last - gcc_c)
            bg = _dot_tn(k_out, uw16[cr])
            bs.append(bg[:, 0:HEAD_DIM])
            lhs.append(jnp.concatenate([q_eff[cr], bg[:, HEAD_DIM:2 * HEAD_DIM]], axis=0).astype(_BF16))
            elast.append(jnp.exp(g_last))
        g_lhs.append(lhs)
        g_b.append(bs)
        g_elast.append(elast)
        fill.emit(1)

    def normalize_chunk(c):
        out_rows = slice(g * GROUP + c * CHUNK, g * GROUP + (c + 1) * CHUNK)
        for h in range(A_HEADS):
            cs = _head_cols(0, h)
            o_ref[out_rows, cs] = _gated_head_norm(o_ref[out_rows, cs], gnw,
                                                   pr_ref[out_rows, _head_cols(Z_OFF, h)])
        for h in range(B_HEADS):
            cs = _head_cols(A_WIDTH, h)
            o_ref[out_rows, cs] = _gated_head_norm(o_ref[out_rows, cs], rnw,
                                                   pr_ref[out_rows, _head_cols(BG_OFF, h)])

    g_state = [sg_ref[h] for h in range(A_HEADS)]
    r_state = [sr_ref[h] for h in range(B_HEADS)]
    for c, cr in enumerate(chunk_rows):
        out_rows = slice(g * GROUP + c * CHUNK, g * GROUP + (c + 1) * CHUNK)
        for h in range(A_HEADS):
            r = _dot(g_lhs[h][c], g_state[h])
            o_ref[out_rows, _head_cols(0, h)] = r[0:CHUNK] + g_oloc[h][cr]
            g_state[h] = g_state[h] * g_elast[h][c] + g_b[h][c] - r[CHUNK:CHUNK + HEAD_DIM]
        for h in range(B_HEADS):
            o_ref[out_rows, _head_cols(A_WIDTH, h)] = _dot(r_qe[h][cr], r_state[h]) + r_oloc[h][cr]
            r_state[h] = r_state[h] * r_echunk[h] + r_b[h][c]
        if c > 0:
            normalize_chunk(c - 1)
        fill.emit(1)
    normalize_chunk(CHUNKS_PER_GROUP - 1)
    for h in range(A_HEADS):
        sg_ref[h] = g_state[h]
    for h in range(B_HEADS):
        sr_ref[h] = r_state[h]


def _ffn_rows(h, wgu_ref, wd_ref, npre, npost, act_ref, d_ff):
    hn = _rmsnorm(h, npre).astype(_BF16)
    for j in range(d_ff // FFN_COL_BLOCK):
        cg = slice(j * FFN_COL_BLOCK, (j + 1) * FFN_COL_BLOCK)
        cu = slice(d_ff + j * FFN_COL_BLOCK, d_ff + (j + 1) * FFN_COL_BLOCK)
        gate = jnp.dot(hn, wgu_ref[:, cg], preferred_element_type=_F32)
        up = jnp.dot(hn, wgu_ref[:, cu], preferred_element_type=_F32)
        act_ref[:, cg] = (_silu(gate) * up).astype(_BF16)
    f = jnp.dot(act_ref[...], wd_ref[...], preferred_element_type=_F32)
    return h + _rmsnorm(f, npost)


def _ffn_pieces(h_ref, hn_ref, wgu_ref, wd_ref, npost, gate_ref, act_ref, f_ref, y_ref, d_ff):
    d_model = f_ref.shape[1]
    k_split = (d_ff // FFN_COL_BLOCK + 1) // 2 * FFN_COL_BLOCK
    pieces = []

    def gate(j):
        def run():
            cg = slice(j * FFN_COL_BLOCK, (j + 1) * FFN_COL_BLOCK)
            gate_ref[j % 2] = _silu(jnp.dot(hn_ref[...], wgu_ref[:, cg], preferred_element_type=_F32))
        return run

    def up(j):
        def run():
            cg = slice(j * FFN_COL_BLOCK, (j + 1) * FFN_COL_BLOCK)
            cu = slice(d_ff + j * FFN_COL_BLOCK, d_ff + (j + 1) * FFN_COL_BLOCK)
            act_ref[:, cg] = (gate_ref[j % 2] * jnp.dot(hn_ref[...], wgu_ref[:, cu],
                                                        preferred_element_type=_F32)).astype(_BF16)
        return run

    def down(j, first):
        def run():
            cs = slice(j * FFN_COL_BLOCK, (j + 1) * FFN_COL_BLOCK)
            if first:
                f_ref[:, cs] = jnp.dot(act_ref[:, 0:k_split], wd_ref[0:k_split, cs],
                                       preferred_element_type=_F32)
            else:
                f_ref[:, cs] = f_ref[:, cs] + jnp.dot(act_ref[:, k_split:d_ff], wd_ref[k_split:d_ff, cs],
                                                      preferred_element_type=_F32)
        return run

    def finish():
        y_ref[...] = h_ref[...] + _rmsnorm(f_ref[...], npost)

    for j in range(d_ff // FFN_COL_BLOCK):
        pieces += [gate(j), up(j)]
    for j in range(d_model // FFN_COL_BLOCK):
        pieces += [down(j, True), down(j, False)]
    pieces.append(finish)
    return pieces


def _column_of_row(row):
    return jnp.broadcast_to(row, (HEAD_DIM, HEAD_DIM)).T


def _sample_state_units(first_seq, per_step, sq_ref, sk_ref, sv_ref, sc_ref, sgi_ref, sri_ref,
                        so_ref, sgo_ref, sro_ref):
    base = pl.multiple_of((first_seq // SUBLANES) * SUBLANES, SUBLANES)
    rows = pl.ds(base, SUBLANES)
    row_id = lax.broadcasted_iota(jnp.int32, (SUBLANES, HEAD_DIM), 0)
    picks = [row_id == (first_seq - base + j) for j in range(per_step)]

    def pick(block, j):
        return jnp.sum(jnp.where(picks[j], block, 0.0), axis=0, keepdims=True)

    def put(cols, outs):
        block = so_ref[rows, cols]
        for j in range(per_step):
            block = jnp.where(picks[j], outs[j], block)
        so_ref[rows, cols] = block

    def deltanet(h):
        def run():
            cols = _head_cols(0, h)
            q8, k8, v8 = sq_ref[rows, cols], sk_ref[rows, cols], sv_ref[rows, cols]
            eg8 = sc_ref[rows, _head_cols(0, h)]
            bt8 = sc_ref[rows, _head_cols(A_WIDTH, h)]
            qk8 = sc_ref[rows, _head_cols(2 * A_WIDTH, h)]
            kq16 = jnp.concatenate([k8, q8], axis=0).astype(_BF16)
            kqs = [jnp.dot(kq16, sgi_ref[j, h].astype(_BF16), preferred_element_type=_F32)
                   for j in range(per_step)]
            outs = []
            for j in range(per_step):
                eg = pick(eg8, j)
                v_new = pick(bt8, j) * (pick(v8, j) - eg * pick(kqs[j][0:SUBLANES], j))
                outs.append(eg * pick(kqs[j][SUBLANES:2 * SUBLANES], j) + pick(qk8, j) * v_new)
                sgo_ref[j, h] = sgi_ref[j, h] * eg + _column_of_row(pick(k8, j)) * v_new
            put(cols, outs)
        return run

    def retention(h):
        def run():
            gamma = math.exp(_log_gamma(h))
            cols = _head_cols(A_WIDTH, h)
            q8, k8, v8 = sq_ref[rows, cols], sk_ref[rows, cols], sv_ref[rows, cols]
            qk8 = sc_ref[rows, _head_cols(3 * A_WIDTH, h)]
            q16 = q8.astype(_BF16)
            qss = [jnp.dot(q16, sri_ref[j, h].astype(_BF16), preferred_element_type=_F32)
                   for j in range(per_step)]
            outs = []
            for j in range(per_step):
                v1 = pick(v8, j)
                outs.append(gamma * pick(qss[j], j) + pick(qk8, j) * v1)
                sro_ref[j, h] = sri_ref[j, h] * gamma + _column_of_row(pick(k8, j)) * v1
            put(cols, outs)
        return run

    return [deltanet(h) for h in range(A_HEADS)] + [retention(h) for h in range(B_HEADS)]


_LayerInputs = collections.namedtuple("_LayerInputs", [
    "x", "cos", "sin", "win", "convw", "alog", "dtb", "gnw", "rnw", "wout", "npre", "npost", "wgu", "wd",
    "fpre", "fpost", "sq", "sk", "sv", "sc", "sgi", "sri"])
_LayerOutputsAndScratch = collections.namedtuple("_LayerOutputsAndScratch", [
    "y", "conv", "sg", "sr", "so", "sgo", "sro",
    "pq", "pr", "act", "small", "gct", "dec", "o", "h", "hn", "gate", "ffn_act", "f"])


def _layer_prompt_refs(refs, aliased):
    n_in = len(_LayerInputs._fields)
    n_skip = len(STACKED_STATE_OUTPUTS) if aliased else 0
    return _LayerInputs(*refs[:n_in]), _LayerOutputsAndScratch(*refs[n_in + n_skip:])


def _layer_prompt_body(*refs, aliased, tb, nt, nblocks, d_ff, per_step):
    s = pl.program_id(0)

    @pl.when(s < nblocks)
    def _():
        _layer_prompt_step(refs, aliased, tb, nt, d_ff, per_step)

    @pl.when(s == nblocks)
    def _():
        ins, rest = _layer_prompt_refs(refs, aliased)
        prev = lax.rem(s + 1, 2)
        _Filler(_ffn_pieces(rest.h.at[prev], rest.hn.at[prev], ins.wgu, ins.wd, ins.fpost[...],
                            rest.gate, rest.ffn_act, rest.f, rest.y, d_ff)).flush()


def _layer_prompt_step(refs, aliased, tb, nt, d_ff, per_step):
    ins, rest = _layer_prompt_refs(refs, aliased)
    (x_ref, cos_ref, sin_ref, win_ref, convw_ref, alog_ref, dtb_ref, gnw_ref, rnw_ref, wout_ref,
     npre_ref, npost_ref, wgu_ref, wd_ref, fpre_ref, fpost_ref,
     sq_ref, sk_ref, sv_ref, sc_ref, sgi_ref, sri_ref) = ins
    (y_ref, conv_ref, sg_ref, sr_ref, so_ref, sgo_ref, sro_ref,
     pq_ref, pr_ref, act_ref, small_ref, gct_ref, dec_ref, o_ref, h_ref, hn_ref, gate_ref,
     ffn_act_ref, f_ref) = rest
    s = pl.program_id(0)
    t = lax.rem(s, nt)
    pad = SUBLANES

    @pl.when(s == 0)
    def _():
        h_ref[1] = jnp.zeros(h_ref.shape[1:], _F32)
        hn_ref[1] = jnp.zeros(hn_ref.shape[1:], _BF16)
        so_ref[...] = jnp.zeros(so_ref.shape, _F32)

    @pl.when(t == 0)
    def _():
        pq_ref[0:pad, :] = jnp.zeros((pad, QKV_WIDTH), _F32)
        sg_ref[...] = jnp.zeros(sg_ref.shape, _F32)
        sr_ref[...] = jnp.zeros(sr_ref.shape, _F32)

    prev = lax.rem(s + 1, 2)
    fill = _Filler(_ffn_pieces(h_ref.at[prev], hn_ref.at[prev], wgu_ref, wd_ref, fpost_ref[...],
                               gate_ref, ffn_act_ref, f_ref, y_ref, d_ff))
    fill.emit(2)
    sample = _Filler(_sample_state_units(s * per_step, per_step, sq_ref, sk_ref, sv_ref, sc_ref, sgi_ref,
                                         sri_ref, so_ref, sgo_ref, sro_ref))

    x = x_ref[...]
    hn = _rmsnorm(x, npre_ref[...]).astype(_BF16)

    def project(lo, hi):
        return jnp.dot(hn, win_ref[:, lo:hi], preferred_element_type=_F32)

    def project_rest(lo, hi):
        pr_ref[:, lo:hi] = project(QKV_WIDTH + lo, QKV_WIDTH + hi)

    ps = project(QKV_WIDTH + SMALL_OFF, IN_WIDTH_PADDED)
    fill.emit(1)
    beta_all = jax.nn.sigmoid(ps)
    gc = -jnp.exp(alog_ref[...]) * _softplus(ps + dtb_ref[...])
    row_in_chunk = lax.broadcasted_iota(jnp.int32, (tb, LANES), 0) % CHUNK
    shift = 1
    while shift < CHUNK:
        gc = gc + jnp.where(row_in_chunk >= shift, pltpu.roll(gc, shift, axis=0), 0.0)
        shift *= 2
    small_ref[0] = beta_all
    small_ref[1] = gc
    gct_ref[...] = gc.T

    half = B_WIDTH // 2
    cos_full = cos_ref[...]
    sin_signed = sin_ref[...]
    ri = lax.broadcasted_iota(jnp.int32, (GROUP, GROUP), 0)
    ci = lax.broadcasted_iota(jnp.int32, (GROUP, GROUP), 1)
    tril = ((ri // CHUNK) == (ci // CHUNK)) & (ri >= ci)
    for pair in range(B_HEADS // 2):
        project_rest(BQ_OFF + pair * half, BQ_OFF + (pair + 1) * half)
        fill.emit(1)
        project_rest(BK_OFF + pair * half, BK_OFF + (pair + 1) * half)
        fill.emit(1)
        sample.emit(1)
        for h in (2 * pair, 2 * pair + 1):
            gc_col = small_ref[1, :, A_HEADS + h:A_HEADS + h + 1]
            gcr = jnp.broadcast_to(gct_ref[A_HEADS + h:A_HEADS + h + 1, :], (GROUP, GROUP))
            diff = jnp.broadcast_to(gc_col, (GROUP, GROUP)) - gcr
            dec_ref[h] = jnp.exp(jnp.where(tril, diff, MASKED_LOG))
            cq = _head_cols(BQ_OFF, h)
            ck = _head_cols(BK_OFF, h)
            pr_ref[:, cq] = _rope(pr_ref[:, cq], cos_full, sin_signed)
            pr_ref[:, ck] = _rope(pr_ref[:, ck], cos_full, sin_signed) * (HEAD_DIM ** -0.5)

    for lo in range(0, QKV_WIDTH, FFN_COL_BLOCK):
        pq_ref[pad:pad + tb, lo:lo + FFN_COL_BLOCK] = project(lo, lo + FFN_COL_BLOCK)
        fill.emit(1)
        sample.emit(1)

    rest = [(lo, lo + FFN_COL_BLOCK) for lo in range(Z_OFF, BQ_OFF, FFN_COL_BLOCK)]
    rest += [(lo, lo + FFN_COL_BLOCK) for lo in range(BV_OFF, SMALL_OFF, FFN_COL_BLOCK)]
    for j in range(QKV_WIDTH // LANES):
        if j % 2 == 0 and rest:
            project_rest(*rest.pop(0))
        else:
            fill.emit(1)
        cs = slice(j * LANES, (j + 1) * LANES)
        acc = pq_ref[pad:pad + tb, cs] * convw_ref[3:4, cs]
        for i in range(CONV_W - 1):
            acc = acc + pq_ref[pad - 3 + i:pad - 3 + i + tb, cs] * convw_ref[i:i + 1, cs]
        a = _silu(acc)
        if j < 2 * A_HEADS:
            a = a * lax.rsqrt(jnp.sum(a * a, axis=-1, keepdims=True) + L2_EPS)
        if j < A_HEADS:
            a = a * (HEAD_DIM ** -0.5)
        act_ref[:, cs] = a
    assert not rest

    tail = pq_ref[pad + tb - 3:pad + tb, :]
    conv_ref[...] = tail
    pq_ref[pad - 3:pad, :] = tail

    for g in range(tb // GROUP):
        _recurrences_one_group(g, fill, act_ref, pr_ref, small_ref, dec_ref, sg_ref, sr_ref, o_ref,
                               gnw_ref[...], rnw_ref[...])

    cat = o_ref[...].astype(_BF16)
    d_model = x_ref.shape[1]
    ms, sq = [], 0.0
    for lo in range(0, d_model, FFN_COL_BLOCK):
        part = jnp.dot(cat, wout_ref[:, lo:lo + FFN_COL_BLOCK], preferred_element_type=_F32)
        sq = sq + jnp.sum(part * part, axis=-1, keepdims=True)
        ms.append(part)
    fill.emit(8)
    sample.flush()
    scale = lax.rsqrt(sq / d_model + EPS)
    h_new = x_ref[...] + jnp.concatenate(ms, axis=1) * scale * npost_ref[...]
    cur = lax.rem(s, 2)
    h_ref[cur] = h_new
    hn_ref[cur] = _rmsnorm(h_new, fpre_ref[...]).astype(_BF16)
    fill.flush()


def _layer_spec(shape, layer):
    zeros = (0,) * len(shape)
    return pl.BlockSpec((None,) + tuple(shape), lambda i: (layer,) + zeros)


def _layer_prompt(x, cos_full, sin_signed, win, conv_w, alog, dtb, gnw, rnw, wout, npre, npost,
                  wgu, wd, fpre, fpost, sq, sk, sv, sc, state_gdn, state_ret, prev_states, layer):
    batch, seq, d_model = x.shape
    nb = sq.shape[0]
    depth = state_gdn.shape[0]
    d_ff = wd.shape[1]
    tb = min(PROMPT_TIME_BLOCK, seq)
    assert seq % tb == 0 and tb == GROUP and d_ff % FFN_COL_BLOCK == 0
    nt = seq // tb
    nblocks = batch * nt
    assert nb % nblocks == 0 and SUBLANES % (nb // nblocks) == 0
    per_step = nb // nblocks
    lspec = functools.partial(_layer_spec, layer=layer)

    def mixer_block(s):
        return jnp.minimum(s, nblocks - 1)

    def ffn_block(s):
        return jnp.maximum(s - 1, 0)

    full = lambda shape: pl.BlockSpec(tuple(shape), lambda s: (0,) * len(shape))
    state_spec = lambda heads: pl.BlockSpec((None, per_step, heads, HEAD_DIM, HEAD_DIM),
                                            lambda s: (layer, mixer_block(s), 0, 0, 0))

    in_specs = [
        pl.BlockSpec((None, tb, d_model), lambda s: (mixer_block(s) // nt, mixer_block(s) % nt, 0)),
        pl.BlockSpec((tb, HEAD_DIM), lambda s: (mixer_block(s) % nt, 0)),
        pl.BlockSpec((tb, HEAD_DIM), lambda s: (mixer_block(s) % nt, 0)),
        lspec((d_model, IN_WIDTH_PADDED)),
        lspec((CONV_W, QKV_WIDTH)),
        lspec((1, LANES)),
        lspec((1, LANES)),
        lspec((1, HEAD_DIM)),
        lspec((1, HEAD_DIM)),
        lspec((A_WIDTH + B_WIDTH, d_model)),
        lspec((1, d_model)),
        lspec((1, d_model)),
        lspec((d_model, 2 * d_ff)),
        lspec((d_ff, d_model)),
        lspec((1, d_model)),
        lspec((1, d_model)),
        full(sq.shape), full(sk.shape), full(sv.shape), full(sc.shape),
        state_spec(A_HEADS), state_spec(B_HEADS),
    ]
    args = [x, cos_full, sin_signed, win, conv_w, alog, dtb, gnw, rnw, wout, npre, npost, wgu, wd, fpre,
            fpost, sq, sk, sv, sc, state_gdn, state_ret]
    aliases = {}
    if prev_states is not None:
        in_specs += [pl.BlockSpec(memory_space=pl.ANY)] * len(STACKED_STATE_OUTPUTS)
        aliases = {len(args) + i: out for i, out in enumerate(STACKED_STATE_OUTPUTS)}
        args += list(prev_states)
    prompt_state_spec = lambda heads: pl.BlockSpec((None, None, heads, HEAD_DIM, HEAD_DIM),
                                                   lambda s: (layer, mixer_block(s) // nt, 0, 0, 0))
    out_specs = [
        pl.BlockSpec((None, tb, d_model), lambda s: (ffn_block(s) // nt, ffn_block(s) % nt, 0)),
        pl.BlockSpec((None, CONV_W - 1, QKV_WIDTH), lambda s: (mixer_block(s) // nt, 0, 0)),
        prompt_state_spec(A_HEADS), prompt_state_spec(B_HEADS),
        full(sq.shape), state_spec(A_HEADS), state_spec(B_HEADS),
    ]
    out_shape = [
        jax.ShapeDtypeStruct((batch, seq, d_model), _F32),
        jax.ShapeDtypeStruct((batch, CONV_W - 1, QKV_WIDTH), _F32),
        jax.ShapeDtypeStruct((depth, batch, A_HEADS, HEAD_DIM, HEAD_DIM), _F32),
        jax.ShapeDtypeStruct((depth, batch, B_HEADS, HEAD_DIM, HEAD_DIM), _F32),
        jax.ShapeDtypeStruct(sq.shape, _F32),
        jax.ShapeDtypeStruct((depth, nb, A_HEADS, HEAD_DIM, HEAD_DIM), _F32),
        jax.ShapeDtypeStruct((depth, nb, B_HEADS, HEAD_DIM, HEAD_DIM), _F32),
    ]
    scratch = [
        pltpu.VMEM((tb + SUBLANES, QKV_WIDTH), _F32),
        pltpu.VMEM((tb, REST_WIDTH), _F32),
        pltpu.VMEM((tb, QKV_WIDTH), _F32),
        pltpu.VMEM((2, tb, LANES), _F32),
        pltpu.VMEM((LANES, tb), _F32),
        pltpu.VMEM((A_HEADS, GROUP, GROUP), _F32),
        pltpu.VMEM((tb, A_WIDTH + B_WIDTH), _F32),
        pltpu.VMEM((2, tb, d_model), _F32),
        pltpu.VMEM((2, tb, d_model), _BF16),
        pltpu.VMEM((2, tb, FFN_COL_BLOCK), _F32),
        pltpu.VMEM((tb, d_ff), _BF16),
        pltpu.VMEM((tb, d_model), _F32),
    ]
    return pl.pallas_call(
        functools.partial(_layer_prompt_body, aliased=prev_states is not None, tb=tb, nt=nt, nblocks=nblocks,
                          d_ff=d_ff, per_step=per_step),
        grid=(nblocks + 1,), in_specs=in_specs, out_specs=out_specs, out_shape=out_shape,
        scratch_shapes=scratch, input_output_aliases=aliases,
        compiler_params=pltpu.CompilerParams(
            dimension_semantics=("arbitrary",), vmem_limit_bytes=VMEM_LIMIT_BYTES),
        name=f"layer_prompt_l{layer}",
    )(*args)


def _sample_pre_body(x_ref, cos_ref, sin_ref, win_ref, convw_ref, alog_ref, dtb_ref, npre_ref, convs_ref,
                     q_ref, k_ref, v_ref, sc_ref, gates_ref, convn_ref):
    nb = x_ref.shape[0]
    hn = _rmsnorm(x_ref[...], npre_ref[...]).astype(_BF16)
    pq = jnp.dot(hn, win_ref[:, 0:QKV_WIDTH], preferred_element_type=_F32)
    pr = jnp.dot(hn, win_ref[:, QKV_WIDTH:IN_WIDTH_PADDED], preferred_element_type=_F32)
    outs = (q_ref, k_ref, v_ref)
    for j in range(QKV_WIDTH // LANES):
        cs = slice(j * LANES, (j + 1) * LANES)
        new = pq[:, cs]
        acc = new * convw_ref[3:4, cs]
        for r in range(CONV_W - 1):
            hist = convs_ref[r, :, cs]
            acc = acc + hist * convw_ref[r:r + 1, cs]
            if r > 0:
                convn_ref[r - 1, :, cs] = hist
        convn_ref[CONV_W - 2, :, cs] = new
        a = _silu(acc)
        if j < 2 * A_HEADS:
            a = a * lax.rsqrt(jnp.sum(a * a, axis=-1, keepdims=True) + L2_EPS)
        if j < A_HEADS:
            a = a * (HEAD_DIM ** -0.5)
        outs[j // A_HEADS][:, _head_cols(0, j % A_HEADS)] = a
    ps = pr[:, SMALL_OFF:SMALL_OFF + LANES]
    beta_all = jax.nn.sigmoid(ps)
    eg_all = jnp.exp(-jnp.exp(alog_ref[...]) * _softplus(ps + dtb_ref[...]))
    cos_full = cos_ref[...]
    sin_signed = sin_ref[...]
    for h in range(A_HEADS):
        sc_ref[:, _head_cols(0, h)] = jnp.broadcast_to(eg_all[:, A_HEADS + h:A_HEADS + h + 1], (nb, LANES))
        sc_ref[:, _head_cols(A_WIDTH, h)] = jnp.broadcast_to(beta_all[:, h:h + 1], (nb, LANES))
        qk = jnp.sum(q_ref[:, _head_cols(0, h)] * k_ref[:, _head_cols(0, h)], axis=-1, keepdims=True)
        sc_ref[:, _head_cols(2 * A_WIDTH, h)] = jnp.broadcast_to(qk, (nb, LANES))
    for h in range(B_HEADS):
        q = _rope(pr[:, _head_cols(BQ_OFF, h)], cos_full, sin_signed)
        k = _rope(pr[:, _head_cols(BK_OFF, h)], cos_full, sin_signed) * (HEAD_DIM ** -0.5)
        q_ref[:, _head_cols(A_WIDTH, h)] = q
        k_ref[:, _head_cols(A_WIDTH, h)] = k
        sc_ref[:, _head_cols(3 * A_WIDTH, h)] = jnp.broadcast_to(
            jnp.sum(q * k, axis=-1, keepdims=True), (nb, LANES))
    v_ref[:, A_WIDTH:A_WIDTH + B_WIDTH] = pr[:, BV_OFF:BG_OFF]
    gates_ref[:, 0:A_WIDTH] = pr[:, Z_OFF:BQ_OFF]
    gates_ref[:, A_WIDTH:A_WIDTH + B_WIDTH] = pr[:, BG_OFF:SMALL_OFF]


def _sample_pre(x, cos_full, sin_signed, win, conv_w, alog, dtb, npre, conv_state, layer):
    nb, d_model = x.shape
    width = A_WIDTH + B_WIDTH
    lspec = functools.partial(_layer_spec, layer=layer)
    full = lambda shape: pl.BlockSpec(tuple(shape), lambda i: (0,) * len(shape))
    shapes = [(nb, width)] * 3 + [(nb, SAMPLE_SCALARS * A_WIDTH), (nb, width), (CONV_W - 1, nb, QKV_WIDTH)]
    return pl.pallas_call(
        _sample_pre_body,
        grid=(1,),
        in_specs=[full((nb, d_model)), full((1, HEAD_DIM)), full((1, HEAD_DIM)),
                  lspec((d_model, IN_WIDTH_PADDED)), lspec((CONV_W, QKV_WIDTH)), lspec((1, LANES)),
                  lspec((1, LANES)), lspec((1, d_model)), lspec((CONV_W - 1, nb, QKV_WIDTH))],
        out_specs=[full(shape) for shape in shapes],
        out_shape=[jax.ShapeDtypeStruct(shape, _F32) for shape in shapes],
        compiler_params=pltpu.CompilerParams(
            dimension_semantics=("arbitrary",), vmem_limit_bytes=VMEM_LIMIT_BYTES),
        name=f"sample_pre_l{layer}",
    )(x, cos_full, sin_signed, win, conv_w, alog, dtb, npre, conv_state)


def _sample_post_body(o_ref, gates_ref, x_ref, gnw_ref, rnw_ref, wout_ref, npost_ref, wgu_ref, wd_ref,
                      fpre_ref, fpost_ref, y_ref, cat_ref, act_ref, *, d_ff):
    for h in range(A_HEADS + B_HEADS):
        cs = _head_cols(0, h)
        w = gnw_ref[...] if h < A_HEADS else rnw_ref[...]
        cat_ref[:, cs] = _gated_head_norm(o_ref[:, cs], w, gates_ref[:, cs]).astype(_BF16)
    m = jnp.dot(cat_ref[...], wout_ref[...], preferred_element_type=_F32)
    h_new = x_ref[...] + _rmsnorm(m, npost_ref[...])
    y_ref[...] = _ffn_rows(h_new, wgu_ref, wd_ref, fpre_ref[...], fpost_ref[...], act_ref, d_ff)


def _sample_post(o, gates, x, gnw, rnw, wout, npost, wgu, wd, fpre, fpost, layer):
    nb, d_model = x.shape
    d_ff = wd.shape[1]
    width = A_WIDTH + B_WIDTH
    lspec = functools.partial(_layer_spec, layer=layer)
    full = lambda shape: pl.BlockSpec(tuple(shape), lambda i: (0,) * len(shape))
    return pl.pallas_call(
        functools.partial(_sample_post_body, d_ff=d_ff),
        grid=(1,),
        in_specs=[full((nb, width)), full((nb, width)), full((nb, d_model)), lspec((1, HEAD_DIM)),
                  lspec((1, HEAD_DIM)), lspec((width, d_model)), lspec((1, d_model)),
                  lspec((d_model, 2 * d_ff)), lspec((d_ff, d_model)), lspec((1, d_model)),
                  lspec((1, d_model))],
        out_specs=full((nb, d_model)),
        out_shape=jax.ShapeDtypeStruct((nb, d_model), _F32),
        scratch_shapes=[pltpu.VMEM((nb, width), _BF16), pltpu.VMEM((nb, d_ff), _BF16)],
        compiler_params=pltpu.CompilerParams(
            dimension_semantics=("arbitrary",), vmem_limit_bytes=VMEM_LIMIT_BYTES),
        name=f"sample_post_l{layer}",
    )(o, gates, x, gnw, rnw, wout, npost, wgu, wd, fpre, fpost)


def _rope_tables(positions):
    half = HEAD_DIM // 2
    inv = ROPE_BASE ** (-np.arange(half, dtype=np.float64) / half)
    ang = np.asarray(positions, dtype=np.float64)[:, None] * inv[None, :]
    cos, sin = np.cos(ang), np.sin(ang)
    return (jnp.asarray(np.concatenate([cos, cos], axis=-1), dtype=_F32),
            jnp.asarray(np.concatenate([-sin, sin], axis=-1), dtype=_F32))


def _rearranged_w_in_body(wt_ref, o_ref):
    small0 = QKV_WIDTH + A_WIDTH
    small1 = small0 + 2 * A_HEADS
    cb = WEIGHT_COL_BLOCK
    for lo in range(0, small0, cb):
        o_ref[:, lo:lo + cb] = wt_ref[lo:lo + cb, :].T.astype(_BF16)
    for lo in range(small0, QKV_WIDTH + SMALL_OFF, cb):
        src = lo + small1 - small0
        o_ref[:, lo:lo + cb] = wt_ref[src:src + cb, :].T.astype(_BF16)
    lane = lax.broadcasted_iota(jnp.int32, (o_ref.shape[0], LANES), 1)
    o_ref[:, QKV_WIDTH + SMALL_OFF:IN_WIDTH_PADDED] = jnp.where(
        lane < 2 * A_HEADS, wt_ref[small0:small0 + LANES, :].T, 0.0).astype(_BF16)


def _rearranged_w_in(w_in):
    depth, d_model, width = w_in.shape
    assert width == IN_WIDTH_PADDED - LANES + 2 * A_HEADS
    return pl.pallas_call(
        _rearranged_w_in_body,
        grid=(depth,),
        in_specs=[pl.BlockSpec((None, width, d_model), lambda l: (l, 0, 0), pipeline_mode=pl.Buffered(1))],
        out_specs=pl.BlockSpec((None, d_model, IN_WIDTH_PADDED), lambda l: (l, 0, 0)),
        out_shape=jax.ShapeDtypeStruct((depth, d_model, IN_WIDTH_PADDED), _BF16),
        compiler_params=pltpu.CompilerParams(
            dimension_semantics=("arbitrary",), vmem_limit_bytes=VMEM_LIMIT_BYTES),
        name="rearranged_w_in",
    )(jnp.swapaxes(w_in, 1, 2))


def kernel(x_prompt, x_sample, state_conv, state_gdn, state_ret, w_in, conv_w, a_log, dt_bias, gdn_norm_w, ret_norm_w, w_out, norm_mix_pre, norm_mix_post, norm_ffn_pre, norm_ffn_post, w_gate_up, w_down):
    depth = w_in.shape[0]
    batch, seq, d_model = x_prompt.shape
    nb, seq_s, _ = x_sample.shape
    assert seq_s == 1

    win = _rearranged_w_in(w_in)
    wout = w_out.astype(_BF16)
    wgu = w_gate_up.astype(_BF16)
    wd = w_down.astype(_BF16)
    alog = jnp.pad(a_log, ((0, 0), (A_HEADS, LANES - 2 * A_HEADS)))[:, None, :]
    dtb = jnp.pad(dt_bias, ((0, 0), (A_HEADS, LANES - 2 * A_HEADS)))[:, None, :]
    gnw = gdn_norm_w[:, None, :]
    rnw = ret_norm_w[:, None, :]
    npre = norm_mix_pre[:, None, :]
    npost = norm_mix_post[:, None, :]
    fpre = norm_ffn_pre[:, None, :]
    fpost = norm_ffn_post[:, None, :]
    cos_p, sin_p = _rope_tables(np.arange(seq))
    cos_s, sin_s = _rope_tables(PAST_LEN + np.arange(seq_s))
    conv_state = jnp.swapaxes(state_conv, 1, 2)

    hp = x_prompt
    hs = x_sample.reshape(nb, d_model)
    convs_p, convs_s = [], []
    states = None
    for l in range(depth):
        sq, sk, sv, sc, gates, conv_s = _sample_pre(hs, cos_s, sin_s, win, conv_w, alog, dtb, npre,
                                                    conv_state, l)
        hp, conv_p, gdn_p, ret_p, so, gdn_s, ret_s = _layer_prompt(
            hp, cos_p, sin_p, win, conv_w, alog, dtb, gnw, rnw, wout, npre, npost, wgu, wd, fpre, fpost,
            sq, sk, sv, sc, state_gdn, state_ret, states, l)
        states = (gdn_p, ret_p, gdn_s, ret_s)
        hs = _sample_post(so, gates, hs, gnw, rnw, wout, npost, wgu, wd, fpre, fpost, l)
        convs_p.append(conv_p)
        convs_s.append(conv_s)
    gdn_p, ret_p, gdn_s, ret_s = states
    return (hp, hs.reshape(nb, seq_s, d_model), jnp.stack(convs_p), gdn_p, ret_p,
            jnp.swapaxes(jnp.stack(convs_s), 1, 2), gdn_s, ret_s)
```
